```python
import jax, jax.numpy as jnp
from jax import lax
import numpy as np

D_MODEL = 1024
BATCH = 16
SEQ = 2048
DEPTH = 4

MEM_LEN = 256
EPS = 1e-6
N_NORMS = 8
D_FF = 2816
RW_HEADS = 8
RW_HEAD_DIM = 64
RW_WIDTH = RW_HEADS * RW_HEAD_DIM
RW_DECAY_RANK = 64
RW_AAA_RANK = 64
RW_GATE_RANK = 128
RW_LN_EPS = 64e-5
RW_SIZES = (RW_WIDTH, RW_WIDTH, RW_WIDTH, RW_DECAY_RANK, RW_AAA_RANK, RW_GATE_RANK)
RW_IN = 3 * RW_WIDTH + RW_DECAY_RANK + RW_AAA_RANK + RW_GATE_RANK
RT_HEADS = 4
RT_KEY_DIM = 128
RT_VAL_DIM = 128
RT_WIDTH = RT_HEADS * RT_VAL_DIM
RT_CHUNK = 128
RT_ROPE_BASE = 10000.0
RT_SIZES = (RT_HEADS * RT_KEY_DIM, RT_HEADS * RT_KEY_DIM, RT_WIDTH, RT_WIDTH)
RT_IN = 2 * RT_HEADS * RT_KEY_DIM + 2 * RT_WIDTH
EV_IN = RW_IN + RT_IN
EV_OUT = RW_WIDTH + RT_WIDTH
DSA_HEADS = 8
DSA_HEAD_DIM = 64
DSA_WIDTH = DSA_HEADS * DSA_HEAD_DIM
DSA_Q_RANK = 256
DSA_KV_RANK = 128
IDX_HEADS = 8
IDX_DIM = 64
TOPK_MAX = 256
Q_BLOCK = 128
DSA_SIZES = (DSA_Q_RANK, DSA_KV_RANK, IDX_DIM, IDX_HEADS)
DSA_IN = DSA_Q_RANK + DSA_KV_RANK + IDX_DIM + IDX_HEADS
SC_WIDTH = 512
SC_KERNEL = 3
SC_SIZES = (SC_WIDTH, SC_WIDTH, SC_WIDTH)
SC_IN = 3 * SC_WIDTH
OD_IN = DSA_IN + SC_IN
OD_OUT = DSA_WIDTH + SC_WIDTH
XA_HEADS = 4
XA_HEAD_DIM = 128
XA_WIDTH = XA_HEADS * XA_HEAD_DIM

kernel_name = 'hybrid_rwkv7_retnet_dsa_shortconv_block'

F32 = jnp.float32


def split_cols(t, sizes):
    out, start = [], 0
    for s in sizes:
        out.append(t[..., start:start + s])
        start += s
    return out


def rms_norm(x, g, eps=EPS):
    xf = x.astype(F32)
    y = xf * lax.rsqrt(jnp.mean(xf * xf, axis=-1, keepdims=True) + eps)
    return (y * g.astype(F32)).astype(x.dtype)


def group_norm_heads(x, g, b, eps):
    xf = x.astype(F32)
    mu = jnp.mean(xf, axis=-1, keepdims=True)
    var = jnp.mean(jnp.square(xf - mu), axis=-1, keepdims=True)
    y = ((xf - mu) * lax.rsqrt(var + eps)).reshape(*x.shape[:-2], -1)
    return y * g.astype(F32) + b.astype(F32)


def token_shift(p):
    return jnp.pad(p, ((0, 0), (1, 0), (0, 0)))[:, :-1]


def swiglu(x, w_gate, w_up, w_down):
    return (jax.nn.silu(x @ w_gate) * (x @ w_up)) @ w_down


def rotary(x, positions):
    half = x.shape[-1] // 2
    inv_freq = RT_ROPE_BASE ** (-jnp.arange(half, dtype=F32) / half)
    ang = positions.astype(F32)[:, None] * inv_freq[None, :]
    cos = jnp.cos(ang)[None, :, None, :]
    sin = jnp.sin(ang)[None, :, None, :]
    x1, x2 = x[..., :half].astype(F32), x[..., half:].astype(F32)
    return jnp.concatenate([x1 * cos - x2 * sin, x1 * sin + x2 * cos], axis=-1).astype(x.dtype)


def rwkv7_mix(p, mu, w0, w2, a0, a2, g2, k_k, k_a, r_k, ln_g, ln_b):
    b_, s_, _ = p.shape
    p = p + (token_shift(p) - p) * mu
    r, k, v, xw, xa, xg = split_cols(p, RW_SIZES)
    w = -jax.nn.softplus(-(w0 + jnp.tanh(xw) @ w2)) - 0.5
    decay = jnp.exp(-jnp.exp(w.astype(F32)))
    a = jax.nn.sigmoid(a0 + xa @ a2)
    g = jax.nn.sigmoid(xg) @ g2
    heads = lambda t: t.reshape(b_, s_, RW_HEADS, RW_HEAD_DIM)
    kk = heads(k * k_k).astype(F32)
    kk = kk / jnp.maximum(jnp.sqrt(jnp.sum(kk * kk, axis=-1, keepdims=True)), 1e-12)
    k = k * (1 + (a - 1) * k_a)
    rh, kh, vh, ah = heads(r), heads(k), heads(v), heads(a).astype(F32)
    xs = tuple(jnp.moveaxis(t.astype(F32), 1, 0) for t in (rh, heads(decay), kh, vh, -kk, kk * ah))

    def step(state, inp):
        r_t, w_t, k_t, v_t, a_t, b_t = inp
        sa = jnp.einsum('bhvk,bhk->bhv', state, a_t)
        state = (state * w_t[:, :, None, :] + sa[..., None] * b_t[:, :, None, :]
                 + v_t[..., None] * k_t[:, :, None, :])
        return state, jnp.einsum('bhvk,bhk->bhv', state, r_t)

    state0 = jnp.zeros((b_, RW_HEADS, RW_HEAD_DIM, RW_HEAD_DIM), F32)
    _, y = lax.scan(step, state0, xs)
    y = group_norm_heads(jnp.moveaxis(y, 0, 1), ln_g, ln_b, RW_LN_EPS)
    bonus = jnp.sum(rh * kh * r_k, axis=-1, keepdims=True) * vh
    y = (y + bonus.reshape(b_, s_, RW_WIDTH).astype(F32)) * g.astype(F32)
    return y.astype(p.dtype)


def retention_mix(p, positions, gn_g, gn_b):
    b_, s_, _ = p.shape
    q, k, v, g = split_cols(p, RT_SIZES)
    q = rotary(q.reshape(b_, s_, RT_HEADS, RT_KEY_DIM), positions)
    k = rotary(k.reshape(b_, s_, RT_HEADS, RT_KEY_DIM), positions) * (RT_KEY_DIM ** -0.5)
    v = v.reshape(b_, s_, RT_HEADS, RT_VAL_DIM)
    n_chunks = s_ // RT_CHUNK
    chunk = lambda t: t.reshape(b_, n_chunks, RT_CHUNK, *t.shape[2:]).astype(F32)
    qc, kc, vc = chunk(q), chunk(k), chunk(v)
    log_gamma = jnp.log1p(-jnp.exp2(-5.0 - jnp.arange(RT_HEADS, dtype=F32)))
    idx = jnp.arange(RT_CHUNK, dtype=F32)
    rel = idx[:, None] - idx[None, :]
    decay_mask = jnp.where(rel[None] >= 0,
                           jnp.exp(jnp.maximum(rel, 0.0)[None] * log_gamma[:, None, None]), 0.0)
    scores = jnp.einsum('bnihd,bnjhd->bnhij', qc, kc) * decay_mask
    o_inner = jnp.einsum('bnhij,bnjhe->bnihe', scores, vc)
    zeta = jnp.exp((RT_CHUNK - 1 - idx)[None, :] * log_gamma[:, None])
    kv = jnp.einsum('bnjhd,hj,bnjhe->bnhde', kc, zeta, vc)
    chunk_gamma = jnp.exp(RT_CHUNK * log_gamma)[None, :, None, None]

    def step(state, kv_n):
        return state * chunk_gamma + kv_n, state

    state0 = jnp.zeros((b_, RT_HEADS, RT_KEY_DIM, RT_VAL_DIM), F32)
    _, r_prev = lax.scan(step, state0, jnp.moveaxis(kv, 1, 0))
    r_prev = jnp.moveaxis(r_prev, 0, 1)
    xi = jnp.exp((idx + 1)[:, None] * log_gamma[None, :])
    o_cross = jnp.einsum('bnihd,bnhde->bnihe', qc, r_prev) * xi[:, :, None]
    o = (o_inner + o_cross).reshape(b_, s_, RT_HEADS, RT_VAL_DIM)
    o = group_norm_heads(o, gn_g, gn_b, EPS)
    return (jax.nn.silu(g.astype(F32)) * o).astype(p.dtype)


def dsa_mix(p, q_norm_g, kv_norm_g, w_uq, w_uk, w_uv, w_qi):
    b_, s_, _ = p.shape
    c_q, c_kv, k_idx, w_idx = split_cols(p, DSA_SIZES)
    c_q = rms_norm(c_q, q_norm_g)
    c_kv = rms_norm(c_kv, kv_norm_g)
    q = jnp.einsum('btr,rhd->bthd', c_q, w_uq)
    q_lat = jnp.einsum('bthd,hdc->bthc', q, w_uk)
    q_idx = jnp.einsum('btr,rhd->bthd', c_q, w_qi)
    w_idx = w_idx * ((IDX_HEADS * IDX_DIM) ** -0.5)
    top_k = min(TOPK_MAX, s_ // 4)
    n_blocks = s_ // Q_BLOCK
    key_pos = jnp.arange(s_)

    def to_blocks(t):
        return jnp.moveaxis(t.reshape(b_, n_blocks, Q_BLOCK, *t.shape[2:]), 1, 0)

    def block(args):
        blk, ql, qi, wi = args
        q_pos = blk * Q_BLOCK + jnp.arange(Q_BLOCK)
        logits = jnp.einsum('bthd,bsd->bths', qi, k_idx)
        score = jnp.einsum('bth,bths->bts', wi, jax.nn.relu(logits)).astype(F32)
        causal = key_pos[None, :] <= q_pos[:, None]
        score = jnp.where(causal[None], score, -jnp.inf)
        top_val, top_idx = lax.top_k(score, top_k)
        valid = jnp.isfinite(top_val)
        sel = jax.vmap(lambda c, i: c[i])(c_kv, top_idx)
        s = jnp.einsum('bthc,btkc->bthk', ql, sel).astype(F32) * (DSA_HEAD_DIM ** -0.5)
        s = jnp.where(valid[:, :, None, :], s, -jnp.inf)
        prob = jax.nn.softmax(s, axis=-1).astype(sel.dtype)
        return jnp.einsum('bthk,btkc->bthc', prob, sel)

    o_lat = lax.map(block, (jnp.arange(n_blocks), to_blocks(q_lat), to_blocks(q_idx), to_blocks(w_idx)))
    o_lat = jnp.moveaxis(o_lat, 0, 1).reshape(b_, s_, DSA_HEADS, DSA_KV_RANK)
    o = jnp.einsum('bthc,hcd->bthd', o_lat, w_uv)
    return o.reshape(b_, s_, DSA_WIDTH).astype(p.dtype)


def short_conv_mix(p, conv_w, conv_b):
    h, gate_b, gate_c = split_cols(p, SC_SIZES)
    u = gate_c * h
    y = lax.conv_general_dilated(u, conv_w[:, None, :].astype(u.dtype), window_strides=(1,),
                                 padding=[(SC_KERNEL - 1, 0)],
                                 dimension_numbers=('NWC', 'WIO', 'NWC'),
                                 feature_group_count=SC_WIDTH) + conv_b
    return (gate_b * y).astype(p.dtype)


def memory_xattn(h, mem_n, wq, wk, wv, wo):
    b_, s_, _ = h.shape
    m_ = mem_n.shape[1]
    q = (h @ wq).reshape(b_, s_, XA_HEADS, XA_HEAD_DIM)
    k = (mem_n @ wk).reshape(b_, m_, XA_HEADS, XA_HEAD_DIM)
    v = (mem_n @ wv).reshape(b_, m_, XA_HEADS, XA_HEAD_DIM)
    s = jnp.einsum('bthd,bmhd->bhtm', q, k).astype(F32) * (XA_HEAD_DIM ** -0.5)
    prob = jax.nn.softmax(s, axis=-1).astype(v.dtype)
    o = jnp.einsum('bhtm,bmhd->bthd', prob, v).reshape(b_, s_, XA_WIDTH)
    return o @ wo


def setup_inputs(seed: int = 0) -> dict:
    key = jax.random.key(seed)
    ks = iter(jax.random.split(key, 48))
    n_even = (DEPTH + 1) // 2
    n_odd = DEPTH // 2
    nrm = lambda shape, scale: jax.random.normal(next(ks), shape, F32) * scale
    gain = lambda shape: 1.0 + nrm(shape, 0.02)
    uni = lambda shape, lo, hi: jax.random.uniform(next(ks), shape, F32, lo, hi)
    return {
        'x': nrm((BATCH, SEQ, D_MODEL), 1.0),
        'mem': nrm((BATCH, MEM_LEN, D_MODEL), 1.0),
        'norm_g': gain((DEPTH, N_NORMS, D_MODEL)),
        'mem_norm_g': gain((D_MODEL,)),
        'ffn_w_gate': nrm((DEPTH, 2, D_MODEL, D_FF), D_MODEL ** -0.5),
        'ffn_w_up': nrm((DEPTH, 2, D_MODEL, D_FF), D_MODEL ** -0.5),
        'ffn_w_down': nrm((DEPTH, 2, D_FF, D_MODEL), D_FF ** -0.5),
        'xa_wq': nrm((DEPTH, D_MODEL, XA_WIDTH), D_MODEL ** -0.5),
        'xa_wk': nrm((DEPTH, D_MODEL, XA_WIDTH), D_MODEL ** -0.5),
        'xa_wv': nrm((DEPTH, D_MODEL, XA_WIDTH), D_MODEL ** -0.5),
        'xa_wo': nrm((DEPTH, XA_WIDTH, D_MODEL), XA_WIDTH ** -0.5),
        'ev_w_in': nrm((n_even, D_MODEL, EV_IN), D_MODEL ** -0.5),
        'ev_w_out': nrm((n_even, EV_OUT, D_MODEL), EV_OUT ** -0.5),
        'rw_mu': uni((n_even, RW_IN), 0.0, 1.0),
        'rw_w0': uni((n_even, RW_WIDTH), -6.0, 1.0),
        'rw_w2': nrm((n_even, RW_DECAY_RANK, RW_WIDTH), 0.5 * RW_DECAY_RANK ** -0.5),
        'rw_a0': nrm((n_even, RW_WIDTH), 0.1),
        'rw_a2': nrm((n_even, RW_AAA_RANK, RW_WIDTH), 0.5 * RW_AAA_RANK ** -0.5),
        'rw_g2': nrm((n_even, RW_GATE_RANK, RW_WIDTH), RW_GATE_RANK ** -0.5),
        'rw_k_k': 0.85 + nrm((n_even, RW_WIDTH), 0.05),
        'rw_k_a': 1.0 + nrm((n_even, RW_WIDTH), 0.05),
        'rw_r_k': nrm((n_even, RW_HEADS, RW_HEAD_DIM), 0.1),
        'rw_ln_g': gain((n_even, RW_WIDTH)),
        'rw_ln_b': nrm((n_even, RW_WIDTH), 0.02),
        'rt_gn_g': gain((n_even, RT_WIDTH)),
        'rt_gn_b': nrm((n_even, RT_WIDTH), 0.02),
        'od_w_in': nrm((n_odd, D_MODEL, OD_IN), D_MODEL ** -0.5),
        'od_w_out': nrm((n_odd, OD_OUT, D_MODEL), OD_OUT ** -0.5),
        'dsa_q_norm_g': gain((n_odd, DSA_Q_RANK)),
        'dsa_kv_norm_g': gain((n_odd, DSA_KV_RANK)),
        'dsa_w_uq': nrm((n_odd, DSA_Q_RANK, DSA_HEADS, DSA_HEAD_DIM), DSA_Q_RANK ** -0.5),
        'dsa_w_uk': nrm((n_odd, DSA_HEADS, DSA_HEAD_DIM, DSA_KV_RANK), DSA_HEAD_DIM ** -0.5),
        'dsa_w_uv': nrm((n_odd, DSA_HEADS, DSA_KV_RANK, DSA_HEAD_DIM), DSA_KV_RANK ** -0.5),
        'dsa_w_qi': nrm((n_odd, DSA_Q_RANK, IDX_HEADS, IDX_DIM), DSA_Q_RANK ** -0.5),
        'sc_conv_w': nrm((n_odd, SC_KERNEL, SC_WIDTH), SC_KERNEL ** -0.5),
        'sc_conv_b': nrm((n_odd, SC_WIDTH), 0.02),
    }


def reference(x, mem, norm_g, mem_norm_g, ffn_w_gate, ffn_w_up, ffn_w_down,
              xa_wq, xa_wk, xa_wv, xa_wo, ev_w_in, ev_w_out,
              rw_mu, rw_w0, rw_w2, rw_a0, rw_a2, rw_g2, rw_k_k, rw_k_a, rw_r_k, rw_ln_g, rw_ln_b,
              rt_gn_g, rt_gn_b, od_w_in, od_w_out,
              dsa_q_norm_g, dsa_kv_norm_g, dsa_w_uq, dsa_w_uk, dsa_w_uv, dsa_w_qi,
              sc_conv_w, sc_conv_b):
    mem_n = rms_norm(mem, mem_norm_g)
    positions = jnp.arange(x.shape[1])
    for l in range(DEPTH):
        ng = norm_g[l]
        h = swiglu(rms_norm(x, ng[0]), ffn_w_gate[l, 0], ffn_w_up[l, 0], ffn_w_down[l, 0])
        x = x + 0.5 * rms_norm(h, ng[1])
        h = rms_norm(x, ng[2])
        i = l // 2
        if l % 2 == 0:
            p = h @ ev_w_in[i]
            y_a = rwkv7_mix(p[..., :RW_IN], rw_mu[i], rw_w0[i], rw_w2[i], rw_a0[i], rw_a2[i],
                            rw_g2[i], rw_k_k[i], rw_k_a[i], rw_r_k[i], rw_ln_g[i], rw_ln_b[i])
            y_b = retention_mix(p[..., RW_IN:], positions, rt_gn_g[i], rt_gn_b[i])
            h = jnp.concatenate([y_a, y_b], axis=-1) @ ev_w_out[i]
        else:
            p = h @ od_w_in[i]
            y_c = dsa_mix(p[..., :DSA_IN], dsa_q_norm_g[i], dsa_kv_norm_g[i], dsa_w_uq[i],
                          dsa_w_uk[i], dsa_w_uv[i], dsa_w_qi[i])
            y_d = short_conv_mix(p[..., DSA_IN:], sc_conv_w[i], sc_conv_b[i])
            h = jnp.concatenate([y_c, y_d], axis=-1) @ od_w_out[i]
        x = x + rms_norm(h, ng[3])
        h = memory_xattn(rms_norm(x, ng[4]), mem_n, xa_wq[l], xa_wk[l], xa_wv[l], xa_wo[l])
        x = x + rms_norm(h, ng[5])
        h = swiglu(rms_norm(x, ng[6]), ffn_w_gate[l, 1], ffn_w_up[l, 1], ffn_w_down[l, 1])
        x = x + 0.5 * rms_norm(h, ng[7])
    return x
```

```python
import functools
import math

import numpy as np
import jax
import jax.numpy as jnp
from jax import lax
from jax.experimental import pallas as pl
from jax.experimental.pallas import tpu as pltpu

F32 = jnp.float32
BF16 = jnp.bfloat16

D_MODEL = 1024
D_FF = 2816
EPS = 1e-6
MEM_LEN = 256
RW_HEADS = 8
RW_HEAD_DIM = 64
RW_WIDTH = RW_HEADS * RW_HEAD_DIM
RW_DECAY_RANK = 64
RW_AAA_RANK = 64
RW_GATE_RANK = 128
RW_LN_EPS = 64e-5
RW_IN = 3 * RW_WIDTH + RW_DECAY_RANK + RW_AAA_RANK + RW_GATE_RANK
RW_CHUNK = 64
RT_HEADS = 4
RT_DIM = 128
RT_WIDTH = RT_HEADS * RT_DIM
RT_CHUNK = 128
RT_ROPE_BASE = 10000.0
RT_IN = 4 * RT_WIDTH
DSA_HEADS = 8
DSA_HEAD_DIM = 64
DSA_WIDTH = DSA_HEADS * DSA_HEAD_DIM
DSA_Q_RANK = 256
DSA_KV_RANK = 128
IDX_HEADS = 8
IDX_DIM = 64
TOPK_MAX = 256
Q_BLOCK = 128
DSA_IN = DSA_Q_RANK + DSA_KV_RANK + IDX_DIM + IDX_HEADS
SC_WIDTH = 512
SC_KERNEL = 3
XA_HEADS = 4
XA_HEAD_DIM = 128
XA_WIDTH = XA_HEADS * XA_HEAD_DIM

LANES = 128
ROW_TILE = 512
VMEM_LIMIT = 56 * 1024 * 1024
INT_MIN = -2 ** 31
NEG_BIG = -1e30


def _params(*sem):
    return pltpu.CompilerParams(dimension_semantics=sem, vmem_limit_bytes=VMEM_LIMIT)


def _rms(x, g):
    return x * lax.rsqrt(jnp.mean(x * x, axis=-1, keepdims=True) + EPS) * g


def _dot(a, b):
    return jnp.dot(a.astype(BF16), b.astype(BF16), preferred_element_type=F32)


def _dot_nt(a, b):
    return lax.dot_general(a.astype(BF16), b.astype(BF16), (((1,), (1,)), ((), ())),
                           preferred_element_type=F32)


def _dot_tn(a, b):
    return lax.dot_general(a.astype(BF16), b.astype(BF16), (((0,), (0,)), ((), ())),
                           preferred_element_type=F32)


def _split3(x):
    hi = x.astype(BF16)
    r1 = x - hi.astype(F32)
    mid = r1.astype(BF16)
    lo = (r1 - mid.astype(F32)).astype(BF16)
    return hi, mid, lo


def _dot_exact_rhs(x, w_bf16):
    hi, mid, lo = _split3(x)
    out = jnp.dot(hi, w_bf16, preferred_element_type=F32)
    out += jnp.dot(mid, w_bf16, preferred_element_type=F32)
    out += jnp.dot(lo, w_bf16, preferred_element_type=F32)
    return out


def _dot_exact_lhs(w_bf16, x):
    hi, mid, lo = _split3(x)
    out = jnp.dot(w_bf16, hi, preferred_element_type=F32)
    out += jnp.dot(w_bf16, mid, preferred_element_type=F32)
    out += jnp.dot(w_bf16, lo, preferred_element_type=F32)
    return out


def _const_spec(shape):
    nd = len(shape)
    return pl.BlockSpec(shape, lambda *_: (0,) * nd, pipeline_mode=pl.Buffered(1))


FF_CHUNK = 1408


def _ffn_kernel(x_ref, gin_ref, wg_ref, wu_ref, wd_ref, gout_ref, o_ref, acc_ref):
    x = x_ref[...]
    xb = _rms(x, gin_ref[...]).astype(BF16)
    for c in range(D_FF // FF_CHUNK):
        sl = slice(c * FF_CHUNK, (c + 1) * FF_CHUNK)
        g = jnp.dot(xb, wg_ref[:, sl], preferred_element_type=F32)
        u = jnp.dot(xb, wu_ref[:, sl], preferred_element_type=F32)
        h = (g * jax.nn.sigmoid(g) * u).astype(BF16)
        part = jnp.dot(h, wd_ref[sl, :], preferred_element_type=F32)
        if c == 0:
            acc_ref[...] = part
        else:
            acc_ref[...] += part
    o_ref[...] = x + 0.5 * _rms(acc_ref[...], gout_ref[...])


def _ffn(x, g_in, wg, wu, wd, g_out):
    t = x.shape[0]
    row = pl.BlockSpec((ROW_TILE, D_MODEL), lambda i: (i, 0))
    return pl.pallas_call(
        _ffn_kernel,
        grid=(t // ROW_TILE,),
        in_specs=[row, _const_spec((1, D_MODEL)), _const_spec((D_MODEL, D_FF)),
                  _const_spec((D_MODEL, D_FF)), _const_spec((D_FF, D_MODEL)), _const_spec((1, D_MODEL))],
        out_specs=row,
        out_shape=jax.ShapeDtypeStruct((t, D_MODEL), F32),
        scratch_shapes=[pltpu.VMEM((ROW_TILE, D_MODEL), F32)],
        compiler_params=_params("parallel"),
        name="ffn_half",
    )(x, g_in.reshape(1, -1), wg, wu, wd, g_out.reshape(1, -1))


PROJ_CHUNK = 512


def _norm_proj_kernel(n_out, x_ref, g_ref, *refs):
    xb = _rms(x_ref[...], g_ref[...]).astype(BF16)
    for w_ref, o_ref in zip(refs[:n_out], refs[n_out:]):
        n = w_ref.shape[1]
        for c in range(0, n, PROJ_CHUNK):
            sl = slice(c, min(c + PROJ_CHUNK, n))
            o_ref[:, sl] = jnp.dot(xb, w_ref[:, sl], preferred_element_type=F32)


def _norm_proj(x, g, ws, name):
    t, d = x.shape
    row = lambda n: pl.BlockSpec((ROW_TILE, n), lambda i: (i, 0))
    return pl.pallas_call(
        functools.partial(_norm_proj_kernel, len(ws)),
        grid=(t // ROW_TILE,),
        in_specs=[row(d), _const_spec((1, d))] + [_const_spec(w.shape) for w in ws],
        out_specs=[row(w.shape[1]) for w in ws],
        out_shape=[jax.ShapeDtypeStruct((t, w.shape[1]), F32) for w in ws],
        compiler_params=_params("parallel"),
        name=name,
    )(x, g.reshape(1, -1), *ws)


def _proj_res_kernel(n_in, *refs):
    y_refs, w_refs = refs[:n_in], refs[n_in:2 * n_in]
    g_ref, x_ref, o_ref = refs[2 * n_in:]
    h = None
    for y_ref, w_ref in zip(y_refs, w_refs):
        part = jnp.dot(y_ref[...].astype(BF16), w_ref[...], preferred_element_type=F32)
        h = part if h is None else h + part
    o_ref[...] = x_ref[...] + _rms(h, g_ref[...])


def _proj_res(ys, ws, g, x, name):
    t, d = x.shape
    row = lambda n: pl.BlockSpec((ROW_TILE, n), lambda i: (i, 0))
    return pl.pallas_call(
        functools.partial(_proj_res_kernel, len(ys)),
        grid=(t // ROW_TILE,),
        in_specs=[row(y.shape[1]) for y in ys] + [_const_spec(w.shape) for w in ws]
                 + [_const_spec((1, d)), row(d)],
        out_specs=row(d),
        out_shape=jax.ShapeDtypeStruct((t, d), F32),
        compiler_params=_params("parallel"),
        name=name,
    )(*ys, *ws, g.reshape(1, -1), x)


RW_TILE = 512


def _shift_rows(x, carry, n):
    rolled = pltpu.roll(x, n, 0)
    row = lax.broadcasted_iota(jnp.int32, x.shape, 0)
    out = rolled
    for i in range(n):
        out = jnp.where(row == i, carry[8 - n + i:8 - n + i + 1, :], out)
    return out


def _rw_prep_kernel(p_ref, mu_ref, w0_ref, w2_ref, a0_ref, a2_ref, g2_ref, kk_ref, ka_ref, ones_ref,
                    r_ref, k_ref, v_ref, ld_ref, a_ref, b_ref, g_ref, carry_ref):
    @pl.when(pl.program_id(1) == 0)
    def _():
        carry_ref[...] = jnp.zeros_like(carry_ref)

    p = p_ref[...]
    prev = _shift_rows(p, carry_ref[...], 1)
    carry_ref[...] = p[RW_TILE - 8:, :]
    xm = p + (prev - p) * mu_ref[...]
    w = RW_WIDTH
    r, k, v = xm[:, :w], xm[:, w:2 * w], xm[:, 2 * w:3 * w]
    xw = xm[:, 3 * w:3 * w + RW_DECAY_RANK]
    xa = xm[:, 3 * w + RW_DECAY_RANK:3 * w + RW_DECAY_RANK + RW_AAA_RANK]
    xg = xm[:, 3 * w + RW_DECAY_RANK + RW_AAA_RANK:]
    wlog = -jax.nn.softplus(-(w0_ref[...] + _dot(jnp.tanh(xw), w2_ref[...]))) - 0.5
    a = jax.nn.sigmoid(a0_ref[...] + _dot(xa, a2_ref[...]))
    kk = k * kk_ref[...]
    ss = _dot_exact_rhs(kk * kk, ones_ref[...])
    kk = kk / jnp.maximum(jnp.sqrt(ss), 1e-12)
    r_ref[...] = r
    k_ref[...] = k * (1.0 + (a - 1.0) * ka_ref[...])
    v_ref[...] = v
    ld_ref[...] = -jnp.exp(wlog)
    a_ref[...] = -kk
    b_ref[...] = kk * a
    g_ref[...] = _dot(jax.nn.sigmoid(xg), g2_ref[...])


def _rw_prep(p_rw, b_, s_, mu, w0, w2, a0, a2, g2, k_k, k_a, ones_blk):
    t = p_rw.shape[0]
    nt = s_ // RW_TILE
    row = lambda n: pl.BlockSpec((RW_TILE, n), lambda b, i: (b * nt + i, 0))
    vec = lambda a: a.reshape(1, -1)
    outs = [jax.ShapeDtypeStruct((t, RW_WIDTH), F32)] * 7
    return pl.pallas_call(
        _rw_prep_kernel,
        grid=(b_, nt),
        in_specs=[row(RW_IN), _const_spec((1, RW_IN)), _const_spec((1, RW_WIDTH)),
                  _const_spec(w2.shape), _const_spec((1, RW_WIDTH)), _const_spec(a2.shape),
                  _const_spec(g2.shape), _const_spec((1, RW_WIDTH)), _const_spec((1, RW_WIDTH)),
                  _const_spec(ones_blk.shape)],
        out_specs=[row(RW_WIDTH)] * 7,
        out_shape=outs,
        scratch_shapes=[pltpu.VMEM((8, RW_IN), F32)],
        compiler_params=_params("parallel", "arbitrary"),
        name="rwkv_prep",
    )(p_rw, vec(mu), vec(w0), w2, vec(a0), a2, g2, vec(k_k), vec(k_a), ones_blk)


def _rw_scan_kernel(r_ref, k_ref, v_ref, ld_ref, a_ref, b_ref, g_ref, tri_ref, avg_ref, ones_ref,
                    rk_ref, lng_ref, lnb_ref, o_ref, state_ref, y_ref):
    c_ = RW_CHUNK
    n_ = RW_HEAD_DIM

    @pl.when(pl.program_id(1) == 0)
    def _():
        state_ref[...] = jnp.zeros_like(state_ref)

    ld = ld_ref[...]
    r = r_ref[...]
    k = k_ref[...]
    v = v_ref[...]
    cum = _dot_exact_lhs(tri_ref[...], ld)
    mid = cum[c_ // 2 - 1:c_ // 2, :]
    e_in = jnp.exp(cum - mid)
    e_out = jnp.exp(mid - cum)
    r_t = r * e_in
    a_t = a_ref[...] * jnp.exp(cum - ld - mid)
    b_t = b_ref[...] * e_out
    k_t = k * e_out
    e_mid = jnp.exp(mid)
    w_all = jnp.exp(cum[c_ - 1:c_, :])
    w_tail = jnp.exp(cum[c_ - 1:c_, :] - mid)

    row = lax.broadcasted_iota(jnp.int32, (c_, c_), 0)
    col = lax.broadcasted_iota(jnp.int32, (c_, c_), 1)
    strict = row > col
    incl = row >= col
    eye = (row == col).astype(F32)

    for h in range(RW_HEADS):
        sl = slice(h * n_, (h + 1) * n_)
        s0 = state_ref[h]
        ar = jnp.concatenate([a_t[:, sl], r_t[:, sl]], axis=0)
        bk = jnp.concatenate([b_t[:, sl], k_t[:, sl]], axis=0)
        m1 = _dot_nt(ar, bk)
        m2 = _dot_nt(ar * e_mid[:, sl], s0)
        l_ab = jnp.where(strict, m1[:c_, :c_], 0.0)
        l_ak = jnp.where(strict, m1[:c_, c_:], 0.0)
        l_rb = jnp.where(incl, m1[c_:, :c_], 0.0)
        l_rk = jnp.where(incl, m1[c_:, c_:], 0.0)
        inv = eye + l_ab
        pw = l_ab
        for _ in range(int(math.log2(c_)) - 1):
            pw = _dot(pw, pw)
            inv = inv + _dot(inv, pw)
        vh = v[:, sl]
        u = _dot(inv, m2[:c_] + _dot(l_ak, vh))
        y_ref[:, sl] = m2[c_:] + _dot(l_rb, u) + _dot(l_rk, vh)
        upd = _dot_tn(u, b_t[:, sl]) + _dot_tn(vh, k_t[:, sl])
        state_ref[h] = s0 * w_all[:, sl] + upd * w_tail[:, sl]

    y = y_ref[...]
    mu = _dot_exact_rhs(y, avg_ref[...])
    d = y - mu
    var = _dot_exact_rhs(d * d, avg_ref[...])
    yn = d * lax.rsqrt(var + RW_LN_EPS) * lng_ref[...] + lnb_ref[...]
    bonus = _dot_exact_rhs(r * k * rk_ref[...], ones_ref[...]) * v
    o_ref[...] = (yn + bonus) * g_ref[...]


def _rw_scan(r, k, v, ld, a, b, g, b_, s_, r_k, ln_g, ln_b, ones_blk, avg_blk, tri):
    t = r.shape[0]
    nc = s_ // RW_CHUNK
    row = pl.BlockSpec((RW_CHUNK, RW_WIDTH), lambda bi, i: (bi * nc + i, 0))
    vec = lambda x: x.reshape(1, -1)
    return pl.pallas_call(
        _rw_scan_kernel,
        grid=(b_, nc),
        in_specs=[row] * 7 + [_const_spec(tri.shape), _const_spec(avg_blk.shape), _const_spec(ones_blk.shape),
                              _const_spec((1, RW_WIDTH)), _const_spec((1, RW_WIDTH)), _const_spec((1, RW_WIDTH))],
        out_specs=row,
        out_shape=jax.ShapeDtypeStruct((t, RW_WIDTH), F32),
        scratch_shapes=[pltpu.VMEM((RW_HEADS, RW_HEAD_DIM, RW_HEAD_DIM), F32),
                        pltpu.VMEM((RW_CHUNK, RW_WIDTH), F32)],
        compiler_params=_params("parallel", "arbitrary"),
        name="rwkv_scan",
    )(r, k, v, ld, a, b, g, tri, avg_blk, ones_blk, vec(r_k), vec(ln_g), vec(ln_b))


def _rt_log_gamma(h):
    return math.log1p(-(2.0 ** (-5.0 - h)))


def _retention_kernel(p_ref, cos_ref, sin_ref, gng_ref, gnb_ref, o_ref, state_ref):
    c_ = RT_CHUNK

    @pl.when(pl.program_id(1) == 0)
    def _():
        state_ref[...] = jnp.zeros_like(state_ref)

    cos = cos_ref[...]
    sin = sin_ref[...]
    row = lax.broadcasted_iota(jnp.int32, (c_, c_), 0).astype(F32)
    col = lax.broadcasted_iota(jnp.int32, (c_, c_), 1).astype(F32)
    rel = row - col
    for h in range(RT_HEADS):
        lg = _rt_log_gamma(h)
        sl = slice(h * RT_DIM, (h + 1) * RT_DIM)
        q = p_ref[:, sl]
        k = p_ref[:, RT_WIDTH + h * RT_DIM:RT_WIDTH + (h + 1) * RT_DIM]
        v = p_ref[:, 2 * RT_WIDTH + h * RT_DIM:2 * RT_WIDTH + (h + 1) * RT_DIM]
        gate = p_ref[:, 3 * RT_WIDTH + h * RT_DIM:3 * RT_WIDTH + (h + 1) * RT_DIM]
        q = q * cos + pltpu.roll(q, RT_DIM // 2, 1) * sin
        k = (k * cos + pltpu.roll(k, RT_DIM // 2, 1) * sin) * (RT_DIM ** -0.5)
        decay = jnp.where(rel >= 0, jnp.exp(jnp.maximum(rel, 0.0) * lg), 0.0)
        scores = _dot_nt(q, k) * decay
        s0 = state_ref[h]
        xi = jnp.exp((row + 1.0) * lg)
        o = _dot(scores, v) + _dot(q, s0) * xi
        zeta = jnp.exp((c_ - 1.0 - row) * lg)
        state_ref[h] = s0 * math.exp(c_ * lg) + _dot_tn(k * zeta, v)
        mu = jnp.mean(o, axis=-1, keepdims=True)
        d = o - mu
        var = jnp.mean(d * d, axis=-1, keepdims=True)
        on = d * lax.rsqrt(var + EPS) * gng_ref[:, sl] + gnb_ref[:, sl]
        o_ref[:, sl] = gate * jax.nn.sigmoid(gate) * on


def _retention(p_rt, b_, s_, cos2, sin2, gn_g, gn_b):
    t = p_rt.shape[0]
    nc = s_ // RT_CHUNK
    return pl.pallas_call(
        _retention_kernel,
        grid=(b_, nc),
        in_specs=[pl.BlockSpec((RT_CHUNK, RT_IN), lambda bi, i: (bi * nc + i, 0)),
                  pl.BlockSpec((RT_CHUNK, RT_DIM), lambda bi, i: (i, 0)),
                  pl.BlockSpec((RT_CHUNK, RT_DIM), lambda bi, i: (i, 0)),
                  _const_spec((1, RT_WIDTH)), _const_spec((1, RT_WIDTH))],
        out_specs=pl.BlockSpec((RT_CHUNK, RT_WIDTH), lambda bi, i: (bi * nc + i, 0)),
        out_shape=jax.ShapeDtypeStruct((t, RT_WIDTH), F32),
        scratch_shapes=[pltpu.VMEM((RT_HEADS, RT_DIM, RT_DIM), F32)],
        compiler_params=_params("parallel", "arbitrary"),
        name="retention",
    )(p_rt, cos2, sin2, gn_g.reshape(1, -1), gn_b.reshape(1, -1))


def _dsa_prep_kernel(cq_ref, ckv_ref, qg_ref, kvg_ref, wuq_ref, wuk_ref, wqi_ref,
                     ckvn_ref, qlat_ref, qidx_ref):
    cq = _rms(cq_ref[...], qg_ref[...]).astype(BF16)
    ckvn_ref[...] = _rms(ckv_ref[...], kvg_ref[...])
    q = jnp.dot(cq, wuq_ref[...], preferred_element_type=F32)
    for h in range(DSA_HEADS):
        qh = q[:, h * DSA_HEAD_DIM:(h + 1) * DSA_HEAD_DIM]
        qlat_ref[:, h * DSA_KV_RANK:(h + 1) * DSA_KV_RANK] = _dot(qh, wuk_ref[h])
    qidx_ref[...] = jnp.dot(cq, wqi_ref[...], preferred_element_type=F32)


def _dsa_prep(c_q, c_kv, q_g, kv_g, w_uq, w_uk, w_qi):
    t = c_q.shape[0]
    row = lambda n: pl.BlockSpec((ROW_TILE, n), lambda i: (i, 0))
    return pl.pallas_call(
        _dsa_prep_kernel,
        grid=(t // ROW_TILE,),
        in_specs=[row(DSA_Q_RANK), row(DSA_KV_RANK), _const_spec((1, DSA_Q_RANK)), _const_spec((1, DSA_KV_RANK)),
                  _const_spec(w_uq.shape), _const_spec(w_uk.shape), _const_spec(w_qi.shape)],
        out_specs=[row(DSA_KV_RANK), row(DSA_HEADS * DSA_KV_RANK), row(IDX_HEADS * IDX_DIM)],
        out_shape=[jax.ShapeDtypeStruct((t, DSA_KV_RANK), F32),
                   jax.ShapeDtypeStruct((t, DSA_HEADS * DSA_KV_RANK), F32),
                   jax.ShapeDtypeStruct((t, IDX_HEADS * IDX_DIM), F32)],
        compiler_params=_params("parallel"),
        name="dsa_prep",
    )(c_q, c_kv, q_g.reshape(1, -1), kv_g.reshape(1, -1), w_uq, w_uk, w_qi)


def _dsa_kernel(top_k, qidx_ref, kw_q_ref, qlat_ref, kw_ref, ckv_ref, wuv_ref, tri_ref, o_ref,
                key_ref, m_ref, l_ref, acc_ref):
    qb = Q_BLOCK
    j = pl.program_id(1)
    n_chunks = j + 1

    q_idx = qidx_ref[...]
    w_idx = kw_q_ref[:, IDX_DIM:IDX_DIM + IDX_HEADS] * ((IDX_HEADS * IDX_DIM) ** -0.5)
    rowi = lax.broadcasted_iota(jnp.int32, (qb, qb), 0)
    coli = lax.broadcasted_iota(jnp.int32, (qb, qb), 1)

    def score_body(c, carry):
        kc = kw_ref[pl.ds(pl.multiple_of(c * qb, qb), qb), :][:, :IDX_DIM]
        score = jnp.zeros((qb, qb), F32)
        for h in range(IDX_HEADS):
            logits = _dot_nt(q_idx[:, h * IDX_DIM:(h + 1) * IDX_DIM], kc)
            score = score + w_idx[:, h:h + 1] * jnp.maximum(logits, 0.0)
        bits = lax.bitcast_convert_type(score, jnp.int32)
        key = jnp.where(bits < 0, bits ^ jnp.int32(0x7FFFFFFF), bits)
        causal = coli + c * qb <= rowi + j * qb
        key_ref[c] = jnp.where(causal, key, jnp.int32(INT_MIN))
        return carry

    lax.fori_loop(0, n_chunks, score_body, 0)

    def count_ge(cand):
        def body(c, acc):
            return acc + jnp.where(key_ref[c] >= cand, 1, 0)
        acc = lax.fori_loop(0, n_chunks, body, jnp.zeros((qb, qb), jnp.int32))
        return jnp.sum(acc, axis=-1, keepdims=True)

    thr = jnp.where(count_ge(jnp.zeros((qb, 1), jnp.int32)) >= top_k, 0, INT_MIN).astype(jnp.int32)

    def bit_body(i, thr):
        cand = thr | lax.shift_left(jnp.int32(1), 30 - i)
        return jnp.where(count_ge(cand) >= top_k, cand, thr)

    thr = lax.fori_loop(0, 31, bit_body, thr)

    def count_gt_body(c, acc):
        return acc + jnp.where(key_ref[c] > thr, 1, 0)

    n_gt = jnp.sum(lax.fori_loop(0, n_chunks, count_gt_body, jnp.zeros((qb, qb), jnp.int32)),
                   axis=-1, keepdims=True)
    need = (top_k - n_gt).astype(F32)

    m_ref[...] = jnp.full(m_ref.shape, NEG_BIG, F32)
    l_ref[...] = jnp.zeros_like(l_ref)
    acc_ref[...] = jnp.zeros_like(acc_ref)
    scale = DSA_HEAD_DIM ** -0.5

    def attn_body(c, taken):
        key = key_ref[c]
        causal = coli + c * qb <= rowi + j * qb
        eq = jnp.logical_and(key == thr, causal)
        eq_f = jnp.where(eq, 1.0, 0.0)
        rank = taken + jnp.dot(eq_f.astype(BF16), tri_ref[...], preferred_element_type=F32)
        sel = jnp.logical_or(jnp.logical_and(key > thr, causal), jnp.logical_and(eq, rank <= need))
        ckv = ckv_ref[pl.ds(pl.multiple_of(c * qb, qb), qb), :].astype(BF16)
        for h in range(DSA_HEADS):
            ql = qlat_ref[:, h * DSA_KV_RANK:(h + 1) * DSA_KV_RANK].astype(BF16)
            s = lax.dot_general(ql, ckv, (((1,), (1,)), ((), ())), preferred_element_type=F32) * scale
            s = jnp.where(sel, s, NEG_BIG)
            m_old = m_ref[h]
            m_new = jnp.maximum(m_old, jnp.max(s, axis=-1, keepdims=True))
            alpha = jnp.exp(m_old - m_new)
            p = jnp.where(sel, jnp.exp(s - m_new), 0.0)
            l_ref[h] = l_ref[h] * alpha + jnp.sum(p, axis=-1, keepdims=True)
            acc_ref[h] = acc_ref[h] * alpha + jnp.dot(p.astype(BF16), ckv, preferred_element_type=F32)
            m_ref[h] = m_new
        return taken + jnp.sum(eq_f, axis=-1, keepdims=True)

    lax.fori_loop(0, n_chunks, attn_body, jnp.zeros((qb, 1), F32))

    for h in range(DSA_HEADS):
        o_lat = acc_ref[h] / l_ref[h]
        o_ref[:, h * DSA_HEAD_DIM:(h + 1) * DSA_HEAD_DIM] = _dot(o_lat, wuv_ref[h])


def _dsa(q_idx, kw, q_lat, ckv_n, w_uv, tri, b_, s_):
    t = q_idx.shape[0]
    nb = s_ // Q_BLOCK
    top_k = min(TOPK_MAX, s_ // 4)
    blk = lambda n: pl.BlockSpec((Q_BLOCK, n), lambda bi, i: (bi * nb + i, 0))
    seq = lambda n: pl.BlockSpec((s_, n), lambda bi, i: (bi, 0))
    return pl.pallas_call(
        functools.partial(_dsa_kernel, top_k),
        grid=(b_, nb),
        in_specs=[blk(IDX_HEADS * IDX_DIM), blk(LANES), blk(DSA_HEADS * DSA_KV_RANK),
                  seq(LANES), seq(DSA_KV_RANK), _const_spec(w_uv.shape), _const_spec(tri.shape)],
        out_specs=blk(DSA_WIDTH),
        out_shape=jax.ShapeDtypeStruct((t, DSA_WIDTH), F32),
        scratch_shapes=[pltpu.VMEM((nb, Q_BLOCK, Q_BLOCK), jnp.int32),
                        pltpu.VMEM((DSA_HEADS, Q_BLOCK, 1), F32),
                        pltpu.VMEM((DSA_HEADS, Q_BLOCK, 1), F32),
                        pltpu.VMEM((DSA_HEADS, Q_BLOCK, DSA_KV_RANK), F32)],
        compiler_params=_params("parallel", "arbitrary"),
        name="dsa_attn",
    )(q_idx, kw, q_lat, kw, ckv_n, w_uv, tri)


SC_TILE = 512


def _sconv_kernel(p_ref, w_ref, b_ref, o_ref, carry_ref):
    @pl.when(pl.program_id(1) == 0)
    def _():
        carry_ref[...] = jnp.zeros_like(carry_ref)

    h = p_ref[:, :SC_WIDTH]
    gate_b = p_ref[:, SC_WIDTH:2 * SC_WIDTH]
    gate_c = p_ref[:, 2 * SC_WIDTH:]
    u = gate_c * h
    carry = carry_ref[...]
    y = u * w_ref[2:3, :] + _shift_rows(u, carry, 1) * w_ref[1:2, :] + _shift_rows(u, carry, 2) * w_ref[0:1, :]
    carry_ref[...] = u[SC_TILE - 8:, :]
    o_ref[...] = gate_b * (y + b_ref[...])


def _sconv(p_sc, b_, s_, conv_w, conv_b):
    t = p_sc.shape[0]
    nt = s_ // SC_TILE
    return pl.pallas_call(
        _sconv_kernel,
        grid=(b_, nt),
        in_specs=[pl.BlockSpec((SC_TILE, 3 * SC_WIDTH), lambda bi, i: (bi * nt + i, 0)),
                  _const_spec((8, SC_WIDTH)), _const_spec((1, SC_WIDTH))],
        out_specs=pl.BlockSpec((SC_TILE, SC_WIDTH), lambda bi, i: (bi * nt + i, 0)),
        out_shape=jax.ShapeDtypeStruct((t, SC_WIDTH), F32),
        scratch_shapes=[pltpu.VMEM((8, SC_WIDTH), F32)],
        compiler_params=_params("parallel", "arbitrary"),
        name="short_conv",
    )(p_sc, jnp.pad(conv_w, ((0, 8 - SC_KERNEL), (0, 0))), conv_b.reshape(1, -1))


def _xattn_kernel(x_ref, gq_ref, wq_ref, k_ref, v_ref, wo_ref, go_ref, o_ref, att_ref):
    x = x_ref[...]
    q = jnp.dot(_rms(x, gq_ref[...]).astype(BF16), wq_ref[...], preferred_element_type=F32)
    for h in range(XA_HEADS):
        sl = slice(h * XA_HEAD_DIM, (h + 1) * XA_HEAD_DIM)
        s = _dot_nt(q[:, sl], k_ref[:, sl]) * (XA_HEAD_DIM ** -0.5)
        s = s - jnp.max(s, axis=-1, keepdims=True)
        p = jnp.exp(s)
        p = p / jnp.sum(p, axis=-1, keepdims=True)
        att_ref[:, sl] = _dot(p, v_ref[:, sl])
    hout = jnp.dot(att_ref[...].astype(BF16), wo_ref[...], preferred_element_type=F32)
    o_ref[...] = x + _rms(hout, go_ref[...])


def _xattn(x, g_q, wq, k_mem, v_mem, wo, g_o, b_, s_):
    t = x.shape[0]
    nt = s_ // ROW_TILE
    row = pl.BlockSpec((ROW_TILE, D_MODEL), lambda bi, i: (bi * nt + i, 0))
    mem = pl.BlockSpec((MEM_LEN, XA_WIDTH), lambda bi, i: (bi, 0))
    return pl.pallas_call(
        _xattn_kernel,
        grid=(b_, nt),
        in_specs=[row, _const_spec((1, D_MODEL)), _const_spec(wq.shape), mem, mem,
                  _const_spec(wo.shape), _const_spec((1, D_MODEL))],
        out_specs=row,
        out_shape=jax.ShapeDtypeStruct((t, D_MODEL), F32),
        scratch_shapes=[pltpu.VMEM((ROW_TILE, XA_WIDTH), F32)],
        compiler_params=_params("parallel", "parallel"),
        name="mem_xattn",
    )(x, g_q.reshape(1, -1), wq, k_mem, v_mem, wo, g_o.reshape(1, -1))


def _block_diag(n_blocks, size, value):
    return np.kron(np.eye(n_blocks, dtype=np.float32), np.full((size, size), value, np.float32))


def _rope_tables(s_):
    half = RT_DIM // 2
    inv_freq = RT_ROPE_BASE ** (-jnp.arange(half, dtype=F32) / half)
    ang = jnp.arange(s_).astype(F32)[:, None] * inv_freq[None, :]
    cos, sin = jnp.cos(ang), jnp.sin(ang)
    return jnp.concatenate([cos, cos], axis=-1), jnp.concatenate([-sin, sin], axis=-1)


def kernel(x, mem, norm_g, mem_norm_g, ffn_w_gate, ffn_w_up, ffn_w_down, xa_wq, xa_wk, xa_wv, xa_wo, ev_w_in, ev_w_out, rw_mu, rw_w0, rw_w2, rw_a0, rw_a2, rw_g2, rw_k_k, rw_k_a, rw_r_k, rw_ln_g, rw_ln_b, rt_gn_g, rt_gn_b, od_w_in, od_w_out, dsa_q_norm_g, dsa_kv_norm_g, dsa_w_uq, dsa_w_uk, dsa_w_uv, dsa_w_qi, sc_conv_w, sc_conv_b):
    b_, s_, d_ = x.shape
    depth = norm_g.shape[0]
    t = b_ * s_
    bf = lambda w: w.astype(BF16)

    ones_blk = jnp.asarray(_block_diag(RW_HEADS, RW_HEAD_DIM, 1.0), BF16)
    avg_blk = jnp.asarray(_block_diag(RW_HEADS, RW_HEAD_DIM, 1.0 / RW_HEAD_DIM), BF16)
    tri_rw = jnp.asarray(np.tril(np.ones((RW_CHUNK, RW_CHUNK), np.float32)), BF16)
    tri_dsa = jnp.asarray(np.triu(np.ones((Q_BLOCK, Q_BLOCK), np.float32)), BF16)
    cos2, sin2 = _rope_tables(s_)

    xf = x.reshape(t, d_)
    mem_f = mem.reshape(b_ * MEM_LEN, d_)
    for l in range(depth):
        ng = norm_g[l]
        i = l // 2
        xf = _ffn(xf, ng[0], bf(ffn_w_gate[l, 0]), bf(ffn_w_up[l, 0]), bf(ffn_w_down[l, 0]), ng[1])
        if l % 2 == 0:
            w_in = bf(ev_w_in[i])
            p_rw, p_rt = _norm_proj(xf, ng[2], [w_in[:, :RW_IN], w_in[:, RW_IN:]], "ev_in_proj")
            r, k, v, ld, a, b, g = _rw_prep(p_rw, b_, s_, rw_mu[i], rw_w0[i], rw_w2[i], rw_a0[i], rw_a2[i],
                                            rw_g2[i], rw_k_k[i], rw_k_a[i], ones_blk)
            y_a = _rw_scan(r, k, v, ld, a, b, g, b_, s_, rw_r_k[i], rw_ln_g[i], rw_ln_b[i],
                           ones_blk, avg_blk, tri_rw)
            y_b = _retention(p_rt, b_, s_, cos2, sin2, rt_gn_g[i], rt_gn_b[i])
            w_out = bf(ev_w_out[i])
            xf = _proj_res([y_a, y_b], [w_out[:RW_WIDTH], w_out[RW_WIDTH:]], ng[3], xf, "ev_out_proj")
        else:
            w_in = od_w_in[i]
            kw_w = jnp.pad(w_in[:, DSA_Q_RANK + DSA_KV_RANK:DSA_IN], ((0, 0), (0, LANES - IDX_DIM - IDX_HEADS)))
            c_q, c_kv, kw, p_sc = _norm_proj(
                xf, ng[2], [bf(w_in[:, :DSA_Q_RANK]), bf(w_in[:, DSA_Q_RANK:DSA_Q_RANK + DSA_KV_RANK]),
                            bf(kw_w), bf(w_in[:, DSA_IN:])], "od_in_proj")
            ckv_n, q_lat, q_idx = _dsa_prep(
                c_q, c_kv, dsa_q_norm_g[i], dsa_kv_norm_g[i],
                bf(dsa_w_uq[i].reshape(DSA_Q_RANK, DSA_WIDTH)), bf(dsa_w_uk[i]),
                bf(dsa_w_qi[i].reshape(DSA_Q_RANK, IDX_HEADS * IDX_DIM)))
            y_c = _dsa(q_idx, kw, q_lat, ckv_n, bf(dsa_w_uv[i]), tri_dsa, b_, s_)
            y_d = _sconv(p_sc, b_, s_, sc_conv_w[i], sc_conv_b[i])
            w_out = bf(od_w_out[i])
            xf = _proj_res([y_c, y_d], [w_out[:DSA_WIDTH], w_out[DSA_WIDTH:]], ng[3], xf, "od_out_proj")
        k_mem, v_mem = _norm_proj(mem_f, mem_norm_g, [bf(xa_wk[l]), bf(xa_wv[l])], "mem_kv_proj")
        xf = _xattn(xf, ng[4], bf(xa_wq[l]), k_mem, v_mem, bf(xa_wo[l]), ng[5], b_, s_)
        xf = _ffn(xf, ng[6], bf(ffn_w_gate[l, 1]), bf(ffn_w_up[l, 1]), bf(ffn_w_down[l, 1]), ng[7])
    return xf.reshape(b_, s_, d_)
```

```python
import functools
import math

import numpy as np
import jax
import jax.numpy as jnp
from jax import lax
from jax.experimental import pallas as pl
from jax.experimental.pallas import tpu as pltpu

F32 = jnp.float32
BF16 = jnp.bfloat16

D_MODEL = 1024
D_FF = 2816
EPS = 1e-6
MEM_LEN = 256
RW_HEADS = 8
RW_HEAD_DIM = 64
RW_WIDTH = RW_HEADS * RW_HEAD_DIM
RW_DECAY_RANK = 64
RW_AAA_RANK = 64
RW_GATE_RANK = 128
RW_LN_EPS = 64e-5
RW_IN = 3 * RW_WIDTH + RW_DECAY_RANK + RW_AAA_RANK + RW_GATE_RANK
RW_CHUNK = 64
RT_HEADS = 4
RT_DIM = 128
RT_WIDTH = RT_HEADS * RT_DIM
RT_CHUNK = 128
RT_ROPE_BASE = 10000.0
RT_IN = 4 * RT_WIDTH
DSA_HEADS = 8
DSA_HEAD_DIM = 64
DSA_WIDTH = DSA_HEADS * DSA_HEAD_DIM
DSA_Q_RANK = 256
DSA_KV_RANK = 128
IDX_HEADS = 8
IDX_DIM = 64
TOPK_MAX = 256
Q_BLOCK = 128
DSA_IN = DSA_Q_RANK + DSA_KV_RANK + IDX_DIM + IDX_HEADS
SC_WIDTH = 512
SC_KERNEL = 3
XA_HEADS = 4
XA_HEAD_DIM = 128
XA_WIDTH = XA_HEADS * XA_HEAD_DIM

LANES = 128
ROW_TILE = 512
VMEM_LIMIT = 56 * 1024 * 1024
INT_MIN = -2 ** 31
NEG_BIG = -1e30


def _params(*sem):
    return pltpu.CompilerParams(dimension_semantics=sem, vmem_limit_bytes=VMEM_LIMIT)


def _rms(x, g):
    return x * lax.rsqrt(jnp.mean(x * x, axis=-1, keepdims=True) + EPS) * g


def _dot(a, b):
    return jnp.dot(a.astype(BF16), b.astype(BF16), preferred_element_type=F32)


def _dot_nt(a, b):
    return lax.dot_general(a.astype(BF16), b.astype(BF16), (((1,), (1,)), ((), ())),
                           preferred_element_type=F32)


def _dot_tn(a, b):
    return lax.dot_general(a.astype(BF16), b.astype(BF16), (((0,), (0,)), ((), ())),
                           preferred_element_type=F32)


def _split3(x):
    hi = x.astype(BF16)
    r1 = x - hi.astype(F32)
    mid = r1.astype(BF16)
    lo = (r1 - mid.astype(F32)).astype(BF16)
    return hi, mid, lo


def _dot_exact_rhs(x, w_bf16):
    hi, mid, lo = _split3(x)
    out = jnp.dot(hi, w_bf16, preferred_element_type=F32)
    out += jnp.dot(mid, w_bf16, preferred_element_type=F32)
    out += jnp.dot(lo, w_bf16, preferred_element_type=F32)
    return out


def _dot_exact_lhs(w_bf16, x):
    hi, mid, lo = _split3(x)
    out = jnp.dot(w_bf16, hi, preferred_element_type=F32)
    out += jnp.dot(w_bf16, mid, preferred_element_type=F32)
    out += jnp.dot(w_bf16, lo, preferred_element_type=F32)
    return out


def _const_spec(shape):
    nd = len(shape)
    return pl.BlockSpec(shape, lambda *_: (0,) * nd, pipeline_mode=pl.Buffered(1))


FF_CHUNK = 1408


def _ffn_kernel(x_ref, gin_ref, wg_ref, wu_ref, wd_ref, gout_ref, o_ref, acc_ref):
    x = x_ref[...]
    xb = _rms(x, gin_ref[...]).astype(BF16)
    for c in range(D_FF // FF_CHUNK):
        sl = slice(c * FF_CHUNK, (c + 1) * FF_CHUNK)
        g = jnp.dot(xb, wg_ref[:, sl], preferred_element_type=F32)
        u = jnp.dot(xb, wu_ref[:, sl], preferred_element_type=F32)
        h = (g * jax.nn.sigmoid(g) * u).astype(BF16)
        part = jnp.dot(h, wd_ref[sl, :], preferred_element_type=F32)
        if c == 0:
            acc_ref[...] = part
        else:
            acc_ref[...] += part
    o_ref[...] = x + 0.5 * _rms(acc_ref[...], gout_ref[...])


def _ffn(x, g_in, wg, wu, wd, g_out):
    t = x.shape[0]
    row = pl.BlockSpec((ROW_TILE, D_MODEL), lambda i: (i, 0))
    return pl.pallas_call(
        _ffn_kernel,
        grid=(t // ROW_TILE,),
        in_specs=[row, _const_spec((1, D_MODEL)), _const_spec((D_MODEL, D_FF)),
                  _const_spec((D_MODEL, D_FF)), _const_spec((D_FF, D_MODEL)), _const_spec((1, D_MODEL))],
        out_specs=row,
        out_shape=jax.ShapeDtypeStruct((t, D_MODEL), F32),
        scratch_shapes=[pltpu.VMEM((ROW_TILE, D_MODEL), F32)],
        compiler_params=_params("parallel"),
        name="ffn_half",
    )(x, g_in.reshape(1, -1), wg, wu, wd, g_out.reshape(1, -1))


PROJ_CHUNK = 512


def _norm_proj_kernel(n_out, x_ref, g_ref, *refs):
    xb = _rms(x_ref[...], g_ref[...]).astype(BF16)
    for w_ref, o_ref in zip(refs[:n_out], refs[n_out:]):
        n = w_ref.shape[1]
        for c in range(0, n, PROJ_CHUNK):
            sl = slice(c, min(c + PROJ_CHUNK, n))
            o_ref[:, sl] = jnp.dot(xb, w_ref[:, sl], preferred_element_type=F32)


def _norm_proj(x, g, ws, name):
    t, d = x.shape
    row = lambda n: pl.BlockSpec((ROW_TILE, n), lambda i: (i, 0))
    return pl.pallas_call(
        functools.partial(_norm_proj_kernel, len(ws)),
        grid=(t // ROW_TILE,),
        in_specs=[row(d), _const_spec((1, d))] + [_const_spec(w.shape) for w in ws],
        out_specs=[row(w.shape[1]) for w in ws],
        out_shape=[jax.ShapeDtypeStruct((t, w.shape[1]), F32) for w in ws],
        compiler_params=_params("parallel"),
        name=name,
    )(x, g.reshape(1, -1), *ws)


def _proj_res_kernel(n_in, *refs):
    y_refs, w_refs = refs[:n_in], refs[n_in:2 * n_in]
    g_ref, x_ref, o_ref = refs[2 * n_in:]
    h = None
    for y_ref, w_ref in zip(y_refs, w_refs):
        part = jnp.dot(y_ref[...].astype(BF16), w_ref[...], preferred_element_type=F32)
        h = part if h is None else h + part
    o_ref[...] = x_ref[...] + _rms(h, g_ref[...])


def _proj_res(ys, ws, g, x, name):
    t, d = x.shape
    row = lambda n: pl.BlockSpec((ROW_TILE, n), lambda i: (i, 0))
    return pl.pallas_call(
        functools.partial(_proj_res_kernel, len(ys)),
        grid=(t // ROW_TILE,),
        in_specs=[row(y.shape[1]) for y in ys] + [_const_spec(w.shape) for w in ws]
                 + [_const_spec((1, d)), row(d)],
        out_specs=row(d),
        out_shape=jax.ShapeDtypeStruct((t, d), F32),
        compiler_params=_params("parallel"),
        name=name,
    )(*ys, *ws, g.reshape(1, -1), x)


RW_TILE = 512


def _shift_rows(x, carry, n):
    rolled = pltpu.roll(x, n, 0)
    row = lax.broadcasted_iota(jnp.int32, x.shape, 0)
    out = rolled
    for i in range(n):
        out = jnp.where(row == i, carry[8 - n + i:8 - n + i + 1, :], out)
    return out


def _rw_prep_kernel(p_ref, mu_ref, w0_ref, w2_ref, a0_ref, a2_ref, g2_ref, kk_ref, ka_ref, ones_ref,
                    r_ref, k_ref, v_ref, ld_ref, a_ref, b_ref, g_ref, carry_ref):
    @pl.when(pl.program_id(1) == 0)
    def _():
        carry_ref[...] = jnp.zeros_like(carry_ref)

    p = p_ref[...]
    prev = _shift_rows(p, carry_ref[...], 1)
    carry_ref[...] = p[RW_TILE - 8:, :]
    xm = p + (prev - p) * mu_ref[...]
    w = RW_WIDTH
    r, k, v = xm[:, :w], xm[:, w:2 * w], xm[:, 2 * w:3 * w]
    xw = xm[:, 3 * w:3 * w + RW_DECAY_RANK]
    xa = xm[:, 3 * w + RW_DECAY_RANK:3 * w + RW_DECAY_RANK + RW_AAA_RANK]
    xg = xm[:, 3 * w + RW_DECAY_RANK + RW_AAA_RANK:]
    wlog = -jax.nn.softplus(-(w0_ref[...] + _dot(jnp.tanh(xw), w2_ref[...]))) - 0.5
    a = jax.nn.sigmoid(a0_ref[...] + _dot(xa, a2_ref[...]))
    kk = k * kk_ref[...]
    ss = _dot_exact_rhs(kk * kk, ones_ref[...])
    kk = kk / jnp.maximum(jnp.sqrt(ss), 1e-12)
    r_ref[...] = r
    k_ref[...] = k * (1.0 + (a - 1.0) * ka_ref[...])
    v_ref[...] = v
    ld_ref[...] = -jnp.exp(wlog)
    a_ref[...] = -kk
    b_ref[...] = kk * a
    g_ref[...] = _dot(jax.nn.sigmoid(xg), g2_ref[...])


def _rw_prep(p_rw, b_, s_, mu, w0, w2, a0, a2, g2, k_k, k_a, ones_blk):
    t = p_rw.shape[0]
    nt = s_ // RW_TILE
    row = lambda n: pl.BlockSpec((RW_TILE, n), lambda b, i: (b * nt + i, 0))
    vec = lambda a: a.reshape(1, -1)
    outs = [jax.ShapeDtypeStruct((t, RW_WIDTH), F32)] * 7
    return pl.pallas_call(
        _rw_prep_kernel,
        grid=(b_, nt),
        in_specs=[row(RW_IN), _const_spec((1, RW_IN)), _const_spec((1, RW_WIDTH)),
                  _const_spec(w2.shape), _const_spec((1, RW_WIDTH)), _const_spec(a2.shape),
                  _const_spec(g2.shape), _const_spec((1, RW_WIDTH)), _const_spec((1, RW_WIDTH)),
                  _const_spec(ones_blk.shape)],
        out_specs=[row(RW_WIDTH)] * 7,
        out_shape=outs,
        scratch_shapes=[pltpu.VMEM((8, RW_IN), F32)],
        compiler_params=_params("parallel", "arbitrary"),
        name="rwkv_prep",
    )(p_rw, vec(mu), vec(w0), w2, vec(a0), a2, g2, vec(k_k), vec(k_a), ones_blk)


def _rw_scan_kernel(r_ref, k_ref, v_ref, ld_ref, a_ref, b_ref, g_ref, tri_ref, avg_ref, ones_ref,
                    rk_ref, lng_ref, lnb_ref, o_ref, state_ref, y_ref):
    c_ = RW_CHUNK
    n_ = RW_HEAD_DIM

    @pl.when(pl.program_id(1) == 0)
    def _():
        state_ref[...] = jnp.zeros_like(state_ref)

    ld = ld_ref[...]
    r = r_ref[...]
    k = k_ref[...]
    v = v_ref[...]
    cum = _dot_exact_lhs(tri_ref[...], ld)
    mid = cum[c_ // 2 - 1:c_ // 2, :]
    e_in = jnp.exp(cum - mid)
    e_out = jnp.exp(mid - cum)
    r_t = r * e_in
    a_t = a_ref[...] * jnp.exp(cum - ld - mid)
    b_t = b_ref[...] * e_out
    k_t = k * e_out
    e_mid = jnp.exp(mid)
    w_all = jnp.exp(cum[c_ - 1:c_, :])
    w_tail = jnp.exp(cum[c_ - 1:c_, :] - mid)

    row = lax.broadcasted_iota(jnp.int32, (c_, c_), 0)
    col = lax.broadcasted_iota(jnp.int32, (c_, c_), 1)
    strict = row > col
    incl = row >= col
    eye = (row == col).astype(F32)

    heads = range(RW_HEADS)
    sls = [slice(h * n_, (h + 1) * n_) for h in heads]
    s0 = [state_ref[h] for h in heads]
    ar = [jnp.concatenate([a_t[:, sl], r_t[:, sl]], axis=0) for sl in sls]
    bk = [jnp.concatenate([b_t[:, sl], k_t[:, sl]], axis=0) for sl in sls]
    m1 = [_dot_nt(ar[h], bk[h]) for h in heads]
    m2 = [_dot_nt(ar[h] * e_mid[:, sls[h]], s0[h]) for h in heads]
    l_ab = [jnp.where(strict, m[:c_, :c_], 0.0) for m in m1]
    l_ak = [jnp.where(strict, m[:c_, c_:], 0.0) for m in m1]
    l_rb = [jnp.where(incl, m[c_:, :c_], 0.0) for m in m1]
    l_rk = [jnp.where(incl, m[c_:, c_:], 0.0) for m in m1]
    vh = [v[:, sl] for sl in sls]
    rhs = [m2[h][:c_] + _dot(l_ak[h], vh[h]) for h in heads]
    inv = [eye + l for l in l_ab]
    pw = l_ab
    for _ in range(int(math.log2(c_)) - 1):
        pw = [_dot(p, p) for p in pw]
        inv = [i + _dot(i, p) for i, p in zip(inv, pw)]
    u = [_dot(inv[h], rhs[h]) for h in heads]
    ys = [m2[h][c_:] + _dot(l_rb[h], u[h]) + _dot(l_rk[h], vh[h]) for h in heads]
    upd = [_dot_tn(u[h], b_t[:, sls[h]]) + _dot_tn(vh[h], k_t[:, sls[h]]) for h in heads]
    for h in heads:
        y_ref[:, sls[h]] = ys[h]
        state_ref[h] = s0[h] * w_all[:, sls[h]] + upd[h] * w_tail[:, sls[h]]

    y = y_ref[...]
    mu = _dot_exact_rhs(y, avg_ref[...])
    d = y - mu
    var = _dot_exact_rhs(d * d, avg_ref[...])
    yn = d * lax.rsqrt(var + RW_LN_EPS) * lng_ref[...] + lnb_ref[...]
    bonus = _dot_exact_rhs(r * k * rk_ref[...], ones_ref[...]) * v
    o_ref[...] = (yn + bonus) * g_ref[...]


def _rw_scan(r, k, v, ld, a, b, g, b_, s_, r_k, ln_g, ln_b, ones_blk, avg_blk, tri):
    t = r.shape[0]
    nc = s_ // RW_CHUNK
    row = pl.BlockSpec((RW_CHUNK, RW_WIDTH), lambda bi, i: (bi * nc + i, 0))
    vec = lambda x: x.reshape(1, -1)
    return pl.pallas_call(
        _rw_scan_kernel,
        grid=(b_, nc),
        in_specs=[row] * 7 + [_const_spec(tri.shape), _const_spec(avg_blk.shape), _const_spec(ones_blk.shape),
                              _const_spec((1, RW_WIDTH)), _const_spec((1, RW_WIDTH)), _const_spec((1, RW_WIDTH))],
        out_specs=row,
        out_shape=jax.ShapeDtypeStruct((t, RW_WIDTH), F32),
        scratch_shapes=[pltpu.VMEM((RW_HEADS, RW_HEAD_DIM, RW_HEAD_DIM), F32),
                        pltpu.VMEM((RW_CHUNK, RW_WIDTH), F32)],
        compiler_params=_params("parallel", "arbitrary"),
        name="rwkv_scan",
    )(r, k, v, ld, a, b, g, tri, avg_blk, ones_blk, vec(r_k), vec(ln_g), vec(ln_b))


def _rt_log_gamma(h):
    return math.log1p(-(2.0 ** (-5.0 - h)))


def _retention_kernel(p_ref, cos_ref, sin_ref, gng_ref, gnb_ref, o_ref, state_ref):
    c_ = RT_CHUNK

    @pl.when(pl.program_id(1) == 0)
    def _():
        state_ref[...] = jnp.zeros_like(state_ref)

    cos = cos_ref[...]
    sin = sin_ref[...]
    row = lax.broadcasted_iota(jnp.int32, (c_, c_), 0).astype(F32)
    col = lax.broadcasted_iota(jnp.int32, (c_, c_), 1).astype(F32)
    rel = row - col
    for h in range(RT_HEADS):
        lg = _rt_log_gamma(h)
        sl = slice(h * RT_DIM, (h + 1) * RT_DIM)
        q = p_ref[:, sl]
        k = p_ref[:, RT_WIDTH + h * RT_DIM:RT_WIDTH + (h + 1) * RT_DIM]
        v = p_ref[:, 2 * RT_WIDTH + h * RT_DIM:2 * RT_WIDTH + (h + 1) * RT_DIM]
        gate = p_ref[:, 3 * RT_WIDTH + h * RT_DIM:3 * RT_WIDTH + (h + 1) * RT_DIM]
        q = q * cos + pltpu.roll(q, RT_DIM // 2, 1) * sin
        k = (k * cos + pltpu.roll(k, RT_DIM // 2, 1) * sin) * (RT_DIM ** -0.5)
        decay = jnp.where(rel >= 0, jnp.exp(jnp.maximum(rel, 0.0) * lg), 0.0)
        scores = _dot_nt(q, k) * decay
        s0 = state_ref[h]
        xi = jnp.exp((row + 1.0) * lg)
        o = _dot(scores, v) + _dot(q, s0) * xi
        zeta = jnp.exp((c_ - 1.0 - row) * lg)
        state_ref[h] = s0 * math.exp(c_ * lg) + _dot_tn(k * zeta, v)
        mu = jnp.mean(o, axis=-1, keepdims=True)
        d = o - mu
        var = jnp.mean(d * d, axis=-1, keepdims=True)
        on = d * lax.rsqrt(var + EPS) * gng_ref[:, sl] + gnb_ref[:, sl]
        o_ref[:, sl] = gate * jax.nn.sigmoid(gate) * on


def _retention(p_rt, b_, s_, cos2, sin2, gn_g, gn_b):
    t = p_rt.shape[0]
    nc = s_ // RT_CHUNK
    return pl.pallas_call(
        _retention_kernel,
        grid=(b_, nc),
        in_specs=[pl.BlockSpec((RT_CHUNK, RT_IN), lambda bi, i: (bi * nc + i, 0)),
                  pl.BlockSpec((RT_CHUNK, RT_DIM), lambda bi, i: (i, 0)),
                  pl.BlockSpec((RT_CHUNK, RT_DIM), lambda bi, i: (i, 0)),
                  _const_spec((1, RT_WIDTH)), _const_spec((1, RT_WIDTH))],
        out_specs=pl.BlockSpec((RT_CHUNK, RT_WIDTH), lambda bi, i: (bi * nc + i, 0)),
        out_shape=jax.ShapeDtypeStruct((t, RT_WIDTH), F32),
        scratch_shapes=[pltpu.VMEM((RT_HEADS, RT_DIM, RT_DIM), F32)],
        compiler_params=_params("parallel", "arbitrary"),
        name="retention",
    )(p_rt, cos2, sin2, gn_g.reshape(1, -1), gn_b.reshape(1, -1))


DSA_BLOCKS_PER_TILE = ROW_TILE // Q_BLOCK


def _dsa_prep_kernel(cq_ref, ckv_ref, kw_ref, qg_ref, kvg_ref, wuq_ref, wuk_ref, wqi_ref,
                     ckvn_ref, kidx_ref, qlat_ref, qidx_ref):
    cq = _rms(cq_ref[...], qg_ref[...]).astype(BF16)
    ckvn_ref[...] = _rms(ckv_ref[...], kvg_ref[...]).astype(BF16)
    kidx_ref[...] = kw_ref[:, :IDX_DIM].astype(BF16)
    q = jnp.dot(cq, wuq_ref[...], preferred_element_type=F32)
    qi = jnp.dot(cq, wqi_ref[...], preferred_element_type=F32)
    for h in range(DSA_HEADS):
        qh = q[:, h * DSA_HEAD_DIM:(h + 1) * DSA_HEAD_DIM]
        ql = (_dot(qh, wuk_ref[h]) * (DSA_HEAD_DIM ** -0.5)).astype(BF16)
        qih = qi[:, h * IDX_DIM:(h + 1) * IDX_DIM].astype(BF16)
        for blk in range(DSA_BLOCKS_PER_TILE):
            rows = slice(blk * Q_BLOCK, (blk + 1) * Q_BLOCK)
            qlat_ref[blk, h] = ql[rows]
            qidx_ref[blk, h] = qih[rows]


def _dsa_prep(c_q, c_kv, kw, q_g, kv_g, w_uq, w_uk, w_qi):
    t = c_q.shape[0]
    row = lambda n: pl.BlockSpec((ROW_TILE, n), lambda i: (i, 0))
    blk4 = lambda n: pl.BlockSpec((DSA_BLOCKS_PER_TILE, DSA_HEADS, Q_BLOCK, n), lambda i: (i, 0, 0, 0))
    return pl.pallas_call(
        _dsa_prep_kernel,
        grid=(t // ROW_TILE,),
        in_specs=[row(DSA_Q_RANK), row(DSA_KV_RANK), row(LANES), _const_spec((1, DSA_Q_RANK)),
                  _const_spec((1, DSA_KV_RANK)),
                  _const_spec(w_uq.shape), _const_spec(w_uk.shape), _const_spec(w_qi.shape)],
        out_specs=[row(DSA_KV_RANK), row(IDX_DIM), blk4(DSA_KV_RANK), blk4(IDX_DIM)],
        out_shape=[jax.ShapeDtypeStruct((t, DSA_KV_RANK), BF16),
                   jax.ShapeDtypeStruct((t, IDX_DIM), BF16),
                   jax.ShapeDtypeStruct((t // Q_BLOCK, DSA_HEADS, Q_BLOCK, DSA_KV_RANK), BF16),
                   jax.ShapeDtypeStruct((t // Q_BLOCK, IDX_HEADS, Q_BLOCK, IDX_DIM), BF16)],
        compiler_params=_params("parallel"),
        name="dsa_prep",
    )(c_q, c_kv, kw, q_g.reshape(1, -1), kv_g.reshape(1, -1), w_uq, w_uk, w_qi)


def _dsa_kernel(top_k, qidx_ref, kw_q_ref, qlat_ref, kidx_ref, ckv_ref, wuv_ref, tri_ref, o_ref,
                key_ref, wt_ref, s_ref, mx_ref, l_ref, acc_ref):
    qb = Q_BLOCK
    nh = DSA_HEADS
    j = pl.program_id(1)
    n_chunks = j + 1

    rowi = lax.broadcasted_iota(jnp.int32, (qb, qb), 0)
    coli = lax.broadcasted_iota(jnp.int32, (qb, qb), 1)
    rows = lambda h: slice(h * qb, (h + 1) * qb)

    def key_chunk(ref, c):
        return ref[pl.ds(pl.multiple_of(c * qb, qb), qb), :]

    w_idx = kw_q_ref[:, IDX_DIM:IDX_DIM + IDX_HEADS] * ((IDX_HEADS * IDX_DIM) ** -0.5)
    for h in range(IDX_HEADS):
        wt_ref[rows(h), :] = jnp.broadcast_to(w_idx[:, h:h + 1], (qb, qb))
    q_idx = qidx_ref[0].reshape(IDX_HEADS * qb, IDX_DIM)
    q_lat = qlat_ref[0].reshape(nh * qb, DSA_KV_RANK)

    def score_body(c, carry):
        logits = lax.dot_general(q_idx, key_chunk(kidx_ref, c), (((1,), (1,)), ((), ())),
                                 preferred_element_type=F32)
        weighted = jnp.maximum(logits, 0.0) * wt_ref[...]
        score = jnp.zeros((qb, qb), F32)
        for h in range(IDX_HEADS):
            score = score + weighted[rows(h)]
        bits = lax.bitcast_convert_type(score, jnp.int32)
        key = jnp.where(bits < 0, bits ^ jnp.int32(0x7FFFFFFF), bits)
        causal = coli + c * qb <= rowi + j * qb
        key_ref[c] = jnp.where(causal, key, jnp.int32(INT_MIN))
        return carry

    lax.fori_loop(0, n_chunks, score_body, 0)

    def count_ge(cand):
        def body(c, acc):
            return acc + jnp.where(key_ref[c] >= cand, 1, 0)
        acc = lax.fori_loop(0, n_chunks, body, jnp.zeros((qb, qb), jnp.int32))
        return jnp.sum(acc, axis=-1, keepdims=True)

    thr = jnp.where(count_ge(jnp.zeros((qb, 1), jnp.int32)) >= top_k, 0, INT_MIN).astype(jnp.int32)

    def bit_body(i, thr):
        cand = thr | lax.shift_left(jnp.int32(1), 30 - i)
        return jnp.where(count_ge(cand) >= top_k, cand, thr)

    thr = lax.fori_loop(0, 31, bit_body, thr)

    def count_gt_body(c, acc):
        return acc + jnp.where(key_ref[c] > thr, 1, 0)

    n_gt = jnp.sum(lax.fori_loop(0, n_chunks, count_gt_body, jnp.zeros((qb, qb), jnp.int32)),
                   axis=-1, keepdims=True)
    need = (top_k - n_gt).astype(F32)
    thr_b = jnp.broadcast_to(thr, (qb, qb))
    need_b = jnp.broadcast_to(need, (qb, qb))

    mx_ref[...] = jnp.full(mx_ref.shape, NEG_BIG, F32)

    def logit_body(c, taken):
        key = key_ref[c]
        causal = coli + c * qb <= rowi + j * qb
        eq = jnp.logical_and(key == thr_b, causal)
        eq_f = jnp.where(eq, 1.0, 0.0)
        rank = taken + jnp.dot(eq_f.astype(BF16), tri_ref[...], preferred_element_type=F32)
        sel = jnp.logical_or(jnp.logical_and(key > thr_b, causal), jnp.logical_and(eq, rank <= need_b))
        s = lax.dot_general(q_lat, key_chunk(ckv_ref, c), (((1,), (1,)), ((), ())),
                            preferred_element_type=F32)
        for h in range(nh):
            sh = jnp.where(sel, s[rows(h)], NEG_BIG)
            s_ref[c, rows(h), :] = sh
            mx_ref[rows(h), :] = jnp.maximum(mx_ref[rows(h), :], sh)
        return taken + jnp.sum(eq_f, axis=-1, keepdims=True)

    lax.fori_loop(0, n_chunks, logit_body, jnp.zeros((qb, 1), F32))

    mx_ref[...] = jnp.broadcast_to(jnp.max(mx_ref[...], axis=-1, keepdims=True), mx_ref.shape)
    l_ref[...] = jnp.zeros_like(l_ref)
    acc_ref[...] = jnp.zeros_like(acc_ref)

    def value_body(c, carry):
        p = jnp.exp(s_ref[c] - mx_ref[...])
        l_ref[...] += p
        acc_ref[...] += jnp.dot(p.astype(BF16), key_chunk(ckv_ref, c), preferred_element_type=F32)
        return carry

    lax.fori_loop(0, n_chunks, value_body, 0)

    o_lat = acc_ref[...] / jnp.sum(l_ref[...], axis=-1, keepdims=True)
    for h in range(nh):
        o_ref[:, h * DSA_HEAD_DIM:(h + 1) * DSA_HEAD_DIM] = _dot(o_lat[rows(h)], wuv_ref[h])


def _dsa(q_idx, kw, q_lat, k_idx, ckv_n, w_uv, tri, b_, s_):
    t = kw.shape[0]
    nb = s_ // Q_BLOCK
    top_k = min(TOPK_MAX, s_ // 4)
    blk = lambda n: pl.BlockSpec((Q_BLOCK, n), lambda bi, i: (bi * nb + i, 0))
    blk4 = lambda n: pl.BlockSpec((1, DSA_HEADS, Q_BLOCK, n), lambda bi, i: (bi * nb + i, 0, 0, 0))
    seq = lambda n: pl.BlockSpec((s_, n), lambda bi, i: (bi, 0))
    stacked = DSA_HEADS * Q_BLOCK
    return pl.pallas_call(
        functools.partial(_dsa_kernel, top_k),
        grid=(b_, nb),
        in_specs=[blk4(IDX_DIM), blk(LANES), blk4(DSA_KV_RANK),
                  seq(IDX_DIM), seq(DSA_KV_RANK), _const_spec(w_uv.shape), _const_spec(tri.shape)],
        out_specs=blk(DSA_WIDTH),
        out_shape=jax.ShapeDtypeStruct((t, DSA_WIDTH), F32),
        scratch_shapes=[pltpu.VMEM((nb, Q_BLOCK, Q_BLOCK), jnp.int32),
                        pltpu.VMEM((stacked, Q_BLOCK), F32),
                        pltpu.VMEM((nb, stacked, Q_BLOCK), F32),
                        pltpu.VMEM((stacked, Q_BLOCK), F32),
                        pltpu.VMEM((stacked, Q_BLOCK), F32),
                        pltpu.VMEM((stacked, DSA_KV_RANK), F32)],
        compiler_params=_params("parallel", "arbitrary"),
        name="dsa_attn",
    )(q_idx, kw, q_lat, k_idx, ckv_n, w_uv, tri)


SC_TILE = 512


def _sconv_kernel(p_ref, w_ref, b_ref, o_ref, carry_ref):
    @pl.when(pl.program_id(1) == 0)
    def _():
        carry_ref[...] = jnp.zeros_like(carry_ref)

    h = p_ref[:, :SC_WIDTH]
    gate_b = p_ref[:, SC_WIDTH:2 * SC_WIDTH]
    gate_c = p_ref[:, 2 * SC_WIDTH:]
    u = gate_c * h
    carry = carry_ref[...]
    y = u * w_ref[2:3, :] + _shift_rows(u, carry, 1) * w_ref[1:2, :] + _shift_rows(u, carry, 2) * w_ref[0:1, :]
    carry_ref[...] = u[SC_TILE - 8:, :]
    o_ref[...] = gate_b * (y + b_ref[...])


def _sconv(p_sc, b_, s_, conv_w, conv_b):
    t = p_sc.shape[0]
    nt = s_ // SC_TILE
    return pl.pallas_call(
        _sconv_kernel,
        grid=(b_, nt),
        in_specs=[pl.BlockSpec((SC_TILE, 3 * SC_WIDTH), lambda bi, i: (bi * nt + i, 0)),
                  _const_spec((8, SC_WIDTH)), _const_spec((1, SC_WIDTH))],
        out_specs=pl.BlockSpec((SC_TILE, SC_WIDTH), lambda bi, i: (bi * nt + i, 0)),
        out_shape=jax.ShapeDtypeStruct((t, SC_WIDTH), F32),
        scratch_shapes=[pltpu.VMEM((8, SC_WIDTH), F32)],
        compiler_params=_params("parallel", "arbitrary"),
        name="short_conv",
    )(p_sc, jnp.pad(conv_w, ((0, 8 - SC_KERNEL), (0, 0))), conv_b.reshape(1, -1))


def _xattn_kernel(x_ref, gq_ref, wq_ref, k_ref, v_ref, wo_ref, go_ref, o_ref, att_ref):
    x = x_ref[...]
    q = jnp.dot(_rms(x, gq_ref[...]).astype(BF16), wq_ref[...], preferred_element_type=F32)
    for h in range(XA_HEADS):
        sl = slice(h * XA_HEAD_DIM, (h + 1) * XA_HEAD_DIM)
        s = _dot_nt(q[:, sl], k_ref[:, sl]) * (XA_HEAD_DIM ** -0.5)
        s = s - jnp.max(s, axis=-1, keepdims=True)
        p = jnp.exp(s)
        p = p / jnp.sum(p, axis=-1, keepdims=True)
        att_ref[:, sl] = _dot(p, v_ref[:, sl])
    hout = jnp.dot(att_ref[...].astype(BF16), wo_ref[...], preferred_element_type=F32)
    o_ref[...] = x + _rms(hout, go_ref[...])


def _xattn(x, g_q, wq, k_mem, v_mem, wo, g_o, b_, s_):
    t = x.shape[0]
    nt = s_ // ROW_TILE
    row = pl.BlockSpec((ROW_TILE, D_MODEL), lambda bi, i: (bi * nt + i, 0))
    mem = pl.BlockSpec((MEM_LEN, XA_WIDTH), lambda bi, i: (bi, 0))
    return pl.pallas_call(
        _xattn_kernel,
        grid=(b_, nt),
        in_specs=[row, _const_spec((1, D_MODEL)), _const_spec(wq.shape), mem, mem,
                  _const_spec(wo.shape), _const_spec((1, D_MODEL))],
        out_specs=row,
        out_shape=jax.ShapeDtypeStruct((t, D_MODEL), F32),
        scratch_shapes=[pltpu.VMEM((ROW_TILE, XA_WIDTH), F32)],
        compiler_params=_params("parallel", "parallel"),
        name="mem_xattn",
    )(x, g_q.reshape(1, -1), wq, k_mem, v_mem, wo, g_o.reshape(1, -1))


def _block_diag(n_blocks, size, value):
    return np.kron(np.eye(n_blocks, dtype=np.float32), np.full((size, size), value, np.float32))


def _rope_tables(s_):
    half = RT_DIM // 2
    inv_freq = RT_ROPE_BASE ** (-jnp.arange(half, dtype=F32) / half)
    ang = jnp.arange(s_).astype(F32)[:, None] * inv_freq[None, :]
    cos, sin = jnp.cos(ang), jnp.sin(ang)
    return jnp.concatenate([cos, cos], axis=-1), jnp.concatenate([-sin, sin], axis=-1)


def kernel(x, mem, norm_g, mem_norm_g, ffn_w_gate, ffn_w_up, ffn_w_down, xa_wq, xa_wk, xa_wv, xa_wo, ev_w_in, ev_w_out, rw_mu, rw_w0, rw_w2, rw_a0, rw_a2, rw_g2, rw_k_k, rw_k_a, rw_r_k, rw_ln_g, rw_ln_b, rt_gn_g, rt_gn_b, od_w_in, od_w_out, dsa_q_norm_g, dsa_kv_norm_g, dsa_w_uq, dsa_w_uk, dsa_w_uv, dsa_w_qi, sc_conv_w, sc_conv_b):
    b_, s_, d_ = x.shape
    depth = norm_g.shape[0]
    t = b_ * s_
    bf = lambda w: w.astype(BF16)

    ones_blk = jnp.asarray(_block_diag(RW_HEADS, RW_HEAD_DIM, 1.0), BF16)
    avg_blk = jnp.asarray(_block_diag(RW_HEADS, RW_HEAD_DIM, 1.0 / RW_HEAD_DIM), BF16)
    tri_rw = jnp.asarray(np.tril(np.ones((RW_CHUNK, RW_CHUNK), np.float32)), BF16)
    tri_dsa = jnp.asarray(np.triu(np.ones((Q_BLOCK, Q_BLOCK), np.float32)), BF16)
    cos2, sin2 = _rope_tables(s_)

    xf = x.reshape(t, d_)
    mem_f = mem.reshape(b_ * MEM_LEN, d_)
    for l in range(depth):
        ng = norm_g[l]
        i = l // 2
        xf = _ffn(xf, ng[0], bf(ffn_w_gate[l, 0]), bf(ffn_w_up[l, 0]), bf(ffn_w_down[l, 0]), ng[1])
        if l % 2 == 0:
            w_in = bf(ev_w_in[i])
            p_rw, p_rt = _norm_proj(xf, ng[2], [w_in[:, :RW_IN], w_in[:, RW_IN:]], "ev_in_proj")
            r, k, v, ld, a, b, g = _rw_prep(p_rw, b_, s_, rw_mu[i], rw_w0[i], rw_w2[i], rw_a0[i], rw_a2[i],
                                            rw_g2[i], rw_k_k[i], rw_k_a[i], ones_blk)
            y_a = _rw_scan(r, k, v, ld, a, b, g, b_, s_, rw_r_k[i], rw_ln_g[i], rw_ln_b[i],
                           ones_blk, avg_blk, tri_rw)
            y_b = _retention(p_rt, b_, s_, cos2, sin2, rt_gn_g[i], rt_gn_b[i])
            w_out = bf(ev_w_out[i])
            xf = _proj_res([y_a, y_b], [w_out[:RW_WIDTH], w_out[RW_WIDTH:]], ng[3], xf, "ev_out_proj")
        else:
            w_in = od_w_in[i]
            kw_w = jnp.pad(w_in[:, DSA_Q_RANK + DSA_KV_RANK:DSA_IN], ((0, 0), (0, LANES - IDX_DIM - IDX_HEADS)))
            c_q, c_kv, kw, p_sc = _norm_proj(
                xf, ng[2], [bf(w_in[:, :DSA_Q_RANK]), bf(w_in[:, DSA_Q_RANK:DSA_Q_RANK + DSA_KV_RANK]),
                            bf(kw_w), bf(w_in[:, DSA_IN:])], "od_in_proj")
            ckv_n, k_idx, q_lat, q_idx = _dsa_prep(
                c_q, c_kv, kw, dsa_q_norm_g[i], dsa_kv_norm_g[i],
                bf(dsa_w_uq[i].reshape(DSA_Q_RANK, DSA_WIDTH)), bf(dsa_w_uk[i]),
                bf(dsa_w_qi[i].reshape(DSA_Q_RANK, IDX_HEADS * IDX_DIM)))
            y_c = _dsa(q_idx, kw, q_lat, k_idx, ckv_n, bf(dsa_w_uv[i]), tri_dsa, b_, s_)
            y_d = _sconv(p_sc, b_, s_, sc_conv_w[i], sc_conv_b[i])
            w_out = bf(od_w_out[i])
            xf = _proj_res([y_c, y_d], [w_out[:DSA_WIDTH], w_out[DSA_WIDTH:]], ng[3], xf, "od_out_proj")
        k_mem, v_mem = _norm_proj(mem_f, mem_norm_g, [bf(xa_wk[l]), bf(xa_wv[l])], "mem_kv_proj")
        xf = _xattn(xf, ng[4], bf(xa_wq[l]), k_mem, v_mem, bf(xa_wo[l]), ng[5], b_, s_)
        xf = _ffn(xf, ng[6], bf(ffn_w_gate[l, 1]), bf(ffn_w_up[l, 1]), bf(ffn_w_down[l, 1]), ng[7])
    return xf.reshape(b_, s_, d_)
```

```python
import functools
import math

import numpy as np
import jax
import jax.numpy as jnp
from jax import lax
from jax.experimental import pallas as pl
from jax.experimental.pallas import tpu as pltpu

F32 = jnp.float32
BF16 = jnp.bfloat16

D_MODEL = 1024
D_FF = 2816
EPS = 1e-6
MEM_LEN = 256
RW_HEADS = 8
RW_HEAD_DIM = 64
RW_WIDTH = RW_HEADS * RW_HEAD_DIM
RW_DECAY_RANK = 64
RW_AAA_RANK = 64
RW_GATE_RANK = 128
RW_LN_EPS = 64e-5
RW_IN = 3 * RW_WIDTH + RW_DECAY_RANK + RW_AAA_RANK + RW_GATE_RANK
RW_CHUNK = 64
RT_HEADS = 4
RT_DIM = 128
RT_WIDTH = RT_HEADS * RT_DIM
RT_CHUNK = 128
RT_ROPE_BASE = 10000.0
RT_IN = 4 * RT_WIDTH
DSA_HEADS = 8
DSA_HEAD_DIM = 64
DSA_WIDTH = DSA_HEADS * DSA_HEAD_DIM
DSA_Q_RANK = 256
DSA_KV_RANK = 128
IDX_HEADS = 8
IDX_DIM = 64
TOPK_MAX = 256
Q_BLOCK = 128
DSA_IN = DSA_Q_RANK + DSA_KV_RANK + IDX_DIM + IDX_HEADS
SC_WIDTH = 512
SC_KERNEL = 3
XA_HEADS = 4
XA_HEAD_DIM = 128
XA_WIDTH = XA_HEADS * XA_HEAD_DIM

LANES = 128
ROW_TILE = 512
VMEM_LIMIT = 56 * 1024 * 1024
INT_MIN = -2 ** 31
NEG_BIG = -1e30


def _params(*sem):
    return pltpu.CompilerParams(dimension_semantics=sem, vmem_limit_bytes=VMEM_LIMIT)


def _rms(x, g):
    return x * lax.rsqrt(jnp.mean(x * x, axis=-1, keepdims=True) + EPS) * g


def _dot(a, b):
    return jnp.dot(a.astype(BF16), b.astype(BF16), preferred_element_type=F32)


def _dot_nt(a, b):
    return lax.dot_general(a.astype(BF16), b.astype(BF16), (((1,), (1,)), ((), ())),
                           preferred_element_type=F32)


def _dot_tn(a, b):
    return lax.dot_general(a.astype(BF16), b.astype(BF16), (((0,), (0,)), ((), ())),
                           preferred_element_type=F32)


def _split3(x):
    hi = x.astype(BF16)
    r1 = x - hi.astype(F32)
    mid = r1.astype(BF16)
    lo = (r1 - mid.astype(F32)).astype(BF16)
    return hi, mid, lo


def _dot_exact_rhs(x, w_bf16):
    hi, mid, lo = _split3(x)
    out = jnp.dot(hi, w_bf16, preferred_element_type=F32)
    out += jnp.dot(mid, w_bf16, preferred_element_type=F32)
    out += jnp.dot(lo, w_bf16, preferred_element_type=F32)
    return out


def _dot_exact_lhs(w_bf16, x):
    hi, mid, lo = _split3(x)
    out = jnp.dot(w_bf16, hi, preferred_element_type=F32)
    out += jnp.dot(w_bf16, mid, preferred_element_type=F32)
    out += jnp.dot(w_bf16, lo, preferred_element_type=F32)
    return out


def _const_spec(shape):
    nd = len(shape)
    return pl.BlockSpec(shape, lambda *_: (0,) * nd, pipeline_mode=pl.Buffered(1))


FF_CHUNK = 1408


def _ffn_kernel(x_ref, gin_ref, wg_ref, wu_ref, wd_ref, gout_ref, o_ref, acc_ref):
    x = x_ref[...]
    xb = _rms(x, gin_ref[...]).astype(BF16)
    for c in range(D_FF // FF_CHUNK):
        sl = slice(c * FF_CHUNK, (c + 1) * FF_CHUNK)
        g = jnp.dot(xb, wg_ref[:, sl], preferred_element_type=F32)
        u = jnp.dot(xb, wu_ref[:, sl], preferred_element_type=F32)
        h = (g * jax.nn.sigmoid(g) * u).astype(BF16)
        part = jnp.dot(h, wd_ref[sl, :], preferred_element_type=F32)
        if c == 0:
            acc_ref[...] = part
        else:
            acc_ref[...] += part
    o_ref[...] = x + 0.5 * _rms(acc_ref[...], gout_ref[...])


def _ffn(x, g_in, wg, wu, wd, g_out):
    t = x.shape[0]
    row = pl.BlockSpec((ROW_TILE, D_MODEL), lambda i: (i, 0))
    return pl.pallas_call(
        _ffn_kernel,
        grid=(t // ROW_TILE,),
        in_specs=[row, _const_spec((1, D_MODEL)), _const_spec((D_MODEL, D_FF)),
                  _const_spec((D_MODEL, D_FF)), _const_spec((D_FF, D_MODEL)), _const_spec((1, D_MODEL))],
        out_specs=row,
        out_shape=jax.ShapeDtypeStruct((t, D_MODEL), F32),
        scratch_shapes=[pltpu.VMEM((ROW_TILE, D_MODEL), F32)],
        compiler_params=_params("parallel"),
        name="ffn_half",
    )(x, g_in.reshape(1, -1), wg, wu, wd, g_out.reshape(1, -1))


PROJ_CHUNK = 512


def _norm_proj_kernel(n_out, x_ref, g_ref, *refs):
    xb = _rms(x_ref[...], g_ref[...]).astype(BF16)
    for w_ref, o_ref in zip(refs[:n_out], refs[n_out:]):
        n = w_ref.shape[1]
        for c in range(0, n, PROJ_CHUNK):
            sl = slice(c, min(c + PROJ_CHUNK, n))
            o_ref[:, sl] = jnp.dot(xb, w_ref[:, sl], preferred_element_type=F32)


def _norm_proj(x, g, ws, name):
    t, d = x.shape
    row = lambda n: pl.BlockSpec((ROW_TILE, n), lambda i: (i, 0))
    return pl.pallas_call(
        functools.partial(_norm_proj_kernel, len(ws)),
        grid=(t // ROW_TILE,),
        in_specs=[row(d), _const_spec((1, d))] + [_const_spec(w.shape) for w in ws],
        out_specs=[row(w.shape[1]) for w in ws],
        out_shape=[jax.ShapeDtypeStruct((t, w.shape[1]), F32) for w in ws],
        compiler_params=_params("parallel"),
        name=name,
    )(x, g.reshape(1, -1), *ws)


def _proj_res_kernel(n_in, *refs):
    y_refs, w_refs = refs[:n_in], refs[n_in:2 * n_in]
    g_ref, x_ref, o_ref = refs[2 * n_in:]
    h = None
    for y_ref, w_ref in zip(y_refs, w_refs):
        part = jnp.dot(y_ref[...].astype(BF16), w_ref[...], preferred_element_type=F32)
        h = part if h is None else h + part
    o_ref[...] = x_ref[...] + _rms(h, g_ref[...])


def _proj_res(ys, ws, g, x, name):
    t, d = x.shape
    row = lambda n: pl.BlockSpec((ROW_TILE, n), lambda i: (i, 0))
    return pl.pallas_call(
        functools.partial(_proj_res_kernel, len(ys)),
        grid=(t // ROW_TILE,),
        in_specs=[row(y.shape[1]) for y in ys] + [_const_spec(w.shape) for w in ws]
                 + [_const_spec((1, d)), row(d)],
        out_specs=row(d),
        out_shape=jax.ShapeDtypeStruct((t, d), F32),
        compiler_params=_params("parallel"),
        name=name,
    )(*ys, *ws, g.reshape(1, -1), x)


RW_TILE = 512


def _shift_rows(x, carry, n):
    rolled = pltpu.roll(x, n, 0)
    row = lax.broadcasted_iota(jnp.int32, x.shape, 0)
    out = rolled
    for i in range(n):
        out = jnp.where(row == i, carry[8 - n + i:8 - n + i + 1, :], out)
    return out


def _rw_prep_kernel(p_ref, mu_ref, w0_ref, w2_ref, a0_ref, a2_ref, g2_ref, kk_ref, ka_ref, ones_ref,
                    r_ref, k_ref, v_ref, ld_ref, a_ref, b_ref, g_ref, carry_ref):
    @pl.when(pl.program_id(1) == 0)
    def _():
        carry_ref[...] = jnp.zeros_like(carry_ref)

    p = p_ref[...]
    prev = _shift_rows(p, carry_ref[...], 1)
    carry_ref[...] = p[RW_TILE - 8:, :]
    xm = p + (prev - p) * mu_ref[...]
    w = RW_WIDTH
    r, k, v = xm[:, :w], xm[:, w:2 * w], xm[:, 2 * w:3 * w]
    xw = xm[:, 3 * w:3 * w + RW_DECAY_RANK]
    xa = xm[:, 3 * w + RW_DECAY_RANK:3 * w + RW_DECAY_RANK + RW_AAA_RANK]
    xg = xm[:, 3 * w + RW_DECAY_RANK + RW_AAA_RANK:]
    wlog = -jax.nn.softplus(-(w0_ref[...] + _dot(jnp.tanh(xw), w2_ref[...]))) - 0.5
    a = jax.nn.sigmoid(a0_ref[...] + _dot(xa, a2_ref[...]))
    kk = k * kk_ref[...]
    ss = _dot_exact_rhs(kk * kk, ones_ref[...])
    kk = kk / jnp.maximum(jnp.sqrt(ss), 1e-12)
    r_ref[...] = r
    k_ref[...] = k * (1.0 + (a - 1.0) * ka_ref[...])
    v_ref[...] = v
    ld_ref[...] = -jnp.exp(wlog)
    a_ref[...] = -kk
    b_ref[...] = kk * a
    g_ref[...] = _dot(jax.nn.sigmoid(xg), g2_ref[...])


def _rw_prep(p_rw, b_, s_, mu, w0, w2, a0, a2, g2, k_k, k_a, ones_blk):
    t = p_rw.shape[0]
    nt = s_ // RW_TILE
    row = lambda n: pl.BlockSpec((RW_TILE, n), lambda b, i: (b * nt + i, 0))
    vec = lambda a: a.reshape(1, -1)
    outs = [jax.ShapeDtypeStruct((t, RW_WIDTH), F32)] * 7
    return pl.pallas_call(
        _rw_prep_kernel,
        grid=(b_, nt),
        in_specs=[row(RW_IN), _const_spec((1, RW_IN)), _const_spec((1, RW_WIDTH)),
                  _const_spec(w2.shape), _const_spec((1, RW_WIDTH)), _const_spec(a2.shape),
                  _const_spec(g2.shape), _const_spec((1, RW_WIDTH)), _const_spec((1, RW_WIDTH)),
                  _const_spec(ones_blk.shape)],
        out_specs=[row(RW_WIDTH)] * 7,
        out_shape=outs,
        scratch_shapes=[pltpu.VMEM((8, RW_IN), F32)],
        compiler_params=_params("parallel", "arbitrary"),
        name="rwkv_prep",
    )(p_rw, vec(mu), vec(w0), w2, vec(a0), a2, g2, vec(k_k), vec(k_a), ones_blk)


def _rw_scan_kernel(r_ref, k_ref, v_ref, ld_ref, a_ref, b_ref, g_ref, tri_ref, ones_ref,
                    rk_ref, lng_ref, lnb_ref, o_ref, state_ref, y_ref):
    c_ = RW_CHUNK
    n_ = RW_HEAD_DIM

    @pl.when(pl.program_id(1) == 0)
    def _():
        state_ref[...] = jnp.zeros_like(state_ref)

    ld = ld_ref[...]
    r = r_ref[...]
    k = k_ref[...]
    v = v_ref[...]
    cum = _dot_exact_lhs(tri_ref[...], ld)
    mid = cum[c_ // 2 - 1:c_ // 2, :]
    e_in = jnp.exp(cum - mid)
    e_out = jnp.exp(mid - cum)
    r_t = r * e_in
    a_t = a_ref[...] * jnp.exp(cum - ld - mid)
    b_t = b_ref[...] * e_out
    k_t = k * e_out
    e_mid = jnp.exp(mid)
    w_all = jnp.exp(cum[c_ - 1:c_, :])
    w_tail = jnp.exp(cum[c_ - 1:c_, :] - mid)

    c2 = 2 * c_
    row = lax.broadcasted_iota(jnp.int32, (c2, c2), 0)
    col = lax.broadcasted_iota(jnp.int32, (c2, c2), 1)
    same_head = (row >= c_) == (col >= c_)
    strict = jnp.logical_and(same_head, row > col)
    incl = jnp.logical_and(same_head, row >= col)
    left = lax.broadcasted_iota(jnp.int32, (c_, LANES), 1) < n_

    def block_diag(x):
        return jnp.concatenate([jnp.where(left, x, 0.0), jnp.where(left, 0.0, x)], axis=0)

    pairs = range(RW_HEADS // 2)
    sls = [slice(p * LANES, (p + 1) * LANES) for p in pairs]
    s0 = [state_ref[p] for p in pairs]
    ar = [jnp.concatenate([block_diag(a_t[:, sl]), block_diag(r_t[:, sl])], axis=0) for sl in sls]
    bk = [jnp.concatenate([block_diag(b_t[:, sl]), block_diag(k_t[:, sl])], axis=0) for sl in sls]
    vb = [block_diag(v[:, sl]) for sl in sls]
    m1 = [_dot_nt(ar[p], bk[p]) for p in pairs]
    m2 = [_dot_nt(ar[p] * e_mid[:, sls[p]], s0[p]) for p in pairs]
    l_ab = [jnp.where(strict, m[:c2, :c2], 0.0) for m in m1]
    l_ak = [jnp.where(strict, m[:c2, c2:], 0.0) for m in m1]
    l_r = [jnp.concatenate([jnp.where(incl, m[c2:, :c2], 0.0), jnp.where(incl, m[c2:, c2:], 0.0)], axis=1)
           for m in m1]
    u = [m2[p][:c2] + _dot(l_ak[p], vb[p]) for p in pairs]
    pw = l_ab
    n_steps = int(math.log2(c_))
    for step in range(n_steps):
        if step < n_steps - 1:
            prod = [_dot(pw[p], jnp.concatenate([pw[p], u[p]], axis=1)) for p in pairs]
            pw = [q[:, :c2] for q in prod]
            u = [u[p] + prod[p][:, c2:] for p in pairs]
        else:
            u = [u[p] + _dot(pw[p], u[p]) for p in pairs]
    uv = [jnp.concatenate([u[p], vb[p]], axis=0) for p in pairs]
    ys = [m2[p][c2:] + _dot(l_r[p], uv[p]) for p in pairs]
    upd = [_dot_tn(uv[p], bk[p]) for p in pairs]
    for p in pairs:
        y_ref[:, sls[p]] = ys[p][:c_] + ys[p][c_:]
        state_ref[p] = s0[p] * w_all[:, sls[p]] + upd[p] * w_tail[:, sls[p]]

    def head_sums(xs, blk):
        parts = [part[:, sl] for x in xs for part in _split3(x) for sl in sls]
        prod = jnp.dot(jnp.concatenate(parts, axis=0), blk, preferred_element_type=F32)
        outs = []
        for i in range(len(xs)):
            slabs = []
            for p in pairs:
                rows = [((3 * i + j) * len(sls) + p) * c_ for j in range(3)]
                slabs.append(prod[rows[0]:rows[0] + c_] + prod[rows[1]:rows[1] + c_] + prod[rows[2]:rows[2] + c_])
            outs.append(jnp.concatenate(slabs, axis=1))
        return outs

    y = y_ref[...]
    ones2 = ones_ref[...]
    sum_y, sum_rk = head_sums([y, r * k * rk_ref[...]], ones2)
    d = y - sum_y * (1.0 / n_)
    var = head_sums([d * d], ones2)[0] * (1.0 / n_)
    yn = d * lax.rsqrt(var + RW_LN_EPS) * lng_ref[...] + lnb_ref[...]
    o_ref[...] = (yn + sum_rk * v) * g_ref[...]


def _rw_scan(r, k, v, ld, a, b, g, b_, s_, r_k, ln_g, ln_b, ones_pair, tri):
    t = r.shape[0]
    nc = s_ // RW_CHUNK
    row = pl.BlockSpec((RW_CHUNK, RW_WIDTH), lambda bi, i: (bi * nc + i, 0))
    vec = lambda x: x.reshape(1, -1)
    return pl.pallas_call(
        _rw_scan_kernel,
        grid=(b_, nc),
        in_specs=[row] * 7 + [_const_spec(tri.shape), _const_spec(ones_pair.shape),
                              _const_spec((1, RW_WIDTH)), _const_spec((1, RW_WIDTH)), _const_spec((1, RW_WIDTH))],
        out_specs=row,
        out_shape=jax.ShapeDtypeStruct((t, RW_WIDTH), F32),
        scratch_shapes=[pltpu.VMEM((RW_HEADS // 2, 2 * RW_HEAD_DIM, 2 * RW_HEAD_DIM), F32),
                        pltpu.VMEM((RW_CHUNK, RW_WIDTH), F32)],
        compiler_params=_params("parallel", "arbitrary"),
        name="rwkv_scan",
    )(r, k, v, ld, a, b, g, tri, ones_pair, vec(r_k), vec(ln_g), vec(ln_b))


def _rt_log_gamma(h):
    return math.log1p(-(2.0 ** (-5.0 - h)))


def _retention_kernel(p_ref, cos_ref, sin_ref, gng_ref, gnb_ref, o_ref, state_ref):
    c_ = RT_CHUNK

    @pl.when(pl.program_id(1) == 0)
    def _():
        state_ref[...] = jnp.zeros_like(state_ref)

    cos = cos_ref[...]
    sin = sin_ref[...]
    row = lax.broadcasted_iota(jnp.int32, (c_, c_), 0).astype(F32)
    col = lax.broadcasted_iota(jnp.int32, (c_, c_), 1).astype(F32)
    rel = row - col
    for h in range(RT_HEADS):
        lg = _rt_log_gamma(h)
        sl = slice(h * RT_DIM, (h + 1) * RT_DIM)
        q = p_ref[:, sl]
        k = p_ref[:, RT_WIDTH + h * RT_DIM:RT_WIDTH + (h + 1) * RT_DIM]
        v = p_ref[:, 2 * RT_WIDTH + h * RT_DIM:2 * RT_WIDTH + (h + 1) * RT_DIM]
        gate = p_ref[:, 3 * RT_WIDTH + h * RT_DIM:3 * RT_WIDTH + (h + 1) * RT_DIM]
        q = q * cos + pltpu.roll(q, RT_DIM // 2, 1) * sin
        k = (k * cos + pltpu.roll(k, RT_DIM // 2, 1) * sin) * (RT_DIM ** -0.5)
        decay = jnp.where(rel >= 0, jnp.exp(jnp.maximum(rel, 0.0) * lg), 0.0)
        scores = _dot_nt(q, k) * decay
        s0 = state_ref[h]
        xi = jnp.exp((row + 1.0) * lg)
        o = _dot(scores, v) + _dot(q, s0) * xi
        zeta = jnp.exp((c_ - 1.0 - row) * lg)
        state_ref[h] = s0 * math.exp(c_ * lg) + _dot_tn(k * zeta, v)
        mu = jnp.mean(o, axis=-1, keepdims=True)
        d = o - mu
        var = jnp.mean(d * d, axis=-1, keepdims=True)
        on = d * lax.rsqrt(var + EPS) * gng_ref[:, sl] + gnb_ref[:, sl]
        o_ref[:, sl] = gate * jax.nn.sigmoid(gate) * on


def _retention(p_rt, b_, s_, cos2, sin2, gn_g, gn_b):
    t = p_rt.shape[0]
    nc = s_ // RT_CHUNK
    return pl.pallas_call(
        _retention_kernel,
        grid=(b_, nc),
        in_specs=[pl.BlockSpec((RT_CHUNK, RT_IN), lambda bi, i: (bi * nc + i, 0)),
                  pl.BlockSpec((RT_CHUNK, RT_DIM), lambda bi, i: (i, 0)),
                  pl.BlockSpec((RT_CHUNK, RT_DIM), lambda bi, i: (i, 0)),
                  _const_spec((1, RT_WIDTH)), _const_spec((1, RT_WIDTH))],
        out_specs=pl.BlockSpec((RT_CHUNK, RT_WIDTH), lambda bi, i: (bi * nc + i, 0)),
        out_shape=jax.ShapeDtypeStruct((t, RT_WIDTH), F32),
        scratch_shapes=[pltpu.VMEM((RT_HEADS, RT_DIM, RT_DIM), F32)],
        compiler_params=_params("parallel", "arbitrary"),
        name="retention",
    )(p_rt, cos2, sin2, gn_g.reshape(1, -1), gn_b.reshape(1, -1))


DSA_BLOCKS_PER_TILE = ROW_TILE // Q_BLOCK


def _dsa_prep_kernel(cq_ref, ckv_ref, kw_ref, qg_ref, kvg_ref, wuq_ref, wuk_ref, wqi_ref,
                     ckvn_ref, kidx_ref, qlat_ref, qidx_ref):
    cq = _rms(cq_ref[...], qg_ref[...]).astype(BF16)
    ckvn_ref[...] = _rms(ckv_ref[...], kvg_ref[...]).astype(BF16)
    kidx_ref[...] = kw_ref[:, :IDX_DIM].astype(BF16)
    q = jnp.dot(cq, wuq_ref[...], preferred_element_type=F32)
    qi = jnp.dot(cq, wqi_ref[...], preferred_element_type=F32)
    for h in range(DSA_HEADS):
        qh = q[:, h * DSA_HEAD_DIM:(h + 1) * DSA_HEAD_DIM]
        ql = (_dot(qh, wuk_ref[h]) * (DSA_HEAD_DIM ** -0.5)).astype(BF16)
        qih = qi[:, h * IDX_DIM:(h + 1) * IDX_DIM].astype(BF16)
        for blk in range(DSA_BLOCKS_PER_TILE):
            rows = slice(blk * Q_BLOCK, (blk + 1) * Q_BLOCK)
            qlat_ref[blk, h] = ql[rows]
            qidx_ref[blk, h] = qih[rows]


def _dsa_prep(c_q, c_kv, kw, q_g, kv_g, w_uq, w_uk, w_qi):
    t = c_q.shape[0]
    row = lambda n: pl.BlockSpec((ROW_TILE, n), lambda i: (i, 0))
    blk4 = lambda n: pl.BlockSpec((DSA_BLOCKS_PER_TILE, DSA_HEADS, Q_BLOCK, n), lambda i: (i, 0, 0, 0))
    return pl.pallas_call(
        _dsa_prep_kernel,
        grid=(t // ROW_TILE,),
        in_specs=[row(DSA_Q_RANK), row(DSA_KV_RANK), row(LANES), _const_spec((1, DSA_Q_RANK)),
                  _const_spec((1, DSA_KV_RANK)),
                  _const_spec(w_uq.shape), _const_spec(w_uk.shape), _const_spec(w_qi.shape)],
        out_specs=[row(DSA_KV_RANK), row(IDX_DIM), blk4(DSA_KV_RANK), blk4(IDX_DIM)],
        out_shape=[jax.ShapeDtypeStruct((t, DSA_KV_RANK), BF16),
                   jax.ShapeDtypeStruct((t, IDX_DIM), BF16),
                   jax.ShapeDtypeStruct((t // Q_BLOCK, DSA_HEADS, Q_BLOCK, DSA_KV_RANK), BF16),
                   jax.ShapeDtypeStruct((t // Q_BLOCK, IDX_HEADS, Q_BLOCK, IDX_DIM), BF16)],
        compiler_params=_params("parallel"),
        name="dsa_prep",
    )(c_q, c_kv, kw, q_g.reshape(1, -1), kv_g.reshape(1, -1), w_uq, w_uk, w_qi)


def _dsa_kernel(top_k, qidx_ref, kw_q_ref, qlat_ref, kidx_ref, ckv_ref, wuv_ref, tri_ref, o_ref,
                key_ref, wt_ref, s_ref, mx_ref, l_ref, acc_ref):
    qb = Q_BLOCK
    nh = DSA_HEADS
    j = pl.program_id(1)
    n_chunks = j + 1

    rowi = lax.broadcasted_iota(jnp.int32, (qb, qb), 0)
    coli = lax.broadcasted_iota(jnp.int32, (qb, qb), 1)
    rows = lambda h: slice(h * qb, (h + 1) * qb)

    def key_chunk(ref, c):
        return ref[pl.ds(pl.multiple_of(c * qb, qb), qb), :]

    w_idx = kw_q_ref[:, IDX_DIM:IDX_DIM + IDX_HEADS] * ((IDX_HEADS * IDX_DIM) ** -0.5)
    for h in range(IDX_HEADS):
        wt_ref[rows(h), :] = jnp.broadcast_to(w_idx[:, h:h + 1], (qb, qb))
    q_idx = qidx_ref[0].reshape(IDX_HEADS * qb, IDX_DIM)
    q_lat = qlat_ref[0].reshape(nh * qb, DSA_KV_RANK)

    n_pairs = (n_chunks + 1) // 2

    def score_chunk(c):
        logits = lax.dot_general(q_idx, key_chunk(kidx_ref, c), (((1,), (1,)), ((), ())),
                                 preferred_element_type=F32)
        score = jnp.zeros((qb, qb), F32)
        for h in range(IDX_HEADS):
            score = score + jnp.maximum(logits[rows(h)], 0.0) * wt_ref[rows(h), :]
        bits = lax.bitcast_convert_type(score, jnp.int32)
        key = jnp.where(bits < 0, bits ^ jnp.int32(0x7FFFFFFF), bits)
        causal = coli + c * qb <= rowi + j * qb
        key_ref[c] = jnp.where(causal, key, jnp.int32(INT_MIN))

    def score_body(pi, carry):
        score_chunk(2 * pi)
        score_chunk(2 * pi + 1)
        return carry

    lax.fori_loop(0, n_pairs, score_body, 0)

    def count(pred):
        def body(pi, acc):
            return acc + jnp.where(pred(key_ref[2 * pi]), 1, 0) + jnp.where(pred(key_ref[2 * pi + 1]), 1, 0)
        acc = lax.fori_loop(0, n_pairs, body, jnp.zeros((qb, qb), jnp.int32))
        return jnp.sum(acc, axis=-1, keepdims=True)

    thr = jnp.where(count(lambda k: k >= 0) >= top_k, 0, INT_MIN).astype(jnp.int32)

    def bit_body(i, thr):
        cand = jnp.broadcast_to(thr | lax.shift_left(jnp.int32(1), 30 - i), (qb, qb))
        return jnp.where(count(lambda k: k >= cand) >= top_k, cand[:, :1], thr)

    thr = lax.fori_loop(0, 31, bit_body, thr)
    thr_b = jnp.broadcast_to(thr, (qb, qb))
    need = (top_k - count(lambda k: k > thr_b)).astype(F32)
    need_b = jnp.broadcast_to(need, (qb, qb))

    mx_ref[...] = jnp.full(mx_ref.shape, NEG_BIG, F32)

    def logit_chunk(c, taken):
        key = key_ref[c]
        causal = coli + c * qb <= rowi + j * qb
        eq = jnp.logical_and(key == thr_b, causal)
        eq_f = jnp.where(eq, 1.0, 0.0)
        rank = taken + jnp.dot(eq_f.astype(BF16), tri_ref[...], preferred_element_type=F32)
        sel = jnp.logical_or(jnp.logical_and(key > thr_b, causal), jnp.logical_and(eq, rank <= need_b))
        s = lax.dot_general(q_lat, key_chunk(ckv_ref, c), (((1,), (1,)), ((), ())),
                            preferred_element_type=F32)
        sh = [jnp.where(sel, s[rows(h)], NEG_BIG) for h in range(nh)]
        for h in range(nh):
            s_ref[c, rows(h), :] = sh[h]
        return sh, taken + jnp.sum(eq_f, axis=-1, keepdims=True)

    def logit_body(pi, taken):
        sa, taken = logit_chunk(2 * pi, taken)
        sb, taken = logit_chunk(2 * pi + 1, taken)
        for h in range(nh):
            mx_ref[rows(h), :] = jnp.maximum(mx_ref[rows(h), :], jnp.maximum(sa[h], sb[h]))
        return taken

    lax.fori_loop(0, n_pairs, logit_body, jnp.zeros((qb, 1), F32))

    mx_ref[...] = jnp.broadcast_to(jnp.max(mx_ref[...], axis=-1, keepdims=True), mx_ref.shape)
    l_ref[...] = jnp.zeros_like(l_ref)
    acc_ref[...] = jnp.zeros_like(acc_ref)

    def value_body(pi, carry):
        ps = []
        for h in range(nh):
            mh = mx_ref[rows(h), :]
            pa = jnp.exp(s_ref[2 * pi, rows(h), :] - mh)
            pb = jnp.exp(s_ref[2 * pi + 1, rows(h), :] - mh)
            l_ref[rows(h), :] += pa + pb
            ps.append(jnp.concatenate([pa.astype(BF16), pb.astype(BF16)], axis=1))
        ckv_pair = ckv_ref[pl.ds(pl.multiple_of(pi * 2 * qb, 2 * qb), 2 * qb), :]
        acc_ref[...] += jnp.dot(jnp.concatenate(ps, axis=0), ckv_pair, preferred_element_type=F32)
        return carry

    lax.fori_loop(0, n_pairs, value_body, 0)

    o_lat = acc_ref[...] / jnp.sum(l_ref[...], axis=-1, keepdims=True)
    for h in range(nh):
        o_ref[:, h * DSA_HEAD_DIM:(h + 1) * DSA_HEAD_DIM] = _dot(o_lat[rows(h)], wuv_ref[h])


def _dsa(q_idx, kw, q_lat, k_idx, ckv_n, w_uv, tri, b_, s_):
    t = kw.shape[0]
    nb = s_ // Q_BLOCK
    top_k = min(TOPK_MAX, s_ // 4)
    blk = lambda n: pl.BlockSpec((Q_BLOCK, n), lambda bi, i: (bi * nb + i, 0))
    blk4 = lambda n: pl.BlockSpec((1, DSA_HEADS, Q_BLOCK, n), lambda bi, i: (bi * nb + i, 0, 0, 0))
    seq = lambda n: pl.BlockSpec((s_, n), lambda bi, i: (bi, 0))
    stacked = DSA_HEADS * Q_BLOCK
    return pl.pallas_call(
        functools.partial(_dsa_kernel, top_k),
        grid=(b_, nb),
        in_specs=[blk4(IDX_DIM), blk(LANES), blk4(DSA_KV_RANK),
                  seq(IDX_DIM), seq(DSA_KV_RANK), _const_spec(w_uv.shape), _const_spec(tri.shape)],
        out_specs=blk(DSA_WIDTH),
        out_shape=jax.ShapeDtypeStruct((t, DSA_WIDTH), F32),
        scratch_shapes=[pltpu.VMEM((nb, Q_BLOCK, Q_BLOCK), jnp.int32),
                        pltpu.VMEM((stacked, Q_BLOCK), F32),
                        pltpu.VMEM((nb, stacked, Q_BLOCK), F32),
                        pltpu.VMEM((stacked, Q_BLOCK), F32),
                        pltpu.VMEM((stacked, Q_BLOCK), F32),
                        pltpu.VMEM((stacked, DSA_KV_RANK), F32)],
        compiler_params=_params("parallel", "arbitrary"),
        name="dsa_attn",
    )(q_idx, kw, q_lat, k_idx, ckv_n, w_uv, tri)


SC_TILE = 512


def _sconv_kernel(p_ref, w_ref, b_ref, o_ref, carry_ref):
    @pl.when(pl.program_id(1) == 0)
    def _():
        carry_ref[...] = jnp.zeros_like(carry_ref)

    h = p_ref[:, :SC_WIDTH]
    gate_b = p_ref[:, SC_WIDTH:2 * SC_WIDTH]
    gate_c = p_ref[:, 2 * SC_WIDTH:]
    u = gate_c * h
    carry = carry_ref[...]
    y = u * w_ref[2:3, :] + _shift_rows(u, carry, 1) * w_ref[1:2, :] + _shift_rows(u, carry, 2) * w_ref[0:1, :]
    carry_ref[...] = u[SC_TILE - 8:, :]
    o_ref[...] = gate_b * (y + b_ref[...])


def _sconv(p_sc, b_, s_, conv_w, conv_b):
    t = p_sc.shape[0]
    nt = s_ // SC_TILE
    return pl.pallas_call(
        _sconv_kernel,
        grid=(b_, nt),
        in_specs=[pl.BlockSpec((SC_TILE, 3 * SC_WIDTH), lambda bi, i: (bi * nt + i, 0)),
                  _const_spec((8, SC_WIDTH)), _const_spec((1, SC_WIDTH))],
        out_specs=pl.BlockSpec((SC_TILE, SC_WIDTH), lambda bi, i: (bi * nt + i, 0)),
        out_shape=jax.ShapeDtypeStruct((t, SC_WIDTH), F32),
        scratch_shapes=[pltpu.VMEM((8, SC_WIDTH), F32)],
        compiler_params=_params("parallel", "arbitrary"),
        name="short_conv",
    )(p_sc, jnp.pad(conv_w, ((0, 8 - SC_KERNEL), (0, 0))), conv_b.reshape(1, -1))


def _xattn_kernel(x_ref, gq_ref, wq_ref, k_ref, v_ref, wo_ref, go_ref, o_ref, att_ref):
    x = x_ref[...]
    q = jnp.dot(_rms(x, gq_ref[...]).astype(BF16), wq_ref[...], preferred_element_type=F32)
    for h in range(XA_HEADS):
        sl = slice(h * XA_HEAD_DIM, (h + 1) * XA_HEAD_DIM)
        s = _dot_nt(q[:, sl], k_ref[:, sl]) * (XA_HEAD_DIM ** -0.5)
        s = s - jnp.max(s, axis=-1, keepdims=True)
        p = jnp.exp(s)
        p = p / jnp.sum(p, axis=-1, keepdims=True)
        att_ref[:, sl] = _dot(p, v_ref[:, sl])
    hout = jnp.dot(att_ref[...].astype(BF16), wo_ref[...], preferred_element_type=F32)
    o_ref[...] = x + _rms(hout, go_ref[...])


def _xattn(x, g_q, wq, k_mem, v_mem, wo, g_o, b_, s_):
    t = x.shape[0]
    nt = s_ // ROW_TILE
    row = pl.BlockSpec((ROW_TILE, D_MODEL), lambda bi, i: (bi * nt + i, 0))
    mem = pl.BlockSpec((MEM_LEN, XA_WIDTH), lambda bi, i: (bi, 0))
    return pl.pallas_call(
        _xattn_kernel,
        grid=(b_, nt),
        in_specs=[row, _const_spec((1, D_MODEL)), _const_spec(wq.shape), mem, mem,
                  _const_spec(wo.shape), _const_spec((1, D_MODEL))],
        out_specs=row,
        out_shape=jax.ShapeDtypeStruct((t, D_MODEL), F32),
        scratch_shapes=[pltpu.VMEM((ROW_TILE, XA_WIDTH), F32)],
        compiler_params=_params("parallel", "parallel"),
        name="mem_xattn",
    )(x, g_q.reshape(1, -1), wq, k_mem, v_mem, wo, g_o.reshape(1, -1))


def _block_diag(n_blocks, size, value):
    return np.kron(np.eye(n_blocks, dtype=np.float32), np.full((size, size), value, np.float32))


def _rope_tables(s_):
    half = RT_DIM // 2
    inv_freq = RT_ROPE_BASE ** (-jnp.arange(half, dtype=F32) / half)
    ang = jnp.arange(s_).astype(F32)[:, None] * inv_freq[None, :]
    cos, sin = jnp.cos(ang), jnp.sin(ang)
    return jnp.concatenate([cos, cos], axis=-1), jnp.concatenate([-sin, sin], axis=-1)


def kernel(x, mem, norm_g, mem_norm_g, ffn_w_gate, ffn_w_up, ffn_w_down, xa_wq, xa_wk, xa_wv, xa_wo, ev_w_in, ev_w_out, rw_mu, rw_w0, rw_w2, rw_a0, rw_a2, rw_g2, rw_k_k, rw_k_a, rw_r_k, rw_ln_g, rw_ln_b, rt_gn_g, rt_gn_b, od_w_in, od_w_out, dsa_q_norm_g, dsa_kv_norm_g, dsa_w_uq, dsa_w_uk, dsa_w_uv, dsa_w_qi, sc_conv_w, sc_conv_b):
    b_, s_, d_ = x.shape
    depth = norm_g.shape[0]
    t = b_ * s_
    bf = lambda w: w.astype(BF16)

    ones_blk = jnp.asarray(_block_diag(RW_HEADS, RW_HEAD_DIM, 1.0), BF16)
    ones_pair = jnp.asarray(_block_diag(2, RW_HEAD_DIM, 1.0), BF16)
    tri_rw = jnp.asarray(np.tril(np.ones((RW_CHUNK, RW_CHUNK), np.float32)), BF16)
    tri_dsa = jnp.asarray(np.triu(np.ones((Q_BLOCK, Q_BLOCK), np.float32)), BF16)
    cos2, sin2 = _rope_tables(s_)

    xf = x.reshape(t, d_)
    mem_f = mem.reshape(b_ * MEM_LEN, d_)
    for l in range(depth):
        ng = norm_g[l]
        i = l // 2
        xf = _ffn(xf, ng[0], bf(ffn_w_gate[l, 0]), bf(ffn_w_up[l, 0]), bf(ffn_w_down[l, 0]), ng[1])
        if l % 2 == 0:
            w_in = bf(ev_w_in[i])
            p_rw, p_rt = _norm_proj(xf, ng[2], [w_in[:, :RW_IN], w_in[:, RW_IN:]], "ev_in_proj")
            r, k, v, ld, a, b, g = _rw_prep(p_rw, b_, s_, rw_mu[i], rw_w0[i], rw_w2[i], rw_a0[i], rw_a2[i],
                                            rw_g2[i], rw_k_k[i], rw_k_a[i], ones_blk)
            y_a = _rw_scan(r, k, v, ld, a, b, g, b_, s_, rw_r_k[i], rw_ln_g[i], rw_ln_b[i],
                           ones_pair, tri_rw)
            y_b = _retention(p_rt, b_, s_, cos2, sin2, rt_gn_g[i], rt_gn_b[i])
            w_out = bf(ev_w_out[i])
            xf = _proj_res([y_a, y_b], [w_out[:RW_WIDTH], w_out[RW_WIDTH:]], ng[3], xf, "ev_out_proj")
        else:
            w_in = od_w_in[i]
            kw_w = jnp.pad(w_in[:, DSA_Q_RANK + DSA_KV_RANK:DSA_IN], ((0, 0), (0, LANES - IDX_DIM - IDX_HEADS)))
            c_q, c_kv, kw, p_sc = _norm_proj(
                xf, ng[2], [bf(w_in[:, :DSA_Q_RANK]), bf(w_in[:, DSA_Q_RANK:DSA_Q_RANK + DSA_KV_RANK]),
                            bf(kw_w), bf(w_in[:, DSA_IN:])], "od_in_proj")
            ckv_n, k_idx, q_lat, q_idx = _dsa_prep(
                c_q, c_kv, kw, dsa_q_norm_g[i], dsa_kv_norm_g[i],
                bf(dsa_w_uq[i].reshape(DSA_Q_RANK, DSA_WIDTH)), bf(dsa_w_uk[i]),
                bf(dsa_w_qi[i].reshape(DSA_Q_RANK, IDX_HEADS * IDX_DIM)))
            y_c = _dsa(q_idx, kw, q_lat, k_idx, ckv_n, bf(dsa_w_uv[i]), tri_dsa, b_, s_)
            y_d = _sconv(p_sc, b_, s_, sc_conv_w[i], sc_conv_b[i])
            w_out = bf(od_w_out[i])
            xf = _proj_res([y_c, y_d], [w_out[:DSA_WIDTH], w_out[DSA_WIDTH:]], ng[3], xf, "od_out_proj")
        k_mem, v_mem = _norm_proj(mem_f, mem_norm_g, [bf(xa_wk[l]), bf(xa_wv[l])], "mem_kv_proj")
        xf = _xattn(xf, ng[4], bf(xa_wq[l]), k_mem, v_mem, bf(xa_wo[l]), ng[5], b_, s_)
        xf = _ffn(xf, ng[6], bf(ffn_w_gate[l, 1]), bf(ffn_w_up[l, 1]), bf(ffn_w_down[l, 1]), ng[7])
    return xf.reshape(b_, s_, d_)
```

```python
import functools
import math

import numpy as np
import jax
import jax.numpy as jnp
from jax import lax
from jax.experimental import pallas as pl
from jax.experimental.pallas import tpu as pltpu

F32 = jnp.float32
BF16 = jnp.bfloat16

D_MODEL = 1024
D_FF = 2816
EPS = 1e-6
MEM_LEN = 256
RW_HEADS = 8
RW_HEAD_DIM = 64
RW_WIDTH = RW_HEADS * RW_HEAD_DIM
RW_DECAY_RANK = 64
RW_AAA_RANK = 64
RW_GATE_RANK = 128
RW_LN_EPS = 64e-5
RW_IN = 3 * RW_WIDTH + RW_DECAY_RANK + RW_AAA_RANK + RW_GATE_RANK
RW_CHUNK = 64
RT_HEADS = 4
RT_DIM = 128
RT_WIDTH = RT_HEADS * RT_DIM
RT_CHUNK = 128
RT_ROPE_BASE = 10000.0
RT_IN = 4 * RT_WIDTH
DSA_HEADS = 8
DSA_HEAD_DIM = 64
DSA_WIDTH = DSA_HEADS * DSA_HEAD_DIM
DSA_Q_RANK = 256
DSA_KV_RANK = 128
IDX_HEADS = 8
IDX_DIM = 64
TOPK_MAX = 256
Q_BLOCK = 128
DSA_IN = DSA_Q_RANK + DSA_KV_RANK + IDX_DIM + IDX_HEADS
SC_WIDTH = 512
SC_KERNEL = 3
XA_HEADS = 4
XA_HEAD_DIM = 128
XA_WIDTH = XA_HEADS * XA_HEAD_DIM

LANES = 128
ROW_TILE = 512
VMEM_LIMIT = 56 * 1024 * 1024
INT_MIN = -2 ** 31
NEG_BIG = -1e30


def _params(*sem):
    return pltpu.CompilerParams(dimension_semantics=sem, vmem_limit_bytes=VMEM_LIMIT)


def _rms(x, g):
    return x * lax.rsqrt(jnp.mean(x * x, axis=-1, keepdims=True) + EPS) * g


def _dot(a, b):
    return jnp.dot(a.astype(BF16), b.astype(BF16), preferred_element_type=F32)


def _dot_nt(a, b):
    return lax.dot_general(a.astype(BF16), b.astype(BF16), (((1,), (1,)), ((), ())),
                           preferred_element_type=F32)


def _dot_tn(a, b):
    return lax.dot_general(a.astype(BF16), b.astype(BF16), (((0,), (0,)), ((), ())),
                           preferred_element_type=F32)


def _split3(x):
    hi = x.astype(BF16)
    r1 = x - hi.astype(F32)
    mid = r1.astype(BF16)
    lo = (r1 - mid.astype(F32)).astype(BF16)
    return hi, mid, lo


def _dot_exact_rhs(x, w_bf16):
    hi, mid, lo = _split3(x)
    out = jnp.dot(hi, w_bf16, preferred_element_type=F32)
    out += jnp.dot(mid, w_bf16, preferred_element_type=F32)
    out += jnp.dot(lo, w_bf16, preferred_element_type=F32)
    return out


def _dot_exact_lhs(w_bf16, x):
    hi, mid, lo = _split3(x)
    out = jnp.dot(w_bf16, hi, preferred_element_type=F32)
    out += jnp.dot(w_bf16, mid, preferred_element_type=F32)
    out += jnp.dot(w_bf16, lo, preferred_element_type=F32)
    return out


def _const_spec(shape):
    nd = len(shape)
    return pl.BlockSpec(shape, lambda *_: (0,) * nd, pipeline_mode=pl.Buffered(1))


FF_CHUNK = 1408


def _ffn_kernel(x_ref, gin_ref, wg_ref, wu_ref, wd_ref, gout_ref, o_ref, acc_ref):
    x = x_ref[...]
    xb = _rms(x, gin_ref[...]).astype(BF16)
    for c in range(D_FF // FF_CHUNK):
        sl = slice(c * FF_CHUNK, (c + 1) * FF_CHUNK)
        g = jnp.dot(xb, wg_ref[:, sl], preferred_element_type=F32)
        u = jnp.dot(xb, wu_ref[:, sl], preferred_element_type=F32)
        h = (g * jax.nn.sigmoid(g) * u).astype(BF16)
        part = jnp.dot(h, wd_ref[sl, :], preferred_element_type=F32)
        if c == 0:
            acc_ref[...] = part
        else:
            acc_ref[...] += part
    o_ref[...] = x + 0.5 * _rms(acc_ref[...], gout_ref[...])


def _ffn(x, g_in, wg, wu, wd, g_out):
    t = x.shape[0]
    row = pl.BlockSpec((ROW_TILE, D_MODEL), lambda i: (i, 0))
    return pl.pallas_call(
        _ffn_kernel,
        grid=(t // ROW_TILE,),
        in_specs=[row, _const_spec((1, D_MODEL)), _const_spec((D_MODEL, D_FF)),
                  _const_spec((D_MODEL, D_FF)), _const_spec((D_FF, D_MODEL)), _const_spec((1, D_MODEL))],
        out_specs=row,
        out_shape=jax.ShapeDtypeStruct((t, D_MODEL), F32),
        scratch_shapes=[pltpu.VMEM((ROW_TILE, D_MODEL), F32)],
        compiler_params=_params("parallel"),
        name="ffn_half",
    )(x, g_in.reshape(1, -1), wg, wu, wd, g_out.reshape(1, -1))


PROJ_CHUNK = 512


def _norm_proj_kernel(n_out, x_ref, g_ref, *refs):
    xb = _rms(x_ref[...], g_ref[...]).astype(BF16)
    for w_ref, o_ref in zip(refs[:n_out], refs[n_out:]):
        n = w_ref.shape[1]
        for c in range(0, n, PROJ_CHUNK):
            sl = slice(c, min(c + PROJ_CHUNK, n))
            o_ref[:, sl] = jnp.dot(xb, w_ref[:, sl], preferred_element_type=F32)


def _norm_proj(x, g, ws, name):
    t, d = x.shape
    row = lambda n: pl.BlockSpec((ROW_TILE, n), lambda i: (i, 0))
    return pl.pallas_call(
        functools.partial(_norm_proj_kernel, len(ws)),
        grid=(t // ROW_TILE,),
        in_specs=[row(d), _const_spec((1, d))] + [_const_spec(w.shape) for w in ws],
        out_specs=[row(w.shape[1]) for w in ws],
        out_shape=[jax.ShapeDtypeStruct((t, w.shape[1]), F32) for w in ws],
        compiler_params=_params("parallel"),
        name=name,
    )(x, g.reshape(1, -1), *ws)


def _proj_res_kernel(n_in, *refs):
    y_refs, w_refs = refs[:n_in], refs[n_in:2 * n_in]
    g_ref, x_ref, o_ref = refs[2 * n_in:]
    h = None
    for y_ref, w_ref in zip(y_refs, w_refs):
        part = jnp.dot(y_ref[...].astype(BF16), w_ref[...], preferred_element_type=F32)
        h = part if h is None else h + part
    o_ref[...] = x_ref[...] + _rms(h, g_ref[...])


def _proj_res(ys, ws, g, x, name):
    t, d = x.shape
    row = lambda n: pl.BlockSpec((ROW_TILE, n), lambda i: (i, 0))
    return pl.pallas_call(
        functools.partial(_proj_res_kernel, len(ys)),
        grid=(t // ROW_TILE,),
        in_specs=[row(y.shape[1]) for y in ys] + [_const_spec(w.shape) for w in ws]
                 + [_const_spec((1, d)), row(d)],
        out_specs=row(d),
        out_shape=jax.ShapeDtypeStruct((t, d), F32),
        compiler_params=_params("parallel"),
        name=name,
    )(*ys, *ws, g.reshape(1, -1), x)


RW_TILE = 512


def _shift_rows(x, carry, n):
    rolled = pltpu.roll(x, n, 0)
    row = lax.broadcasted_iota(jnp.int32, x.shape, 0)
    out = rolled
    for i in range(n):
        out = jnp.where(row == i, carry[8 - n + i:8 - n + i + 1, :], out)
    return out


def _rw_prep_kernel(p_ref, mu_ref, w0_ref, w2_ref, a0_ref, a2_ref, g2_ref, kk_ref, ka_ref, ones_ref,
                    r_ref, k_ref, v_ref, ld_ref, a_ref, b_ref, g_ref, carry_ref):
    @pl.when(pl.program_id(1) == 0)
    def _():
        carry_ref[...] = jnp.zeros_like(carry_ref)

    p = p_ref[...]
    prev = _shift_rows(p, carry_ref[...], 1)
    carry_ref[...] = p[RW_TILE - 8:, :]
    xm = p + (prev - p) * mu_ref[...]
    w = RW_WIDTH
    r, k, v = xm[:, :w], xm[:, w:2 * w], xm[:, 2 * w:3 * w]
    xw = xm[:, 3 * w:3 * w + RW_DECAY_RANK]
    xa = xm[:, 3 * w + RW_DECAY_RANK:3 * w + RW_DECAY_RANK + RW_AAA_RANK]
    xg = xm[:, 3 * w + RW_DECAY_RANK + RW_AAA_RANK:]
    wlog = -jax.nn.softplus(-(w0_ref[...] + _dot(jnp.tanh(xw), w2_ref[...]))) - 0.5
    a = jax.nn.sigmoid(a0_ref[...] + _dot(xa, a2_ref[...]))
    kk = k * kk_ref[...]
    ss = _dot_exact_rhs(kk * kk, ones_ref[...])
    kk = kk / jnp.maximum(jnp.sqrt(ss), 1e-12)
    r_ref[...] = r
    k_ref[...] = k * (1.0 + (a - 1.0) * ka_ref[...])
    v_ref[...] = v
    ld_ref[...] = -jnp.exp(wlog)
    a_ref[...] = -kk
    b_ref[...] = kk * a
    g_ref[...] = _dot(jax.nn.sigmoid(xg), g2_ref[...])


def _rw_prep(p_rw, b_, s_, mu, w0, w2, a0, a2, g2, k_k, k_a, ones_blk):
    t = p_rw.shape[0]
    nt = s_ // RW_TILE
    row = lambda n: pl.BlockSpec((RW_TILE, n), lambda b, i: (b * nt + i, 0))
    vec = lambda a: a.reshape(1, -1)
    outs = [jax.ShapeDtypeStruct((t, RW_WIDTH), F32)] * 7
    return pl.pallas_call(
        _rw_prep_kernel,
        grid=(b_, nt),
        in_specs=[row(RW_IN), _const_spec((1, RW_IN)), _const_spec((1, RW_WIDTH)),
                  _const_spec(w2.shape), _const_spec((1, RW_WIDTH)), _const_spec(a2.shape),
                  _const_spec(g2.shape), _const_spec((1, RW_WIDTH)), _const_spec((1, RW_WIDTH)),
                  _const_spec(ones_blk.shape)],
        out_specs=[row(RW_WIDTH)] * 7,
        out_shape=outs,
        scratch_shapes=[pltpu.VMEM((8, RW_IN), F32)],
        compiler_params=_params("parallel", "arbitrary"),
        name="rwkv_prep",
    )(p_rw, vec(mu), vec(w0), w2, vec(a0), a2, g2, vec(k_k), vec(k_a), ones_blk)


def _rw_scan_kernel(r_ref, k_ref, v_ref, ld_ref, a_ref, b_ref, g_ref, tri_ref, ones_ref,
                    rk_ref, lng_ref, lnb_ref, o_ref, state_ref, y_ref):
    c_ = RW_CHUNK
    n_ = RW_HEAD_DIM

    @pl.when(pl.program_id(1) == 0)
    def _():
        state_ref[...] = jnp.zeros_like(state_ref)

    ld = ld_ref[...]
    r = r_ref[...]
    k = k_ref[...]
    v = v_ref[...]
    cum = _dot_exact_lhs(tri_ref[...], ld)
    mid = cum[c_ // 2 - 1:c_ // 2, :]
    e_in = jnp.exp(cum - mid)
    e_out = jnp.exp(mid - cum)
    r_t = r * e_in
    a_t = a_ref[...] * jnp.exp(cum - ld - mid)
    b_t = b_ref[...] * e_out
    k_t = k * e_out
    e_mid = jnp.exp(mid)
    w_all = jnp.exp(cum[c_ - 1:c_, :])
    w_tail = jnp.exp(cum[c_ - 1:c_, :] - mid)

    c2 = 2 * c_
    row = lax.broadcasted_iota(jnp.int32, (c2, c2), 0)
    col = lax.broadcasted_iota(jnp.int32, (c2, c2), 1)
    same_head = (row >= c_) == (col >= c_)
    strict = jnp.logical_and(same_head, row > col)
    incl = jnp.logical_and(same_head, row >= col)
    left = lax.broadcasted_iota(jnp.int32, (c_, LANES), 1) < n_

    def block_diag(x):
        return jnp.concatenate([jnp.where(left, x, 0.0), jnp.where(left, 0.0, x)], axis=0)

    pairs = range(RW_HEADS // 2)
    sls = [slice(p * LANES, (p + 1) * LANES) for p in pairs]
    s0 = [state_ref[p] for p in pairs]
    ar = [jnp.concatenate([block_diag(a_t[:, sl]), block_diag(r_t[:, sl])], axis=0) for sl in sls]
    bk = [jnp.concatenate([block_diag(b_t[:, sl]), block_diag(k_t[:, sl])], axis=0) for sl in sls]
    vb = [block_diag(v[:, sl]) for sl in sls]
    m1 = [_dot_nt(ar[p], bk[p]) for p in pairs]
    m2 = [_dot_nt(ar[p] * e_mid[:, sls[p]], s0[p]) for p in pairs]
    l_ab = [jnp.where(strict, m[:c2, :c2], 0.0) for m in m1]
    l_ak = [jnp.where(strict, m[:c2, c2:], 0.0) for m in m1]
    l_r = [jnp.concatenate([jnp.where(incl, m[c2:, :c2], 0.0), jnp.where(incl, m[c2:, c2:], 0.0)], axis=1)
           for m in m1]
    u = [m2[p][:c2] + _dot(l_ak[p], vb[p]) for p in pairs]
    pw = l_ab
    n_steps = int(math.log2(c_))
    for step in range(n_steps):
        if step < n_steps - 1:
            prod = [_dot(pw[p], jnp.concatenate([pw[p], u[p]], axis=1)) for p in pairs]
            pw = [q[:, :c2] for q in prod]
            u = [u[p] + prod[p][:, c2:] for p in pairs]
        else:
            u = [u[p] + _dot(pw[p], u[p]) for p in pairs]
    uv = [jnp.concatenate([u[p], vb[p]], axis=0) for p in pairs]
    ys = [m2[p][c2:] + _dot(l_r[p], uv[p]) for p in pairs]
    upd = [_dot_tn(uv[p], bk[p]) for p in pairs]
    for p in pairs:
        y_ref[:, sls[p]] = ys[p][:c_] + ys[p][c_:]
        state_ref[p] = s0[p] * w_all[:, sls[p]] + upd[p] * w_tail[:, sls[p]]

    def head_sums(xs, blk):
        parts = [part[:, sl] for x in xs for part in _split3(x) for sl in sls]
        prod = jnp.dot(jnp.concatenate(parts, axis=0), blk, preferred_element_type=F32)
        outs = []
        for i in range(len(xs)):
            slabs = []
            for p in pairs:
                rows = [((3 * i + j) * len(sls) + p) * c_ for j in range(3)]
                slabs.append(prod[rows[0]:rows[0] + c_] + prod[rows[1]:rows[1] + c_] + prod[rows[2]:rows[2] + c_])
            outs.append(jnp.concatenate(slabs, axis=1))
        return outs

    y = y_ref[...]
    ones2 = ones_ref[...]
    sum_y, sum_rk = head_sums([y, r * k * rk_ref[...]], ones2)
    d = y - sum_y * (1.0 / n_)
    var = head_sums([d * d], ones2)[0] * (1.0 / n_)
    yn = d * lax.rsqrt(var + RW_LN_EPS) * lng_ref[...] + lnb_ref[...]
    o_ref[...] = (yn + sum_rk * v) * g_ref[...]


def _rw_scan(r, k, v, ld, a, b, g, b_, s_, r_k, ln_g, ln_b, ones_pair, tri):
    t = r.shape[0]
    nc = s_ // RW_CHUNK
    row = pl.BlockSpec((RW_CHUNK, RW_WIDTH), lambda bi, i: (bi * nc + i, 0))
    vec = lambda x: x.reshape(1, -1)
    return pl.pallas_call(
        _rw_scan_kernel,
        grid=(b_, nc),
        in_specs=[row] * 7 + [_const_spec(tri.shape), _const_spec(ones_pair.shape),
                              _const_spec((1, RW_WIDTH)), _const_spec((1, RW_WIDTH)), _const_spec((1, RW_WIDTH))],
        out_specs=row,
        out_shape=jax.ShapeDtypeStruct((t, RW_WIDTH), F32),
        scratch_shapes=[pltpu.VMEM((RW_HEADS // 2, 2 * RW_HEAD_DIM, 2 * RW_HEAD_DIM), F32),
                        pltpu.VMEM((RW_CHUNK, RW_WIDTH), F32)],
        compiler_params=_params("parallel", "arbitrary"),
        name="rwkv_scan",
    )(r, k, v, ld, a, b, g, tri, ones_pair, vec(r_k), vec(ln_g), vec(ln_b))


def _rt_log_gamma(h):
    return math.log1p(-(2.0 ** (-5.0 - h)))


def _retention_kernel(p_ref, cos_ref, sin_ref, gng_ref, gnb_ref, o_ref, state_ref):
    c_ = RT_CHUNK

    @pl.when(pl.program_id(1) == 0)
    def _():
        state_ref[...] = jnp.zeros_like(state_ref)

    cos = cos_ref[...]
    sin = sin_ref[...]
    row = lax.broadcasted_iota(jnp.int32, (c_, c_), 0).astype(F32)
    col = lax.broadcasted_iota(jnp.int32, (c_, c_), 1).astype(F32)
    rel = row - col
    for h in range(RT_HEADS):
        lg = _rt_log_gamma(h)
        sl = slice(h * RT_DIM, (h + 1) * RT_DIM)
        q = p_ref[:, sl]
        k = p_ref[:, RT_WIDTH + h * RT_DIM:RT_WIDTH + (h + 1) * RT_DIM]
        v = p_ref[:, 2 * RT_WIDTH + h * RT_DIM:2 * RT_WIDTH + (h + 1) * RT_DIM]
        gate = p_ref[:, 3 * RT_WIDTH + h * RT_DIM:3 * RT_WIDTH + (h + 1) * RT_DIM]
        q = q * cos + pltpu.roll(q, RT_DIM // 2, 1) * sin
        k = (k * cos + pltpu.roll(k, RT_DIM // 2, 1) * sin) * (RT_DIM ** -0.5)
        decay = jnp.where(rel >= 0, jnp.exp(jnp.maximum(rel, 0.0) * lg), 0.0)
        scores = _dot_nt(q, k) * decay
        s0 = state_ref[h]
        xi = jnp.exp((row + 1.0) * lg)
        o = _dot(scores, v) + _dot(q, s0) * xi
        zeta = jnp.exp((c_ - 1.0 - row) * lg)
        state_ref[h] = s0 * math.exp(c_ * lg) + _dot_tn(k * zeta, v)
        mu = jnp.mean(o, axis=-1, keepdims=True)
        d = o - mu
        var = jnp.mean(d * d, axis=-1, keepdims=True)
        on = d * lax.rsqrt(var + EPS) * gng_ref[:, sl] + gnb_ref[:, sl]
        o_ref[:, sl] = gate * jax.nn.sigmoid(gate) * on


def _retention(p_rt, b_, s_, cos2, sin2, gn_g, gn_b):
    t = p_rt.shape[0]
    nc = s_ // RT_CHUNK
    return pl.pallas_call(
        _retention_kernel,
        grid=(b_, nc),
        in_specs=[pl.BlockSpec((RT_CHUNK, RT_IN), lambda bi, i: (bi * nc + i, 0)),
                  pl.BlockSpec((RT_CHUNK, RT_DIM), lambda bi, i: (i, 0)),
                  pl.BlockSpec((RT_CHUNK, RT_DIM), lambda bi, i: (i, 0)),
                  _const_spec((1, RT_WIDTH)), _const_spec((1, RT_WIDTH))],
        out_specs=pl.BlockSpec((RT_CHUNK, RT_WIDTH), lambda bi, i: (bi * nc + i, 0)),
        out_shape=jax.ShapeDtypeStruct((t, RT_WIDTH), F32),
        scratch_shapes=[pltpu.VMEM((RT_HEADS, RT_DIM, RT_DIM), F32)],
        compiler_params=_params("parallel", "arbitrary"),
        name="retention",
    )(p_rt, cos2, sin2, gn_g.reshape(1, -1), gn_b.reshape(1, -1))


DSA_BLOCKS_PER_TILE = ROW_TILE // Q_BLOCK


def _dsa_prep_kernel(cq_ref, ckv_ref, kw_ref, qg_ref, kvg_ref, wuq_ref, wuk_ref, wqi_ref,
                     ckvn_ref, ckvt_ref, kidx_ref, widx_ref, qlat_ref, qidx_ref):
    cq = _rms(cq_ref[...], qg_ref[...]).astype(BF16)
    ckvn = _rms(ckv_ref[...], kvg_ref[...])
    ckvn_ref[...] = ckvn.astype(BF16)
    kw = kw_ref[...]
    kidx_ref[...] = kw[:, :IDX_DIM].astype(BF16)
    q = jnp.dot(cq, wuq_ref[...], preferred_element_type=F32)
    qi = jnp.dot(cq, wqi_ref[...], preferred_element_type=F32)
    for blk in range(DSA_BLOCKS_PER_TILE):
        rows = slice(blk * Q_BLOCK, (blk + 1) * Q_BLOCK)
        ckvt_ref[blk] = ckvn[rows].T.astype(BF16)
        widx_ref[blk] = kw[rows].T[IDX_DIM:IDX_DIM + IDX_HEADS] * ((IDX_HEADS * IDX_DIM) ** -0.5)
    for h in range(DSA_HEADS):
        qh = q[:, h * DSA_HEAD_DIM:(h + 1) * DSA_HEAD_DIM]
        ql = (_dot(qh, wuk_ref[h]) * (DSA_HEAD_DIM ** -0.5)).astype(BF16)
        qih = qi[:, h * IDX_DIM:(h + 1) * IDX_DIM].astype(BF16)
        for blk in range(DSA_BLOCKS_PER_TILE):
            rows = slice(blk * Q_BLOCK, (blk + 1) * Q_BLOCK)
            qlat_ref[blk, h] = ql[rows]
            qidx_ref[blk, h] = qih[rows]


def _dsa_prep(c_q, c_kv, kw, q_g, kv_g, w_uq, w_uk, w_qi):
    t = c_q.shape[0]
    row = lambda n: pl.BlockSpec((ROW_TILE, n), lambda i: (i, 0))
    blk3 = lambda m, n: pl.BlockSpec((DSA_BLOCKS_PER_TILE, m, n), lambda i: (i, 0, 0))
    blk4 = lambda n: pl.BlockSpec((DSA_BLOCKS_PER_TILE, DSA_HEADS, Q_BLOCK, n), lambda i: (i, 0, 0, 0))
    return pl.pallas_call(
        _dsa_prep_kernel,
        grid=(t // ROW_TILE,),
        in_specs=[row(DSA_Q_RANK), row(DSA_KV_RANK), row(LANES), _const_spec((1, DSA_Q_RANK)),
                  _const_spec((1, DSA_KV_RANK)),
                  _const_spec(w_uq.shape), _const_spec(w_uk.shape), _const_spec(w_qi.shape)],
        out_specs=[row(DSA_KV_RANK), blk3(DSA_KV_RANK, Q_BLOCK), row(IDX_DIM), blk3(IDX_HEADS, Q_BLOCK),
                   blk4(DSA_KV_RANK), blk4(IDX_DIM)],
        out_shape=[jax.ShapeDtypeStruct((t, DSA_KV_RANK), BF16),
                   jax.ShapeDtypeStruct((t // Q_BLOCK, DSA_KV_RANK, Q_BLOCK), BF16),
                   jax.ShapeDtypeStruct((t, IDX_DIM), BF16),
                   jax.ShapeDtypeStruct((t // Q_BLOCK, IDX_HEADS, Q_BLOCK), F32),
                   jax.ShapeDtypeStruct((t // Q_BLOCK, DSA_HEADS, Q_BLOCK, DSA_KV_RANK), BF16),
                   jax.ShapeDtypeStruct((t // Q_BLOCK, IDX_HEADS, Q_BLOCK, IDX_DIM), BF16)],
        compiler_params=_params("parallel"),
        name="dsa_prep",
    )(c_q, c_kv, kw, q_g.reshape(1, -1), kv_g.reshape(1, -1), w_uq, w_uk, w_qi)


def _fold_rows(x, op):
    return functools.reduce(op, [x[i:i + 8] for i in range(0, x.shape[0], 8)])


def _dsa_kernel(top_k, qidx_ref, widx_ref, qlat_ref, kidx_ref, ckv_ref, ckvt_ref, wuv_ref, tril_ref, o_ref,
                key_ref, s_ref, acc_ref):
    qb = Q_BLOCK
    nh = DSA_HEADS
    j = pl.program_id(1)
    n_chunks = j + 1

    keyi = lax.broadcasted_iota(jnp.int32, (qb, qb), 0)
    qryi = lax.broadcasted_iota(jnp.int32, (qb, qb), 1)
    lanes = lambda h: slice(h * qb, (h + 1) * qb)

    def key_chunk(ref, c):
        return ref[pl.ds(pl.multiple_of(c * qb, qb), qb), :]

    w_idx = widx_ref[0]
    q_idx = qidx_ref[0].reshape(IDX_HEADS * qb, IDX_DIM)
    q_lat = qlat_ref[0].reshape(nh * qb, DSA_KV_RANK)

    n_pairs = (n_chunks + 1) // 2

    def score_chunk(c):
        logits = lax.dot_general(key_chunk(kidx_ref, c), q_idx, (((1,), (1,)), ((), ())),
                                 preferred_element_type=F32)
        score = jnp.zeros((qb, qb), F32)
        for h in range(IDX_HEADS):
            score = score + jnp.maximum(logits[:, lanes(h)], 0.0) * w_idx[h:h + 1, :]
        bits = lax.bitcast_convert_type(score, jnp.int32)
        key = jnp.where(bits < 0, bits ^ jnp.int32(0x7FFFFFFF), bits)
        causal = keyi + c * qb <= qryi + j * qb
        key_ref[c] = jnp.where(causal, key, jnp.int32(INT_MIN))

    def score_body(pi, carry):
        score_chunk(2 * pi)
        score_chunk(2 * pi + 1)
        return carry

    lax.fori_loop(0, n_pairs, score_body, 0)

    def count(pred):
        def body(pi, acc):
            return acc + jnp.where(pred(key_ref[2 * pi]), 1, 0) + jnp.where(pred(key_ref[2 * pi + 1]), 1, 0)
        acc = lax.fori_loop(0, n_pairs, body, jnp.zeros((qb, qb), jnp.int32))
        return jnp.sum(_fold_rows(acc, jnp.add), axis=0, keepdims=True)

    thr = jnp.where(count(lambda k: k >= 0) >= top_k, 0, INT_MIN).astype(jnp.int32)

    def bit_body(i, thr):
        cand = thr | lax.shift_left(jnp.int32(1), 30 - i)
        return jnp.where(count(lambda k: k >= cand) >= top_k, cand, thr)

    thr = lax.fori_loop(0, 31, bit_body, thr)
    need = (top_k - count(lambda k: k > thr)).astype(F32)

    def logit_chunk(c, taken):
        key = key_ref[c]
        causal = keyi + c * qb <= qryi + j * qb
        eq = jnp.logical_and(key == thr, causal)
        eq_f = jnp.where(eq, 1.0, 0.0)
        rank = taken + jnp.dot(tril_ref[...], eq_f.astype(BF16), preferred_element_type=F32)
        sel = jnp.logical_or(jnp.logical_and(key > thr, causal), jnp.logical_and(eq, rank <= need))
        s = lax.dot_general(key_chunk(ckv_ref, c), q_lat, (((1,), (1,)), ((), ())),
                            preferred_element_type=F32)
        tops = []
        for h in range(nh):
            sh = jnp.where(sel, s[:, lanes(h)], NEG_BIG)
            s_ref[c, :, lanes(h)] = sh
            tops.append(_fold_rows(sh, jnp.maximum))
        return tops, taken + jnp.sum(eq_f, axis=0, keepdims=True)

    def logit_body(pi, carry):
        taken, tops = carry
        ta, taken = logit_chunk(2 * pi, taken)
        tb, taken = logit_chunk(2 * pi + 1, taken)
        return taken, tuple(jnp.maximum(t, jnp.maximum(a, b)) for t, a, b in zip(tops, ta, tb))

    _, tops = lax.fori_loop(0, n_pairs, logit_body,
                            (jnp.zeros((1, qb), F32), tuple(jnp.full((8, qb), NEG_BIG, F32) for _ in range(nh))))
    tops = [jnp.max(t, axis=0, keepdims=True) for t in tops]

    acc_ref[...] = jnp.zeros_like(acc_ref)

    def value_body(pi, sums):
        ps, new_sums = [], []
        for h in range(nh):
            pa = jnp.exp(s_ref[2 * pi, :, lanes(h)] - tops[h])
            pb = jnp.exp(s_ref[2 * pi + 1, :, lanes(h)] - tops[h])
            new_sums.append(sums[h] + _fold_rows(pa + pb, jnp.add))
            ps.append(jnp.concatenate([pa.astype(BF16), pb.astype(BF16)], axis=0))
        ckvt_pair = jnp.concatenate([ckvt_ref[2 * pi], ckvt_ref[2 * pi + 1]], axis=1)
        acc_ref[...] += jnp.dot(ckvt_pair, jnp.concatenate(ps, axis=1), preferred_element_type=F32)
        return tuple(new_sums)

    sums = lax.fori_loop(0, n_pairs, value_body, tuple(jnp.zeros((8, qb), F32) for _ in range(nh)))

    for h in range(nh):
        o_lat_t = acc_ref[:, lanes(h)] / jnp.sum(sums[h], axis=0, keepdims=True)
        o_ref[:, h * DSA_HEAD_DIM:(h + 1) * DSA_HEAD_DIM] = _dot_tn(o_lat_t, wuv_ref[h])


def _dsa(q_idx, w_idx, q_lat, k_idx, ckv_n, ckv_t, w_uv, tril, b_, s_):
    nb = s_ // Q_BLOCK
    t = b_ * s_
    top_k = min(TOPK_MAX, s_ // 4)
    blk = lambda n: pl.BlockSpec((Q_BLOCK, n), lambda bi, i: (bi * nb + i, 0))
    blk3 = lambda m, n: pl.BlockSpec((1, m, n), lambda bi, i: (bi * nb + i, 0, 0))
    blk4 = lambda n: pl.BlockSpec((1, DSA_HEADS, Q_BLOCK, n), lambda bi, i: (bi * nb + i, 0, 0, 0))
    seq = lambda n: pl.BlockSpec((s_, n), lambda bi, i: (bi, 0))
    stacked = DSA_HEADS * Q_BLOCK
    return pl.pallas_call(
        functools.partial(_dsa_kernel, top_k),
        grid=(b_, nb),
        in_specs=[blk4(IDX_DIM), blk3(IDX_HEADS, Q_BLOCK), blk4(DSA_KV_RANK),
                  seq(IDX_DIM), seq(DSA_KV_RANK),
                  pl.BlockSpec((nb, DSA_KV_RANK, Q_BLOCK), lambda bi, i: (bi, 0, 0)),
                  _const_spec(w_uv.shape), _const_spec(tril.shape)],
        out_specs=blk(DSA_WIDTH),
        out_shape=jax.ShapeDtypeStruct((t, DSA_WIDTH), F32),
        scratch_shapes=[pltpu.VMEM((nb, Q_BLOCK, Q_BLOCK), jnp.int32),
                        pltpu.VMEM((nb, Q_BLOCK, stacked), F32),
                        pltpu.VMEM((DSA_KV_RANK, stacked), F32)],
        compiler_params=_params("parallel", "arbitrary"),
        name="dsa_attn",
    )(q_idx, w_idx, q_lat, k_idx, ckv_n, ckv_t, w_uv, tril)


SC_TILE = 512


def _sconv_kernel(p_ref, w_ref, b_ref, o_ref, carry_ref):
    @pl.when(pl.program_id(1) == 0)
    def _():
        carry_ref[...] = jnp.zeros_like(carry_ref)

    h = p_ref[:, :SC_WIDTH]
    gate_b = p_ref[:, SC_WIDTH:2 * SC_WIDTH]
    gate_c = p_ref[:, 2 * SC_WIDTH:]
    u = gate_c * h
    carry = carry_ref[...]
    y = u * w_ref[2:3, :] + _shift_rows(u, carry, 1) * w_ref[1:2, :] + _shift_rows(u, carry, 2) * w_ref[0:1, :]
    carry_ref[...] = u[SC_TILE - 8:, :]
    o_ref[...] = gate_b * (y + b_ref[...])


def _sconv(p_sc, b_, s_, conv_w, conv_b):
    t = p_sc.shape[0]
    nt = s_ // SC_TILE
    return pl.pallas_call(
        _sconv_kernel,
        grid=(b_, nt),
        in_specs=[pl.BlockSpec((SC_TILE, 3 * SC_WIDTH), lambda bi, i: (bi * nt + i, 0)),
                  _const_spec((8, SC_WIDTH)), _const_spec((1, SC_WIDTH))],
        out_specs=pl.BlockSpec((SC_TILE, SC_WIDTH), lambda bi, i: (bi * nt + i, 0)),
        out_shape=jax.ShapeDtypeStruct((t, SC_WIDTH), F32),
        scratch_shapes=[pltpu.VMEM((8, SC_WIDTH), F32)],
        compiler_params=_params("parallel", "arbitrary"),
        name="short_conv",
    )(p_sc, jnp.pad(conv_w, ((0, 8 - SC_KERNEL), (0, 0))), conv_b.reshape(1, -1))


def _xattn_kernel(x_ref, gq_ref, wq_ref, k_ref, v_ref, wo_ref, go_ref, o_ref, att_ref):
    x = x_ref[...]
    q = jnp.dot(_rms(x, gq_ref[...]).astype(BF16), wq_ref[...], preferred_element_type=F32)
    for h in range(XA_HEADS):
        sl = slice(h * XA_HEAD_DIM, (h + 1) * XA_HEAD_DIM)
        s = _dot_nt(q[:, sl], k_ref[:, sl]) * (XA_HEAD_DIM ** -0.5)
        s = s - jnp.max(s, axis=-1, keepdims=True)
        p = jnp.exp(s)
        p = p / jnp.sum(p, axis=-1, keepdims=True)
        att_ref[:, sl] = _dot(p, v_ref[:, sl])
    hout = jnp.dot(att_ref[...].astype(BF16), wo_ref[...], preferred_element_type=F32)
    o_ref[...] = x + _rms(hout, go_ref[...])


def _xattn(x, g_q, wq, k_mem, v_mem, wo, g_o, b_, s_):
    t = x.shape[0]
    nt = s_ // ROW_TILE
    row = pl.BlockSpec((ROW_TILE, D_MODEL), lambda bi, i: (bi * nt + i, 0))
    mem = pl.BlockSpec((MEM_LEN, XA_WIDTH), lambda bi, i: (bi, 0))
    return pl.pallas_call(
        _xattn_kernel,
        grid=(b_, nt),
        in_specs=[row, _const_spec((1, D_MODEL)), _const_spec(wq.shape), mem, mem,
                  _const_spec(wo.shape), _const_spec((1, D_MODEL))],
        out_specs=row,
        out_shape=jax.ShapeDtypeStruct((t, D_MODEL), F32),
        scratch_shapes=[pltpu.VMEM((ROW_TILE, XA_WIDTH), F32)],
        compiler_params=_params("parallel", "parallel"),
        name="mem_xattn",
    )(x, g_q.reshape(1, -1), wq, k_mem, v_mem, wo, g_o.reshape(1, -1))


def _block_diag(n_blocks, size, value):
    return np.kron(np.eye(n_blocks, dtype=np.float32), np.full((size, size), value, np.float32))


def _rope_tables(s_):
    half = RT_DIM // 2
    inv_freq = RT_ROPE_BASE ** (-jnp.arange(half, dtype=F32) / half)
    ang = jnp.arange(s_).astype(F32)[:, None] * inv_freq[None, :]
    cos, sin = jnp.cos(ang), jnp.sin(ang)
    return jnp.concatenate([cos, cos], axis=-1), jnp.concatenate([-sin, sin], axis=-1)


def kernel(x, mem, norm_g, mem_norm_g, ffn_w_gate, ffn_w_up, ffn_w_down, xa_wq, xa_wk, xa_wv, xa_wo, ev_w_in, ev_w_out, rw_mu, rw_w0, rw_w2, rw_a0, rw_a2, rw_g2, rw_k_k, rw_k_a, rw_r_k, rw_ln_g, rw_ln_b, rt_gn_g, rt_gn_b, od_w_in, od_w_out, dsa_q_norm_g, dsa_kv_norm_g, dsa_w_uq, dsa_w_uk, dsa_w_uv, dsa_w_qi, sc_conv_w, sc_conv_b):
    b_, s_, d_ = x.shape
    depth = norm_g.shape[0]
    t = b_ * s_
    bf = lambda w: w.astype(BF16)

    ones_blk = jnp.asarray(_block_diag(RW_HEADS, RW_HEAD_DIM, 1.0), BF16)
    ones_pair = jnp.asarray(_block_diag(2, RW_HEAD_DIM, 1.0), BF16)
    tri_rw = jnp.asarray(np.tril(np.ones((RW_CHUNK, RW_CHUNK), np.float32)), BF16)
    tril_dsa = jnp.asarray(np.tril(np.ones((Q_BLOCK, Q_BLOCK), np.float32)), BF16)
    cos2, sin2 = _rope_tables(s_)

    xf = x.reshape(t, d_)
    mem_f = mem.reshape(b_ * MEM_LEN, d_)
    for l in range(depth):
        ng = norm_g[l]
        i = l // 2
        xf = _ffn(xf, ng[0], bf(ffn_w_gate[l, 0]), bf(ffn_w_up[l, 0]), bf(ffn_w_down[l, 0]), ng[1])
        if l % 2 == 0:
            w_in = bf(ev_w_in[i])
            p_rw, p_rt = _norm_proj(xf, ng[2], [w_in[:, :RW_IN], w_in[:, RW_IN:]], "ev_in_proj")
            r, k, v, ld, a, b, g = _rw_prep(p_rw, b_, s_, rw_mu[i], rw_w0[i], rw_w2[i], rw_a0[i], rw_a2[i],
                                            rw_g2[i], rw_k_k[i], rw_k_a[i], ones_blk)
            y_a = _rw_scan(r, k, v, ld, a, b, g, b_, s_, rw_r_k[i], rw_ln_g[i], rw_ln_b[i],
                           ones_pair, tri_rw)
            y_b = _retention(p_rt, b_, s_, cos2, sin2, rt_gn_g[i], rt_gn_b[i])
            w_out = bf(ev_w_out[i])
            xf = _proj_res([y_a, y_b], [w_out[:RW_WIDTH], w_out[RW_WIDTH:]], ng[3], xf, "ev_out_proj")
        else:
            w_in = od_w_in[i]
            kw_w = jnp.pad(w_in[:, DSA_Q_RANK + DSA_KV_RANK:DSA_IN], ((0, 0), (0, LANES - IDX_DIM - IDX_HEADS)))
            c_q, c_kv, kw, p_sc = _norm_proj(
                xf, ng[2], [bf(w_in[:, :DSA_Q_RANK]), bf(w_in[:, DSA_Q_RANK:DSA_Q_RANK + DSA_KV_RANK]),
                            bf(kw_w), bf(w_in[:, DSA_IN:])], "od_in_proj")
            ckv_n, ckv_t, k_idx, w_idx, q_lat, q_idx = _dsa_prep(
                c_q, c_kv, kw, dsa_q_norm_g[i], dsa_kv_norm_g[i],
                bf(dsa_w_uq[i].reshape(DSA_Q_RANK, DSA_WIDTH)), bf(dsa_w_uk[i]),
                bf(dsa_w_qi[i].reshape(DSA_Q_RANK, IDX_HEADS * IDX_DIM)))
            y_c = _dsa(q_idx, w_idx, q_lat, k_idx, ckv_n, ckv_t, bf(dsa_w_uv[i]), tril_dsa, b_, s_)
            y_d = _sconv(p_sc, b_, s_, sc_conv_w[i], sc_conv_b[i])
            w_out = bf(od_w_out[i])
            xf = _proj_res([y_c, y_d], [w_out[:DSA_WIDTH], w_out[DSA_WIDTH:]], ng[3], xf, "od_out_proj")
        k_mem, v_mem = _norm_proj(mem_f, mem_norm_g, [bf(xa_wk[l]), bf(xa_wv[l])], "mem_kv_proj")
        xf = _xattn(xf, ng[4], bf(xa_wq[l]), k_mem, v_mem, bf(xa_wo[l]), ng[5], b_, s_)
        xf = _ffn(xf, ng[6], bf(ffn_w_gate[l, 1]), bf(ffn_w_up[l, 1]), bf(ffn_w_down[l, 1]), ng[7])
    return xf.reshape(b_, s_, d_)
```

```python
import functools
import math

import numpy as np
import jax
import jax.numpy as jnp
from jax import lax
from jax.experimental import pallas as pl
from jax.experimental.pallas import tpu as pltpu

F32 = jnp.float32
BF16 = jnp.bfloat16

D_MODEL = 1024
D_FF = 2816
EPS = 1e-6
MEM_LEN = 256
RW_HEADS = 8
RW_HEAD_DIM = 64
RW_WIDTH = RW_HEADS * RW_HEAD_DIM
RW_DECAY_RANK = 64
RW_AAA_RANK = 64
RW_GATE_RANK = 128
RW_LN_EPS = 64e-5
RW_IN = 3 * RW_WIDTH + RW_DECAY_RANK + RW_AAA_RANK + RW_GATE_RANK
RW_CHUNK = 64
RW_ROWS = 4
RT_HEADS = 4
RT_DIM = 128
RT_WIDTH = RT_HEADS * RT_DIM
RT_CHUNK = 128
RT_ROPE_BASE = 10000.0
RT_IN = 4 * RT_WIDTH
DSA_HEADS = 8
DSA_HEAD_DIM = 64
DSA_WIDTH = DSA_HEADS * DSA_HEAD_DIM
DSA_Q_RANK = 256
DSA_KV_RANK = 128
IDX_HEADS = 8
IDX_DIM = 64
TOPK_MAX = 256
Q_BLOCK = 128
DSA_IN = DSA_Q_RANK + DSA_KV_RANK + IDX_DIM + IDX_HEADS
SC_WIDTH = 512
SC_KERNEL = 3
XA_HEADS = 4
XA_HEAD_DIM = 128
XA_WIDTH = XA_HEADS * XA_HEAD_DIM

LANES = 128
ROW_TILE = 512
VMEM_LIMIT = 56 * 1024 * 1024
INT_MIN = -2 ** 31
NEG_BIG = -1e30


def _params(*sem):
    return pltpu.CompilerParams(dimension_semantics=sem, vmem_limit_bytes=VMEM_LIMIT)


def _rms(x, g):
    return x * lax.rsqrt(jnp.mean(x * x, axis=-1, keepdims=True) + EPS) * g


def _dot(a, b):
    return jnp.dot(a.astype(BF16), b.astype(BF16), preferred_element_type=F32)


def _dot_nt(a, b):
    return lax.dot_general(a.astype(BF16), b.astype(BF16), (((1,), (1,)), ((), ())),
                           preferred_element_type=F32)


def _dot_tn(a, b):
    return lax.dot_general(a.astype(BF16), b.astype(BF16), (((0,), (0,)), ((), ())),
                           preferred_element_type=F32)


def _split3(x):
    hi = x.astype(BF16)
    r1 = x - hi.astype(F32)
    mid = r1.astype(BF16)
    lo = (r1 - mid.astype(F32)).astype(BF16)
    return hi, mid, lo


def _dot_exact_rhs(x, w_bf16):
    hi, mid, lo = _split3(x)
    out = jnp.dot(hi, w_bf16, preferred_element_type=F32)
    out += jnp.dot(mid, w_bf16, preferred_element_type=F32)
    out += jnp.dot(lo, w_bf16, preferred_element_type=F32)
    return out


def _dot_exact_lhs(w_bf16, x):
    hi, mid, lo = _split3(x)
    out = jnp.dot(w_bf16, hi, preferred_element_type=F32)
    out += jnp.dot(w_bf16, mid, preferred_element_type=F32)
    out += jnp.dot(w_bf16, lo, preferred_element_type=F32)
    return out


def _const_spec(shape):
    nd = len(shape)
    return pl.BlockSpec(shape, lambda *_: (0,) * nd, pipeline_mode=pl.Buffered(1))


FF_CHUNK = 256


def _ffn_kernel(x_ref, gin_ref, wg_ref, wu_ref, wd_ref, gout_ref, o_ref, acc_ref):
    x = x_ref[...]
    xb = _rms(x, gin_ref[...]).astype(BF16)
    for c in range(D_FF // FF_CHUNK):
        sl = slice(c * FF_CHUNK, (c + 1) * FF_CHUNK)
        g = jnp.dot(xb, wg_ref[:, sl], preferred_element_type=F32)
        u = jnp.dot(xb, wu_ref[:, sl], preferred_element_type=F32)
        h = (g * jax.nn.sigmoid(g) * u).astype(BF16)
        part = jnp.dot(h, wd_ref[sl, :], preferred_element_type=F32)
        if c == 0:
            acc_ref[...] = part
        else:
            acc_ref[...] += part
    o_ref[...] = x + 0.5 * _rms(acc_ref[...], gout_ref[...])


def _ffn(x, g_in, wg, wu, wd, g_out):
    t = x.shape[0]
    row = pl.BlockSpec((ROW_TILE, D_MODEL), lambda i: (i, 0))
    return pl.pallas_call(
        _ffn_kernel,
        grid=(t // ROW_TILE,),
        in_specs=[row, _const_spec((1, D_MODEL)), _const_spec((D_MODEL, D_FF)),
                  _const_spec((D_MODEL, D_FF)), _const_spec((D_FF, D_MODEL)), _const_spec((1, D_MODEL))],
        out_specs=row,
        out_shape=jax.ShapeDtypeStruct((t, D_MODEL), F32),
        scratch_shapes=[pltpu.VMEM((ROW_TILE, D_MODEL), F32)],
        compiler_params=_params("parallel"),
        name="ffn_half",
    )(x, g_in.reshape(1, -1), wg, wu, wd, g_out.reshape(1, -1))


PROJ_CHUNK = 512


def _norm_proj_kernel(n_out, x_ref, g_ref, *refs):
    xb = _rms(x_ref[...], g_ref[...]).astype(BF16)
    for w_ref, o_ref in zip(refs[:n_out], refs[n_out:]):
        n = w_ref.shape[1]
        for c in range(0, n, PROJ_CHUNK):
            sl = slice(c, min(c + PROJ_CHUNK, n))
            o_ref[:, sl] = jnp.dot(xb, w_ref[:, sl], preferred_element_type=F32)


def _norm_proj(x, g, ws, name):
    t, d = x.shape
    row = lambda n: pl.BlockSpec((ROW_TILE, n), lambda i: (i, 0))
    return pl.pallas_call(
        functools.partial(_norm_proj_kernel, len(ws)),
        grid=(t // ROW_TILE,),
        in_specs=[row(d), _const_spec((1, d))] + [_const_spec(w.shape) for w in ws],
        out_specs=[row(w.shape[1]) for w in ws],
        out_shape=[jax.ShapeDtypeStruct((t, w.shape[1]), F32) for w in ws],
        compiler_params=_params("parallel"),
        name=name,
    )(x, g.reshape(1, -1), *ws)


def _proj_res_kernel(n_in, *refs):
    y_refs, w_refs = refs[:n_in], refs[n_in:2 * n_in]
    g_ref, x_ref, o_ref = refs[2 * n_in:]
    h = None
    for y_ref, w_ref in zip(y_refs, w_refs):
        part = jnp.dot(y_ref[...].astype(BF16), w_ref[...], preferred_element_type=F32)
        h = part if h is None else h + part
    o_ref[...] = x_ref[...] + _rms(h, g_ref[...])


def _proj_res(ys, ws, g, x, name):
    t, d = x.shape
    row = lambda n: pl.BlockSpec((ROW_TILE, n), lambda i: (i, 0))
    return pl.pallas_call(
        functools.partial(_proj_res_kernel, len(ys)),
        grid=(t // ROW_TILE,),
        in_specs=[row(y.shape[1]) for y in ys] + [_const_spec(w.shape) for w in ws]
                 + [_const_spec((1, d)), row(d)],
        out_specs=row(d),
        out_shape=jax.ShapeDtypeStruct((t, d), F32),
        compiler_params=_params("parallel"),
        name=name,
    )(*ys, *ws, g.reshape(1, -1), x)


RW_TILE = 512


def _shift_rows(x, carry, n):
    rolled = pltpu.roll(x, n, 0)
    row = lax.broadcasted_iota(jnp.int32, x.shape, 0)
    out = rolled
    for i in range(n):
        out = jnp.where(row == i, carry[8 - n + i:8 - n + i + 1, :], out)
    return out


def _rw_prep_kernel(p_ref, mu_ref, w0_ref, w2_ref, a0_ref, a2_ref, g2_ref, kk_ref, ka_ref, ones_ref,
                    r_ref, k_ref, v_ref, ld_ref, a_ref, b_ref, g_ref, carry_ref):
    @pl.when(pl.program_id(1) == 0)
    def _():
        carry_ref[...] = jnp.zeros_like(carry_ref)

    p = p_ref[...]
    prev = _shift_rows(p, carry_ref[...], 1)
    carry_ref[...] = p[RW_TILE - 8:, :]
    xm = p + (prev - p) * mu_ref[...]
    w = RW_WIDTH
    r, k, v = xm[:, :w], xm[:, w:2 * w], xm[:, 2 * w:3 * w]
    xw = xm[:, 3 * w:3 * w + RW_DECAY_RANK]
    xa = xm[:, 3 * w + RW_DECAY_RANK:3 * w + RW_DECAY_RANK + RW_AAA_RANK]
    xg = xm[:, 3 * w + RW_DECAY_RANK + RW_AAA_RANK:]
    wlog = -jax.nn.softplus(-(w0_ref[...] + _dot(jnp.tanh(xw), w2_ref[...]))) - 0.5
    a = jax.nn.sigmoid(a0_ref[...] + _dot(xa, a2_ref[...]))
    kk = k * kk_ref[...]
    ss = _dot_exact_rhs(kk * kk, ones_ref[...])
    kk = kk / jnp.maximum(jnp.sqrt(ss), 1e-12)
    r_ref[...] = r
    k_ref[...] = k * (1.0 + (a - 1.0) * ka_ref[...])
    v_ref[...] = v
    ld_ref[...] = -jnp.exp(wlog)
    a_ref[...] = -kk
    b_ref[...] = kk * a
    g_ref[...] = _dot(jax.nn.sigmoid(xg), g2_ref[...])


def _rw_prep(p_rw, b_, s_, mu, w0, w2, a0, a2, g2, k_k, k_a, ones_blk):
    t = p_rw.shape[0]
    nt = s_ // RW_TILE
    row = lambda n: pl.BlockSpec((RW_TILE, n), lambda b, i: (b * nt + i, 0))
    vec = lambda a: a.reshape(1, -1)
    outs = [jax.ShapeDtypeStruct((t, RW_WIDTH), F32)] * 7
    return pl.pallas_call(
        _rw_prep_kernel,
        grid=(b_, nt),
        in_specs=[row(RW_IN), _const_spec((1, RW_IN)), _const_spec((1, RW_WIDTH)),
                  _const_spec(w2.shape), _const_spec((1, RW_WIDTH)), _const_spec(a2.shape),
                  _const_spec(g2.shape), _const_spec((1, RW_WIDTH)), _const_spec((1, RW_WIDTH)),
                  _const_spec(ones_blk.shape)],
        out_specs=[row(RW_WIDTH)] * 7,
        out_shape=outs,
        scratch_shapes=[pltpu.VMEM((8, RW_IN), F32)],
        compiler_params=_params("parallel", "arbitrary"),
        name="rwkv_prep",
    )(p_rw, vec(mu), vec(w0), w2, vec(a0), a2, g2, vec(k_k), vec(k_a), ones_blk)


def _rw_scan_kernel(r_ref, k_ref, v_ref, ld_ref, a_ref, b_ref, g_ref, tri_ref, ones_ref,
                    rk_ref, lng_ref, lnb_ref, o_ref, state_ref):
    c_ = RW_CHUNK
    n_ = RW_HEAD_DIM

    @pl.when(pl.program_id(1) == 0)
    def _():
        state_ref[...] = jnp.zeros_like(state_ref)

    c2 = 2 * c_
    row = lax.broadcasted_iota(jnp.int32, (c2, c2), 0)
    col = lax.broadcasted_iota(jnp.int32, (c2, c2), 1)
    same_head = (row >= c_) == (col >= c_)
    strict = jnp.logical_and(same_head, row > col)
    incl = jnp.logical_and(same_head, row >= col)
    left = lax.broadcasted_iota(jnp.int32, (c_, LANES), 1) < n_

    def block_diag(x):
        return jnp.concatenate([jnp.where(left, x, 0.0), jnp.where(left, 0.0, x)], axis=0)

    pairs = range(RW_HEADS // 2)
    sls = [slice(p * LANES, (p + 1) * LANES) for p in pairs]
    ones2 = ones_ref[...]

    def head_sums(xs):
        parts = [part[:, sl] for x in xs for part in _split3(x) for sl in sls]
        prod = jnp.dot(jnp.concatenate(parts, axis=0), ones2, preferred_element_type=F32)
        outs = []
        for i in range(len(xs)):
            slabs = []
            for p in pairs:
                rows = [((3 * i + j) * len(sls) + p) * c_ for j in range(3)]
                slabs.append(prod[rows[0]:rows[0] + c_] + prod[rows[1]:rows[1] + c_] + prod[rows[2]:rows[2] + c_])
            outs.append(jnp.concatenate(slabs, axis=1))
        return outs

    def chunk_program(bb):
        ld = ld_ref[bb]
        r = r_ref[bb]
        k = k_ref[bb]
        v = v_ref[bb]
        cum = _dot_exact_lhs(tri_ref[...], ld)
        mid = cum[c_ // 2 - 1:c_ // 2, :]
        e_in = jnp.exp(cum - mid)
        e_out = jnp.exp(mid - cum)
        r_t = r * e_in
        a_t = a_ref[bb] * jnp.exp(cum - ld - mid)
        b_t = b_ref[bb] * e_out
        k_t = k * e_out
        e_mid = jnp.exp(mid)
        w_all = jnp.exp(cum[c_ - 1:c_, :])
        w_tail = jnp.exp(cum[c_ - 1:c_, :] - mid)
        yield
        s0 = [state_ref[bb, p] for p in pairs]
        ar = [jnp.concatenate([block_diag(a_t[:, sl]), block_diag(r_t[:, sl])], axis=0) for sl in sls]
        bk = [jnp.concatenate([block_diag(b_t[:, sl]), block_diag(k_t[:, sl])], axis=0) for sl in sls]
        vb = [block_diag(v[:, sl]) for sl in sls]
        m1 = [_dot_nt(ar[p], bk[p]) for p in pairs]
        m2 = [_dot_nt(ar[p] * e_mid[:, sls[p]], s0[p]) for p in pairs]
        yield
        l_ab = [jnp.where(strict, m[:c2, :c2], 0.0) for m in m1]
        l_ak = [jnp.where(strict, m[:c2, c2:], 0.0) for m in m1]
        l_r = [jnp.concatenate([jnp.where(incl, m[c2:, :c2], 0.0), jnp.where(incl, m[c2:, c2:], 0.0)], axis=1)
               for m in m1]
        u = [m2[p][:c2] + _dot(l_ak[p], vb[p]) for p in pairs]
        pw = l_ab
        yield
        n_steps = int(math.log2(c_))
        for step in range(n_steps):
            if step < n_steps - 1:
                prod = [_dot(pw[p], jnp.concatenate([pw[p], u[p]], axis=1)) for p in pairs]
                pw = [q[:, :c2] for q in prod]
                u = [u[p] + prod[p][:, c2:] for p in pairs]
            else:
                u = [u[p] + _dot(pw[p], u[p]) for p in pairs]
            yield
        uv = [jnp.concatenate([u[p], vb[p]], axis=0) for p in pairs]
        ys = [m2[p][c2:] + _dot(l_r[p], uv[p]) for p in pairs]
        upd = [_dot_tn(uv[p], bk[p]) for p in pairs]
        yield
        for p in pairs:
            state_ref[bb, p] = s0[p] * w_all[:, sls[p]] + upd[p] * w_tail[:, sls[p]]
        y = jnp.concatenate([ys[p][:c_] + ys[p][c_:] for p in pairs], axis=1)
        sum_y, sum_rk = head_sums([y, r * k * rk_ref[...]])
        d = y - sum_y * (1.0 / n_)
        yield
        var = head_sums([d * d])[0] * (1.0 / n_)
        yn = d * lax.rsqrt(var + RW_LN_EPS) * lng_ref[...] + lnb_ref[...]
        o_ref[bb] = (yn + sum_rk * v) * g_ref[bb]
        yield

    for _ in zip(*[chunk_program(bb) for bb in range(RW_ROWS)]):
        pass


def _rw_scan(r, k, v, ld, a, b, g, b_, s_, r_k, ln_g, ln_b, ones_pair, tri):
    t = r.shape[0]
    nc = s_ // RW_CHUNK
    row = pl.BlockSpec((RW_ROWS, RW_CHUNK, RW_WIDTH), lambda bi, i: (bi, i, 0))
    vec = lambda x: x.reshape(1, -1)
    seq = lambda x: x.reshape(b_, s_, RW_WIDTH)
    out = pl.pallas_call(
        _rw_scan_kernel,
        grid=(b_ // RW_ROWS, nc),
        in_specs=[row] * 7 + [_const_spec(tri.shape), _const_spec(ones_pair.shape),
                              _const_spec((1, RW_WIDTH)), _const_spec((1, RW_WIDTH)), _const_spec((1, RW_WIDTH))],
        out_specs=row,
        out_shape=jax.ShapeDtypeStruct((b_, s_, RW_WIDTH), F32),
        scratch_shapes=[pltpu.VMEM((RW_ROWS, RW_HEADS // 2, 2 * RW_HEAD_DIM, 2 * RW_HEAD_DIM), F32)],
        compiler_params=_params("parallel", "arbitrary"),
        name="rwkv_scan",
    )(seq(r), seq(k), seq(v), seq(ld), seq(a), seq(b), seq(g), tri, ones_pair, vec(r_k), vec(ln_g), vec(ln_b))
    return out.reshape(t, RW_WIDTH)


def _rt_log_gamma(h):
    return math.log1p(-(2.0 ** (-5.0 - h)))


def _retention_kernel(p_ref, cos_ref, sin_ref, gng_ref, gnb_ref, o_ref, state_ref):
    c_ = RT_CHUNK

    @pl.when(pl.program_id(1) == 0)
    def _():
        state_ref[...] = jnp.zeros_like(state_ref)

    cos = cos_ref[...]
    sin = sin_ref[...]
    row = lax.broadcasted_iota(jnp.int32, (c_, c_), 0).astype(F32)
    col = lax.broadcasted_iota(jnp.int32, (c_, c_), 1).astype(F32)
    rel = row - col
    for h in range(RT_HEADS):
        lg = _rt_log_gamma(h)
        sl = slice(h * RT_DIM, (h + 1) * RT_DIM)
        q = p_ref[:, sl]
        k = p_ref[:, RT_WIDTH + h * RT_DIM:RT_WIDTH + (h + 1) * RT_DIM]
        v = p_ref[:, 2 * RT_WIDTH + h * RT_DIM:2 * RT_WIDTH + (h + 1) * RT_DIM]
        gate = p_ref[:, 3 * RT_WIDTH + h * RT_DIM:3 * RT_WIDTH + (h + 1) * RT_DIM]
        q = q * cos + pltpu.roll(q, RT_DIM // 2, 1) * sin
        k = (k * cos + pltpu.roll(k, RT_DIM // 2, 1) * sin) * (RT_DIM ** -0.5)
        decay = jnp.where(rel >= 0, jnp.exp(jnp.maximum(rel, 0.0) * lg), 0.0)
        scores = _dot_nt(q, k) * decay
        s0 = state_ref[h]
        xi = jnp.exp((row + 1.0) * lg)
        o = _dot(scores, v) + _dot(q, s0) * xi
        zeta = jnp.exp((c_ - 1.0 - row) * lg)
        state_ref[h] = s0 * math.exp(c_ * lg) + _dot_tn(k * zeta, v)
        mu = jnp.mean(o, axis=-1, keepdims=True)
        d = o - mu
        var = jnp.mean(d * d, axis=-1, keepdims=True)
        on = d * lax.rsqrt(var + EPS) * gng_ref[:, sl] + gnb_ref[:, sl]
        o_ref[:, sl] = gate * jax.nn.sigmoid(gate) * on


def _retention(p_rt, b_, s_, cos2, sin2, gn_g, gn_b):
    t = p_rt.shape[0]
    nc = s_ // RT_CHUNK
    return pl.pallas_call(
        _retention_kernel,
        grid=(b_, nc),
        in_specs=[pl.BlockSpec((RT_CHUNK, RT_IN), lambda bi, i: (bi * nc + i, 0)),
                  pl.BlockSpec((RT_CHUNK, RT_DIM), lambda bi, i: (i, 0)),
                  pl.BlockSpec((RT_CHUNK, RT_DIM), lambda bi, i: (i, 0)),
                  _const_spec((1, RT_WIDTH)), _const_spec((1, RT_WIDTH))],
        out_specs=pl.BlockSpec((RT_CHUNK, RT_WIDTH), lambda bi, i: (bi * nc + i, 0)),
        out_shape=jax.ShapeDtypeStruct((t, RT_WIDTH), F32),
        scratch_shapes=[pltpu.VMEM((RT_HEADS, RT_DIM, RT_DIM), F32)],
        compiler_params=_params("parallel", "arbitrary"),
        name="retention",
    )(p_rt, cos2, sin2, gn_g.reshape(1, -1), gn_b.reshape(1, -1))


DSA_BLOCKS_PER_TILE = ROW_TILE // Q_BLOCK


def _dsa_prep_kernel(cq_ref, ckv_ref, kw_ref, qg_ref, kvg_ref, wuq_ref, wuk_ref, wqi_ref,
                     ckvn_ref, ckvt_ref, kidx_ref, widx_ref, qlat_ref, qidx_ref):
    cq = _rms(cq_ref[...], qg_ref[...]).astype(BF16)
    ckvn = _rms(ckv_ref[...], kvg_ref[...])
    ckvn_ref[...] = ckvn.astype(BF16)
    kw = kw_ref[...]
    kidx_ref[...] = kw[:, :IDX_DIM].astype(BF16)
    q = jnp.dot(cq, wuq_ref[...], preferred_element_type=F32)
    qi = jnp.dot(cq, wqi_ref[...], preferred_element_type=F32)
    for blk in range(DSA_BLOCKS_PER_TILE):
        rows = slice(blk * Q_BLOCK, (blk + 1) * Q_BLOCK)
        ckvt_ref[blk] = ckvn[rows].T.astype(BF16)
        widx_ref[blk] = kw[rows].T[IDX_DIM:IDX_DIM + IDX_HEADS] * ((IDX_HEADS * IDX_DIM) ** -0.5)
    for h in range(DSA_HEADS):
        qh = q[:, h * DSA_HEAD_DIM:(h + 1) * DSA_HEAD_DIM]
        ql = (_dot(qh, wuk_ref[h]) * (DSA_HEAD_DIM ** -0.5)).astype(BF16)
        qih = qi[:, h * IDX_DIM:(h + 1) * IDX_DIM].astype(BF16)
        for blk in range(DSA_BLOCKS_PER_TILE):
            rows = slice(blk * Q_BLOCK, (blk + 1) * Q_BLOCK)
            qlat_ref[blk, h] = ql[rows]
            qidx_ref[blk, h] = qih[rows]


def _dsa_prep(c_q, c_kv, kw, q_g, kv_g, w_uq, w_uk, w_qi):
    t = c_q.shape[0]
    row = lambda n: pl.BlockSpec((ROW_TILE, n), lambda i: (i, 0))
    blk3 = lambda m, n: pl.BlockSpec((DSA_BLOCKS_PER_TILE, m, n), lambda i: (i, 0, 0))
    blk4 = lambda n: pl.BlockSpec((DSA_BLOCKS_PER_TILE, DSA_HEADS, Q_BLOCK, n), lambda i: (i, 0, 0, 0))
    return pl.pallas_call(
        _dsa_prep_kernel,
        grid=(t // ROW_TILE,),
        in_specs=[row(DSA_Q_RANK), row(DSA_KV_RANK), row(LANES), _const_spec((1, DSA_Q_RANK)),
                  _const_spec((1, DSA_KV_RANK)),
                  _const_spec(w_uq.shape), _const_spec(w_uk.shape), _const_spec(w_qi.shape)],
        out_specs=[row(DSA_KV_RANK), blk3(DSA_KV_RANK, Q_BLOCK), row(IDX_DIM), blk3(IDX_HEADS, Q_BLOCK),
                   blk4(DSA_KV_RANK), blk4(IDX_DIM)],
        out_shape=[jax.ShapeDtypeStruct((t, DSA_KV_RANK), BF16),
                   jax.ShapeDtypeStruct((t // Q_BLOCK, DSA_KV_RANK, Q_BLOCK), BF16),
                   jax.ShapeDtypeStruct((t, IDX_DIM), BF16),
                   jax.ShapeDtypeStruct((t // Q_BLOCK, IDX_HEADS, Q_BLOCK), F32),
                   jax.ShapeDtypeStruct((t // Q_BLOCK, DSA_HEADS, Q_BLOCK, DSA_KV_RANK), BF16),
                   jax.ShapeDtypeStruct((t // Q_BLOCK, IDX_HEADS, Q_BLOCK, IDX_DIM), BF16)],
        compiler_params=_params("parallel"),
        name="dsa_prep",
    )(c_q, c_kv, kw, q_g.reshape(1, -1), kv_g.reshape(1, -1), w_uq, w_uk, w_qi)


def _fold_rows(x, op):
    return functools.reduce(op, [x[i:i + 8] for i in range(0, x.shape[0], 8)])


def _dsa_kernel(top_k, qidx_ref, widx_ref, qlat_ref, kidx_ref, ckv_ref, ckvt_ref, wuv_ref, tril_ref, o_ref,
                key_ref, s_ref, acc_ref):
    qb = Q_BLOCK
    nh = DSA_HEADS
    j = pl.program_id(1)
    n_chunks = j + 1

    keyi = lax.broadcasted_iota(jnp.int32, (qb, qb), 0)
    qryi = lax.broadcasted_iota(jnp.int32, (qb, qb), 1)
    lanes = lambda h: slice(h * qb, (h + 1) * qb)

    def key_chunk(ref, c):
        return ref[pl.ds(pl.multiple_of(c * qb, qb), qb), :]

    w_idx = widx_ref[0]
    q_idx = qidx_ref[0].reshape(IDX_HEADS * qb, IDX_DIM)
    q_lat = qlat_ref[0].reshape(nh * qb, DSA_KV_RANK)

    n_pairs = (n_chunks + 1) // 2

    def score_chunk(c):
        logits = lax.dot_general(key_chunk(kidx_ref, c), q_idx, (((1,), (1,)), ((), ())),
                                 preferred_element_type=F32)
        score = jnp.zeros((qb, qb), F32)
        for h in range(IDX_HEADS):
            score = score + jnp.maximum(logits[:, lanes(h)], 0.0) * w_idx[h:h + 1, :]
        bits = lax.bitcast_convert_type(score, jnp.int32)
        key = jnp.where(bits < 0, bits ^ jnp.int32(0x7FFFFFFF), bits)
        causal = keyi + c * qb <= qryi + j * qb
        key_ref[c] = jnp.where(causal, key, jnp.int32(INT_MIN))

    def score_body(pi, carry):
        score_chunk(2 * pi)
        score_chunk(2 * pi + 1)
        return carry

    lax.fori_loop(0, n_pairs, score_body, 0)

    def count(pred):
        def body(pi, acc):
            return acc + jnp.where(pred(key_ref[2 * pi]), 1, 0) + jnp.where(pred(key_ref[2 * pi + 1]), 1, 0)
        acc = lax.fori_loop(0, n_pairs, body, jnp.zeros((qb, qb), jnp.int32))
        return jnp.sum(_fold_rows(acc, jnp.add), axis=0, keepdims=True)

    thr = jnp.where(count(lambda k: k >= 0) >= top_k, 0, INT_MIN).astype(jnp.int32)

    def bit_body(i, thr):
        cand = thr | lax.shift_left(jnp.int32(1), 30 - i)
        return jnp.where(count(lambda k: k >= cand) >= top_k, cand, thr)

    thr = lax.fori_loop(0, 31, bit_body, thr)
    need = (top_k - count(lambda k: k > thr)).astype(F32)

    def logit_chunk(c, taken):
        key = key_ref[c]
        causal = keyi + c * qb <= qryi + j * qb
        eq = jnp.logical_and(key == thr, causal)
        eq_f = jnp.where(eq, 1.0, 0.0)
        rank = taken + jnp.dot(tril_ref[...], eq_f.astype(BF16), preferred_element_type=F32)
        sel = jnp.logical_or(jnp.logical_and(key > thr, causal), jnp.logical_and(eq, rank <= need))
        s = lax.dot_general(key_chunk(ckv_ref, c), q_lat, (((1,), (1,)), ((), ())),
                            preferred_element_type=F32)
        tops = []
        for h in range(nh):
            sh = jnp.where(sel, s[:, lanes(h)], NEG_BIG)
            s_ref[c, :, lanes(h)] = sh
            tops.append(_fold_rows(sh, jnp.maximum))
        return tops, taken + jnp.sum(eq_f, axis=0, keepdims=True)

    def logit_body(pi, carry):
        taken, tops = carry
        ta, taken = logit_chunk(2 * pi, taken)
        tb, taken = logit_chunk(2 * pi + 1, taken)
        return taken, tuple(jnp.maximum(t, jnp.maximum(a, b)) for t, a, b in zip(tops, ta, tb))

    _, tops = lax.fori_loop(0, n_pairs, logit_body,
                            (jnp.zeros((1, qb), F32), tuple(jnp.full((8, qb), NEG_BIG, F32) for _ in range(nh))))
    tops = [jnp.max(t, axis=0, keepdims=True) for t in tops]

    acc_ref[...] = jnp.zeros_like(acc_ref)

    def value_body(pi, sums):
        ps, new_sums = [], []
        for h in range(nh):
            pa = jnp.exp(s_ref[2 * pi, :, lanes(h)] - tops[h])
            pb = jnp.exp(s_ref[2 * pi + 1, :, lanes(h)] - tops[h])
            new_sums.append(sums[h] + _fold_rows(pa + pb, jnp.add))
            ps.append(jnp.concatenate([pa.astype(BF16), pb.astype(BF16)], axis=0))
        ckvt_pair = jnp.concatenate([ckvt_ref[2 * pi], ckvt_ref[2 * pi + 1]], axis=1)
        acc_ref[...] += jnp.dot(ckvt_pair, jnp.concatenate(ps, axis=1), preferred_element_type=F32)
        return tuple(new_sums)

    sums = lax.fori_loop(0, n_pairs, value_body, tuple(jnp.zeros((8, qb), F32) for _ in range(nh)))

    for h in range(nh):
        o_lat_t = acc_ref[:, lanes(h)] / jnp.sum(sums[h], axis=0, keepdims=True)
        o_ref[:, h * DSA_HEAD_DIM:(h + 1) * DSA_HEAD_DIM] = _dot_tn(o_lat_t, wuv_ref[h])


def _dsa(q_idx, w_idx, q_lat, k_idx, ckv_n, ckv_t, w_uv, tril, b_, s_):
    nb = s_ // Q_BLOCK
    t = b_ * s_
    top_k = min(TOPK_MAX, s_ // 4)
    blk = lambda n: pl.BlockSpec((Q_BLOCK, n), lambda bi, i: (bi * nb + i, 0))
    blk3 = lambda m, n: pl.BlockSpec((1, m, n), lambda bi, i: (bi * nb + i, 0, 0))
    blk4 = lambda n: pl.BlockSpec((1, DSA_HEADS, Q_BLOCK, n), lambda bi, i: (bi * nb + i, 0, 0, 0))
    seq = lambda n: pl.BlockSpec((s_, n), lambda bi, i: (bi, 0))
    stacked = DSA_HEADS * Q_BLOCK
    return pl.pallas_call(
        functools.partial(_dsa_kernel, top_k),
        grid=(b_, nb),
        in_specs=[blk4(IDX_DIM), blk3(IDX_HEADS, Q_BLOCK), blk4(DSA_KV_RANK),
                  seq(IDX_DIM), seq(DSA_KV_RANK),
                  pl.BlockSpec((nb, DSA_KV_RANK, Q_BLOCK), lambda bi, i: (bi, 0, 0)),
                  _const_spec(w_uv.shape), _const_spec(tril.shape)],
        out_specs=blk(DSA_WIDTH),
        out_shape=jax.ShapeDtypeStruct((t, DSA_WIDTH), F32),
        scratch_shapes=[pltpu.VMEM((nb, Q_BLOCK, Q_BLOCK), jnp.int32),
                        pltpu.VMEM((nb, Q_BLOCK, stacked), F32),
                        pltpu.VMEM((DSA_KV_RANK, stacked), F32)],
        compiler_params=_params("parallel", "arbitrary"),
        name="dsa_attn",
    )(q_idx, w_idx, q_lat, k_idx, ckv_n, ckv_t, w_uv, tril)


SC_TILE = 512


def _sconv_kernel(p_ref, w_ref, b_ref, o_ref, carry_ref):
    @pl.when(pl.program_id(1) == 0)
    def _():
        carry_ref[...] = jnp.zeros_like(carry_ref)

    h = p_ref[:, :SC_WIDTH]
    gate_b = p_ref[:, SC_WIDTH:2 * SC_WIDTH]
    gate_c = p_ref[:, 2 * SC_WIDTH:]
    u = gate_c * h
    carry = carry_ref[...]
    y = u * w_ref[2:3, :] + _shift_rows(u, carry, 1) * w_ref[1:2, :] + _shift_rows(u, carry, 2) * w_ref[0:1, :]
    carry_ref[...] = u[SC_TILE - 8:, :]
    o_ref[...] = gate_b * (y + b_ref[...])


def _sconv(p_sc, b_, s_, conv_w, conv_b):
    t = p_sc.shape[0]
    nt = s_ // SC_TILE
    return pl.pallas_call(
        _sconv_kernel,
        grid=(b_, nt),
        in_specs=[pl.BlockSpec((SC_TILE, 3 * SC_WIDTH), lambda bi, i: (bi * nt + i, 0)),
                  _const_spec((8, SC_WIDTH)), _const_spec((1, SC_WIDTH))],
        out_specs=pl.BlockSpec((SC_TILE, SC_WIDTH), lambda bi, i: (bi * nt + i, 0)),
        out_shape=jax.ShapeDtypeStruct((t, SC_WIDTH), F32),
        scratch_shapes=[pltpu.VMEM((8, SC_WIDTH), F32)],
        compiler_params=_params("parallel", "arbitrary"),
        name="short_conv",
    )(p_sc, jnp.pad(conv_w, ((0, 8 - SC_KERNEL), (0, 0))), conv_b.reshape(1, -1))


def _xattn_kernel(x_ref, gq_ref, wq_ref, k_ref, v_ref, wo_ref, go_ref, o_ref, att_ref):
    x = x_ref[...]
    q = jnp.dot(_rms(x, gq_ref[...]).astype(BF16), wq_ref[...], preferred_element_type=F32)
    for h in range(XA_HEADS):
        sl = slice(h * XA_HEAD_DIM, (h + 1) * XA_HEAD_DIM)
        s = _dot_nt(q[:, sl], k_ref[:, sl]) * (XA_HEAD_DIM ** -0.5)
        s = s - jnp.max(s, axis=-1, keepdims=True)
        p = jnp.exp(s)
        p = p / jnp.sum(p, axis=-1, keepdims=True)
        att_ref[:, sl] = _dot(p, v_ref[:, sl])
    hout = jnp.dot(att_ref[...].astype(BF16), wo_ref[...], preferred_element_type=F32)
    o_ref[...] = x + _rms(hout, go_ref[...])


def _xattn(x, g_q, wq, k_mem, v_mem, wo, g_o, b_, s_):
    t = x.shape[0]
    nt = s_ // ROW_TILE
    row = pl.BlockSpec((ROW_TILE, D_MODEL), lambda bi, i: (bi * nt + i, 0))
    mem = pl.BlockSpec((MEM_LEN, XA_WIDTH), lambda bi, i: (bi, 0))
    return pl.pallas_call(
        _xattn_kernel,
        grid=(b_, nt),
        in_specs=[row, _const_spec((1, D_MODEL)), _const_spec(wq.shape), mem, mem,
                  _const_spec(wo.shape), _const_spec((1, D_MODEL))],
        out_specs=row,
        out_shape=jax.ShapeDtypeStruct((t, D_MODEL), F32),
        scratch_shapes=[pltpu.VMEM((ROW_TILE, XA_WIDTH), F32)],
        compiler_params=_params("parallel", "parallel"),
        name="mem_xattn",
    )(x, g_q.reshape(1, -1), wq, k_mem, v_mem, wo, g_o.reshape(1, -1))


def _block_diag(n_blocks, size, value):
    return np.kron(np.eye(n_blocks, dtype=np.float32), np.full((size, size), value, np.float32))


def _rope_tables(s_):
    half = RT_DIM // 2
    inv_freq = RT_ROPE_BASE ** (-jnp.arange(half, dtype=F32) / half)
    ang = jnp.arange(s_).astype(F32)[:, None] * inv_freq[None, :]
    cos, sin = jnp.cos(ang), jnp.sin(ang)
    return jnp.concatenate([cos, cos], axis=-1), jnp.concatenate([-sin, sin], axis=-1)


def kernel(x, mem, norm_g, mem_norm_g, ffn_w_gate, ffn_w_up, ffn_w_down, xa_wq, xa_wk, xa_wv, xa_wo, ev_w_in, ev_w_out, rw_mu, rw_w0, rw_w2, rw_a0, rw_a2, rw_g2, rw_k_k, rw_k_a, rw_r_k, rw_ln_g, rw_ln_b, rt_gn_g, rt_gn_b, od_w_in, od_w_out, dsa_q_norm_g, dsa_kv_norm_g, dsa_w_uq, dsa_w_uk, dsa_w_uv, dsa_w_qi, sc_conv_w, sc_conv_b):
    b_, s_, d_ = x.shape
    depth = norm_g.shape[0]
    t = b_ * s_
    bf = lambda w: w.astype(BF16)

    ones_blk = jnp.asarray(_block_diag(RW_HEADS, RW_HEAD_DIM, 1.0), BF16)
    ones_pair = jnp.asarray(_block_diag(2, RW_HEAD_DIM, 1.0), BF16)
    tri_rw = jnp.asarray(np.tril(np.ones((RW_CHUNK, RW_CHUNK), np.float32)), BF16)
    tril_dsa = jnp.asarray(np.tril(np.ones((Q_BLOCK, Q_BLOCK), np.float32)), BF16)
    cos2, sin2 = _rope_tables(s_)

    xf = x.reshape(t, d_)
    mem_f = mem.reshape(b_ * MEM_LEN, d_)
    for l in range(depth):
        ng = norm_g[l]
        i = l // 2
        xf = _ffn(xf, ng[0], bf(ffn_w_gate[l, 0]), bf(ffn_w_up[l, 0]), bf(ffn_w_down[l, 0]), ng[1])
        if l % 2 == 0:
            w_in = bf(ev_w_in[i])
            p_rw, p_rt = _norm_proj(xf, ng[2], [w_in[:, :RW_IN], w_in[:, RW_IN:]], "ev_in_proj")
            r, k, v, ld, a, b, g = _rw_prep(p_rw, b_, s_, rw_mu[i], rw_w0[i], rw_w2[i], rw_a0[i], rw_a2[i],
                                            rw_g2[i], rw_k_k[i], rw_k_a[i], ones_blk)
            y_a = _rw_scan(r, k, v, ld, a, b, g, b_, s_, rw_r_k[i], rw_ln_g[i], rw_ln_b[i],
                           ones_pair, tri_rw)
            y_b = _retention(p_rt, b_, s_, cos2, sin2, rt_gn_g[i], rt_gn_b[i])
            w_out = bf(ev_w_out[i])
            xf = _proj_res([y_a, y_b], [w_out[:RW_WIDTH], w_out[RW_WIDTH:]], ng[3], xf, "ev_out_proj")
        else:
            w_in = od_w_in[i]
            kw_w = jnp.pad(w_in[:, DSA_Q_RANK + DSA_KV_RANK:DSA_IN], ((0, 0), (0, LANES - IDX_DIM - IDX_HEADS)))
            c_q, c_kv, kw, p_sc = _norm_proj(
                xf, ng[2], [bf(w_in[:, :DSA_Q_RANK]), bf(w_in[:, DSA_Q_RANK:DSA_Q_RANK + DSA_KV_RANK]),
                            bf(kw_w), bf(w_in[:, DSA_IN:])], "od_in_proj")
            ckv_n, ckv_t, k_idx, w_idx, q_lat, q_idx = _dsa_prep(
                c_q, c_kv, kw, dsa_q_norm_g[i], dsa_kv_norm_g[i],
                bf(dsa_w_uq[i].reshape(DSA_Q_RANK, DSA_WIDTH)), bf(dsa_w_uk[i]),
                bf(dsa_w_qi[i].reshape(DSA_Q_RANK, IDX_HEADS * IDX_DIM)))
            y_c = _dsa(q_idx, w_idx, q_lat, k_idx, ckv_n, ckv_t, bf(dsa_w_uv[i]), tril_dsa, b_, s_)
            y_d = _sconv(p_sc, b_, s_, sc_conv_w[i], sc_conv_b[i])
            w_out = bf(od_w_out[i])
            xf = _proj_res([y_c, y_d], [w_out[:DSA_WIDTH], w_out[DSA_WIDTH:]], ng[3], xf, "od_out_proj")
        k_mem, v_mem = _norm_proj(mem_f, mem_norm_g, [bf(xa_wk[l]), bf(xa_wv[l])], "mem_kv_proj")
        xf = _xattn(xf, ng[4], bf(xa_wq[l]), k_mem, v_mem, bf(xa_wo[l]), ng[5], b_, s_)
        xf = _ffn(xf, ng[6], bf(ffn_w_gate[l, 1]), bf(ffn_w_up[l, 1]), bf(ffn_w_down[l, 1]), ng[7])
    return xf.reshape(b_, s_, d_)
```

```python
import functools
import math

import numpy as np
import jax
import jax.numpy as jnp
from jax import lax
from jax.experimental import pallas as pl
from jax.experimental.pallas import tpu as pltpu

F32 = jnp.float32
BF16 = jnp.bfloat16

D_MODEL = 1024
D_FF = 2816
EPS = 1e-6
MEM_LEN = 256
RW_HEADS = 8
RW_HEAD_DIM = 64
RW_WIDTH = RW_HEADS * RW_HEAD_DIM
RW_DECAY_RANK = 64
RW_AAA_RANK = 64
RW_GATE_RANK = 128
RW_LN_EPS = 64e-5
RW_IN = 3 * RW_WIDTH + RW_DECAY_RANK + RW_AAA_RANK + RW_GATE_RANK
RW_CHUNK = 64
RW_ROWS = 4
RT_HEADS = 4
RT_DIM = 128
RT_WIDTH = RT_HEADS * RT_DIM
RT_CHUNK = 128
RT_ROWS = 4
RT_ROPE_BASE = 10000.0
RT_IN = 4 * RT_WIDTH
DSA_HEADS = 8
DSA_HEAD_DIM = 64
DSA_WIDTH = DSA_HEADS * DSA_HEAD_DIM
DSA_Q_RANK = 256
DSA_KV_RANK = 128
IDX_HEADS = 8
IDX_DIM = 64
TOPK_MAX = 256
Q_BLOCK = 128
DSA_IN = DSA_Q_RANK + DSA_KV_RANK + IDX_DIM + IDX_HEADS
SC_WIDTH = 512
SC_KERNEL = 3
XA_HEADS = 4
XA_HEAD_DIM = 128
XA_WIDTH = XA_HEADS * XA_HEAD_DIM

LANES = 128
ROW_TILE = 512
VMEM_LIMIT = 56 * 1024 * 1024
INT_MIN = -2 ** 31
HALF_BIAS = 2 ** 15
NEG_BIG = -1e30


def _params(*sem):
    return pltpu.CompilerParams(dimension_semantics=sem, vmem_limit_bytes=VMEM_LIMIT)


def _rms(x, g):
    return x * lax.rsqrt(jnp.mean(x * x, axis=-1, keepdims=True) + EPS) * g


def _dot(a, b):
    return jnp.dot(a.astype(BF16), b.astype(BF16), preferred_element_type=F32)


def _dot_nt(a, b):
    return lax.dot_general(a.astype(BF16), b.astype(BF16), (((1,), (1,)), ((), ())),
                           preferred_element_type=F32)


def _dot_tn(a, b):
    return lax.dot_general(a.astype(BF16), b.astype(BF16), (((0,), (0,)), ((), ())),
                           preferred_element_type=F32)


def _split3(x):
    hi = x.astype(BF16)
    r1 = x - hi.astype(F32)
    mid = r1.astype(BF16)
    lo = (r1 - mid.astype(F32)).astype(BF16)
    return hi, mid, lo


def _dot_exact_rhs(x, w_bf16):
    hi, mid, lo = _split3(x)
    out = jnp.dot(hi, w_bf16, preferred_element_type=F32)
    out += jnp.dot(mid, w_bf16, preferred_element_type=F32)
    out += jnp.dot(lo, w_bf16, preferred_element_type=F32)
    return out


def _dot_exact_lhs(w_bf16, x):
    hi, mid, lo = _split3(x)
    out = jnp.dot(w_bf16, hi, preferred_element_type=F32)
    out += jnp.dot(w_bf16, mid, preferred_element_type=F32)
    out += jnp.dot(w_bf16, lo, preferred_element_type=F32)
    return out


def _const_spec(shape):
    nd = len(shape)
    return pl.BlockSpec(shape, lambda *_: (0,) * nd, pipeline_mode=pl.Buffered(1))


FF_CHUNK = 256
FF_TILE = 512


def _ffn_kernel(x_ref, gin_ref, wg_ref, wu_ref, wd_ref, gout_ref, o_ref, acc_ref):
    x = x_ref[...]
    xb = _rms(x, gin_ref[...]).astype(BF16)
    for c in range(D_FF // FF_CHUNK):
        sl = slice(c * FF_CHUNK, (c + 1) * FF_CHUNK)
        g = jnp.dot(xb, wg_ref[:, sl], preferred_element_type=F32)
        u = jnp.dot(xb, wu_ref[:, sl], preferred_element_type=F32)
        h = (g * jax.nn.sigmoid(g) * u).astype(BF16)
        part = jnp.dot(h, wd_ref[sl, :], preferred_element_type=F32)
        if c == 0:
            acc_ref[...] = part
        else:
            acc_ref[...] += part
    o_ref[...] = x + 0.5 * _rms(acc_ref[...], gout_ref[...])


def _ffn(x, g_in, wg, wu, wd, g_out):
    t = x.shape[0]
    row = pl.BlockSpec((FF_TILE, D_MODEL), lambda i: (i, 0))
    return pl.pallas_call(
        _ffn_kernel,
        grid=(t // FF_TILE,),
        in_specs=[row, _const_spec((1, D_MODEL)), _const_spec((D_MODEL, D_FF)),
                  _const_spec((D_MODEL, D_FF)), _const_spec((D_FF, D_MODEL)), _const_spec((1, D_MODEL))],
        out_specs=row,
        out_shape=jax.ShapeDtypeStruct((t, D_MODEL), F32),
        scratch_shapes=[pltpu.VMEM((FF_TILE, D_MODEL), F32)],
        compiler_params=_params("parallel"),
        name="ffn_half",
    )(x, g_in.reshape(1, -1), wg, wu, wd, g_out.reshape(1, -1))


PROJ_CHUNK = 512


def _norm_proj_kernel(n_out, x_ref, g_ref, *refs):
    xb = _rms(x_ref[...], g_ref[...]).astype(BF16)
    for w_ref, o_ref in zip(refs[:n_out], refs[n_out:]):
        n = w_ref.shape[1]
        for c in range(0, n, PROJ_CHUNK):
            sl = slice(c, min(c + PROJ_CHUNK, n))
            o_ref[:, sl] = jnp.dot(xb, w_ref[:, sl], preferred_element_type=F32)


def _norm_proj(x, g, ws, name):
    t, d = x.shape
    row = lambda n: pl.BlockSpec((ROW_TILE, n), lambda i: (i, 0))
    return pl.pallas_call(
        functools.partial(_norm_proj_kernel, len(ws)),
        grid=(t // ROW_TILE,),
        in_specs=[row(d), _const_spec((1, d))] + [_const_spec(w.shape) for w in ws],
        out_specs=[row(w.shape[1]) for w in ws],
        out_shape=[jax.ShapeDtypeStruct((t, w.shape[1]), F32) for w in ws],
        compiler_params=_params("parallel"),
        name=name,
    )(x, g.reshape(1, -1), *ws)


def _proj_res_kernel(n_in, *refs):
    y_refs, w_refs = refs[:n_in], refs[n_in:2 * n_in]
    g_ref, x_ref, o_ref = refs[2 * n_in:]
    h = None
    for y_ref, w_ref in zip(y_refs, w_refs):
        part = jnp.dot(y_ref[...].astype(BF16), w_ref[...], preferred_element_type=F32)
        h = part if h is None else h + part
    o_ref[...] = x_ref[...] + _rms(h, g_ref[...])


def _proj_res(ys, ws, g, x, name):
    t, d = x.shape
    row = lambda n: pl.BlockSpec((ROW_TILE, n), lambda i: (i, 0))
    return pl.pallas_call(
        functools.partial(_proj_res_kernel, len(ys)),
        grid=(t // ROW_TILE,),
        in_specs=[row(y.shape[1]) for y in ys] + [_const_spec(w.shape) for w in ws]
                 + [_const_spec((1, d)), row(d)],
        out_specs=row(d),
        out_shape=jax.ShapeDtypeStruct((t, d), F32),
        compiler_params=_params("parallel"),
        name=name,
    )(*ys, *ws, g.reshape(1, -1), x)


RW_TILE = 512


def _shift_rows(x, carry, n):
    rolled = pltpu.roll(x, n, 0)
    row = lax.broadcasted_iota(jnp.int32, x.shape, 0)
    out = rolled
    for i in range(n):
        out = jnp.where(row == i, carry[8 - n + i:8 - n + i + 1, :], out)
    return out


def _rw_prep_kernel(p_ref, mu_ref, w0_ref, w2_ref, a0_ref, a2_ref, g2_ref, kk_ref, ka_ref, ones_ref,
                    r_ref, k_ref, v_ref, ld_ref, a_ref, b_ref, g_ref, carry_ref):
    @pl.when(pl.program_id(1) == 0)
    def _():
        carry_ref[...] = jnp.zeros_like(carry_ref)

    p = p_ref[...]
    prev = _shift_rows(p, carry_ref[...], 1)
    carry_ref[...] = p[RW_TILE - 8:, :]
    xm = p + (prev - p) * mu_ref[...]
    w = RW_WIDTH
    r, k, v = xm[:, :w], xm[:, w:2 * w], xm[:, 2 * w:3 * w]
    xw = xm[:, 3 * w:3 * w + RW_DECAY_RANK]
    xa = xm[:, 3 * w + RW_DECAY_RANK:3 * w + RW_DECAY_RANK + RW_AAA_RANK]
    xg = xm[:, 3 * w + RW_DECAY_RANK + RW_AAA_RANK:]
    wlog = -jax.nn.softplus(-(w0_ref[...] + _dot(jnp.tanh(xw), w2_ref[...]))) - 0.5
    a = jax.nn.sigmoid(a0_ref[...] + _dot(xa, a2_ref[...]))
    kk = k * kk_ref[...]
    ss = _dot_exact_rhs(kk * kk, ones_ref[...])
    kk = kk / jnp.maximum(jnp.sqrt(ss), 1e-12)
    r_ref[...] = r
    k_ref[...] = k * (1.0 + (a - 1.0) * ka_ref[...])
    v_ref[...] = v
    ld_ref[...] = -jnp.exp(wlog)
    a_ref[...] = -kk
    b_ref[...] = kk * a
    g_ref[...] = _dot(jax.nn.sigmoid(xg), g2_ref[...])


def _rw_prep(p_rw, b_, s_, mu, w0, w2, a0, a2, g2, k_k, k_a, ones_blk):
    t = p_rw.shape[0]
    nt = s_ // RW_TILE
    row = lambda n: pl.BlockSpec((RW_TILE, n), lambda b, i: (b * nt + i, 0))
    vec = lambda a: a.reshape(1, -1)
    outs = [jax.ShapeDtypeStruct((t, RW_WIDTH), F32)] * 7
    return pl.pallas_call(
        _rw_prep_kernel,
        grid=(b_, nt),
        in_specs=[row(RW_IN), _const_spec((1, RW_IN)), _const_spec((1, RW_WIDTH)),
                  _const_spec(w2.shape), _const_spec((1, RW_WIDTH)), _const_spec(a2.shape),
                  _const_spec(g2.shape), _const_spec((1, RW_WIDTH)), _const_spec((1, RW_WIDTH)),
                  _const_spec(ones_blk.shape)],
        out_specs=[row(RW_WIDTH)] * 7,
        out_shape=outs,
        scratch_shapes=[pltpu.VMEM((8, RW_IN), F32)],
        compiler_params=_params("parallel", "arbitrary"),
        name="rwkv_prep",
    )(p_rw, vec(mu), vec(w0), w2, vec(a0), a2, g2, vec(k_k), vec(k_a), ones_blk)


def _rw_scan_kernel(r_ref, k_ref, v_ref, ld_ref, a_ref, b_ref, g_ref, tri_ref, ones_ref,
                    rk_ref, lng_ref, lnb_ref, o_ref, state_ref):
    c_ = RW_CHUNK
    n_ = RW_HEAD_DIM

    @pl.when(pl.program_id(1) == 0)
    def _():
        state_ref[...] = jnp.zeros_like(state_ref)

    c2 = 2 * c_
    row = lax.broadcasted_iota(jnp.int32, (c2, c2), 0)
    col = lax.broadcasted_iota(jnp.int32, (c2, c2), 1)
    same_head = (row >= c_) == (col >= c_)
    strict = jnp.logical_and(same_head, row > col)
    incl = jnp.logical_and(same_head, row >= col)
    left = lax.broadcasted_iota(jnp.int32, (c_, LANES), 1) < n_

    def block_diag(x):
        return jnp.concatenate([jnp.where(left, x, 0.0), jnp.where(left, 0.0, x)], axis=0)

    pairs = range(RW_HEADS // 2)
    sls = [slice(p * LANES, (p + 1) * LANES) for p in pairs]
    ones2 = ones_ref[...]

    def head_sums(xs):
        parts = [part[:, sl] for x in xs for part in _split3(x) for sl in sls]
        prod = jnp.dot(jnp.concatenate(parts, axis=0), ones2, preferred_element_type=F32)
        outs = []
        for i in range(len(xs)):
            slabs = []
            for p in pairs:
                rows = [((3 * i + j) * len(sls) + p) * c_ for j in range(3)]
                slabs.append(prod[rows[0]:rows[0] + c_] + prod[rows[1]:rows[1] + c_] + prod[rows[2]:rows[2] + c_])
            outs.append(jnp.concatenate(slabs, axis=1))
        return outs

    def chunk_program(bb):
        ld = ld_ref[bb]
        r = r_ref[bb]
        k = k_ref[bb]
        v = v_ref[bb]
        cum = _dot_exact_lhs(tri_ref[...], ld)
        mid = cum[c_ // 2 - 1:c_ // 2, :]
        e_in = jnp.exp(cum - mid)
        e_out = jnp.exp(mid - cum)
        r_t = r * e_in
        a_t = a_ref[bb] * jnp.exp(cum - ld - mid)
        b_t = b_ref[bb] * e_out
        k_t = k * e_out
        e_mid = jnp.exp(mid)
        w_all = jnp.exp(cum[c_ - 1:c_, :])
        w_tail = jnp.exp(cum[c_ - 1:c_, :] - mid)
        yield
        s0 = [state_ref[bb, p] for p in pairs]
        ar = [jnp.concatenate([block_diag(a_t[:, sl]), block_diag(r_t[:, sl])], axis=0) for sl in sls]
        bk = [jnp.concatenate([block_diag(b_t[:, sl]), block_diag(k_t[:, sl])], axis=0) for sl in sls]
        vb = [block_diag(v[:, sl]) for sl in sls]
        m1 = [_dot_nt(ar[p], bk[p]) for p in pairs]
        m2 = [_dot_nt(ar[p] * e_mid[:, sls[p]], s0[p]) for p in pairs]
        yield
        l_ab = [jnp.where(strict, m[:c2, :c2], 0.0) for m in m1]
        l_ak = [jnp.where(strict, m[:c2, c2:], 0.0) for m in m1]
        l_r = [jnp.concatenate([jnp.where(incl, m[c2:, :c2], 0.0), jnp.where(incl, m[c2:, c2:], 0.0)], axis=1)
               for m in m1]
        u = [m2[p][:c2] + _dot(l_ak[p], vb[p]) for p in pairs]
        pw = l_ab
        yield
        n_steps = int(math.log2(c_))
        for step in range(n_steps):
            if step < n_steps - 1:
                prod = [_dot(pw[p], jnp.concatenate([pw[p], u[p]], axis=1)) for p in pairs]
                pw = [q[:, :c2] for q in prod]
                u = [u[p] + prod[p][:, c2:] for p in pairs]
            else:
                u = [u[p] + _dot(pw[p], u[p]) for p in pairs]
            yield
        uv = [jnp.concatenate([u[p], vb[p]], axis=0) for p in pairs]
        ys = [m2[p][c2:] + _dot(l_r[p], uv[p]) for p in pairs]
        upd = [_dot_tn(uv[p], bk[p]) for p in pairs]
        yield
        for p in pairs:
            state_ref[bb, p] = s0[p] * w_all[:, sls[p]] + upd[p] * w_tail[:, sls[p]]
        y = jnp.concatenate([ys[p][:c_] + ys[p][c_:] for p in pairs], axis=1)
        sum_y, sum_rk = head_sums([y, r * k * rk_ref[...]])
        d = y - sum_y * (1.0 / n_)
        yield
        var = head_sums([d * d])[0] * (1.0 / n_)
        yn = d * lax.rsqrt(var + RW_LN_EPS) * lng_ref[...] + lnb_ref[...]
        o_ref[bb] = ((yn + sum_rk * v) * g_ref[bb]).astype(o_ref.dtype)
        yield

    for _ in zip(*[chunk_program(bb) for bb in range(RW_ROWS)]):
        pass


def _rw_scan(r, k, v, ld, a, b, g, b_, s_, r_k, ln_g, ln_b, ones_pair, tri):
    t = r.shape[0]
    nc = s_ // RW_CHUNK
    row = pl.BlockSpec((RW_ROWS, RW_CHUNK, RW_WIDTH), lambda bi, i: (bi, i, 0))
    vec = lambda x: x.reshape(1, -1)
    seq = lambda x: x.reshape(b_, s_, RW_WIDTH)
    out = pl.pallas_call(
        _rw_scan_kernel,
        grid=(b_ // RW_ROWS, nc),
        in_specs=[row] * 7 + [_const_spec(tri.shape), _const_spec(ones_pair.shape),
                              _const_spec((1, RW_WIDTH)), _const_spec((1, RW_WIDTH)), _const_spec((1, RW_WIDTH))],
        out_specs=row,
        out_shape=jax.ShapeDtypeStruct((b_, s_, RW_WIDTH), BF16),
        scratch_shapes=[pltpu.VMEM((RW_ROWS, RW_HEADS // 2, 2 * RW_HEAD_DIM, 2 * RW_HEAD_DIM), F32)],
        compiler_params=_params("parallel", "arbitrary"),
        name="rwkv_scan",
    )(seq(r), seq(k), seq(v), seq(ld), seq(a), seq(b), seq(g), tri, ones_pair, vec(r_k), vec(ln_g), vec(ln_b))
    return out.reshape(t, RW_WIDTH)


def _rt_log_gamma(h):
    return math.log1p(-(2.0 ** (-5.0 - h)))


def _retention_kernel(p_ref, cos_ref, sin_ref, gng_ref, gnb_ref, o_ref, state_ref):
    c_ = RT_CHUNK

    @pl.when(pl.program_id(1) == 0)
    def _():
        state_ref[...] = jnp.zeros_like(state_ref)

    cos = cos_ref[...]
    sin = sin_ref[...]
    row = lax.broadcasted_iota(jnp.int32, (c_, c_), 0).astype(F32)
    col = lax.broadcasted_iota(jnp.int32, (c_, c_), 1).astype(F32)
    rel = row - col
    heads = range(RT_HEADS)
    lgs = [_rt_log_gamma(h) for h in heads]
    sls = [slice(h * RT_DIM, (h + 1) * RT_DIM) for h in heads]
    decay = [jnp.where(rel >= 0, jnp.exp(jnp.maximum(rel, 0.0) * lg), 0.0) for lg in lgs]
    xi = [jnp.exp((row + 1.0) * lg) for lg in lgs]
    zeta = [jnp.exp((c_ - 1.0 - row) * lg) for lg in lgs]

    def chunk_program(bb):
        col_of = lambda part, h: slice(part * RT_WIDTH + h * RT_DIM, part * RT_WIDTH + (h + 1) * RT_DIM)
        q = [p_ref[bb, :, col_of(0, h)] for h in heads]
        k = [p_ref[bb, :, col_of(1, h)] for h in heads]
        v = [p_ref[bb, :, col_of(2, h)] for h in heads]
        q = [x * cos + pltpu.roll(x, RT_DIM // 2, 1) * sin for x in q]
        k = [(x * cos + pltpu.roll(x, RT_DIM // 2, 1) * sin) * (RT_DIM ** -0.5) for x in k]
        s0 = [state_ref[bb, h] for h in heads]
        yield
        scores = [_dot_nt(q[h], k[h]) * decay[h] for h in heads]
        cross = [_dot(q[h], s0[h]) * xi[h] for h in heads]
        for h in heads:
            state_ref[bb, h] = s0[h] * math.exp(c_ * lgs[h]) + _dot_tn(k[h] * zeta[h], v[h])
        yield
        o = [_dot(scores[h], v[h]) + cross[h] for h in heads]
        yield
        for h in heads:
            mu = jnp.mean(o[h], axis=-1, keepdims=True)
            d = o[h] - mu
            var = jnp.mean(d * d, axis=-1, keepdims=True)
            on = d * lax.rsqrt(var + EPS) * gng_ref[:, sls[h]] + gnb_ref[:, sls[h]]
            gate = p_ref[bb, :, col_of(3, h)]
            o_ref[bb, :, sls[h]] = (gate * jax.nn.sigmoid(gate) * on).astype(o_ref.dtype)
        yield

    for _ in zip(*[chunk_program(bb) for bb in range(RT_ROWS)]):
        pass


def _retention(p_rt, b_, s_, cos2, sin2, gn_g, gn_b):
    t = p_rt.shape[0]
    nc = s_ // RT_CHUNK
    out = pl.pallas_call(
        _retention_kernel,
        grid=(b_ // RT_ROWS, nc),
        in_specs=[pl.BlockSpec((RT_ROWS, RT_CHUNK, RT_IN), lambda bi, i: (bi, i, 0)),
                  pl.BlockSpec((RT_CHUNK, RT_DIM), lambda bi, i: (i, 0)),
                  pl.BlockSpec((RT_CHUNK, RT_DIM), lambda bi, i: (i, 0)),
                  _const_spec((1, RT_WIDTH)), _const_spec((1, RT_WIDTH))],
        out_specs=pl.BlockSpec((RT_ROWS, RT_CHUNK, RT_WIDTH), lambda bi, i: (bi, i, 0)),
        out_shape=jax.ShapeDtypeStruct((b_, s_, RT_WIDTH), BF16),
        scratch_shapes=[pltpu.VMEM((RT_ROWS, RT_HEADS, RT_DIM, RT_DIM), F32)],
        compiler_params=_params("parallel", "arbitrary"),
        name="retention",
    )(p_rt.reshape(b_, s_, RT_IN), cos2, sin2, gn_g.reshape(1, -1), gn_b.reshape(1, -1))
    return out.reshape(t, RT_WIDTH)


DSA_BLOCKS_PER_TILE = ROW_TILE // Q_BLOCK


def _dsa_prep_kernel(cq_ref, ckv_ref, kw_ref, qg_ref, kvg_ref, wuq_ref, wuk_ref, wqi_ref,
                     ckvn_ref, ckvt_ref, kidx_ref, widx_ref, qlat_ref, qidx_ref):
    cq = _rms(cq_ref[...], qg_ref[...]).astype(BF16)
    ckvn = _rms(ckv_ref[...], kvg_ref[...])
    ckvn_ref[...] = ckvn.astype(BF16)
    kw = kw_ref[...]
    kidx_ref[...] = kw[:, :IDX_DIM].astype(BF16)
    q = jnp.dot(cq, wuq_ref[...], preferred_element_type=F32)
    qi = jnp.dot(cq, wqi_ref[...], preferred_element_type=F32)
    for blk in range(DSA_BLOCKS_PER_TILE):
        rows = slice(blk * Q_BLOCK, (blk + 1) * Q_BLOCK)
        ckvt_ref[blk] = ckvn[rows].T.astype(BF16)
        widx_ref[blk] = kw[rows].T[IDX_DIM:IDX_DIM + IDX_HEADS] * ((IDX_HEADS * IDX_DIM) ** -0.5)
    for h in range(DSA_HEADS):
        qh = q[:, h * DSA_HEAD_DIM:(h + 1) * DSA_HEAD_DIM]
        ql = (_dot(qh, wuk_ref[h]) * (DSA_HEAD_DIM ** -0.5)).astype(BF16)
        qih = qi[:, h * IDX_DIM:(h + 1) * IDX_DIM].astype(BF16)
        for blk in range(DSA_BLOCKS_PER_TILE):
            rows = slice(blk * Q_BLOCK, (blk + 1) * Q_BLOCK)
            qlat_ref[blk, h] = ql[rows]
            qidx_ref[blk, h] = qih[rows]


def _dsa_prep(c_q, c_kv, kw, q_g, kv_g, w_uq, w_uk, w_qi):
    t = c_q.shape[0]
    row = lambda n: pl.BlockSpec((ROW_TILE, n), lambda i: (i, 0))
    blk3 = lambda m, n: pl.BlockSpec((DSA_BLOCKS_PER_TILE, m, n), lambda i: (i, 0, 0))
    blk4 = lambda n: pl.BlockSpec((DSA_BLOCKS_PER_TILE, DSA_HEADS, Q_BLOCK, n), lambda i: (i, 0, 0, 0))
    return pl.pallas_call(
        _dsa_prep_kernel,
        grid=(t // ROW_TILE,),
        in_specs=[row(DSA_Q_RANK), row(DSA_KV_RANK), row(LANES), _const_spec((1, DSA_Q_RANK)),
                  _const_spec((1, DSA_KV_RANK)),
                  _const_spec(w_uq.shape), _const_spec(w_uk.shape), _const_spec(w_qi.shape)],
        out_specs=[row(DSA_KV_RANK), blk3(DSA_KV_RANK, Q_BLOCK), row(IDX_DIM), blk3(IDX_HEADS, Q_BLOCK),
                   blk4(DSA_KV_RANK), blk4(IDX_DIM)],
        out_shape=[jax.ShapeDtypeStruct((t, DSA_KV_RANK), BF16),
                   jax.ShapeDtypeStruct((t // Q_BLOCK, DSA_KV_RANK, Q_BLOCK), BF16),
                   jax.ShapeDtypeStruct((t, IDX_DIM), BF16),
                   jax.ShapeDtypeStruct((t // Q_BLOCK, IDX_HEADS, Q_BLOCK), F32),
                   jax.ShapeDtypeStruct((t // Q_BLOCK, DSA_HEADS, Q_BLOCK, DSA_KV_RANK), BF16),
                   jax.ShapeDtypeStruct((t // Q_BLOCK, IDX_HEADS, Q_BLOCK, IDX_DIM), BF16)],
        compiler_params=_params("parallel"),
        name="dsa_prep",
    )(c_q, c_kv, kw, q_g.reshape(1, -1), kv_g.reshape(1, -1), w_uq, w_uk, w_qi)


def _fold_rows(x, op):
    return functools.reduce(op, [x[i:i + 8] for i in range(0, x.shape[0], 8)])


def _dsa_kernel(top_k, qidx_ref, widx_ref, qlat_ref, kidx_ref, ckv_ref, ckvt_ref, wuv_ref, tril_ref, o_ref,
                key_ref, khi_ref, klo_ref, s_ref, acc_ref):
    qb = Q_BLOCK
    nh = DSA_HEADS
    j = pl.program_id(1)
    n_chunks = j + 1

    keyi = lax.broadcasted_iota(jnp.int32, (qb, qb), 0)
    qryi = lax.broadcasted_iota(jnp.int32, (qb, qb), 1)
    lanes = lambda h: slice(h * qb, (h + 1) * qb)

    def key_chunk(ref, c):
        return ref[pl.ds(pl.multiple_of(c * qb, qb), qb), :]

    w_idx = widx_ref[0]
    q_idx = qidx_ref[0].reshape(IDX_HEADS * qb, IDX_DIM)
    q_lat = qlat_ref[0].reshape(nh * qb, DSA_KV_RANK)

    n_pairs = (n_chunks + 1) // 2

    def score_chunk(c):
        logits = lax.dot_general(key_chunk(kidx_ref, c), q_idx, (((1,), (1,)), ((), ())),
                                 preferred_element_type=F32)
        score = jnp.zeros((qb, qb), F32)
        for h in range(IDX_HEADS):
            score = score + jnp.maximum(logits[:, lanes(h)], 0.0) * w_idx[h:h + 1, :]
        bits = lax.bitcast_convert_type(score, jnp.int32)
        key = jnp.where(bits < 0, bits ^ jnp.int32(0x7FFFFFFF), bits)
        causal = keyi + c * qb <= qryi + j * qb
        key = jnp.where(causal, key, jnp.int32(INT_MIN))
        key_ref[c] = key
        khi_ref[c] = lax.shift_right_arithmetic(key, 16).astype(jnp.int16)
        klo_ref[c] = ((key & 0xFFFF) - HALF_BIAS).astype(jnp.int16)

    def score_body(pi, carry):
        score_chunk(2 * pi)
        score_chunk(2 * pi + 1)
        return carry

    lax.fori_loop(0, n_pairs, score_body, 0)

    one16, zero16 = jnp.int16(1), jnp.int16(0)
    tile16 = lambda row32: jnp.broadcast_to(row32, (qb, qb)).astype(jnp.int16)

    def count16(ref, pred):
        def body(pi, acc):
            return (acc + jnp.where(pred(ref[2 * pi]), one16, zero16)
                    + jnp.where(pred(ref[2 * pi + 1]), one16, zero16))
        acc = lax.fori_loop(0, n_pairs, body, jnp.zeros((qb, qb), jnp.int16))
        acc = functools.reduce(jnp.add, [acc[i:i + 16] for i in range(0, qb, 16)])
        return jnp.sum(acc.astype(jnp.int32), axis=0, keepdims=True)

    def bisect16(ref, base):
        zero = tile16(jnp.zeros((1, qb), jnp.int32))
        t = jnp.where(base + count16(ref, lambda k: k >= zero) >= top_k, 0, -HALF_BIAS).astype(jnp.int32)

        def bit_body(i, t):
            cand = t | lax.shift_left(jnp.int32(1), 14 - i)
            cand16 = tile16(cand)
            return jnp.where(base + count16(ref, lambda k: k >= cand16) >= top_k, cand, t)

        return lax.fori_loop(0, 15, bit_body, t)

    thr_hi = bisect16(khi_ref, 0)
    thr_hi16 = tile16(thr_hi)
    above = count16(khi_ref, lambda k: k > thr_hi16)

    def low_body(pi, carry):
        for c in (2 * pi, 2 * pi + 1):
            klo_ref[c] = jnp.where(khi_ref[c] == thr_hi16, klo_ref[c], jnp.int16(-HALF_BIAS))
        return carry

    lax.fori_loop(0, n_pairs, low_body, 0)
    thr_lo = bisect16(klo_ref, above)
    thr = lax.shift_left(thr_hi, 16) | (thr_lo + HALF_BIAS)

    def count_above(pi, acc):
        return acc + jnp.where(key_ref[2 * pi] > thr, 1, 0) + jnp.where(key_ref[2 * pi + 1] > thr, 1, 0)

    n_above = lax.fori_loop(0, n_pairs, count_above, jnp.zeros((qb, qb), jnp.int32))
    n_above = jnp.sum(_fold_rows(n_above, jnp.add), axis=0, keepdims=True)
    need = (top_k - n_above).astype(F32)

    def logit_chunk(c, taken):
        key = key_ref[c]
        causal = keyi + c * qb <= qryi + j * qb
        eq = jnp.logical_and(key == thr, causal)
        eq_f = jnp.where(eq, 1.0, 0.0)
        rank = taken + jnp.dot(tril_ref[...], eq_f.astype(BF16), preferred_element_type=F32)
        sel = jnp.logical_or(jnp.logical_and(key > thr, causal), jnp.logical_and(eq, rank <= need))
        s = lax.dot_general(key_chunk(ckv_ref, c), q_lat, (((1,), (1,)), ((), ())),
                            preferred_element_type=F32)
        tops = []
        for h in range(nh):
            sh = jnp.where(sel, s[:, lanes(h)], NEG_BIG)
            s_ref[c, :, lanes(h)] = sh
            tops.append(_fold_rows(sh, jnp.maximum))
        return tops, taken + jnp.sum(eq_f, axis=0, keepdims=True)

    def logit_body(pi, carry):
        taken, tops = carry
        ta, taken = logit_chunk(2 * pi, taken)
        tb, taken = logit_chunk(2 * pi + 1, taken)
        return taken, tuple(jnp.maximum(t, jnp.maximum(a, b)) for t, a, b in zip(tops, ta, tb))

    _, tops = lax.fori_loop(0, n_pairs, logit_body,
                            (jnp.zeros((1, qb), F32), tuple(jnp.full((8, qb), NEG_BIG, F32) for _ in range(nh))))
    tops = [jnp.max(t, axis=0, keepdims=True) for t in tops]

    acc_ref[...] = jnp.zeros_like(acc_ref)

    def value_body(pi, sums):
        ps, new_sums = [], []
        for h in range(nh):
            pa = jnp.exp(s_ref[2 * pi, :, lanes(h)] - tops[h])
            pb = jnp.exp(s_ref[2 * pi + 1, :, lanes(h)] - tops[h])
            new_sums.append(sums[h] + _fold_rows(pa + pb, jnp.add))
            ps.append(jnp.concatenate([pa.astype(BF16), pb.astype(BF16)], axis=0))
        ckvt_pair = jnp.concatenate([ckvt_ref[2 * pi], ckvt_ref[2 * pi + 1]], axis=1)
        acc_ref[...] += jnp.dot(ckvt_pair, jnp.concatenate(ps, axis=1), preferred_element_type=F32)
        return tuple(new_sums)

    sums = lax.fori_loop(0, n_pairs, value_body, tuple(jnp.zeros((8, qb), F32) for _ in range(nh)))

    outs = []
    for h in range(nh):
        o_lat_t = acc_ref[:, lanes(h)] / jnp.sum(sums[h], axis=0, keepdims=True)
        outs.append(_dot_tn(o_lat_t, wuv_ref[h]))
    o_ref[...] = jnp.concatenate(outs, axis=1).astype(o_ref.dtype)


def _dsa(q_idx, w_idx, q_lat, k_idx, ckv_n, ckv_t, w_uv, tril, b_, s_):
    nb = s_ // Q_BLOCK
    t = b_ * s_
    top_k = min(TOPK_MAX, s_ // 4)
    blk = lambda n: pl.BlockSpec((Q_BLOCK, n), lambda bi, i: (bi * nb + i, 0))
    blk3 = lambda m, n: pl.BlockSpec((1, m, n), lambda bi, i: (bi * nb + i, 0, 0))
    blk4 = lambda n: pl.BlockSpec((1, DSA_HEADS, Q_BLOCK, n), lambda bi, i: (bi * nb + i, 0, 0, 0))
    seq = lambda n: pl.BlockSpec((s_, n), lambda bi, i: (bi, 0))
    stacked = DSA_HEADS * Q_BLOCK
    return pl.pallas_call(
        functools.partial(_dsa_kernel, top_k),
        grid=(b_, nb),
        in_specs=[blk4(IDX_DIM), blk3(IDX_HEADS, Q_BLOCK), blk4(DSA_KV_RANK),
                  seq(IDX_DIM), seq(DSA_KV_RANK),
                  pl.BlockSpec((nb, DSA_KV_RANK, Q_BLOCK), lambda bi, i: (bi, 0, 0)),
                  _const_spec(w_uv.shape), _const_spec(tril.shape)],
        out_specs=blk(DSA_WIDTH),
        out_shape=jax.ShapeDtypeStruct((t, DSA_WIDTH), BF16),
        scratch_shapes=[pltpu.VMEM((nb, Q_BLOCK, Q_BLOCK), jnp.int32),
                        pltpu.VMEM((nb, Q_BLOCK, Q_BLOCK), jnp.int16),
                        pltpu.VMEM((nb, Q_BLOCK, Q_BLOCK), jnp.int16),
                        pltpu.VMEM((nb, Q_BLOCK, stacked), F32),
                        pltpu.VMEM((DSA_KV_RANK, stacked), F32)],
        compiler_params=_params("parallel", "arbitrary"),
        name="dsa_attn",
    )(q_idx, w_idx, q_lat, k_idx, ckv_n, ckv_t, w_uv, tril)


SC_TILE = 512


def _sconv_kernel(p_ref, w_ref, b_ref, o_ref, carry_ref):
    @pl.when(pl.program_id(1) == 0)
    def _():
        carry_ref[...] = jnp.zeros_like(carry_ref)

    h = p_ref[:, :SC_WIDTH]
    gate_b = p_ref[:, SC_WIDTH:2 * SC_WIDTH]
    gate_c = p_ref[:, 2 * SC_WIDTH:]
    u = gate_c * h
    carry = carry_ref[...]
    y = u * w_ref[2:3, :] + _shift_rows(u, carry, 1) * w_ref[1:2, :] + _shift_rows(u, carry, 2) * w_ref[0:1, :]
    carry_ref[...] = u[SC_TILE - 8:, :]
    o_ref[...] = (gate_b * (y + b_ref[...])).astype(o_ref.dtype)


def _sconv(p_sc, b_, s_, conv_w, conv_b):
    t = p_sc.shape[0]
    nt = s_ // SC_TILE
    return pl.pallas_call(
        _sconv_kernel,
        grid=(b_, nt),
        in_specs=[pl.BlockSpec((SC_TILE, 3 * SC_WIDTH), lambda bi, i: (bi * nt + i, 0)),
                  _const_spec((8, SC_WIDTH)), _const_spec((1, SC_WIDTH))],
        out_specs=pl.BlockSpec((SC_TILE, SC_WIDTH), lambda bi, i: (bi * nt + i, 0)),
        out_shape=jax.ShapeDtypeStruct((t, SC_WIDTH), BF16),
        scratch_shapes=[pltpu.VMEM((8, SC_WIDTH), F32)],
        compiler_params=_params("parallel", "arbitrary"),
        name="short_conv",
    )(p_sc, jnp.pad(conv_w, ((0, 8 - SC_KERNEL), (0, 0))), conv_b.reshape(1, -1))


def _xattn_kernel(x_ref, gq_ref, wq_ref, k_ref, v_ref, wo_ref, go_ref, o_ref, att_ref):
    x = x_ref[...]
    q = jnp.dot(_rms(x, gq_ref[...]).astype(BF16), wq_ref[...], preferred_element_type=F32)
    for h in range(XA_HEADS):
        sl = slice(h * XA_HEAD_DIM, (h + 1) * XA_HEAD_DIM)
        s = _dot_nt(q[:, sl], k_ref[:, sl]) * (XA_HEAD_DIM ** -0.5)
        s = s - jnp.max(s, axis=-1, keepdims=True)
        p = jnp.exp(s)
        p = p / jnp.sum(p, axis=-1, keepdims=True)
        att_ref[:, sl] = _dot(p, v_ref[:, sl])
    hout = jnp.dot(att_ref[...].astype(BF16), wo_ref[...], preferred_element_type=F32)
    o_ref[...] = x + _rms(hout, go_ref[...])


def _xattn(x, g_q, wq, k_mem, v_mem, wo, g_o, b_, s_):
    t = x.shape[0]
    nt = s_ // ROW_TILE
    row = pl.BlockSpec((ROW_TILE, D_MODEL), lambda bi, i: (bi * nt + i, 0))
    mem = pl.BlockSpec((MEM_LEN, XA_WIDTH), lambda bi, i: (bi, 0))
    return pl.pallas_call(
        _xattn_kernel,
        grid=(b_, nt),
        in_specs=[row, _const_spec((1, D_MODEL)), _const_spec(wq.shape), mem, mem,
                  _const_spec(wo.shape), _const_spec((1, D_MODEL))],
        out_specs=row,
        out_shape=jax.ShapeDtypeStruct((t, D_MODEL), F32),
        scratch_shapes=[pltpu.VMEM((ROW_TILE, XA_WIDTH), F32)],
        compiler_params=_params("parallel", "parallel"),
        name="mem_xattn",
    )(x, g_q.reshape(1, -1), wq, k_mem, v_mem, wo, g_o.reshape(1, -1))


def _block_diag(n_blocks, size, value):
    return np.kron(np.eye(n_blocks, dtype=np.float32), np.full((size, size), value, np.float32))


def _rope_tables(s_):
    half = RT_DIM // 2
    inv_freq = RT_ROPE_BASE ** (-jnp.arange(half, dtype=F32) / half)
    ang = jnp.arange(s_).astype(F32)[:, None] * inv_freq[None, :]
    cos, sin = jnp.cos(ang), jnp.sin(ang)
    return jnp.concatenate([cos, cos], axis=-1), jnp.concatenate([-sin, sin], axis=-1)


def kernel(x, mem, norm_g, mem_norm_g, ffn_w_gate, ffn_w_up, ffn_w_down, xa_wq, xa_wk, xa_wv, xa_wo, ev_w_in, ev_w_out, rw_mu, rw_w0, rw_w2, rw_a0, rw_a2, rw_g2, rw_k_k, rw_k_a, rw_r_k, rw_ln_g, rw_ln_b, rt_gn_g, rt_gn_b, od_w_in, od_w_out, dsa_q_norm_g, dsa_kv_norm_g, dsa_w_uq, dsa_w_uk, dsa_w_uv, dsa_w_qi, sc_conv_w, sc_conv_b):
    b_, s_, d_ = x.shape
    depth = norm_g.shape[0]
    t = b_ * s_
    bf = lambda w: w.astype(BF16)

    ones_blk = jnp.asarray(_block_diag(RW_HEADS, RW_HEAD_DIM, 1.0), BF16)
    ones_pair = jnp.asarray(_block_diag(2, RW_HEAD_DIM, 1.0), BF16)
    tri_rw = jnp.asarray(np.tril(np.ones((RW_CHUNK, RW_CHUNK), np.float32)), BF16)
    tril_dsa = jnp.asarray(np.tril(np.ones((Q_BLOCK, Q_BLOCK), np.float32)), BF16)
    cos2, sin2 = _rope_tables(s_)

    xf = x.reshape(t, d_)
    mem_f = mem.reshape(b_ * MEM_LEN, d_)
    for l in range(depth):
        ng = norm_g[l]
        i = l // 2
        xf = _ffn(xf, ng[0], bf(ffn_w_gate[l, 0]), bf(ffn_w_up[l, 0]), bf(ffn_w_down[l, 0]), ng[1])
        if l % 2 == 0:
            w_in = bf(ev_w_in[i])
            p_rw, p_rt = _norm_proj(xf, ng[2], [w_in[:, :RW_IN], w_in[:, RW_IN:]], "ev_in_proj")
            r, k, v, ld, a, b, g = _rw_prep(p_rw, b_, s_, rw_mu[i], rw_w0[i], rw_w2[i], rw_a0[i], rw_a2[i],
                                            rw_g2[i], rw_k_k[i], rw_k_a[i], ones_blk)
            y_a = _rw_scan(r, k, v, ld, a, b, g, b_, s_, rw_r_k[i], rw_ln_g[i], rw_ln_b[i],
                           ones_pair, tri_rw)
            y_b = _retention(p_rt, b_, s_, cos2, sin2, rt_gn_g[i], rt_gn_b[i])
            w_out = bf(ev_w_out[i])
            xf = _proj_res([y_a, y_b], [w_out[:RW_WIDTH], w_out[RW_WIDTH:]], ng[3], xf, "ev_out_proj")
        else:
            w_in = od_w_in[i]
            kw_w = jnp.pad(w_in[:, DSA_Q_RANK + DSA_KV_RANK:DSA_IN], ((0, 0), (0, LANES - IDX_DIM - IDX_HEADS)))
            c_q, c_kv, kw, p_sc = _norm_proj(
                xf, ng[2], [bf(w_in[:, :DSA_Q_RANK]), bf(w_in[:, DSA_Q_RANK:DSA_Q_RANK + DSA_KV_RANK]),
                            bf(kw_w), bf(w_in[:, DSA_IN:])], "od_in_proj")
            ckv_n, ckv_t, k_idx, w_idx, q_lat, q_idx = _dsa_prep(
                c_q, c_kv, kw, dsa_q_norm_g[i], dsa_kv_norm_g[i],
                bf(dsa_w_uq[i].reshape(DSA_Q_RANK, DSA_WIDTH)), bf(dsa_w_uk[i]),
                bf(dsa_w_qi[i].reshape(DSA_Q_RANK, IDX_HEADS * IDX_DIM)))
            y_c = _dsa(q_idx, w_idx, q_lat, k_idx, ckv_n, ckv_t, bf(dsa_w_uv[i]), tril_dsa, b_, s_)
            y_d = _sconv(p_sc, b_, s_, sc_conv_w[i], sc_conv_b[i])
            w_out = bf(od_w_out[i])
            xf = _proj_res([y_c, y_d], [w_out[:DSA_WIDTH], w_out[DSA_WIDTH:]], ng[3], xf, "od_out_proj")
        k_mem, v_mem = _norm_proj(mem_f, mem_norm_g, [bf(xa_wk[l]), bf(xa_wv[l])], "mem_kv_proj")
        xf = _xattn(xf, ng[4], bf(xa_wq[l]), k_mem, v_mem, bf(xa_wo[l]), ng[5], b_, s_)
        xf = _ffn(xf, ng[6], bf(ffn_w_gate[l, 1]), bf(ffn_w_up[l, 1]), bf(ffn_w_down[l, 1]), ng[7])
    return xf.reshape(b_, s_, d_)
```

```python
import functools
import math

import numpy as np
import jax
import jax.numpy as jnp
from jax import lax
from jax.experimental import pallas as pl
from jax.experimental.pallas import tpu as pltpu

F32 = jnp.float32
BF16 = jnp.bfloat16

D_MODEL = 1024
D_FF = 2816
EPS = 1e-6
MEM_LEN = 256
RW_HEADS = 8
RW_HEAD_DIM = 64
RW_WIDTH = RW_HEADS * RW_HEAD_DIM
RW_DECAY_RANK = 64
RW_AAA_RANK = 64
RW_GATE_RANK = 128
RW_LN_EPS = 64e-5
RW_IN = 3 * RW_WIDTH + RW_DECAY_RANK + RW_AAA_RANK + RW_GATE_RANK
RW_CHUNK = 64
RW_ROWS = 4
RT_HEADS = 4
RT_DIM = 128
RT_WIDTH = RT_HEADS * RT_DIM
RT_CHUNK = 128
RT_ROWS = 4
RT_ROPE_BASE = 10000.0
RT_IN = 4 * RT_WIDTH
DSA_HEADS = 8
DSA_HEAD_DIM = 64
DSA_WIDTH = DSA_HEADS * DSA_HEAD_DIM
DSA_Q_RANK = 256
DSA_KV_RANK = 128
IDX_HEADS = 8
IDX_DIM = 64
TOPK_MAX = 256
Q_BLOCK = 128
DSA_IN = DSA_Q_RANK + DSA_KV_RANK + IDX_DIM + IDX_HEADS
SC_WIDTH = 512
SC_KERNEL = 3
XA_HEADS = 4
XA_HEAD_DIM = 128
XA_WIDTH = XA_HEADS * XA_HEAD_DIM

LANES = 128
ROW_TILE = 512
VMEM_LIMIT = 56 * 1024 * 1024
INT_MIN = -2 ** 31
NEG_BIG = -1e30


def _params(*sem):
    return pltpu.CompilerParams(dimension_semantics=sem, vmem_limit_bytes=VMEM_LIMIT)


def _rms(x, g):
    return x * lax.rsqrt(jnp.mean(x * x, axis=-1, keepdims=True) + EPS) * g


def _dot(a, b):
    return jnp.dot(a.astype(BF16), b.astype(BF16), preferred_element_type=F32)


def _dot_nt(a, b):
    return lax.dot_general(a.astype(BF16), b.astype(BF16), (((1,), (1,)), ((), ())),
                           preferred_element_type=F32)


def _dot_tn(a, b):
    return lax.dot_general(a.astype(BF16), b.astype(BF16), (((0,), (0,)), ((), ())),
                           preferred_element_type=F32)


def _split3(x):
    hi = x.astype(BF16)
    r1 = x - hi.astype(F32)
    mid = r1.astype(BF16)
    lo = (r1 - mid.astype(F32)).astype(BF16)
    return hi, mid, lo


def _dot_exact_rhs(x, w_bf16):
    hi, mid, lo = _split3(x)
    out = jnp.dot(hi, w_bf16, preferred_element_type=F32)
    out += jnp.dot(mid, w_bf16, preferred_element_type=F32)
    out += jnp.dot(lo, w_bf16, preferred_element_type=F32)
    return out


def _dot_exact_lhs(w_bf16, x):
    hi, mid, lo = _split3(x)
    out = jnp.dot(w_bf16, hi, preferred_element_type=F32)
    out += jnp.dot(w_bf16, mid, preferred_element_type=F32)
    out += jnp.dot(w_bf16, lo, preferred_element_type=F32)
    return out


def _const_spec(shape):
    nd = len(shape)
    return pl.BlockSpec(shape, lambda *_: (0,) * nd, pipeline_mode=pl.Buffered(1))


FF_CHUNK = 256
FF_TILE = 512


def _ffn_kernel(x_ref, gin_ref, wg_ref, wu_ref, wd_ref, gout_ref, o_ref, acc_ref):
    x = x_ref[...]
    xb = _rms(x, gin_ref[...]).astype(BF16)
    for c in range(D_FF // FF_CHUNK):
        sl = slice(c * FF_CHUNK, (c + 1) * FF_CHUNK)
        g = jnp.dot(xb, wg_ref[:, sl], preferred_element_type=F32)
        u = jnp.dot(xb, wu_ref[:, sl], preferred_element_type=F32)
        h = (g * jax.nn.sigmoid(g) * u).astype(BF16)
        part = jnp.dot(h, wd_ref[sl, :], preferred_element_type=F32)
        if c == 0:
            acc_ref[...] = part
        else:
            acc_ref[...] += part
    o_ref[...] = x + 0.5 * _rms(acc_ref[...], gout_ref[...])


def _ffn(x, g_in, wg, wu, wd, g_out):
    t = x.shape[0]
    row = pl.BlockSpec((FF_TILE, D_MODEL), lambda i: (i, 0))
    return pl.pallas_call(
        _ffn_kernel,
        grid=(t // FF_TILE,),
        in_specs=[row, _const_spec((1, D_MODEL)), _const_spec((D_MODEL, D_FF)),
                  _const_spec((D_MODEL, D_FF)), _const_spec((D_FF, D_MODEL)), _const_spec((1, D_MODEL))],
        out_specs=row,
        out_shape=jax.ShapeDtypeStruct((t, D_MODEL), F32),
        scratch_shapes=[pltpu.VMEM((FF_TILE, D_MODEL), F32)],
        compiler_params=_params("parallel"),
        name="ffn_half",
    )(x, g_in.reshape(1, -1), wg, wu, wd, g_out.reshape(1, -1))


PROJ_CHUNK = 512


def _norm_proj_kernel(n_out, x_ref, g_ref, *refs):
    xb = _rms(x_ref[...], g_ref[...]).astype(BF16)
    for w_ref, o_ref in zip(refs[:n_out], refs[n_out:]):
        n = w_ref.shape[1]
        for c in range(0, n, PROJ_CHUNK):
            sl = slice(c, min(c + PROJ_CHUNK, n))
            o_ref[:, sl] = jnp.dot(xb, w_ref[:, sl], preferred_element_type=F32)


def _norm_proj(x, g, ws, name):
    t, d = x.shape
    row = lambda n: pl.BlockSpec((ROW_TILE, n), lambda i: (i, 0))
    return pl.pallas_call(
        functools.partial(_norm_proj_kernel, len(ws)),
        grid=(t // ROW_TILE,),
        in_specs=[row(d), _const_spec((1, d))] + [_const_spec(w.shape) for w in ws],
        out_specs=[row(w.shape[1]) for w in ws],
        out_shape=[jax.ShapeDtypeStruct((t, w.shape[1]), F32) for w in ws],
        compiler_params=_params("parallel"),
        name=name,
    )(x, g.reshape(1, -1), *ws)


def _proj_res_kernel(n_in, *refs):
    y_refs, w_refs = refs[:n_in], refs[n_in:2 * n_in]
    g_ref, x_ref, o_ref = refs[2 * n_in:]
    h = None
    for y_ref, w_ref in zip(y_refs, w_refs):
        part = jnp.dot(y_ref[...].astype(BF16), w_ref[...], preferred_element_type=F32)
        h = part if h is None else h + part
    o_ref[...] = x_ref[...] + _rms(h, g_ref[...])


def _proj_res(ys, ws, g, x, name):
    t, d = x.shape
    row = lambda n: pl.BlockSpec((ROW_TILE, n), lambda i: (i, 0))
    return pl.pallas_call(
        functools.partial(_proj_res_kernel, len(ys)),
        grid=(t // ROW_TILE,),
        in_specs=[row(y.shape[1]) for y in ys] + [_const_spec(w.shape) for w in ws]
                 + [_const_spec((1, d)), row(d)],
        out_specs=row(d),
        out_shape=jax.ShapeDtypeStruct((t, d), F32),
        compiler_params=_params("parallel"),
        name=name,
    )(*ys, *ws, g.reshape(1, -1), x)


RW_TILE = 512


def _shift_rows(x, carry, n):
    rolled = pltpu.roll(x, n, 0)
    row = lax.broadcasted_iota(jnp.int32, x.shape, 0)
    out = rolled
    for i in range(n):
        out = jnp.where(row == i, carry[8 - n + i:8 - n + i + 1, :], out)
    return out


def _rw_prep_kernel(p_ref, mu_ref, w0_ref, w2_ref, a0_ref, a2_ref, g2_ref, kk_ref, ka_ref, ones_ref,
                    r_ref, k_ref, v_ref, ld_ref, a_ref, b_ref, g_ref, carry_ref):
    @pl.when(pl.program_id(1) == 0)
    def _():
        carry_ref[...] = jnp.zeros_like(carry_ref)

    p = p_ref[...]
    prev = _shift_rows(p, carry_ref[...], 1)
    carry_ref[...] = p[RW_TILE - 8:, :]
    xm = p + (prev - p) * mu_ref[...]
    w = RW_WIDTH
    r, k, v = xm[:, :w], xm[:, w:2 * w], xm[:, 2 * w:3 * w]
    xw = xm[:, 3 * w:3 * w + RW_DECAY_RANK]
    xa = xm[:, 3 * w + RW_DECAY_RANK:3 * w + RW_DECAY_RANK + RW_AAA_RANK]
    xg = xm[:, 3 * w + RW_DECAY_RANK + RW_AAA_RANK:]
    wlog = -jax.nn.softplus(-(w0_ref[...] + _dot(jnp.tanh(xw), w2_ref[...]))) - 0.5
    a = jax.nn.sigmoid(a0_ref[...] + _dot(xa, a2_ref[...]))
    kk = k * kk_ref[...]
    ss = _dot_exact_rhs(kk * kk, ones_ref[...])
    kk = kk / jnp.maximum(jnp.sqrt(ss), 1e-12)
    r_ref[...] = r
    k_ref[...] = k * (1.0 + (a - 1.0) * ka_ref[...])
    v_ref[...] = v
    ld_ref[...] = -jnp.exp(wlog)
    a_ref[...] = -kk
    b_ref[...] = kk * a
    g_ref[...] = _dot(jax.nn.sigmoid(xg), g2_ref[...])


def _rw_prep(p_rw, b_, s_, mu, w0, w2, a0, a2, g2, k_k, k_a, ones_blk):
    t = p_rw.shape[0]
    nt = s_ // RW_TILE
    row = lambda n: pl.BlockSpec((RW_TILE, n), lambda b, i: (b * nt + i, 0))
    vec = lambda a: a.reshape(1, -1)
    outs = [jax.ShapeDtypeStruct((t, RW_WIDTH), F32)] * 7
    return pl.pallas_call(
        _rw_prep_kernel,
        grid=(b_, nt),
        in_specs=[row(RW_IN), _const_spec((1, RW_IN)), _const_spec((1, RW_WIDTH)),
                  _const_spec(w2.shape), _const_spec((1, RW_WIDTH)), _const_spec(a2.shape),
                  _const_spec(g2.shape), _const_spec((1, RW_WIDTH)), _const_spec((1, RW_WIDTH)),
                  _const_spec(ones_blk.shape)],
        out_specs=[row(RW_WIDTH)] * 7,
        out_shape=outs,
        scratch_shapes=[pltpu.VMEM((8, RW_IN), F32)],
        compiler_params=_params("parallel", "arbitrary"),
        name="rwkv_prep",
    )(p_rw, vec(mu), vec(w0), w2, vec(a0), a2, g2, vec(k_k), vec(k_a), ones_blk)


def _rw_scan_kernel(r_ref, k_ref, v_ref, ld_ref, a_ref, b_ref, g_ref, tri_ref, ones_ref,
                    rk_ref, lng_ref, lnb_ref, o_ref, state_ref):
    c_ = RW_CHUNK
    n_ = RW_HEAD_DIM

    @pl.when(pl.program_id(1) == 0)
    def _():
        state_ref[...] = jnp.zeros_like(state_ref)

    c2 = 2 * c_
    row = lax.broadcasted_iota(jnp.int32, (c2, c2), 0)
    col = lax.broadcasted_iota(jnp.int32, (c2, c2), 1)
    same_head = (row >= c_) == (col >= c_)
    strict = jnp.logical_and(same_head, row > col)
    incl = jnp.logical_and(same_head, row >= col)
    left = lax.broadcasted_iota(jnp.int32, (c_, LANES), 1) < n_

    def block_diag(x):
        return jnp.concatenate([jnp.where(left, x, 0.0), jnp.where(left, 0.0, x)], axis=0)

    pairs = range(RW_HEADS // 2)
    sls = [slice(p * LANES, (p + 1) * LANES) for p in pairs]
    ones2 = ones_ref[...]

    def head_sums(xs):
        parts = [part[:, sl] for x in xs for part in _split3(x) for sl in sls]
        prod = jnp.dot(jnp.concatenate(parts, axis=0), ones2, preferred_element_type=F32)
        outs = []
        for i in range(len(xs)):
            slabs = []
            for p in pairs:
                rows = [((3 * i + j) * len(sls) + p) * c_ for j in range(3)]
                slabs.append(prod[rows[0]:rows[0] + c_] + prod[rows[1]:rows[1] + c_] + prod[rows[2]:rows[2] + c_])
            outs.append(jnp.concatenate(slabs, axis=1))
        return outs

    def chunk_program(bb):
        ld = ld_ref[bb]
        r = r_ref[bb]
        k = k_ref[bb]
        v = v_ref[bb]
        cum = _dot_exact_lhs(tri_ref[...], ld)
        mid = cum[c_ // 2 - 1:c_ // 2, :]
        e_in = jnp.exp(cum - mid)
        e_out = jnp.exp(mid - cum)
        r_t = r * e_in
        a_t = a_ref[bb] * jnp.exp(cum - ld - mid)
        b_t = b_ref[bb] * e_out
        k_t = k * e_out
        e_mid = jnp.exp(mid)
        w_all = jnp.exp(cum[c_ - 1:c_, :])
        w_tail = jnp.exp(cum[c_ - 1:c_, :] - mid)
        yield
        s0 = [state_ref[bb, p] for p in pairs]
        ar = [jnp.concatenate([block_diag(a_t[:, sl]), block_diag(r_t[:, sl])], axis=0) for sl in sls]
        bk = [jnp.concatenate([block_diag(b_t[:, sl]), block_diag(k_t[:, sl])], axis=0) for sl in sls]
        vb = [block_diag(v[:, sl]) for sl in sls]
        m1 = [_dot_nt(ar[p], bk[p]) for p in pairs]
        m2 = [_dot_nt(ar[p] * e_mid[:, sls[p]], s0[p]) for p in pairs]
        yield
        l_ab = [jnp.where(strict, m[:c2, :c2], 0.0) for m in m1]
        l_ak = [jnp.where(strict, m[:c2, c2:], 0.0) for m in m1]
        l_r = [jnp.concatenate([jnp.where(incl, m[c2:, :c2], 0.0), jnp.where(incl, m[c2:, c2:], 0.0)], axis=1)
               for m in m1]
        u = [m2[p][:c2] + _dot(l_ak[p], vb[p]) for p in pairs]
        pw = l_ab
        yield
        n_steps = int(math.log2(c_))
        for step in range(n_steps):
            if step < n_steps - 1:
                prod = [_dot(pw[p], jnp.concatenate([pw[p], u[p]], axis=1)) for p in pairs]
                pw = [q[:, :c2] for q in prod]
                u = [u[p] + prod[p][:, c2:] for p in pairs]
            else:
                u = [u[p] + _dot(pw[p], u[p]) for p in pairs]
            yield
        uv = [jnp.concatenate([u[p], vb[p]], axis=0) for p in pairs]
        ys = [m2[p][c2:] + _dot(l_r[p], uv[p]) for p in pairs]
        upd = [_dot_tn(uv[p], bk[p]) for p in pairs]
        yield
        for p in pairs:
            state_ref[bb, p] = s0[p] * w_all[:, sls[p]] + upd[p] * w_tail[:, sls[p]]
        y = jnp.concatenate([ys[p][:c_] + ys[p][c_:] for p in pairs], axis=1)
        sum_y, sum_rk = head_sums([y, r * k * rk_ref[...]])
        d = y - sum_y * (1.0 / n_)
        yield
        var = head_sums([d * d])[0] * (1.0 / n_)
        yn = d * lax.rsqrt(var + RW_LN_EPS) * lng_ref[...] + lnb_ref[...]
        o_ref[bb] = ((yn + sum_rk * v) * g_ref[bb]).astype(o_ref.dtype)
        yield

    for _ in zip(*[chunk_program(bb) for bb in range(RW_ROWS)]):
        pass


def _rw_scan(r, k, v, ld, a, b, g, b_, s_, r_k, ln_g, ln_b, ones_pair, tri):
    t = r.shape[0]
    nc = s_ // RW_CHUNK
    row = pl.BlockSpec((RW_ROWS, RW_CHUNK, RW_WIDTH), lambda bi, i: (bi, i, 0))
    vec = lambda x: x.reshape(1, -1)
    seq = lambda x: x.reshape(b_, s_, RW_WIDTH)
    out = pl.pallas_call(
        _rw_scan_kernel,
        grid=(b_ // RW_ROWS, nc),
        in_specs=[row] * 7 + [_const_spec(tri.shape), _const_spec(ones_pair.shape),
                              _const_spec((1, RW_WIDTH)), _const_spec((1, RW_WIDTH)), _const_spec((1, RW_WIDTH))],
        out_specs=row,
        out_shape=jax.ShapeDtypeStruct((b_, s_, RW_WIDTH), BF16),
        scratch_shapes=[pltpu.VMEM((RW_ROWS, RW_HEADS // 2, 2 * RW_HEAD_DIM, 2 * RW_HEAD_DIM), F32)],
        compiler_params=_params("parallel", "arbitrary"),
        name="rwkv_scan",
    )(seq(r), seq(k), seq(v), seq(ld), seq(a), seq(b), seq(g), tri, ones_pair, vec(r_k), vec(ln_g), vec(ln_b))
    return out.reshape(t, RW_WIDTH)


def _rt_log_gamma(h):
    return math.log1p(-(2.0 ** (-5.0 - h)))


def _retention_kernel(p_ref, cos_ref, sin_ref, gng_ref, gnb_ref, o_ref, state_ref):
    c_ = RT_CHUNK

    @pl.when(pl.program_id(1) == 0)
    def _():
        state_ref[...] = jnp.zeros_like(state_ref)

    cos = cos_ref[...]
    sin = sin_ref[...]
    row = lax.broadcasted_iota(jnp.int32, (c_, c_), 0).astype(F32)
    col = lax.broadcasted_iota(jnp.int32, (c_, c_), 1).astype(F32)
    rel = row - col
    heads = range(RT_HEADS)
    lgs = [_rt_log_gamma(h) for h in heads]
    sls = [slice(h * RT_DIM, (h + 1) * RT_DIM) for h in heads]
    decay = [jnp.where(rel >= 0, jnp.exp(jnp.maximum(rel, 0.0) * lg), 0.0) for lg in lgs]
    xi = [jnp.exp((row + 1.0) * lg) for lg in lgs]
    zeta = [jnp.exp((c_ - 1.0 - row) * lg) for lg in lgs]

    def chunk_program(bb):
        col_of = lambda part, h: slice(part * RT_WIDTH + h * RT_DIM, part * RT_WIDTH + (h + 1) * RT_DIM)
        q = [p_ref[bb, :, col_of(0, h)] for h in heads]
        k = [p_ref[bb, :, col_of(1, h)] for h in heads]
        v = [p_ref[bb, :, col_of(2, h)] for h in heads]
        q = [x * cos + pltpu.roll(x, RT_DIM // 2, 1) * sin for x in q]
        k = [(x * cos + pltpu.roll(x, RT_DIM // 2, 1) * sin) * (RT_DIM ** -0.5) for x in k]
        s0 = [state_ref[bb, h] for h in heads]
        yield
        scores = [_dot_nt(q[h], k[h]) * decay[h] for h in heads]
        cross = [_dot(q[h], s0[h]) * xi[h] for h in heads]
        for h in heads:
            state_ref[bb, h] = s0[h] * math.exp(c_ * lgs[h]) + _dot_tn(k[h] * zeta[h], v[h])
        yield
        o = [_dot(scores[h], v[h]) + cross[h] for h in heads]
        yield
        for h in heads:
            mu = jnp.mean(o[h], axis=-1, keepdims=True)
            d = o[h] - mu
            var = jnp.mean(d * d, axis=-1, keepdims=True)
            on = d * lax.rsqrt(var + EPS) * gng_ref[:, sls[h]] + gnb_ref[:, sls[h]]
            gate = p_ref[bb, :, col_of(3, h)]
            o_ref[bb, :, sls[h]] = (gate * jax.nn.sigmoid(gate) * on).astype(o_ref.dtype)
        yield

    for _ in zip(*[chunk_program(bb) for bb in range(RT_ROWS)]):
        pass


def _retention(p_rt, b_, s_, cos2, sin2, gn_g, gn_b):
    t = p_rt.shape[0]
    nc = s_ // RT_CHUNK
    out = pl.pallas_call(
        _retention_kernel,
        grid=(b_ // RT_ROWS, nc),
        in_specs=[pl.BlockSpec((RT_ROWS, RT_CHUNK, RT_IN), lambda bi, i: (bi, i, 0)),
                  pl.BlockSpec((RT_CHUNK, RT_DIM), lambda bi, i: (i, 0)),
                  pl.BlockSpec((RT_CHUNK, RT_DIM), lambda bi, i: (i, 0)),
                  _const_spec((1, RT_WIDTH)), _const_spec((1, RT_WIDTH))],
        out_specs=pl.BlockSpec((RT_ROWS, RT_CHUNK, RT_WIDTH), lambda bi, i: (bi, i, 0)),
        out_shape=jax.ShapeDtypeStruct((b_, s_, RT_WIDTH), BF16),
        scratch_shapes=[pltpu.VMEM((RT_ROWS, RT_HEADS, RT_DIM, RT_DIM), F32)],
        compiler_params=_params("parallel", "arbitrary"),
        name="retention",
    )(p_rt.reshape(b_, s_, RT_IN), cos2, sin2, gn_g.reshape(1, -1), gn_b.reshape(1, -1))
    return out.reshape(t, RT_WIDTH)


DSA_BLOCKS_PER_TILE = ROW_TILE // Q_BLOCK


def _dsa_prep_kernel(cq_ref, ckv_ref, kw_ref, qg_ref, kvg_ref, wuq_ref, wuk_ref, wqi_ref,
                     ckvn_ref, ckvt_ref, kidx_ref, widx_ref, qlat_ref, qidx_ref):
    cq = _rms(cq_ref[...], qg_ref[...]).astype(BF16)
    ckvn = _rms(ckv_ref[...], kvg_ref[...])
    ckvn_ref[...] = ckvn.astype(BF16)
    kw = kw_ref[...]
    kidx_ref[...] = kw[:, :IDX_DIM].astype(BF16)
    q = jnp.dot(cq, wuq_ref[...], preferred_element_type=F32)
    qi = jnp.dot(cq, wqi_ref[...], preferred_element_type=F32)
    for blk in range(DSA_BLOCKS_PER_TILE):
        rows = slice(blk * Q_BLOCK, (blk + 1) * Q_BLOCK)
        ckvt_ref[blk] = ckvn[rows].T.astype(BF16)
        widx_ref[blk] = kw[rows].T[IDX_DIM:IDX_DIM + IDX_HEADS] * ((IDX_HEADS * IDX_DIM) ** -0.5)
    for h in range(DSA_HEADS):
        qh = q[:, h * DSA_HEAD_DIM:(h + 1) * DSA_HEAD_DIM]
        ql = (_dot(qh, wuk_ref[h]) * (DSA_HEAD_DIM ** -0.5)).astype(BF16)
        qih = qi[:, h * IDX_DIM:(h + 1) * IDX_DIM].astype(BF16)
        for blk in range(DSA_BLOCKS_PER_TILE):
            rows = slice(blk * Q_BLOCK, (blk + 1) * Q_BLOCK)
            qlat_ref[blk, h] = ql[rows]
            qidx_ref[blk, h] = qih[rows]


def _dsa_prep(c_q, c_kv, kw, q_g, kv_g, w_uq, w_uk, w_qi):
    t = c_q.shape[0]
    row = lambda n: pl.BlockSpec((ROW_TILE, n), lambda i: (i, 0))
    blk3 = lambda m, n: pl.BlockSpec((DSA_BLOCKS_PER_TILE, m, n), lambda i: (i, 0, 0))
    blk4 = lambda n: pl.BlockSpec((DSA_BLOCKS_PER_TILE, DSA_HEADS, Q_BLOCK, n), lambda i: (i, 0, 0, 0))
    return pl.pallas_call(
        _dsa_prep_kernel,
        grid=(t // ROW_TILE,),
        in_specs=[row(DSA_Q_RANK), row(DSA_KV_RANK), row(LANES), _const_spec((1, DSA_Q_RANK)),
                  _const_spec((1, DSA_KV_RANK)),
                  _const_spec(w_uq.shape), _const_spec(w_uk.shape), _const_spec(w_qi.shape)],
        out_specs=[row(DSA_KV_RANK), blk3(DSA_KV_RANK, Q_BLOCK), row(IDX_DIM), blk3(IDX_HEADS, Q_BLOCK),
                   blk4(DSA_KV_RANK), blk4(IDX_DIM)],
        out_shape=[jax.ShapeDtypeStruct((t, DSA_KV_RANK), BF16),
                   jax.ShapeDtypeStruct((t // Q_BLOCK, DSA_KV_RANK, Q_BLOCK), BF16),
                   jax.ShapeDtypeStruct((t, IDX_DIM), BF16),
                   jax.ShapeDtypeStruct((t // Q_BLOCK, IDX_HEADS, Q_BLOCK), F32),
                   jax.ShapeDtypeStruct((t // Q_BLOCK, DSA_HEADS, Q_BLOCK, DSA_KV_RANK), BF16),
                   jax.ShapeDtypeStruct((t // Q_BLOCK, IDX_HEADS, Q_BLOCK, IDX_DIM), BF16)],
        compiler_params=_params("parallel"),
        name="dsa_prep",
    )(c_q, c_kv, kw, q_g.reshape(1, -1), kv_g.reshape(1, -1), w_uq, w_uk, w_qi)


def _fold_rows(x, op):
    return functools.reduce(op, [x[i:i + 8] for i in range(0, x.shape[0], 8)])


def _bit_transpose32(words):
    a = list(words)
    j, m = 16, 0x0000FFFF
    while j:
        k = 0
        while k < 32:
            t = (a[k] ^ lax.shift_right_logical(a[k + j], jnp.int32(j))) & jnp.int32(m)
            a[k] = a[k] ^ t
            a[k + j] = a[k + j] ^ lax.shift_left(t, jnp.int32(j))
            k = (k + j + 1) & ~j
        j >>= 1
        m ^= m << j
    return a


def _dsa_kernel(top_k, qidx_ref, widx_ref, qlat_ref, kidx_ref, ckv_ref, ckvt_ref, wuv_ref, tril_ref, o_ref,
                key_ref, planes_ref, s_ref, acc_ref):
    qb = Q_BLOCK
    nh = DSA_HEADS
    j = pl.program_id(1)
    n_chunks = j + 1

    keyi = lax.broadcasted_iota(jnp.int32, (qb, qb), 0)
    qryi = lax.broadcasted_iota(jnp.int32, (qb, qb), 1)
    lanes = lambda h: slice(h * qb, (h + 1) * qb)

    def key_chunk(ref, c):
        return ref[pl.ds(pl.multiple_of(c * qb, qb), qb), :]

    w_idx = widx_ref[0]
    q_idx = qidx_ref[0].reshape(IDX_HEADS * qb, IDX_DIM)
    q_lat = qlat_ref[0].reshape(nh * qb, DSA_KV_RANK)

    n_pairs = (n_chunks + 1) // 2

    def score_chunk(c):
        logits = lax.dot_general(key_chunk(kidx_ref, c), q_idx, (((1,), (1,)), ((), ())),
                                 preferred_element_type=F32)
        score = jnp.zeros((qb, qb), F32)
        for h in range(IDX_HEADS):
            score = score + jnp.maximum(logits[:, lanes(h)], 0.0) * w_idx[h:h + 1, :]
        bits = lax.bitcast_convert_type(score, jnp.int32)
        key = jnp.where(bits < 0, bits ^ jnp.int32(0x7FFFFFFF), bits)
        causal = keyi + c * qb <= qryi + j * qb
        key = jnp.where(causal, key, jnp.int32(INT_MIN))
        key_ref[c] = key
        return key

    def score_body(pi, carry):
        keys = [score_chunk(2 * pi), score_chunk(2 * pi + 1)]
        words = [k[r:r + 8] ^ jnp.int32(INT_MIN) for k in keys for r in range(0, qb, 8)]
        for idx, plane in enumerate(_bit_transpose32(words)):
            planes_ref[31 - idx, pi] = plane
        return carry

    lax.fori_loop(0, n_pairs, score_body, 0)

    n_pairs_max = planes_ref.shape[1]
    live = tuple(jnp.where(p < n_pairs, jnp.full((8, qb), -1, jnp.int32), jnp.zeros((8, qb), jnp.int32))
                 for p in range(n_pairs_max))

    def bit_body(i, carry):
        thr_u, above, eq = carry
        plane = planes_ref[31 - i]
        hits = [eq[p] & plane[p] for p in range(n_pairs_max)]
        cnt = functools.reduce(jnp.add, [lax.population_count(h) for h in hits])
        cnt = above + jnp.sum(cnt, axis=0, keepdims=True)
        accept = cnt >= top_k
        thr_u = jnp.where(accept, thr_u | lax.shift_left(jnp.int32(1), 31 - i), thr_u)
        above = jnp.where(accept, above, cnt)
        eq = tuple(jnp.where(accept, h, e ^ h) for e, h in zip(eq, hits))
        return thr_u, above, eq

    zero_row = jnp.zeros((1, qb), jnp.int32)
    thr_u, above, _ = lax.fori_loop(0, 32, bit_body, (zero_row, zero_row, live))
    thr = thr_u ^ jnp.int32(INT_MIN)
    need = (top_k - above).astype(F32)

    def logit_chunk(c, taken):
        key = key_ref[c]
        causal = keyi + c * qb <= qryi + j * qb
        eq = jnp.logical_and(key == thr, causal)
        eq_f = jnp.where(eq, 1.0, 0.0)
        rank = taken + jnp.dot(tril_ref[...], eq_f.astype(BF16), preferred_element_type=F32)
        sel = jnp.logical_or(jnp.logical_and(key > thr, causal), jnp.logical_and(eq, rank <= need))
        s = lax.dot_general(key_chunk(ckv_ref, c), q_lat, (((1,), (1,)), ((), ())),
                            preferred_element_type=F32)
        tops = []
        for h in range(nh):
            sh = jnp.where(sel, s[:, lanes(h)], NEG_BIG)
            s_ref[c, :, lanes(h)] = sh
            tops.append(_fold_rows(sh, jnp.maximum))
        return tops, taken + jnp.sum(eq_f, axis=0, keepdims=True)

    def logit_body(pi, carry):
        taken, tops = carry
        ta, taken = logit_chunk(2 * pi, taken)
        tb, taken = logit_chunk(2 * pi + 1, taken)
        return taken, tuple(jnp.maximum(t, jnp.maximum(a, b)) for t, a, b in zip(tops, ta, tb))

    _, tops = lax.fori_loop(0, n_pairs, logit_body,
                            (jnp.zeros((1, qb), F32), tuple(jnp.full((8, qb), NEG_BIG, F32) for _ in range(nh))))
    tops = [jnp.max(t, axis=0, keepdims=True) for t in tops]

    acc_ref[...] = jnp.zeros_like(acc_ref)

    def value_body(pi, sums):
        ps, new_sums = [], []
        for h in range(nh):
            pa = jnp.exp(s_ref[2 * pi, :, lanes(h)] - tops[h])
            pb = jnp.exp(s_ref[2 * pi + 1, :, lanes(h)] - tops[h])
            new_sums.append(sums[h] + _fold_rows(pa + pb, jnp.add))
            ps.append(jnp.concatenate([pa.astype(BF16), pb.astype(BF16)], axis=0))
        ckvt_pair = jnp.concatenate([ckvt_ref[2 * pi], ckvt_ref[2 * pi + 1]], axis=1)
        acc_ref[...] += jnp.dot(ckvt_pair, jnp.concatenate(ps, axis=1), preferred_element_type=F32)
        return tuple(new_sums)

    sums = lax.fori_loop(0, n_pairs, value_body, tuple(jnp.zeros((8, qb), F32) for _ in range(nh)))

    outs = []
    for h in range(nh):
        o_lat_t = acc_ref[:, lanes(h)] / jnp.sum(sums[h], axis=0, keepdims=True)
        outs.append(_dot_tn(o_lat_t, wuv_ref[h]))
    o_ref[...] = jnp.concatenate(outs, axis=1).astype(o_ref.dtype)


def _dsa(q_idx, w_idx, q_lat, k_idx, ckv_n, ckv_t, w_uv, tril, b_, s_):
    nb = s_ // Q_BLOCK
    t = b_ * s_
    top_k = min(TOPK_MAX, s_ // 4)
    blk = lambda n: pl.BlockSpec((Q_BLOCK, n), lambda bi, i: (bi * nb + i, 0))
    blk3 = lambda m, n: pl.BlockSpec((1, m, n), lambda bi, i: (bi * nb + i, 0, 0))
    blk4 = lambda n: pl.BlockSpec((1, DSA_HEADS, Q_BLOCK, n), lambda bi, i: (bi * nb + i, 0, 0, 0))
    seq = lambda n: pl.BlockSpec((s_, n), lambda bi, i: (bi, 0))
    stacked = DSA_HEADS * Q_BLOCK
    return pl.pallas_call(
        functools.partial(_dsa_kernel, top_k),
        grid=(b_, nb),
        in_specs=[blk4(IDX_DIM), blk3(IDX_HEADS, Q_BLOCK), blk4(DSA_KV_RANK),
                  seq(IDX_DIM), seq(DSA_KV_RANK),
                  pl.BlockSpec((nb, DSA_KV_RANK, Q_BLOCK), lambda bi, i: (bi, 0, 0)),
                  _const_spec(w_uv.shape), _const_spec(tril.shape)],
        out_specs=blk(DSA_WIDTH),
        out_shape=jax.ShapeDtypeStruct((t, DSA_WIDTH), BF16),
        scratch_shapes=[pltpu.VMEM((nb, Q_BLOCK, Q_BLOCK), jnp.int32),
                        pltpu.VMEM((32, nb // 2, 8, Q_BLOCK), jnp.int32),
                        pltpu.VMEM((nb, Q_BLOCK, stacked), F32),
                        pltpu.VMEM((DSA_KV_RANK, stacked), F32)],
        compiler_params=_params("parallel", "arbitrary"),
        name="dsa_attn",
    )(q_idx, w_idx, q_lat, k_idx, ckv_n, ckv_t, w_uv, tril)


SC_TILE = 512


def _sconv_kernel(p_ref, w_ref, b_ref, o_ref, carry_ref):
    @pl.when(pl.program_id(1) == 0)
    def _():
        carry_ref[...] = jnp.zeros_like(carry_ref)

    h = p_ref[:, :SC_WIDTH]
    gate_b = p_ref[:, SC_WIDTH:2 * SC_WIDTH]
    gate_c = p_ref[:, 2 * SC_WIDTH:]
    u = gate_c * h
    carry = carry_ref[...]
    y = u * w_ref[2:3, :] + _shift_rows(u, carry, 1) * w_ref[1:2, :] + _shift_rows(u, carry, 2) * w_ref[0:1, :]
    carry_ref[...] = u[SC_TILE - 8:, :]
    o_ref[...] = (gate_b * (y + b_ref[...])).astype(o_ref.dtype)


def _sconv(p_sc, b_, s_, conv_w, conv_b):
    t = p_sc.shape[0]
    nt = s_ // SC_TILE
    return pl.pallas_call(
        _sconv_kernel,
        grid=(b_, nt),
        in_specs=[pl.BlockSpec((SC_TILE, 3 * SC_WIDTH), lambda bi, i: (bi * nt + i, 0)),
                  _const_spec((8, SC_WIDTH)), _const_spec((1, SC_WIDTH))],
        out_specs=pl.BlockSpec((SC_TILE, SC_WIDTH), lambda bi, i: (bi * nt + i, 0)),
        out_shape=jax.ShapeDtypeStruct((t, SC_WIDTH), BF16),
        scratch_shapes=[pltpu.VMEM((8, SC_WIDTH), F32)],
        compiler_params=_params("parallel", "arbitrary"),
        name="short_conv",
    )(p_sc, jnp.pad(conv_w, ((0, 8 - SC_KERNEL), (0, 0))), conv_b.reshape(1, -1))


def _xattn_kernel(x_ref, gq_ref, wq_ref, k_ref, v_ref, wo_ref, go_ref, o_ref, att_ref):
    x = x_ref[...]
    q = jnp.dot(_rms(x, gq_ref[...]).astype(BF16), wq_ref[...], preferred_element_type=F32)
    for h in range(XA_HEADS):
        sl = slice(h * XA_HEAD_DIM, (h + 1) * XA_HEAD_DIM)
        s = _dot_nt(q[:, sl], k_ref[:, sl]) * (XA_HEAD_DIM ** -0.5)
        s = s - jnp.max(s, axis=-1, keepdims=True)
        p = jnp.exp(s)
        p = p / jnp.sum(p, axis=-1, keepdims=True)
        att_ref[:, sl] = _dot(p, v_ref[:, sl])
    hout = jnp.dot(att_ref[...].astype(BF16), wo_ref[...], preferred_element_type=F32)
    o_ref[...] = x + _rms(hout, go_ref[...])


def _xattn(x, g_q, wq, k_mem, v_mem, wo, g_o, b_, s_):
    t = x.shape[0]
    nt = s_ // ROW_TILE
    row = pl.BlockSpec((ROW_TILE, D_MODEL), lambda bi, i: (bi * nt + i, 0))
    mem = pl.BlockSpec((MEM_LEN, XA_WIDTH), lambda bi, i: (bi, 0))
    return pl.pallas_call(
        _xattn_kernel,
        grid=(b_, nt),
        in_specs=[row, _const_spec((1, D_MODEL)), _const_spec(wq.shape), mem, mem,
                  _const_spec(wo.shape), _const_spec((1, D_MODEL))],
        out_specs=row,
        out_shape=jax.ShapeDtypeStruct((t, D_MODEL), F32),
        scratch_shapes=[pltpu.VMEM((ROW_TILE, XA_WIDTH), F32)],
        compiler_params=_params("parallel", "parallel"),
        name="mem_xattn",
    )(x, g_q.reshape(1, -1), wq, k_mem, v_mem, wo, g_o.reshape(1, -1))


def _block_diag(n_blocks, size, value):
    return np.kron(np.eye(n_blocks, dtype=np.float32), np.full((size, size), value, np.float32))


def _rope_tables(s_):
    half = RT_DIM // 2
    inv_freq = RT_ROPE_BASE ** (-jnp.arange(half, dtype=F32) / half)
    ang = jnp.arange(s_).astype(F32)[:, None] * inv_freq[None, :]
    cos, sin = jnp.cos(ang), jnp.sin(ang)
    return jnp.concatenate([cos, cos], axis=-1), jnp.concatenate([-sin, sin], axis=-1)


def kernel(x, mem, norm_g, mem_norm_g, ffn_w_gate, ffn_w_up, ffn_w_down, xa_wq, xa_wk, xa_wv, xa_wo, ev_w_in, ev_w_out, rw_mu, rw_w0, rw_w2, rw_a0, rw_a2, rw_g2, rw_k_k, rw_k_a, rw_r_k, rw_ln_g, rw_ln_b, rt_gn_g, rt_gn_b, od_w_in, od_w_out, dsa_q_norm_g, dsa_kv_norm_g, dsa_w_uq, dsa_w_uk, dsa_w_uv, dsa_w_qi, sc_conv_w, sc_conv_b):
    b_, s_, d_ = x.shape
    depth = norm_g.shape[0]
    t = b_ * s_
    bf = lambda w: w.astype(BF16)

    ones_blk = jnp.asarray(_block_diag(RW_HEADS, RW_HEAD_DIM, 1.0), BF16)
    ones_pair = jnp.asarray(_block_diag(2, RW_HEAD_DIM, 1.0), BF16)
    tri_rw = jnp.asarray(np.tril(np.ones((RW_CHUNK, RW_CHUNK), np.float32)), BF16)
    tril_dsa = jnp.asarray(np.tril(np.ones((Q_BLOCK, Q_BLOCK), np.float32)), BF16)
    cos2, sin2 = _rope_tables(s_)

    xf = x.reshape(t, d_)
    mem_f = mem.reshape(b_ * MEM_LEN, d_)
    for l in range(depth):
        ng = norm_g[l]
        i = l // 2
        xf = _ffn(xf, ng[0], bf(ffn_w_gate[l, 0]), bf(ffn_w_up[l, 0]), bf(ffn_w_down[l, 0]), ng[1])
        if l % 2 == 0:
            w_in = bf(ev_w_in[i])
            p_rw, p_rt = _norm_proj(xf, ng[2], [w_in[:, :RW_IN], w_in[:, RW_IN:]], "ev_in_proj")
            r, k, v, ld, a, b, g = _rw_prep(p_rw, b_, s_, rw_mu[i], rw_w0[i], rw_w2[i], rw_a0[i], rw_a2[i],
                                            rw_g2[i], rw_k_k[i], rw_k_a[i], ones_blk)
            y_a = _rw_scan(r, k, v, ld, a, b, g, b_, s_, rw_r_k[i], rw_ln_g[i], rw_ln_b[i],
                           ones_pair, tri_rw)
            y_b = _retention(p_rt, b_, s_, cos2, sin2, rt_gn_g[i], rt_gn_b[i])
            w_out = bf(ev_w_out[i])
            xf = _proj_res([y_a, y_b], [w_out[:RW_WIDTH], w_out[RW_WIDTH:]], ng[3], xf, "ev_out_proj")
        else:
            w_in = od_w_in[i]
            kw_w = jnp.pad(w_in[:, DSA_Q_RANK + DSA_KV_RANK:DSA_IN], ((0, 0), (0, LANES - IDX_DIM - IDX_HEADS)))
            c_q, c_kv, kw, p_sc = _norm_proj(
                xf, ng[2], [bf(w_in[:, :DSA_Q_RANK]), bf(w_in[:, DSA_Q_RANK:DSA_Q_RANK + DSA_KV_RANK]),
                            bf(kw_w), bf(w_in[:, DSA_IN:])], "od_in_proj")
            ckv_n, ckv_t, k_idx, w_idx, q_lat, q_idx = _dsa_prep(
                c_q, c_kv, kw, dsa_q_norm_g[i], dsa_kv_norm_g[i],
                bf(dsa_w_uq[i].reshape(DSA_Q_RANK, DSA_WIDTH)), bf(dsa_w_uk[i]),
                bf(dsa_w_qi[i].reshape(DSA_Q_RANK, IDX_HEADS * IDX_DIM)))
            y_c = _dsa(q_idx, w_idx, q_lat, k_idx, ckv_n, ckv_t, bf(dsa_w_uv[i]), tril_dsa, b_, s_)
            y_d = _sconv(p_sc, b_, s_, sc_conv_w[i], sc_conv_b[i])
            w_out = bf(od_w_out[i])
            xf = _proj_res([y_c, y_d], [w_out[:DSA_WIDTH], w_out[DSA_WIDTH:]], ng[3], xf, "od_out_proj")
        k_mem, v_mem = _norm_proj(mem_f, mem_norm_g, [bf(xa_wk[l]), bf(xa_wv[l])], "mem_kv_proj")
        xf = _xattn(xf, ng[4], bf(xa_wq[l]), k_mem, v_mem, bf(xa_wo[l]), ng[5], b_, s_)
        xf = _ffn(xf, ng[6], bf(ffn_w_gate[l, 1]), bf(ffn_w_up[l, 1]), bf(ffn_w_down[l, 1]), ng[7])
    return xf.reshape(b_, s_, d_)
```

```python
import functools
import math

import numpy as np
import jax
import jax.numpy as jnp
from jax import lax
from jax.experimental import pallas as pl
from jax.experimental.pallas import tpu as pltpu

F32 = jnp.float32
BF16 = jnp.bfloat16

D_MODEL = 1024
D_FF = 2816
EPS = 1e-6
MEM_LEN = 256
RW_HEADS = 8
RW_HEAD_DIM = 64
RW_WIDTH = RW_HEADS * RW_HEAD_DIM
RW_DECAY_RANK = 64
RW_AAA_RANK = 64
RW_GATE_RANK = 128
RW_LN_EPS = 64e-5
RW_IN = 3 * RW_WIDTH + RW_DECAY_RANK + RW_AAA_RANK + RW_GATE_RANK
RW_CHUNK = 64
RW_ROWS = 4
RT_HEADS = 4
RT_DIM = 128
RT_WIDTH = RT_HEADS * RT_DIM
RT_CHUNK = 128
RT_ROWS = 4
RT_ROPE_BASE = 10000.0
RT_IN = 4 * RT_WIDTH
DSA_HEADS = 8
DSA_HEAD_DIM = 64
DSA_WIDTH = DSA_HEADS * DSA_HEAD_DIM
DSA_Q_RANK = 256
DSA_KV_RANK = 128
IDX_HEADS = 8
IDX_DIM = 64
TOPK_MAX = 256
Q_BLOCK = 128
DSA_IN = DSA_Q_RANK + DSA_KV_RANK + IDX_DIM + IDX_HEADS
SC_WIDTH = 512
SC_KERNEL = 3
XA_HEADS = 4
XA_HEAD_DIM = 128
XA_WIDTH = XA_HEADS * XA_HEAD_DIM

LANES = 128
ROW_TILE = 512
VMEM_LIMIT = 56 * 1024 * 1024
INT_MIN = -2 ** 31
NEG_BIG = -1e30


def _params(*sem):
    return pltpu.CompilerParams(dimension_semantics=sem, vmem_limit_bytes=VMEM_LIMIT)


def _rms(x, g):
    return x * lax.rsqrt(jnp.mean(x * x, axis=-1, keepdims=True) + EPS) * g


def _dot(a, b):
    return jnp.dot(a.astype(BF16), b.astype(BF16), preferred_element_type=F32)


def _dot_nt(a, b):
    return lax.dot_general(a.astype(BF16), b.astype(BF16), (((1,), (1,)), ((), ())),
                           preferred_element_type=F32)


def _dot_tn(a, b):
    return lax.dot_general(a.astype(BF16), b.astype(BF16), (((0,), (0,)), ((), ())),
                           preferred_element_type=F32)


def _split3(x):
    hi = x.astype(BF16)
    r1 = x - hi.astype(F32)
    mid = r1.astype(BF16)
    lo = (r1 - mid.astype(F32)).astype(BF16)
    return hi, mid, lo


def _dot_exact_rhs(x, w_bf16):
    hi, mid, lo = _split3(x)
    out = jnp.dot(hi, w_bf16, preferred_element_type=F32)
    out += jnp.dot(mid, w_bf16, preferred_element_type=F32)
    out += jnp.dot(lo, w_bf16, preferred_element_type=F32)
    return out


def _dot_exact_lhs(w_bf16, x):
    hi, mid, lo = _split3(x)
    out = jnp.dot(w_bf16, hi, preferred_element_type=F32)
    out += jnp.dot(w_bf16, mid, preferred_element_type=F32)
    out += jnp.dot(w_bf16, lo, preferred_element_type=F32)
    return out


def _const_spec(shape):
    nd = len(shape)
    return pl.BlockSpec(shape, lambda *_: (0,) * nd, pipeline_mode=pl.Buffered(1))


FF_CHUNK = 256
FF_TILE = 512


def _ffn_kernel(x_ref, gin_ref, wg_ref, wu_ref, wd_ref, gout_ref, o_ref, acc_ref):
    x = x_ref[...]
    xb = _rms(x, gin_ref[...]).astype(BF16)
    for c in range(D_FF // FF_CHUNK):
        sl = slice(c * FF_CHUNK, (c + 1) * FF_CHUNK)
        g = jnp.dot(xb, wg_ref[:, sl], preferred_element_type=F32)
        u = jnp.dot(xb, wu_ref[:, sl], preferred_element_type=F32)
        h = (g * jax.nn.sigmoid(g) * u).astype(BF16)
        part = jnp.dot(h, wd_ref[sl, :], preferred_element_type=F32)
        if c == 0:
            acc_ref[...] = part
        else:
            acc_ref[...] += part
    o_ref[...] = x + 0.5 * _rms(acc_ref[...], gout_ref[...])


def _ffn(x, g_in, wg, wu, wd, g_out):
    t = x.shape[0]
    row = pl.BlockSpec((FF_TILE, D_MODEL), lambda i: (i, 0))
    return pl.pallas_call(
        _ffn_kernel,
        grid=(t // FF_TILE,),
        in_specs=[row, _const_spec((1, D_MODEL)), _const_spec((D_MODEL, D_FF)),
                  _const_spec((D_MODEL, D_FF)), _const_spec((D_FF, D_MODEL)), _const_spec((1, D_MODEL))],
        out_specs=row,
        out_shape=jax.ShapeDtypeStruct((t, D_MODEL), F32),
        scratch_shapes=[pltpu.VMEM((FF_TILE, D_MODEL), F32)],
        compiler_params=_params("parallel"),
        name="ffn_half",
    )(x, g_in.reshape(1, -1), wg, wu, wd, g_out.reshape(1, -1))


PROJ_CHUNK = 512


def _norm_proj_kernel(n_out, x_ref, g_ref, *refs):
    xb = _rms(x_ref[...], g_ref[...]).astype(BF16)
    for w_ref, o_ref in zip(refs[:n_out], refs[n_out:]):
        n = w_ref.shape[1]
        for c in range(0, n, PROJ_CHUNK):
            sl = slice(c, min(c + PROJ_CHUNK, n))
            o_ref[:, sl] = jnp.dot(xb, w_ref[:, sl], preferred_element_type=F32)


def _norm_proj(x, g, ws, name):
    t, d = x.shape
    row = lambda n: pl.BlockSpec((ROW_TILE, n), lambda i: (i, 0))
    return pl.pallas_call(
        functools.partial(_norm_proj_kernel, len(ws)),
        grid=(t // ROW_TILE,),
        in_specs=[row(d), _const_spec((1, d))] + [_const_spec(w.shape) for w in ws],
        out_specs=[row(w.shape[1]) for w in ws],
        out_shape=[jax.ShapeDtypeStruct((t, w.shape[1]), F32) for w in ws],
        compiler_params=_params("parallel"),
        name=name,
    )(x, g.reshape(1, -1), *ws)


def _proj_res_kernel(n_in, *refs):
    y_refs, w_refs = refs[:n_in], refs[n_in:2 * n_in]
    g_ref, x_ref, o_ref = refs[2 * n_in:]
    h = None
    for y_ref, w_ref in zip(y_refs, w_refs):
        part = jnp.dot(y_ref[...].astype(BF16), w_ref[...], preferred_element_type=F32)
        h = part if h is None else h + part
    o_ref[...] = x_ref[...] + _rms(h, g_ref[...])


def _proj_res(ys, ws, g, x, name):
    t, d = x.shape
    row = lambda n: pl.BlockSpec((ROW_TILE, n), lambda i: (i, 0))
    return pl.pallas_call(
        functools.partial(_proj_res_kernel, len(ys)),
        grid=(t // ROW_TILE,),
        in_specs=[row(y.shape[1]) for y in ys] + [_const_spec(w.shape) for w in ws]
                 + [_const_spec((1, d)), row(d)],
        out_specs=row(d),
        out_shape=jax.ShapeDtypeStruct((t, d), F32),
        compiler_params=_params("parallel"),
        name=name,
    )(*ys, *ws, g.reshape(1, -1), x)


RW_TILE = 512


def _shift_rows(x, carry, n):
    rolled = pltpu.roll(x, n, 0)
    row = lax.broadcasted_iota(jnp.int32, x.shape, 0)
    out = rolled
    for i in range(n):
        out = jnp.where(row == i, carry[8 - n + i:8 - n + i + 1, :], out)
    return out


def _rw_prep_kernel(p_ref, mu_ref, w0_ref, w2_ref, a0_ref, a2_ref, g2_ref, kk_ref, ka_ref, ones_ref,
                    r_ref, k_ref, v_ref, ld_ref, a_ref, b_ref, g_ref, carry_ref):
    @pl.when(pl.program_id(1) == 0)
    def _():
        carry_ref[...] = jnp.zeros_like(carry_ref)

    p = p_ref[...]
    prev = _shift_rows(p, carry_ref[...], 1)
    carry_ref[...] = p[RW_TILE - 8:, :]
    xm = p + (prev - p) * mu_ref[...]
    w = RW_WIDTH
    r, k, v = xm[:, :w], xm[:, w:2 * w], xm[:, 2 * w:3 * w]
    xw = xm[:, 3 * w:3 * w + RW_DECAY_RANK]
    xa = xm[:, 3 * w + RW_DECAY_RANK:3 * w + RW_DECAY_RANK + RW_AAA_RANK]
    xg = xm[:, 3 * w + RW_DECAY_RANK + RW_AAA_RANK:]
    wlog = -jax.nn.softplus(-(w0_ref[...] + _dot(jnp.tanh(xw), w2_ref[...]))) - 0.5
    a = jax.nn.sigmoid(a0_ref[...] + _dot(xa, a2_ref[...]))
    kk = k * kk_ref[...]
    ss = _dot_exact_rhs(kk * kk, ones_ref[...])
    kk = kk / jnp.maximum(jnp.sqrt(ss), 1e-12)
    r_ref[...] = r
    k_ref[...] = k * (1.0 + (a - 1.0) * ka_ref[...])
    v_ref[...] = v
    ld_ref[...] = -jnp.exp(wlog)
    a_ref[...] = -kk
    b_ref[...] = kk * a
    g_ref[...] = _dot(jax.nn.sigmoid(xg), g2_ref[...])


def _rw_prep(p_rw, b_, s_, mu, w0, w2, a0, a2, g2, k_k, k_a, ones_blk):
    t = p_rw.shape[0]
    nt = s_ // RW_TILE
    row = lambda n: pl.BlockSpec((RW_TILE, n), lambda b, i: (b * nt + i, 0))
    vec = lambda a: a.reshape(1, -1)
    outs = [jax.ShapeDtypeStruct((t, RW_WIDTH), F32)] * 7
    return pl.pallas_call(
        _rw_prep_kernel,
        grid=(b_, nt),
        in_specs=[row(RW_IN), _const_spec((1, RW_IN)), _const_spec((1, RW_WIDTH)),
                  _const_spec(w2.shape), _const_spec((1, RW_WIDTH)), _const_spec(a2.shape),
                  _const_spec(g2.shape), _const_spec((1, RW_WIDTH)), _const_spec((1, RW_WIDTH)),
                  _const_spec(ones_blk.shape)],
        out_specs=[row(RW_WIDTH)] * 7,
        out_shape=outs,
        scratch_shapes=[pltpu.VMEM((8, RW_IN), F32)],
        compiler_params=_params("parallel", "arbitrary"),
        name="rwkv_prep",
    )(p_rw, vec(mu), vec(w0), w2, vec(a0), a2, g2, vec(k_k), vec(k_a), ones_blk)


def _rw_scan_kernel(r_ref, k_ref, v_ref, ld_ref, a_ref, b_ref, g_ref, tri_ref, ones_ref,
                    rk_ref, lng_ref, lnb_ref, o_ref, state_ref):
    c_ = RW_CHUNK
    n_ = RW_HEAD_DIM

    @pl.when(pl.program_id(1) == 0)
    def _():
        state_ref[...] = jnp.zeros_like(state_ref)

    c2 = 2 * c_
    row = lax.broadcasted_iota(jnp.int32, (c2, c2), 0)
    col = lax.broadcasted_iota(jnp.int32, (c2, c2), 1)
    same_head = (row >= c_) == (col >= c_)
    strict = jnp.logical_and(same_head, row > col)
    incl = jnp.logical_and(same_head, row >= col)
    left = lax.broadcasted_iota(jnp.int32, (c_, LANES), 1) < n_

    def block_diag(x):
        return jnp.concatenate([jnp.where(left, x, 0.0), jnp.where(left, 0.0, x)], axis=0)

    pairs = range(RW_HEADS // 2)
    sls = [slice(p * LANES, (p + 1) * LANES) for p in pairs]
    ones2 = ones_ref[...]

    def head_sums(xs):
        parts = [part[:, sl] for x in xs for part in _split3(x) for sl in sls]
        prod = jnp.dot(jnp.concatenate(parts, axis=0), ones2, preferred_element_type=F32)
        outs = []
        for i in range(len(xs)):
            slabs = []
            for p in pairs:
                rows = [((3 * i + j) * len(sls) + p) * c_ for j in range(3)]
                slabs.append(prod[rows[0]:rows[0] + c_] + prod[rows[1]:rows[1] + c_] + prod[rows[2]:rows[2] + c_])
            outs.append(jnp.concatenate(slabs, axis=1))
        return outs

    def chunk_program(bb):
        ld = ld_ref[bb]
        r = r_ref[bb]
        k = k_ref[bb]
        v = v_ref[bb]
        cum = _dot_exact_lhs(tri_ref[...], ld)
        mid = cum[c_ // 2 - 1:c_ // 2, :]
        e_in = jnp.exp(cum - mid)
        e_out = jnp.exp(mid - cum)
        r_t = r * e_in
        a_t = a_ref[bb] * jnp.exp(cum - ld - mid)
        b_t = b_ref[bb] * e_out
        k_t = k * e_out
        e_mid = jnp.exp(mid)
        w_all = jnp.exp(cum[c_ - 1:c_, :])
        w_tail = jnp.exp(cum[c_ - 1:c_, :] - mid)
        yield
        s0 = [state_ref[bb, p] for p in pairs]
        ar = [jnp.concatenate([block_diag(a_t[:, sl]), block_diag(r_t[:, sl])], axis=0) for sl in sls]
        bk = [jnp.concatenate([block_diag(b_t[:, sl]), block_diag(k_t[:, sl])], axis=0) for sl in sls]
        vb = [block_diag(v[:, sl]) for sl in sls]
        m1 = [_dot_nt(ar[p], bk[p]) for p in pairs]
        m2 = [_dot_nt(ar[p] * e_mid[:, sls[p]], s0[p]) for p in pairs]
        yield
        l_ab = [jnp.where(strict, m[:c2, :c2], 0.0) for m in m1]
        l_ak = [jnp.where(strict, m[:c2, c2:], 0.0) for m in m1]
        l_r = [jnp.concatenate([jnp.where(incl, m[c2:, :c2], 0.0), jnp.where(incl, m[c2:, c2:], 0.0)], axis=1)
               for m in m1]
        u = [m2[p][:c2] + _dot(l_ak[p], vb[p]) for p in pairs]
        pw = l_ab
        yield
        n_steps = int(math.log2(c_))
        for step in range(n_steps):
            if step < n_steps - 1:
                prod = [_dot(pw[p], jnp.concatenate([pw[p], u[p]], axis=1)) for p in pairs]
                pw = [q[:, :c2] for q in prod]
                u = [u[p] + prod[p][:, c2:] for p in pairs]
            else:
                u = [u[p] + _dot(pw[p], u[p]) for p in pairs]
            yield
        uv = [jnp.concatenate([u[p], vb[p]], axis=0) for p in pairs]
        ys = [m2[p][c2:] + _dot(l_r[p], uv[p]) for p in pairs]
        upd = [_dot_tn(uv[p], bk[p]) for p in pairs]
        yield
        for p in pairs:
            state_ref[bb, p] = s0[p] * w_all[:, sls[p]] + upd[p] * w_tail[:, sls[p]]
        y = jnp.concatenate([ys[p][:c_] + ys[p][c_:] for p in pairs], axis=1)
        sum_y, sum_rk = head_sums([y, r * k * rk_ref[...]])
        d = y - sum_y * (1.0 / n_)
        yield
        var = head_sums([d * d])[0] * (1.0 / n_)
        yn = d * lax.rsqrt(var + RW_LN_EPS) * lng_ref[...] + lnb_ref[...]
        o_ref[bb] = ((yn + sum_rk * v) * g_ref[bb]).astype(o_ref.dtype)
        yield

    for _ in zip(*[chunk_program(bb) for bb in range(RW_ROWS)]):
        pass


def _rw_scan(r, k, v, ld, a, b, g, b_, s_, r_k, ln_g, ln_b, ones_pair, tri):
    t = r.shape[0]
    nc = s_ // RW_CHUNK
    row = pl.BlockSpec((RW_ROWS, RW_CHUNK, RW_WIDTH), lambda bi, i: (bi, i, 0))
    vec = lambda x: x.reshape(1, -1)
    seq = lambda x: x.reshape(b_, s_, RW_WIDTH)
    out = pl.pallas_call(
        _rw_scan_kernel,
        grid=(b_ // RW_ROWS, nc),
        in_specs=[row] * 7 + [_const_spec(tri.shape), _const_spec(ones_pair.shape),
                              _const_spec((1, RW_WIDTH)), _const_spec((1, RW_WIDTH)), _const_spec((1, RW_WIDTH))],
        out_specs=row,
        out_shape=jax.ShapeDtypeStruct((b_, s_, RW_WIDTH), BF16),
        scratch_shapes=[pltpu.VMEM((RW_ROWS, RW_HEADS // 2, 2 * RW_HEAD_DIM, 2 * RW_HEAD_DIM), F32)],
        compiler_params=_params("parallel", "arbitrary"),
        name="rwkv_scan",
    )(seq(r), seq(k), seq(v), seq(ld), seq(a), seq(b), seq(g), tri, ones_pair, vec(r_k), vec(ln_g), vec(ln_b))
    return out.reshape(t, RW_WIDTH)


def _rt_log_gamma(h):
    return math.log1p(-(2.0 ** (-5.0 - h)))


def _retention_kernel(p_ref, cos_ref, sin_ref, gng_ref, gnb_ref, o_ref, state_ref):
    c_ = RT_CHUNK

    @pl.when(pl.program_id(1) == 0)
    def _():
        state_ref[...] = jnp.zeros_like(state_ref)

    cos = cos_ref[...]
    sin = sin_ref[...]
    row = lax.broadcasted_iota(jnp.int32, (c_, c_), 0).astype(F32)
    col = lax.broadcasted_iota(jnp.int32, (c_, c_), 1).astype(F32)
    rel = row - col
    heads = range(RT_HEADS)
    lgs = [_rt_log_gamma(h) for h in heads]
    sls = [slice(h * RT_DIM, (h + 1) * RT_DIM) for h in heads]
    decay = [jnp.where(rel >= 0, jnp.exp(jnp.maximum(rel, 0.0) * lg), 0.0) for lg in lgs]
    xi = [jnp.exp((row + 1.0) * lg) for lg in lgs]
    zeta = [jnp.exp((c_ - 1.0 - row) * lg) for lg in lgs]

    def chunk_program(bb):
        col_of = lambda part, h: slice(part * RT_WIDTH + h * RT_DIM, part * RT_WIDTH + (h + 1) * RT_DIM)
        q = [p_ref[bb, :, col_of(0, h)] for h in heads]
        k = [p_ref[bb, :, col_of(1, h)] for h in heads]
        v = [p_ref[bb, :, col_of(2, h)] for h in heads]
        q = [x * cos + pltpu.roll(x, RT_DIM // 2, 1) * sin for x in q]
        k = [(x * cos + pltpu.roll(x, RT_DIM // 2, 1) * sin) * (RT_DIM ** -0.5) for x in k]
        s0 = [state_ref[bb, h] for h in heads]
        yield
        scores = [_dot_nt(q[h], k[h]) * decay[h] for h in heads]
        cross = [_dot(q[h], s0[h]) * xi[h] for h in heads]
        for h in heads:
            state_ref[bb, h] = s0[h] * math.exp(c_ * lgs[h]) + _dot_tn(k[h] * zeta[h], v[h])
        yield
        o = [_dot(scores[h], v[h]) + cross[h] for h in heads]
        yield
        for h in heads:
            mu = jnp.mean(o[h], axis=-1, keepdims=True)
            d = o[h] - mu
            var = jnp.mean(d * d, axis=-1, keepdims=True)
            on = d * lax.rsqrt(var + EPS) * gng_ref[:, sls[h]] + gnb_ref[:, sls[h]]
            gate = p_ref[bb, :, col_of(3, h)]
            o_ref[bb, :, sls[h]] = (gate * jax.nn.sigmoid(gate) * on).astype(o_ref.dtype)
        yield

    for _ in zip(*[chunk_program(bb) for bb in range(RT_ROWS)]):
        pass


def _retention(p_rt, b_, s_, cos2, sin2, gn_g, gn_b):
    t = p_rt.shape[0]
    nc = s_ // RT_CHUNK
    out = pl.pallas_call(
        _retention_kernel,
        grid=(b_ // RT_ROWS, nc),
        in_specs=[pl.BlockSpec((RT_ROWS, RT_CHUNK, RT_IN), lambda bi, i: (bi, i, 0)),
                  pl.BlockSpec((RT_CHUNK, RT_DIM), lambda bi, i: (i, 0)),
                  pl.BlockSpec((RT_CHUNK, RT_DIM), lambda bi, i: (i, 0)),
                  _const_spec((1, RT_WIDTH)), _const_spec((1, RT_WIDTH))],
        out_specs=pl.BlockSpec((RT_ROWS, RT_CHUNK, RT_WIDTH), lambda bi, i: (bi, i, 0)),
        out_shape=jax.ShapeDtypeStruct((b_, s_, RT_WIDTH), BF16),
        scratch_shapes=[pltpu.VMEM((RT_ROWS, RT_HEADS, RT_DIM, RT_DIM), F32)],
        compiler_params=_params("parallel", "arbitrary"),
        name="retention",
    )(p_rt.reshape(b_, s_, RT_IN), cos2, sin2, gn_g.reshape(1, -1), gn_b.reshape(1, -1))
    return out.reshape(t, RT_WIDTH)


DSA_BLOCKS_PER_TILE = ROW_TILE // Q_BLOCK


def _dsa_prep_kernel(cq_ref, ckv_ref, kw_ref, qg_ref, kvg_ref, wuq_ref, wuk_ref, wqi_ref,
                     ckvn_ref, ckvt_ref, kidx_ref, widx_ref, qlat_ref, qidx_ref):
    cq = _rms(cq_ref[...], qg_ref[...]).astype(BF16)
    ckvn = _rms(ckv_ref[...], kvg_ref[...])
    ckvn_ref[...] = ckvn.astype(BF16)
    kw = kw_ref[...]
    kidx_ref[...] = kw[:, :IDX_DIM].astype(BF16)
    q = jnp.dot(cq, wuq_ref[...], preferred_element_type=F32)
    qi = jnp.dot(cq, wqi_ref[...], preferred_element_type=F32)
    for blk in range(DSA_BLOCKS_PER_TILE):
        rows = slice(blk * Q_BLOCK, (blk + 1) * Q_BLOCK)
        ckvt_ref[blk] = ckvn[rows].T.astype(BF16)
        widx_ref[blk] = kw[rows].T[IDX_DIM:IDX_DIM + IDX_HEADS] * ((IDX_HEADS * IDX_DIM) ** -0.5)
    for h in range(DSA_HEADS):
        qh = q[:, h * DSA_HEAD_DIM:(h + 1) * DSA_HEAD_DIM]
        ql = (_dot(qh, wuk_ref[h]) * (DSA_HEAD_DIM ** -0.5)).astype(BF16)
        qih = qi[:, h * IDX_DIM:(h + 1) * IDX_DIM].astype(BF16)
        for blk in range(DSA_BLOCKS_PER_TILE):
            rows = slice(blk * Q_BLOCK, (blk + 1) * Q_BLOCK)
            qlat_ref[blk, h] = ql[rows]
            qidx_ref[blk, h] = qih[rows]


def _dsa_prep(c_q, c_kv, kw, q_g, kv_g, w_uq, w_uk, w_qi):
    t = c_q.shape[0]
    row = lambda n: pl.BlockSpec((ROW_TILE, n), lambda i: (i, 0))
    blk3 = lambda m, n: pl.BlockSpec((DSA_BLOCKS_PER_TILE, m, n), lambda i: (i, 0, 0))
    blk4 = lambda n: pl.BlockSpec((DSA_BLOCKS_PER_TILE, DSA_HEADS, Q_BLOCK, n), lambda i: (i, 0, 0, 0))
    return pl.pallas_call(
        _dsa_prep_kernel,
        grid=(t // ROW_TILE,),
        in_specs=[row(DSA_Q_RANK), row(DSA_KV_RANK), row(LANES), _const_spec((1, DSA_Q_RANK)),
                  _const_spec((1, DSA_KV_RANK)),
                  _const_spec(w_uq.shape), _const_spec(w_uk.shape), _const_spec(w_qi.shape)],
        out_specs=[row(DSA_KV_RANK), blk3(DSA_KV_RANK, Q_BLOCK), row(IDX_DIM), blk3(IDX_HEADS, Q_BLOCK),
                   blk4(DSA_KV_RANK), blk4(IDX_DIM)],
        out_shape=[jax.ShapeDtypeStruct((t, DSA_KV_RANK), BF16),
                   jax.ShapeDtypeStruct((t // Q_BLOCK, DSA_KV_RANK, Q_BLOCK), BF16),
                   jax.ShapeDtypeStruct((t, IDX_DIM), BF16),
                   jax.ShapeDtypeStruct((t // Q_BLOCK, IDX_HEADS, Q_BLOCK), F32),
                   jax.ShapeDtypeStruct((t // Q_BLOCK, DSA_HEADS, Q_BLOCK, DSA_KV_RANK), BF16),
                   jax.ShapeDtypeStruct((t // Q_BLOCK, IDX_HEADS, Q_BLOCK, IDX_DIM), BF16)],
        compiler_params=_params("parallel"),
        name="dsa_prep",
    )(c_q, c_kv, kw, q_g.reshape(1, -1), kv_g.reshape(1, -1), w_uq, w_uk, w_qi)


def _fold_rows(x, op):
    return functools.reduce(op, [x[i:i + 8] for i in range(0, x.shape[0], 8)])


def _bit_transpose32(words):
    a = list(words)
    j, m = 16, 0x0000FFFF
    while j:
        k = 0
        while k < 32:
            t = (a[k] ^ lax.shift_right_logical(a[k + j], jnp.int32(j))) & jnp.int32(m)
            a[k] = a[k] ^ t
            a[k + j] = a[k + j] ^ lax.shift_left(t, jnp.int32(j))
            k = (k + j + 1) & ~j
        j >>= 1
        m ^= m << j
    return a


def _dsa_kernel(top_k, qidx_ref, widx_ref, qlat_ref, kidx_ref, ckv_ref, ckvt_ref, wuv_ref, tril_ref, o_ref,
                key_ref, planes_ref, s_ref, acc_ref):
    qb = Q_BLOCK
    nh = DSA_HEADS
    j = pl.program_id(1)
    n_chunks = j + 1

    keyi = lax.broadcasted_iota(jnp.int32, (qb, qb), 0)
    qryi = lax.broadcasted_iota(jnp.int32, (qb, qb), 1)
    lanes = lambda h: slice(h * qb, (h + 1) * qb)

    def key_chunk(ref, c):
        return ref[pl.ds(pl.multiple_of(c * qb, qb), qb), :]

    w_idx = widx_ref[0]
    q_idx = qidx_ref[0].reshape(IDX_HEADS * qb, IDX_DIM)
    q_lat = qlat_ref[0].reshape(nh * qb, DSA_KV_RANK)

    n_pairs = (n_chunks + 1) // 2

    def score_chunk(c):
        logits = lax.dot_general(key_chunk(kidx_ref, c), q_idx, (((1,), (1,)), ((), ())),
                                 preferred_element_type=F32)
        score = jnp.zeros((qb, qb), F32)
        for h in range(IDX_HEADS):
            score = score + jnp.maximum(logits[:, lanes(h)], 0.0) * w_idx[h:h + 1, :]
        bits = lax.bitcast_convert_type(score, jnp.int32)
        key = jnp.where(bits < 0, bits ^ jnp.int32(0x7FFFFFFF), bits)
        causal = keyi + c * qb <= qryi + j * qb
        key = jnp.where(causal, key, jnp.int32(INT_MIN))
        key_ref[c] = key
        return key

    def score_body(pi, carry):
        keys = [score_chunk(2 * pi), score_chunk(2 * pi + 1)]
        words = [k[r:r + 8] ^ jnp.int32(INT_MIN) for k in keys for r in range(0, qb, 8)]
        for idx, plane in enumerate(_bit_transpose32(words)):
            planes_ref[31 - idx, pi] = plane
        return carry

    lax.fori_loop(0, n_pairs, score_body, 0)

    n_pairs_max = planes_ref.shape[1]
    live = tuple(jnp.where(p < n_pairs, jnp.full((8, qb), -1, jnp.int32), jnp.zeros((8, qb), jnp.int32))
                 for p in range(n_pairs_max))

    def bit_body(i, carry):
        thr_u, above, eq = carry
        plane = planes_ref[31 - i]
        hits = [eq[p] & plane[p] for p in range(n_pairs_max)]
        cnt = functools.reduce(jnp.add, [lax.population_count(h) for h in hits])
        cnt = above + jnp.sum(cnt, axis=0, keepdims=True)
        accept = cnt >= top_k
        thr_u = jnp.where(accept, thr_u | lax.shift_left(jnp.int32(1), 31 - i), thr_u)
        above = jnp.where(accept, above, cnt)
        eq = tuple(jnp.where(accept, h, e ^ h) for e, h in zip(eq, hits))
        return thr_u, above, eq

    zero_row = jnp.zeros((1, qb), jnp.int32)
    thr_u, above, _ = lax.fori_loop(0, 32, bit_body, (zero_row, zero_row, live))
    thr = thr_u ^ jnp.int32(INT_MIN)
    need = (top_k - above).astype(F32)

    def pair_logits(t, taken):
        tops = [None] * nh
        for c in (2 * t, 2 * t + 1):
            key = key_ref[c]
            causal = keyi + c * qb <= qryi + j * qb
            eq = jnp.logical_and(key == thr, causal)
            eq_f = jnp.where(eq, 1.0, 0.0)
            rank = taken + jnp.dot(tril_ref[...], eq_f.astype(BF16), preferred_element_type=F32)
            sel = jnp.logical_or(jnp.logical_and(key > thr, causal), jnp.logical_and(eq, rank <= need))
            s = lax.dot_general(key_chunk(ckv_ref, c), q_lat, (((1,), (1,)), ((), ())),
                                preferred_element_type=F32)
            for h in range(nh):
                sh = jnp.where(sel, s[:, lanes(h)], NEG_BIG)
                s_ref[c, :, lanes(h)] = sh
                top = _fold_rows(sh, jnp.maximum)
                tops[h] = top if tops[h] is None else jnp.maximum(tops[h], top)
            taken = taken + jnp.sum(eq_f, axis=0, keepdims=True)
        return taken, tuple(jnp.max(top, axis=0, keepdims=True) for top in tops)

    def pair_values(t, tops, peak, sums):
        ps, scales, new_peak, new_sums = [], [], [], []
        for h in range(nh):
            pk = jnp.maximum(peak[h], tops[h])
            scale = jnp.exp(peak[h] - pk)
            pa = jnp.exp(s_ref[2 * t, :, lanes(h)] - pk)
            pb = jnp.exp(s_ref[2 * t + 1, :, lanes(h)] - pk)
            new_sums.append(sums[h] * scale + _fold_rows(pa + pb, jnp.add))
            ps.append(jnp.concatenate([pa.astype(BF16), pb.astype(BF16)], axis=0))
            scales.append(scale)
            new_peak.append(pk)
        ckvt_pair = jnp.concatenate([ckvt_ref[2 * t], ckvt_ref[2 * t + 1]], axis=1)
        upd = jnp.dot(ckvt_pair, jnp.concatenate(ps, axis=1), preferred_element_type=F32)
        acc_ref[...] = acc_ref[...] * jnp.concatenate(scales, axis=1) + upd
        return tuple(new_peak), tuple(new_sums)

    acc_ref[...] = jnp.zeros_like(acc_ref)

    def attn_body(t, carry):
        taken, tops, peak, sums = carry
        peak, sums = pair_values(t, tops, peak, sums)
        taken, tops = pair_logits(jnp.minimum(t + 1, n_pairs - 1), taken)
        return taken, tops, peak, sums

    taken, tops = pair_logits(0, jnp.zeros((1, qb), F32))
    init = (taken, tops, tuple(jnp.full((1, qb), NEG_BIG, F32) for _ in range(nh)),
            tuple(jnp.zeros((8, qb), F32) for _ in range(nh)))
    _, _, _, sums = lax.fori_loop(0, n_pairs, attn_body, init)

    outs = []
    for h in range(nh):
        o_lat_t = acc_ref[:, lanes(h)] / jnp.sum(sums[h], axis=0, keepdims=True)
        outs.append(_dot_tn(o_lat_t, wuv_ref[h]))
    o_ref[...] = jnp.concatenate(outs, axis=1).astype(o_ref.dtype)


def _dsa(q_idx, w_idx, q_lat, k_idx, ckv_n, ckv_t, w_uv, tril, b_, s_):
    nb = s_ // Q_BLOCK
    t = b_ * s_
    top_k = min(TOPK_MAX, s_ // 4)
    blk = lambda n: pl.BlockSpec((Q_BLOCK, n), lambda bi, i: (bi * nb + i, 0))
    blk3 = lambda m, n: pl.BlockSpec((1, m, n), lambda bi, i: (bi * nb + i, 0, 0))
    blk4 = lambda n: pl.BlockSpec((1, DSA_HEADS, Q_BLOCK, n), lambda bi, i: (bi * nb + i, 0, 0, 0))
    seq = lambda n: pl.BlockSpec((s_, n), lambda bi, i: (bi, 0))
    stacked = DSA_HEADS * Q_BLOCK
    return pl.pallas_call(
        functools.partial(_dsa_kernel, top_k),
        grid=(b_, nb),
        in_specs=[blk4(IDX_DIM), blk3(IDX_HEADS, Q_BLOCK), blk4(DSA_KV_RANK),
                  seq(IDX_DIM), seq(DSA_KV_RANK),
                  pl.BlockSpec((nb, DSA_KV_RANK, Q_BLOCK), lambda bi, i: (bi, 0, 0)),
                  _const_spec(w_uv.shape), _const_spec(tril.shape)],
        out_specs=blk(DSA_WIDTH),
        out_shape=jax.ShapeDtypeStruct((t, DSA_WIDTH), BF16),
        scratch_shapes=[pltpu.VMEM((nb, Q_BLOCK, Q_BLOCK), jnp.int32),
                        pltpu.VMEM((32, nb // 2, 8, Q_BLOCK), jnp.int32),
                        pltpu.VMEM((nb, Q_BLOCK, stacked), F32),
                        pltpu.VMEM((DSA_KV_RANK, stacked), F32)],
        compiler_params=_params("parallel", "arbitrary"),
        name="dsa_attn",
    )(q_idx, w_idx, q_lat, k_idx, ckv_n, ckv_t, w_uv, tril)


SC_TILE = 512


def _sconv_kernel(p_ref, w_ref, b_ref, o_ref, carry_ref):
    @pl.when(pl.program_id(1) == 0)
    def _():
        carry_ref[...] = jnp.zeros_like(carry_ref)

    h = p_ref[:, :SC_WIDTH]
    gate_b = p_ref[:, SC_WIDTH:2 * SC_WIDTH]
    gate_c = p_ref[:, 2 * SC_WIDTH:]
    u = gate_c * h
    carry = carry_ref[...]
    y = u * w_ref[2:3, :] + _shift_rows(u, carry, 1) * w_ref[1:2, :] + _shift_rows(u, carry, 2) * w_ref[0:1, :]
    carry_ref[...] = u[SC_TILE - 8:, :]
    o_ref[...] = (gate_b * (y + b_ref[...])).astype(o_ref.dtype)


def _sconv(p_sc, b_, s_, conv_w, conv_b):
    t = p_sc.shape[0]
    nt = s_ // SC_TILE
    return pl.pallas_call(
        _sconv_kernel,
        grid=(b_, nt),
        in_specs=[pl.BlockSpec((SC_TILE, 3 * SC_WIDTH), lambda bi, i: (bi * nt + i, 0)),
                  _const_spec((8, SC_WIDTH)), _const_spec((1, SC_WIDTH))],
        out_specs=pl.BlockSpec((SC_TILE, SC_WIDTH), lambda bi, i: (bi * nt + i, 0)),
        out_shape=jax.ShapeDtypeStruct((t, SC_WIDTH), BF16),
        scratch_shapes=[pltpu.VMEM((8, SC_WIDTH), F32)],
        compiler_params=_params("parallel", "arbitrary"),
        name="short_conv",
    )(p_sc, jnp.pad(conv_w, ((0, 8 - SC_KERNEL), (0, 0))), conv_b.reshape(1, -1))


def _xattn_kernel(x_ref, gq_ref, wq_ref, k_ref, v_ref, wo_ref, go_ref, o_ref, att_ref):
    x = x_ref[...]
    q = jnp.dot(_rms(x, gq_ref[...]).astype(BF16), wq_ref[...], preferred_element_type=F32)
    for h in range(XA_HEADS):
        sl = slice(h * XA_HEAD_DIM, (h + 1) * XA_HEAD_DIM)
        s = _dot_nt(q[:, sl], k_ref[:, sl]) * (XA_HEAD_DIM ** -0.5)
        s = s - jnp.max(s, axis=-1, keepdims=True)
        p = jnp.exp(s)
        p = p / jnp.sum(p, axis=-1, keepdims=True)
        att_ref[:, sl] = _dot(p, v_ref[:, sl])
    hout = jnp.dot(att_ref[...].astype(BF16), wo_ref[...], preferred_element_type=F32)
    o_ref[...] = x + _rms(hout, go_ref[...])


def _xattn(x, g_q, wq, k_mem, v_mem, wo, g_o, b_, s_):
    t = x.shape[0]
    nt = s_ // ROW_TILE
    row = pl.BlockSpec((ROW_TILE, D_MODEL), lambda bi, i: (bi * nt + i, 0))
    mem = pl.BlockSpec((MEM_LEN, XA_WIDTH), lambda bi, i: (bi, 0))
    return pl.pallas_call(
        _xattn_kernel,
        grid=(b_, nt),
        in_specs=[row, _const_spec((1, D_MODEL)), _const_spec(wq.shape), mem, mem,
                  _const_spec(wo.shape), _const_spec((1, D_MODEL))],
        out_specs=row,
        out_shape=jax.ShapeDtypeStruct((t, D_MODEL), F32),
        scratch_shapes=[pltpu.VMEM((ROW_TILE, XA_WIDTH), F32)],
        compiler_params=_params("parallel", "parallel"),
        name="mem_xattn",
    )(x, g_q.reshape(1, -1), wq, k_mem, v_mem, wo, g_o.reshape(1, -1))


def _block_diag(n_blocks, size, value):
    return np.kron(np.eye(n_blocks, dtype=np.float32), np.full((size, size), value, np.float32))


def _rope_tables(s_):
    half = RT_DIM // 2
    inv_freq = RT_ROPE_BASE ** (-jnp.arange(half, dtype=F32) / half)
    ang = jnp.arange(s_).astype(F32)[:, None] * inv_freq[None, :]
    cos, sin = jnp.cos(ang), jnp.sin(ang)
    return jnp.concatenate([cos, cos], axis=-1), jnp.concatenate([-sin, sin], axis=-1)


def kernel(x, mem, norm_g, mem_norm_g, ffn_w_gate, ffn_w_up, ffn_w_down, xa_wq, xa_wk, xa_wv, xa_wo, ev_w_in, ev_w_out, rw_mu, rw_w0, rw_w2, rw_a0, rw_a2, rw_g2, rw_k_k, rw_k_a, rw_r_k, rw_ln_g, rw_ln_b, rt_gn_g, rt_gn_b, od_w_in, od_w_out, dsa_q_norm_g, dsa_kv_norm_g, dsa_w_uq, dsa_w_uk, dsa_w_uv, dsa_w_qi, sc_conv_w, sc_conv_b):
    b_, s_, d_ = x.shape
    depth = norm_g.shape[0]
    t = b_ * s_
    bf = lambda w: w.astype(BF16)

    ones_blk = jnp.asarray(_block_diag(RW_HEADS, RW_HEAD_DIM, 1.0), BF16)
    ones_pair = jnp.asarray(_block_diag(2, RW_HEAD_DIM, 1.0), BF16)
    tri_rw = jnp.asarray(np.tril(np.ones((RW_CHUNK, RW_CHUNK), np.float32)), BF16)
    tril_dsa = jnp.asarray(np.tril(np.ones((Q_BLOCK, Q_BLOCK), np.float32)), BF16)
    cos2, sin2 = _rope_tables(s_)

    xf = x.reshape(t, d_)
    mem_f = mem.reshape(b_ * MEM_LEN, d_)
    for l in range(depth):
        ng = norm_g[l]
        i = l // 2
        xf = _ffn(xf, ng[0], bf(ffn_w_gate[l, 0]), bf(ffn_w_up[l, 0]), bf(ffn_w_down[l, 0]), ng[1])
        if l % 2 == 0:
            w_in = bf(ev_w_in[i])
            p_rw, p_rt = _norm_proj(xf, ng[2], [w_in[:, :RW_IN], w_in[:, RW_IN:]], "ev_in_proj")
            r, k, v, ld, a, b, g = _rw_prep(p_rw, b_, s_, rw_mu[i], rw_w0[i], rw_w2[i], rw_a0[i], rw_a2[i],
                                            rw_g2[i], rw_k_k[i], rw_k_a[i], ones_blk)
            y_a = _rw_scan(r, k, v, ld, a, b, g, b_, s_, rw_r_k[i], rw_ln_g[i], rw_ln_b[i],
                           ones_pair, tri_rw)
            y_b = _retention(p_rt, b_, s_, cos2, sin2, rt_gn_g[i], rt_gn_b[i])
            w_out = bf(ev_w_out[i])
            xf = _proj_res([y_a, y_b], [w_out[:RW_WIDTH], w_out[RW_WIDTH:]], ng[3], xf, "ev_out_proj")
        else:
            w_in = od_w_in[i]
            kw_w = jnp.pad(w_in[:, DSA_Q_RANK + DSA_KV_RANK:DSA_IN], ((0, 0), (0, LANES - IDX_DIM - IDX_HEADS)))
            c_q, c_kv, kw, p_sc = _norm_proj(
                xf, ng[2], [bf(w_in[:, :DSA_Q_RANK]), bf(w_in[:, DSA_Q_RANK:DSA_Q_RANK + DSA_KV_RANK]),
                            bf(kw_w), bf(w_in[:, DSA_IN:])], "od_in_proj")
            ckv_n, ckv_t, k_idx, w_idx, q_lat, q_idx = _dsa_prep(
                c_q, c_kv, kw, dsa_q_norm_g[i], dsa_kv_norm_g[i],
                bf(dsa_w_uq[i].reshape(DSA_Q_RANK, DSA_WIDTH)), bf(dsa_w_uk[i]),
                bf(dsa_w_qi[i].reshape(DSA_Q_RANK, IDX_HEADS * IDX_DIM)))
            y_c = _dsa(q_idx, w_idx, q_lat, k_idx, ckv_n, ckv_t, bf(dsa_w_uv[i]), tril_dsa, b_, s_)
            y_d = _sconv(p_sc, b_, s_, sc_conv_w[i], sc_conv_b[i])
            w_out = bf(od_w_out[i])
            xf = _proj_res([y_c, y_d], [w_out[:DSA_WIDTH], w_out[DSA_WIDTH:]], ng[3], xf, "od_out_proj")
        k_mem, v_mem = _norm_proj(mem_f, mem_norm_g, [bf(xa_wk[l]), bf(xa_wv[l])], "mem_kv_proj")
        xf = _xattn(xf, ng[4], bf(xa_wq[l]), k_mem, v_mem, bf(xa_wo[l]), ng[5], b_, s_)
        xf = _ffn(xf, ng[6], bf(ffn_w_gate[l, 1]), bf(ffn_w_up[l, 1]), bf(ffn_w_down[l, 1]), ng[7])
    return xf.reshape(b_, s_, d_)
```

```python
import functools
import math

import numpy as np
import jax
import jax.numpy as jnp
from jax import lax
from jax.experimental import pallas as pl
from jax.experimental.pallas import tpu as pltpu

F32 = jnp.float32
BF16 = jnp.bfloat16

D_MODEL = 1024
D_FF = 2816
EPS = 1e-6
MEM_LEN = 256
RW_HEADS = 8
RW_HEAD_DIM = 64
RW_WIDTH = RW_HEADS * RW_HEAD_DIM
RW_DECAY_RANK = 64
RW_AAA_RANK = 64
RW_GATE_RANK = 128
RW_LN_EPS = 64e-5
RW_IN = 3 * RW_WIDTH + RW_DECAY_RANK + RW_AAA_RANK + RW_GATE_RANK
RW_CHUNK = 64
RW_ROWS = 4
RT_HEADS = 4
RT_DIM = 128
RT_WIDTH = RT_HEADS * RT_DIM
RT_CHUNK = 128
RT_ROWS = 4
RT_ROPE_BASE = 10000.0
RT_IN = 4 * RT_WIDTH
DSA_HEADS = 8
DSA_HEAD_DIM = 64
DSA_WIDTH = DSA_HEADS * DSA_HEAD_DIM
DSA_Q_RANK = 256
DSA_KV_RANK = 128
IDX_HEADS = 8
IDX_DIM = 64
TOPK_MAX = 256
Q_BLOCK = 128
DSA_IN = DSA_Q_RANK + DSA_KV_RANK + IDX_DIM + IDX_HEADS
SC_WIDTH = 512
SC_KERNEL = 3
XA_HEADS = 4
XA_HEAD_DIM = 128
XA_WIDTH = XA_HEADS * XA_HEAD_DIM

LANES = 128
ROW_TILE = 512
VMEM_LIMIT = 56 * 1024 * 1024
INT_MIN = -2 ** 31
NEG_BIG = -1e30


def _params(*sem):
    return pltpu.CompilerParams(dimension_semantics=sem, vmem_limit_bytes=VMEM_LIMIT)


def _rms(x, g):
    return x * lax.rsqrt(jnp.mean(x * x, axis=-1, keepdims=True) + EPS) * g


def _dot(a, b):
    return jnp.dot(a.astype(BF16), b.astype(BF16), preferred_element_type=F32)


def _dot_nt(a, b):
    return lax.dot_general(a.astype(BF16), b.astype(BF16), (((1,), (1,)), ((), ())),
                           preferred_element_type=F32)


def _dot_tn(a, b):
    return lax.dot_general(a.astype(BF16), b.astype(BF16), (((0,), (0,)), ((), ())),
                           preferred_element_type=F32)


def _split3(x):
    hi = x.astype(BF16)
    r1 = x - hi.astype(F32)
    mid = r1.astype(BF16)
    lo = (r1 - mid.astype(F32)).astype(BF16)
    return hi, mid, lo


def _dot_exact_rhs(x, w_bf16):
    hi, mid, lo = _split3(x)
    out = jnp.dot(hi, w_bf16, preferred_element_type=F32)
    out += jnp.dot(mid, w_bf16, preferred_element_type=F32)
    out += jnp.dot(lo, w_bf16, preferred_element_type=F32)
    return out


def _dot_exact_lhs(w_bf16, x):
    hi, mid, lo = _split3(x)
    out = jnp.dot(w_bf16, hi, preferred_element_type=F32)
    out += jnp.dot(w_bf16, mid, preferred_element_type=F32)
    out += jnp.dot(w_bf16, lo, preferred_element_type=F32)
    return out


def _const_spec(shape):
    nd = len(shape)
    return pl.BlockSpec(shape, lambda *_: (0,) * nd, pipeline_mode=pl.Buffered(1))


FF_CHUNK = 256
FF_TILE = 512


def _ffn_kernel(x_ref, gin_ref, wg_ref, wu_ref, wd_ref, gout_ref, o_ref, acc_ref):
    x = x_ref[...]
    xb = _rms(x, gin_ref[...]).astype(BF16)
    for c in range(D_FF // FF_CHUNK):
        sl = slice(c * FF_CHUNK, (c + 1) * FF_CHUNK)
        g = jnp.dot(xb, wg_ref[:, sl], preferred_element_type=F32)
        u = jnp.dot(xb, wu_ref[:, sl], preferred_element_type=F32)
        h = (g * jax.nn.sigmoid(g) * u).astype(BF16)
        part = jnp.dot(h, wd_ref[sl, :], preferred_element_type=F32)
        if c == 0:
            acc_ref[...] = part
        else:
            acc_ref[...] += part
    o_ref[...] = x + 0.5 * _rms(acc_ref[...], gout_ref[...])


def _ffn(x, g_in, wg, wu, wd, g_out):
    t = x.shape[0]
    row = pl.BlockSpec((FF_TILE, D_MODEL), lambda i: (i, 0))
    return pl.pallas_call(
        _ffn_kernel,
        grid=(t // FF_TILE,),
        in_specs=[row, _const_spec((1, D_MODEL)), _const_spec((D_MODEL, D_FF)),
                  _const_spec((D_MODEL, D_FF)), _const_spec((D_FF, D_MODEL)), _const_spec((1, D_MODEL))],
        out_specs=row,
        out_shape=jax.ShapeDtypeStruct((t, D_MODEL), F32),
        scratch_shapes=[pltpu.VMEM((FF_TILE, D_MODEL), F32)],
        compiler_params=_params("parallel"),
        name="ffn_half",
    )(x, g_in.reshape(1, -1), wg, wu, wd, g_out.reshape(1, -1))


PROJ_CHUNK = 512


def _norm_proj_kernel(n_out, x_ref, g_ref, *refs):
    xb = _rms(x_ref[...], g_ref[...]).astype(BF16)
    for w_ref, o_ref in zip(refs[:n_out], refs[n_out:]):
        n = w_ref.shape[1]
        for c in range(0, n, PROJ_CHUNK):
            sl = slice(c, min(c + PROJ_CHUNK, n))
            o_ref[:, sl] = jnp.dot(xb, w_ref[:, sl], preferred_element_type=F32)


def _norm_proj(x, g, ws, name):
    t, d = x.shape
    row = lambda n: pl.BlockSpec((ROW_TILE, n), lambda i: (i, 0))
    return pl.pallas_call(
        functools.partial(_norm_proj_kernel, len(ws)),
        grid=(t // ROW_TILE,),
        in_specs=[row(d), _const_spec((1, d))] + [_const_spec(w.shape) for w in ws],
        out_specs=[row(w.shape[1]) for w in ws],
        out_shape=[jax.ShapeDtypeStruct((t, w.shape[1]), F32) for w in ws],
        compiler_params=_params("parallel"),
        name=name,
    )(x, g.reshape(1, -1), *ws)


def _proj_res_kernel(n_in, *refs):
    y_refs, w_refs = refs[:n_in], refs[n_in:2 * n_in]
    g_ref, x_ref, o_ref = refs[2 * n_in:]
    h = None
    for y_ref, w_ref in zip(y_refs, w_refs):
        part = jnp.dot(y_ref[...].astype(BF16), w_ref[...], preferred_element_type=F32)
        h = part if h is None else h + part
    o_ref[...] = x_ref[...] + _rms(h, g_ref[...])


def _proj_res(ys, ws, g, x, name):
    t, d = x.shape
    row = lambda n: pl.BlockSpec((ROW_TILE, n), lambda i: (i, 0))
    return pl.pallas_call(
        functools.partial(_proj_res_kernel, len(ys)),
        grid=(t // ROW_TILE,),
        in_specs=[row(y.shape[1]) for y in ys] + [_const_spec(w.shape) for w in ws]
                 + [_const_spec((1, d)), row(d)],
        out_specs=row(d),
        out_shape=jax.ShapeDtypeStruct((t, d), F32),
        compiler_params=_params("parallel"),
        name=name,
    )(*ys, *ws, g.reshape(1, -1), x)


RW_TILE = 512


def _shift_rows(x, carry, n):
    rolled = pltpu.roll(x, n, 0)
    row = lax.broadcasted_iota(jnp.int32, x.shape, 0)
    out = rolled
    for i in range(n):
        out = jnp.where(row == i, carry[8 - n + i:8 - n + i + 1, :], out)
    return out


def _ev_in_kernel(x_ref, gn_ref, wrw_ref, wrt_ref, mu_ref, w0_ref, w2_ref, a0_ref, a2_ref, g2_ref, kk_ref, ka_ref,
                  ones_ref, r_ref, k_ref, v_ref, ld_ref, a_ref, b_ref, g_ref, prt_ref, carry_ref, p_ref):
    @pl.when(pl.program_id(1) == 0)
    def _():
        carry_ref[...] = jnp.zeros_like(carry_ref)

    xb = _rms(x_ref[...], gn_ref[...]).astype(BF16)
    for w_ref, o_ref in ((wrw_ref, p_ref), (wrt_ref, prt_ref)):
        n = w_ref.shape[1]
        for c in range(0, n, PROJ_CHUNK):
            sl = slice(c, min(c + PROJ_CHUNK, n))
            o_ref[:, sl] = jnp.dot(xb, w_ref[:, sl], preferred_element_type=F32)

    p = p_ref[...]
    prev = _shift_rows(p, carry_ref[...], 1)
    carry_ref[...] = p[RW_TILE - 8:, :]
    xm = p + (prev - p) * mu_ref[...]
    w = RW_WIDTH
    r, k, v = xm[:, :w], xm[:, w:2 * w], xm[:, 2 * w:3 * w]
    xw = xm[:, 3 * w:3 * w + RW_DECAY_RANK]
    xa = xm[:, 3 * w + RW_DECAY_RANK:3 * w + RW_DECAY_RANK + RW_AAA_RANK]
    xg = xm[:, 3 * w + RW_DECAY_RANK + RW_AAA_RANK:]
    wlog = -jax.nn.softplus(-(w0_ref[...] + _dot(jnp.tanh(xw), w2_ref[...]))) - 0.5
    a = jax.nn.sigmoid(a0_ref[...] + _dot(xa, a2_ref[...]))
    kk = k * kk_ref[...]
    ss = _dot_exact_rhs(kk * kk, ones_ref[...])
    kk = kk / jnp.maximum(jnp.sqrt(ss), 1e-12)
    r_ref[...] = r
    k_ref[...] = k * (1.0 + (a - 1.0) * ka_ref[...])
    v_ref[...] = v
    ld_ref[...] = -jnp.exp(wlog)
    a_ref[...] = -kk
    b_ref[...] = kk * a
    g_ref[...] = _dot(jax.nn.sigmoid(xg), g2_ref[...])


def _ev_in(x, gn, w_rw, w_rt, b_, s_, mu, w0, w2, a0, a2, g2, k_k, k_a, ones_blk):
    t, d = x.shape
    nt = s_ // RW_TILE
    row = lambda n: pl.BlockSpec((RW_TILE, n), lambda b, i: (b * nt + i, 0))
    vec = lambda a: a.reshape(1, -1)
    outs = [jax.ShapeDtypeStruct((t, RW_WIDTH), F32)] * 7 + [jax.ShapeDtypeStruct((t, RT_IN), F32)]
    return pl.pallas_call(
        _ev_in_kernel,
        grid=(b_, nt),
        in_specs=[row(d), _const_spec((1, d)), _const_spec(w_rw.shape), _const_spec(w_rt.shape),
                  _const_spec((1, RW_IN)), _const_spec((1, RW_WIDTH)),
                  _const_spec(w2.shape), _const_spec((1, RW_WIDTH)), _const_spec(a2.shape),
                  _const_spec(g2.shape), _const_spec((1, RW_WIDTH)), _const_spec((1, RW_WIDTH)),
                  _const_spec(ones_blk.shape)],
        out_specs=[row(RW_WIDTH)] * 7 + [row(RT_IN)],
        out_shape=outs,
        scratch_shapes=[pltpu.VMEM((8, RW_IN), F32), pltpu.VMEM((RW_TILE, RW_IN), F32)],
        compiler_params=_params("parallel", "arbitrary"),
        name="ev_in_proj",
    )(x, vec(gn), w_rw, w_rt, vec(mu), vec(w0), w2, vec(a0), a2, g2, vec(k_k), vec(k_a), ones_blk)


def _rw_scan_kernel(r_ref, k_ref, v_ref, ld_ref, a_ref, b_ref, g_ref, tri_ref, ones_ref,
                    rk_ref, lng_ref, lnb_ref, o_ref, state_ref):
    c_ = RW_CHUNK
    n_ = RW_HEAD_DIM

    @pl.when(pl.program_id(1) == 0)
    def _():
        state_ref[...] = jnp.zeros_like(state_ref)

    c2 = 2 * c_
    row = lax.broadcasted_iota(jnp.int32, (c2, c2), 0)
    col = lax.broadcasted_iota(jnp.int32, (c2, c2), 1)
    same_head = (row >= c_) == (col >= c_)
    strict = jnp.logical_and(same_head, row > col)
    incl = jnp.logical_and(same_head, row >= col)
    left = lax.broadcasted_iota(jnp.int32, (c_, LANES), 1) < n_

    def block_diag(x):
        return jnp.concatenate([jnp.where(left, x, 0.0), jnp.where(left, 0.0, x)], axis=0)

    pairs = range(RW_HEADS // 2)
    sls = [slice(p * LANES, (p + 1) * LANES) for p in pairs]
    ones2 = ones_ref[...]

    def head_sums(xs):
        parts = [part[:, sl] for x in xs for part in _split3(x) for sl in sls]
        prod = jnp.dot(jnp.concatenate(parts, axis=0), ones2, preferred_element_type=F32)
        outs = []
        for i in range(len(xs)):
            slabs = []
            for p in pairs:
                rows = [((3 * i + j) * len(sls) + p) * c_ for j in range(3)]
                slabs.append(prod[rows[0]:rows[0] + c_] + prod[rows[1]:rows[1] + c_] + prod[rows[2]:rows[2] + c_])
            outs.append(jnp.concatenate(slabs, axis=1))
        return outs

    def chunk_program(bb):
        ld = ld_ref[bb]
        r = r_ref[bb]
        k = k_ref[bb]
        v = v_ref[bb]
        cum = _dot_exact_lhs(tri_ref[...], ld)
        mid = cum[c_ // 2 - 1:c_ // 2, :]
        e_in = jnp.exp(cum - mid)
        e_out = jnp.exp(mid - cum)
        r_t = r * e_in
        a_t = a_ref[bb] * jnp.exp(cum - ld - mid)
        b_t = b_ref[bb] * e_out
        k_t = k * e_out
        e_mid = jnp.exp(mid)
        w_all = jnp.exp(cum[c_ - 1:c_, :])
        w_tail = jnp.exp(cum[c_ - 1:c_, :] - mid)
        yield
        s0 = [state_ref[bb, p] for p in pairs]
        ar = [jnp.concatenate([block_diag(a_t[:, sl]), block_diag(r_t[:, sl])], axis=0) for sl in sls]
        bk = [jnp.concatenate([block_diag(b_t[:, sl]), block_diag(k_t[:, sl])], axis=0) for sl in sls]
        vb = [block_diag(v[:, sl]) for sl in sls]
        m1 = [_dot_nt(ar[p], bk[p]) for p in pairs]
        m2 = [_dot_nt(ar[p] * e_mid[:, sls[p]], s0[p]) for p in pairs]
        yield
        l_ab = [jnp.where(strict, m[:c2, :c2], 0.0) for m in m1]
        l_ak = [jnp.where(strict, m[:c2, c2:], 0.0) for m in m1]
        l_r = [jnp.concatenate([jnp.where(incl, m[c2:, :c2], 0.0), jnp.where(incl, m[c2:, c2:], 0.0)], axis=1)
               for m in m1]
        u = [m2[p][:c2] + _dot(l_ak[p], vb[p]) for p in pairs]
        pw = l_ab
        yield
        n_steps = int(math.log2(c_))
        for step in range(n_steps):
            if step < n_steps - 1:
                prod = [_dot(pw[p], jnp.concatenate([pw[p], u[p]], axis=1)) for p in pairs]
                pw = [q[:, :c2] for q in prod]
                u = [u[p] + prod[p][:, c2:] for p in pairs]
            else:
                u = [u[p] + _dot(pw[p], u[p]) for p in pairs]
            yield
        uv = [jnp.concatenate([u[p], vb[p]], axis=0) for p in pairs]
        ys = [m2[p][c2:] + _dot(l_r[p], uv[p]) for p in pairs]
        upd = [_dot_tn(uv[p], bk[p]) for p in pairs]
        yield
        for p in pairs:
            state_ref[bb, p] = s0[p] * w_all[:, sls[p]] + upd[p] * w_tail[:, sls[p]]
        y = jnp.concatenate([ys[p][:c_] + ys[p][c_:] for p in pairs], axis=1)
        sum_y, sum_rk = head_sums([y, r * k * rk_ref[...]])
        d = y - sum_y * (1.0 / n_)
        yield
        var = head_sums([d * d])[0] * (1.0 / n_)
        yn = d * lax.rsqrt(var + RW_LN_EPS) * lng_ref[...] + lnb_ref[...]
        o_ref[bb] = ((yn + sum_rk * v) * g_ref[bb]).astype(o_ref.dtype)
        yield

    for _ in zip(*[chunk_program(bb) for bb in range(RW_ROWS)]):
        pass


def _rw_scan(r, k, v, ld, a, b, g, b_, s_, r_k, ln_g, ln_b, ones_pair, tri):
    t = r.shape[0]
    nc = s_ // RW_CHUNK
    row = pl.BlockSpec((RW_ROWS, RW_CHUNK, RW_WIDTH), lambda bi, i: (bi, i, 0))
    vec = lambda x: x.reshape(1, -1)
    seq = lambda x: x.reshape(b_, s_, RW_WIDTH)
    out = pl.pallas_call(
        _rw_scan_kernel,
        grid=(b_ // RW_ROWS, nc),
        in_specs=[row] * 7 + [_const_spec(tri.shape), _const_spec(ones_pair.shape),
                              _const_spec((1, RW_WIDTH)), _const_spec((1, RW_WIDTH)), _const_spec((1, RW_WIDTH))],
        out_specs=row,
        out_shape=jax.ShapeDtypeStruct((b_, s_, RW_WIDTH), BF16),
        scratch_shapes=[pltpu.VMEM((RW_ROWS, RW_HEADS // 2, 2 * RW_HEAD_DIM, 2 * RW_HEAD_DIM), F32)],
        compiler_params=_params("parallel", "arbitrary"),
        name="rwkv_scan",
    )(seq(r), seq(k), seq(v), seq(ld), seq(a), seq(b), seq(g), tri, ones_pair, vec(r_k), vec(ln_g), vec(ln_b))
    return out.reshape(t, RW_WIDTH)


def _rt_log_gamma(h):
    return math.log1p(-(2.0 ** (-5.0 - h)))


def _retention_kernel(p_ref, cos_ref, sin_ref, gng_ref, gnb_ref, o_ref, state_ref):
    c_ = RT_CHUNK

    @pl.when(pl.program_id(1) == 0)
    def _():
        state_ref[...] = jnp.zeros_like(state_ref)

    cos = cos_ref[...]
    sin = sin_ref[...]
    row = lax.broadcasted_iota(jnp.int32, (c_, c_), 0).astype(F32)
    col = lax.broadcasted_iota(jnp.int32, (c_, c_), 1).astype(F32)
    rel = row - col
    heads = range(RT_HEADS)
    lgs = [_rt_log_gamma(h) for h in heads]
    sls = [slice(h * RT_DIM, (h + 1) * RT_DIM) for h in heads]
    decay = [jnp.where(rel >= 0, jnp.exp(jnp.maximum(rel, 0.0) * lg), 0.0) for lg in lgs]
    xi = [jnp.exp((row + 1.0) * lg) for lg in lgs]
    zeta = [jnp.exp((c_ - 1.0 - row) * lg) for lg in lgs]

    def chunk_program(bb):
        col_of = lambda part, h: slice(part * RT_WIDTH + h * RT_DIM, part * RT_WIDTH + (h + 1) * RT_DIM)
        q = [p_ref[bb, :, col_of(0, h)] for h in heads]
        k = [p_ref[bb, :, col_of(1, h)] for h in heads]
        v = [p_ref[bb, :, col_of(2, h)] for h in heads]
        q = [x * cos + pltpu.roll(x, RT_DIM // 2, 1) * sin for x in q]
        k = [(x * cos + pltpu.roll(x, RT_DIM // 2, 1) * sin) * (RT_DIM ** -0.5) for x in k]
        s0 = [state_ref[bb, h] for h in heads]
        yield
        scores = [_dot_nt(q[h], k[h]) * decay[h] for h in heads]
        cross = [_dot(q[h], s0[h]) * xi[h] for h in heads]
        for h in heads:
            state_ref[bb, h] = s0[h] * math.exp(c_ * lgs[h]) + _dot_tn(k[h] * zeta[h], v[h])
        yield
        o = [_dot(scores[h], v[h]) + cross[h] for h in heads]
        yield
        for h in heads:
            mu = jnp.mean(o[h], axis=-1, keepdims=True)
            d = o[h] - mu
            var = jnp.mean(d * d, axis=-1, keepdims=True)
            on = d * lax.rsqrt(var + EPS) * gng_ref[:, sls[h]] + gnb_ref[:, sls[h]]
            gate = p_ref[bb, :, col_of(3, h)]
            o_ref[bb, :, sls[h]] = (gate * jax.nn.sigmoid(gate) * on).astype(o_ref.dtype)
        yield

    for _ in zip(*[chunk_program(bb) for bb in range(RT_ROWS)]):
        pass


def _retention(p_rt, b_, s_, cos2, sin2, gn_g, gn_b):
    t = p_rt.shape[0]
    nc = s_ // RT_CHUNK
    out = pl.pallas_call(
        _retention_kernel,
        grid=(b_ // RT_ROWS, nc),
        in_specs=[pl.BlockSpec((RT_ROWS, RT_CHUNK, RT_IN), lambda bi, i: (bi, i, 0)),
                  pl.BlockSpec((RT_CHUNK, RT_DIM), lambda bi, i: (i, 0)),
                  pl.BlockSpec((RT_CHUNK, RT_DIM), lambda bi, i: (i, 0)),
                  _const_spec((1, RT_WIDTH)), _const_spec((1, RT_WIDTH))],
        out_specs=pl.BlockSpec((RT_ROWS, RT_CHUNK, RT_WIDTH), lambda bi, i: (bi, i, 0)),
        out_shape=jax.ShapeDtypeStruct((b_, s_, RT_WIDTH), BF16),
        scratch_shapes=[pltpu.VMEM((RT_ROWS, RT_HEADS, RT_DIM, RT_DIM), F32)],
        compiler_params=_params("parallel", "arbitrary"),
        name="retention",
    )(p_rt.reshape(b_, s_, RT_IN), cos2, sin2, gn_g.reshape(1, -1), gn_b.reshape(1, -1))
    return out.reshape(t, RT_WIDTH)


DSA_BLOCKS_PER_TILE = ROW_TILE // Q_BLOCK


def _dsa_prep_kernel(cq_ref, ckv_ref, kw_ref, qg_ref, kvg_ref, wuq_ref, wuk_ref, wqi_ref,
                     ckvn_ref, ckvt_ref, kidx_ref, widx_ref, qlat_ref, qidx_ref):
    cq = _rms(cq_ref[...], qg_ref[...]).astype(BF16)
    ckvn = _rms(ckv_ref[...], kvg_ref[...])
    ckvn_ref[...] = ckvn.astype(BF16)
    kw = kw_ref[...]
    kidx_ref[...] = kw[:, :IDX_DIM].astype(BF16)
    q = jnp.dot(cq, wuq_ref[...], preferred_element_type=F32)
    qi = jnp.dot(cq, wqi_ref[...], preferred_element_type=F32)
    for blk in range(DSA_BLOCKS_PER_TILE):
        rows = slice(blk * Q_BLOCK, (blk + 1) * Q_BLOCK)
        ckvt_ref[blk] = ckvn[rows].T.astype(BF16)
        widx_ref[blk] = kw[rows].T[IDX_DIM:IDX_DIM + IDX_HEADS] * ((IDX_HEADS * IDX_DIM) ** -0.5)
    for h in range(DSA_HEADS):
        qh = q[:, h * DSA_HEAD_DIM:(h + 1) * DSA_HEAD_DIM]
        ql = (_dot(qh, wuk_ref[h]) * (DSA_HEAD_DIM ** -0.5)).astype(BF16)
        qih = qi[:, h * IDX_DIM:(h + 1) * IDX_DIM].astype(BF16)
        for blk in range(DSA_BLOCKS_PER_TILE):
            rows = slice(blk * Q_BLOCK, (blk + 1) * Q_BLOCK)
            qlat_ref[blk, h] = ql[rows]
            qidx_ref[blk, h] = qih[rows]


def _dsa_prep(c_q, c_kv, kw, q_g, kv_g, w_uq, w_uk, w_qi):
    t = c_q.shape[0]
    row = lambda n: pl.BlockSpec((ROW_TILE, n), lambda i: (i, 0))
    blk3 = lambda m, n: pl.BlockSpec((DSA_BLOCKS_PER_TILE, m, n), lambda i: (i, 0, 0))
    blk4 = lambda n: pl.BlockSpec((DSA_BLOCKS_PER_TILE, DSA_HEADS, Q_BLOCK, n), lambda i: (i, 0, 0, 0))
    return pl.pallas_call(
        _dsa_prep_kernel,
        grid=(t // ROW_TILE,),
        in_specs=[row(DSA_Q_RANK), row(DSA_KV_RANK), row(LANES), _const_spec((1, DSA_Q_RANK)),
                  _const_spec((1, DSA_KV_RANK)),
                  _const_spec(w_uq.shape), _const_spec(w_uk.shape), _const_spec(w_qi.shape)],
        out_specs=[row(DSA_KV_RANK), blk3(DSA_KV_RANK, Q_BLOCK), row(IDX_DIM), blk3(IDX_HEADS, Q_BLOCK),
                   blk4(DSA_KV_RANK), blk4(IDX_DIM)],
        out_shape=[jax.ShapeDtypeStruct((t, DSA_KV_RANK), BF16),
                   jax.ShapeDtypeStruct((t // Q_BLOCK, DSA_KV_RANK, Q_BLOCK), BF16),
                   jax.ShapeDtypeStruct((t, IDX_DIM), BF16),
                   jax.ShapeDtypeStruct((t // Q_BLOCK, IDX_HEADS, Q_BLOCK), F32),
                   jax.ShapeDtypeStruct((t // Q_BLOCK, DSA_HEADS, Q_BLOCK, DSA_KV_RANK), BF16),
                   jax.ShapeDtypeStruct((t // Q_BLOCK, IDX_HEADS, Q_BLOCK, IDX_DIM), BF16)],
        compiler_params=_params("parallel"),
        name="dsa_prep",
    )(c_q, c_kv, kw, q_g.reshape(1, -1), kv_g.reshape(1, -1), w_uq, w_uk, w_qi)


def _fold_rows(x, op):
    return functools.reduce(op, [x[i:i + 8] for i in range(0, x.shape[0], 8)])


def _bit_transpose32(words):
    a = list(words)
    j, m = 16, 0x0000FFFF
    while j:
        k = 0
        while k < 32:
            t = (a[k] ^ lax.shift_right_logical(a[k + j], jnp.int32(j))) & jnp.int32(m)
            a[k] = a[k] ^ t
            a[k + j] = a[k + j] ^ lax.shift_left(t, jnp.int32(j))
            k = (k + j + 1) & ~j
        j >>= 1
        m ^= m << j
    return a


def _dsa_kernel(top_k, qidx_ref, widx_ref, qlat_ref, kidx_ref, ckv_ref, ckvt_ref, wuv_ref, tril_ref, o_ref,
                key_ref, planes_ref, s_ref, acc_ref):
    qb = Q_BLOCK
    nh = DSA_HEADS
    j = pl.program_id(1)
    n_chunks = j + 1

    keyi = lax.broadcasted_iota(jnp.int32, (qb, qb), 0)
    qryi = lax.broadcasted_iota(jnp.int32, (qb, qb), 1)
    lanes = lambda h: slice(h * qb, (h + 1) * qb)

    def key_chunk(ref, c):
        return ref[pl.ds(pl.multiple_of(c * qb, qb), qb), :]

    w_idx = widx_ref[0]
    q_idx = qidx_ref[0].reshape(IDX_HEADS * qb, IDX_DIM)
    q_lat = qlat_ref[0].reshape(nh * qb, DSA_KV_RANK)

    n_pairs = (n_chunks + 1) // 2

    def score_chunk(c):
        logits = lax.dot_general(key_chunk(kidx_ref, c), q_idx, (((1,), (1,)), ((), ())),
                                 preferred_element_type=F32)
        score = jnp.zeros((qb, qb), F32)
        for h in range(IDX_HEADS):
            score = score + jnp.maximum(logits[:, lanes(h)], 0.0) * w_idx[h:h + 1, :]
        bits = lax.bitcast_convert_type(score, jnp.int32)
        key = jnp.where(bits < 0, bits ^ jnp.int32(0x7FFFFFFF), bits)
        causal = keyi + c * qb <= qryi + j * qb
        key = jnp.where(causal, key, jnp.int32(INT_MIN))
        key_ref[c] = key
        return key

    def score_body(pi, carry):
        keys = [score_chunk(2 * pi), score_chunk(2 * pi + 1)]
        words = [k[r:r + 8] ^ jnp.int32(INT_MIN) for k in keys for r in range(0, qb, 8)]
        for idx, plane in enumerate(_bit_transpose32(words)):
            planes_ref[31 - idx, pi] = plane
        return carry

    lax.fori_loop(0, n_pairs, score_body, 0)

    n_pairs_max = planes_ref.shape[1]
    live = tuple(jnp.where(p < n_pairs, jnp.full((8, qb), -1, jnp.int32), jnp.zeros((8, qb), jnp.int32))
                 for p in range(n_pairs_max))

    def bit_body(i, carry):
        thr_u, above, eq = carry
        plane = planes_ref[31 - i]
        hits = [eq[p] & plane[p] for p in range(n_pairs_max)]
        cnt = functools.reduce(jnp.add, [lax.population_count(h) for h in hits])
        cnt = above + jnp.sum(cnt, axis=0, keepdims=True)
        accept = cnt >= top_k
        thr_u = jnp.where(accept, thr_u | lax.shift_left(jnp.int32(1), 31 - i), thr_u)
        above = jnp.where(accept, above, cnt)
        eq = tuple(jnp.where(accept, h, e ^ h) for e, h in zip(eq, hits))
        return thr_u, above, eq

    zero_row = jnp.zeros((1, qb), jnp.int32)
    thr_u, above, _ = lax.fori_loop(0, 32, bit_body, (zero_row, zero_row, live))
    thr = thr_u ^ jnp.int32(INT_MIN)
    need = (top_k - above).astype(F32)

    def pair_logits(t, taken):
        tops = [None] * nh
        for c in (2 * t, 2 * t + 1):
            key = key_ref[c]
            causal = keyi + c * qb <= qryi + j * qb
            eq = jnp.logical_and(key == thr, causal)
            eq_f = jnp.where(eq, 1.0, 0.0)
            rank = taken + jnp.dot(tril_ref[...], eq_f.astype(BF16), preferred_element_type=F32)
            sel = jnp.logical_or(jnp.logical_and(key > thr, causal), jnp.logical_and(eq, rank <= need))
            s = lax.dot_general(key_chunk(ckv_ref, c), q_lat, (((1,), (1,)), ((), ())),
                                preferred_element_type=F32)
            for h in range(nh):
                sh = jnp.where(sel, s[:, lanes(h)], NEG_BIG)
                s_ref[c, :, lanes(h)] = sh
                top = _fold_rows(sh, jnp.maximum)
                tops[h] = top if tops[h] is None else jnp.maximum(tops[h], top)
            taken = taken + jnp.sum(eq_f, axis=0, keepdims=True)
        return taken, tuple(jnp.max(top, axis=0, keepdims=True) for top in tops)

    def pair_values(t, tops, peak, sums):
        ps, scales, new_peak, new_sums = [], [], [], []
        for h in range(nh):
            pk = jnp.maximum(peak[h], tops[h])
            scale = jnp.exp(peak[h] - pk)
            pa = jnp.exp(s_ref[2 * t, :, lanes(h)] - pk)
            pb = jnp.exp(s_ref[2 * t + 1, :, lanes(h)] - pk)
            new_sums.append(sums[h] * scale + _fold_rows(pa + pb, jnp.add))
            ps.append(jnp.concatenate([pa.astype(BF16), pb.astype(BF16)], axis=0))
            scales.append(scale)
            new_peak.append(pk)
        ckvt_pair = jnp.concatenate([ckvt_ref[2 * t], ckvt_ref[2 * t + 1]], axis=1)
        upd = jnp.dot(ckvt_pair, jnp.concatenate(ps, axis=1), preferred_element_type=F32)
        acc_ref[...] = acc_ref[...] * jnp.concatenate(scales, axis=1) + upd
        return tuple(new_peak), tuple(new_sums)

    acc_ref[...] = jnp.zeros_like(acc_ref)

    def attn_body(t, carry):
        taken, tops, peak, sums = carry
        peak, sums = pair_values(t, tops, peak, sums)
        taken, tops = pair_logits(jnp.minimum(t + 1, n_pairs - 1), taken)
        return taken, tops, peak, sums

    taken, tops = pair_logits(0, jnp.zeros((1, qb), F32))
    init = (taken, tops, tuple(jnp.full((1, qb), NEG_BIG, F32) for _ in range(nh)),
            tuple(jnp.zeros((8, qb), F32) for _ in range(nh)))
    _, _, _, sums = lax.fori_loop(0, n_pairs, attn_body, init)

    outs = []
    for h in range(nh):
        o_lat_t = acc_ref[:, lanes(h)] / jnp.sum(sums[h], axis=0, keepdims=True)
        outs.append(_dot_tn(o_lat_t, wuv_ref[h]))
    o_ref[...] = jnp.concatenate(outs, axis=1).astype(o_ref.dtype)


def _dsa(q_idx, w_idx, q_lat, k_idx, ckv_n, ckv_t, w_uv, tril, b_, s_):
    nb = s_ // Q_BLOCK
    t = b_ * s_
    top_k = min(TOPK_MAX, s_ // 4)
    blk = lambda n: pl.BlockSpec((Q_BLOCK, n), lambda bi, i: (bi * nb + i, 0))
    blk3 = lambda m, n: pl.BlockSpec((1, m, n), lambda bi, i: (bi * nb + i, 0, 0))
    blk4 = lambda n: pl.BlockSpec((1, DSA_HEADS, Q_BLOCK, n), lambda bi, i: (bi * nb + i, 0, 0, 0))
    seq = lambda n: pl.BlockSpec((s_, n), lambda bi, i: (bi, 0))
    stacked = DSA_HEADS * Q_BLOCK
    return pl.pallas_call(
        functools.partial(_dsa_kernel, top_k),
        grid=(b_, nb),
        in_specs=[blk4(IDX_DIM), blk3(IDX_HEADS, Q_BLOCK), blk4(DSA_KV_RANK),
                  seq(IDX_DIM), seq(DSA_KV_RANK),
                  pl.BlockSpec((nb, DSA_KV_RANK, Q_BLOCK), lambda bi, i: (bi, 0, 0)),
                  _const_spec(w_uv.shape), _const_spec(tril.shape)],
        out_specs=blk(DSA_WIDTH),
        out_shape=jax.ShapeDtypeStruct((t, DSA_WIDTH), BF16),
        scratch_shapes=[pltpu.VMEM((nb, Q_BLOCK, Q_BLOCK), jnp.int32),
                        pltpu.VMEM((32, nb // 2, 8, Q_BLOCK), jnp.int32),
                        pltpu.VMEM((nb, Q_BLOCK, stacked), F32),
                        pltpu.VMEM((DSA_KV_RANK, stacked), F32)],
        compiler_params=_params("parallel", "arbitrary"),
        name="dsa_attn",
    )(q_idx, w_idx, q_lat, k_idx, ckv_n, ckv_t, w_uv, tril)


SC_TILE = 512


def _od_in_kernel(x_ref, gn_ref, wq_ref, wkv_ref, wkw_ref, wsc_ref, cw_ref, cb_ref,
                  cq_ref, ckv_ref, kw_ref, yd_ref, carry_ref, p_ref):
    @pl.when(pl.program_id(1) == 0)
    def _():
        carry_ref[...] = jnp.zeros_like(carry_ref)

    xb = _rms(x_ref[...], gn_ref[...]).astype(BF16)
    for w_ref, o_ref in ((wq_ref, cq_ref), (wkv_ref, ckv_ref), (wkw_ref, kw_ref), (wsc_ref, p_ref)):
        n = w_ref.shape[1]
        for c in range(0, n, PROJ_CHUNK):
            sl = slice(c, min(c + PROJ_CHUNK, n))
            o_ref[:, sl] = jnp.dot(xb, w_ref[:, sl], preferred_element_type=F32)

    h = p_ref[:, :SC_WIDTH]
    gate_b = p_ref[:, SC_WIDTH:2 * SC_WIDTH]
    gate_c = p_ref[:, 2 * SC_WIDTH:]
    u = gate_c * h
    carry = carry_ref[...]
    y = u * cw_ref[2:3, :] + _shift_rows(u, carry, 1) * cw_ref[1:2, :] + _shift_rows(u, carry, 2) * cw_ref[0:1, :]
    carry_ref[...] = u[SC_TILE - 8:, :]
    yd_ref[...] = (gate_b * (y + cb_ref[...])).astype(yd_ref.dtype)


def _od_in(x, gn, w_q, w_kv, w_kw, w_sc, b_, s_, conv_w, conv_b):
    t, d = x.shape
    nt = s_ // SC_TILE
    row = lambda n: pl.BlockSpec((SC_TILE, n), lambda bi, i: (bi * nt + i, 0))
    ws = [w_q, w_kv, w_kw, w_sc]
    return pl.pallas_call(
        _od_in_kernel,
        grid=(b_, nt),
        in_specs=[row(d), _const_spec((1, d))] + [_const_spec(w.shape) for w in ws]
                 + [_const_spec((8, SC_WIDTH)), _const_spec((1, SC_WIDTH))],
        out_specs=[row(w.shape[1]) for w in ws[:3]] + [row(SC_WIDTH)],
        out_shape=[jax.ShapeDtypeStruct((t, w.shape[1]), F32) for w in ws[:3]]
                  + [jax.ShapeDtypeStruct((t, SC_WIDTH), BF16)],
        scratch_shapes=[pltpu.VMEM((8, SC_WIDTH), F32), pltpu.VMEM((SC_TILE, 3 * SC_WIDTH), F32)],
        compiler_params=_params("parallel", "arbitrary"),
        name="od_in_proj",
    )(x, gn.reshape(1, -1), *ws, jnp.pad(conv_w, ((0, 8 - SC_KERNEL), (0, 0))), conv_b.reshape(1, -1))


def _xattn_kernel(x_ref, gq_ref, wq_ref, k_ref, v_ref, wo_ref, go_ref, o_ref, att_ref):
    x = x_ref[...]
    q = jnp.dot(_rms(x, gq_ref[...]).astype(BF16), wq_ref[...], preferred_element_type=F32)
    for h in range(XA_HEADS):
        sl = slice(h * XA_HEAD_DIM, (h + 1) * XA_HEAD_DIM)
        s = _dot_nt(q[:, sl], k_ref[:, sl]) * (XA_HEAD_DIM ** -0.5)
        s = s - jnp.max(s, axis=-1, keepdims=True)
        p = jnp.exp(s)
        p = p / jnp.sum(p, axis=-1, keepdims=True)
        att_ref[:, sl] = _dot(p, v_ref[:, sl])
    hout = jnp.dot(att_ref[...].astype(BF16), wo_ref[...], preferred_element_type=F32)
    o_ref[...] = x + _rms(hout, go_ref[...])


def _xattn(x, g_q, wq, k_mem, v_mem, wo, g_o, b_, s_):
    t = x.shape[0]
    nt = s_ // ROW_TILE
    row = pl.BlockSpec((ROW_TILE, D_MODEL), lambda bi, i: (bi * nt + i, 0))
    mem = pl.BlockSpec((MEM_LEN, XA_WIDTH), lambda bi, i: (bi, 0))
    return pl.pallas_call(
        _xattn_kernel,
        grid=(b_, nt),
        in_specs=[row, _const_spec((1, D_MODEL)), _const_spec(wq.shape), mem, mem,
                  _const_spec(wo.shape), _const_spec((1, D_MODEL))],
        out_specs=row,
        out_shape=jax.ShapeDtypeStruct((t, D_MODEL), F32),
        scratch_shapes=[pltpu.VMEM((ROW_TILE, XA_WIDTH), F32)],
        compiler_params=_params("parallel", "parallel"),
        name="mem_xattn",
    )(x, g_q.reshape(1, -1), wq, k_mem, v_mem, wo, g_o.reshape(1, -1))


def _block_diag(n_blocks, size, value):
    return np.kron(np.eye(n_blocks, dtype=np.float32), np.full((size, size), value, np.float32))


def _rope_tables(s_):
    half = RT_DIM // 2
    inv_freq = RT_ROPE_BASE ** (-jnp.arange(half, dtype=F32) / half)
    ang = jnp.arange(s_).astype(F32)[:, None] * inv_freq[None, :]
    cos, sin = jnp.cos(ang), jnp.sin(ang)
    return jnp.concatenate([cos, cos], axis=-1), jnp.concatenate([-sin, sin], axis=-1)


def kernel(x, mem, norm_g, mem_norm_g, ffn_w_gate, ffn_w_up, ffn_w_down, xa_wq, xa_wk, xa_wv, xa_wo, ev_w_in, ev_w_out, rw_mu, rw_w0, rw_w2, rw_a0, rw_a2, rw_g2, rw_k_k, rw_k_a, rw_r_k, rw_ln_g, rw_ln_b, rt_gn_g, rt_gn_b, od_w_in, od_w_out, dsa_q_norm_g, dsa_kv_norm_g, dsa_w_uq, dsa_w_uk, dsa_w_uv, dsa_w_qi, sc_conv_w, sc_conv_b):
    b_, s_, d_ = x.shape
    depth = norm_g.shape[0]
    t = b_ * s_
    bf = lambda w: w.astype(BF16)

    ones_blk = jnp.asarray(_block_diag(RW_HEADS, RW_HEAD_DIM, 1.0), BF16)
    ones_pair = jnp.asarray(_block_diag(2, RW_HEAD_DIM, 1.0), BF16)
    tri_rw = jnp.asarray(np.tril(np.ones((RW_CHUNK, RW_CHUNK), np.float32)), BF16)
    tril_dsa = jnp.asarray(np.tril(np.ones((Q_BLOCK, Q_BLOCK), np.float32)), BF16)
    cos2, sin2 = _rope_tables(s_)

    xf = x.reshape(t, d_)
    mem_f = mem.reshape(b_ * MEM_LEN, d_)
    for l in range(depth):
        ng = norm_g[l]
        i = l // 2
        xf = _ffn(xf, ng[0], bf(ffn_w_gate[l, 0]), bf(ffn_w_up[l, 0]), bf(ffn_w_down[l, 0]), ng[1])
        if l % 2 == 0:
            w_in = bf(ev_w_in[i])
            r, k, v, ld, a, b, g, p_rt = _ev_in(xf, ng[2], w_in[:, :RW_IN], w_in[:, RW_IN:], b_, s_, rw_mu[i],
                                                rw_w0[i], rw_w2[i], rw_a0[i], rw_a2[i], rw_g2[i], rw_k_k[i],
                                                rw_k_a[i], ones_blk)
            y_a = _rw_scan(r, k, v, ld, a, b, g, b_, s_, rw_r_k[i], rw_ln_g[i], rw_ln_b[i],
                           ones_pair, tri_rw)
            y_b = _retention(p_rt, b_, s_, cos2, sin2, rt_gn_g[i], rt_gn_b[i])
            w_out = bf(ev_w_out[i])
            xf = _proj_res([y_a, y_b], [w_out[:RW_WIDTH], w_out[RW_WIDTH:]], ng[3], xf, "ev_out_proj")
        else:
            w_in = od_w_in[i]
            kw_w = jnp.pad(w_in[:, DSA_Q_RANK + DSA_KV_RANK:DSA_IN], ((0, 0), (0, LANES - IDX_DIM - IDX_HEADS)))
            c_q, c_kv, kw, y_d = _od_in(
                xf, ng[2], bf(w_in[:, :DSA_Q_RANK]), bf(w_in[:, DSA_Q_RANK:DSA_Q_RANK + DSA_KV_RANK]),
                bf(kw_w), bf(w_in[:, DSA_IN:]), b_, s_, sc_conv_w[i], sc_conv_b[i])
            ckv_n, ckv_t, k_idx, w_idx, q_lat, q_idx = _dsa_prep(
                c_q, c_kv, kw, dsa_q_norm_g[i], dsa_kv_norm_g[i],
                bf(dsa_w_uq[i].reshape(DSA_Q_RANK, DSA_WIDTH)), bf(dsa_w_uk[i]),
                bf(dsa_w_qi[i].reshape(DSA_Q_RANK, IDX_HEADS * IDX_DIM)))
            y_c = _dsa(q_idx, w_idx, q_lat, k_idx, ckv_n, ckv_t, bf(dsa_w_uv[i]), tril_dsa, b_, s_)
            w_out = bf(od_w_out[i])
            xf = _proj_res([y_c, y_d], [w_out[:DSA_WIDTH], w_out[DSA_WIDTH:]], ng[3], xf, "od_out_proj")
        k_mem, v_mem = _norm_proj(mem_f, mem_norm_g, [bf(xa_wk[l]), bf(xa_wv[l])], "mem_kv_proj")
        xf = _xattn(xf, ng[4], bf(xa_wq[l]), k_mem, v_mem, bf(xa_wo[l]), ng[5], b_, s_)
        xf = _ffn(xf, ng[6], bf(ffn_w_gate[l, 1]), bf(ffn_w_up[l, 1]), bf(ffn_w_down[l, 1]), ng[7])
    return xf.reshape(b_, s_, d_)
```

```python
import functools
import math

import numpy as np
import jax
import jax.numpy as jnp
from jax import lax
from jax.experimental import pallas as pl
from jax.experimental.pallas import tpu as pltpu

F32 = jnp.float32
BF16 = jnp.bfloat16

D_MODEL = 1024
D_FF = 2816
EPS = 1e-6
MEM_LEN = 256
RW_HEADS = 8
RW_HEAD_DIM = 64
RW_WIDTH = RW_HEADS * RW_HEAD_DIM
RW_DECAY_RANK = 64
RW_AAA_RANK = 64
RW_GATE_RANK = 128
RW_LN_EPS = 64e-5
RW_IN = 3 * RW_WIDTH + RW_DECAY_RANK + RW_AAA_RANK + RW_GATE_RANK
RW_CHUNK = 64
RW_ROWS = 4
RT_HEADS = 4
RT_DIM = 128
RT_WIDTH = RT_HEADS * RT_DIM
RT_CHUNK = 128
RT_ROWS = 4
RT_ROPE_BASE = 10000.0
RT_IN = 4 * RT_WIDTH
DSA_HEADS = 8
DSA_HEAD_DIM = 64
DSA_WIDTH = DSA_HEADS * DSA_HEAD_DIM
DSA_Q_RANK = 256
DSA_KV_RANK = 128
IDX_HEADS = 8
IDX_DIM = 64
TOPK_MAX = 256
Q_BLOCK = 128
DSA_IN = DSA_Q_RANK + DSA_KV_RANK + IDX_DIM + IDX_HEADS
SC_WIDTH = 512
SC_KERNEL = 3
XA_HEADS = 4
XA_HEAD_DIM = 128
XA_WIDTH = XA_HEADS * XA_HEAD_DIM

LANES = 128
ROW_TILE = 512
VMEM_LIMIT = 56 * 1024 * 1024
INT_MIN = -2 ** 31
NEG_BIG = -1e30


def _params(*sem):
    return pltpu.CompilerParams(dimension_semantics=sem, vmem_limit_bytes=VMEM_LIMIT)


def _rms(x, g):
    return x * lax.rsqrt(jnp.mean(x * x, axis=-1, keepdims=True) + EPS) * g


def _dot(a, b):
    return jnp.dot(a.astype(BF16), b.astype(BF16), preferred_element_type=F32)


def _dot_nt(a, b):
    return lax.dot_general(a.astype(BF16), b.astype(BF16), (((1,), (1,)), ((), ())),
                           preferred_element_type=F32)


def _dot_tn(a, b):
    return lax.dot_general(a.astype(BF16), b.astype(BF16), (((0,), (0,)), ((), ())),
                           preferred_element_type=F32)


def _split3(x):
    hi = x.astype(BF16)
    r1 = x - hi.astype(F32)
    mid = r1.astype(BF16)
    lo = (r1 - mid.astype(F32)).astype(BF16)
    return hi, mid, lo


def _dot_exact_rhs(x, w_bf16):
    hi, mid, lo = _split3(x)
    out = jnp.dot(hi, w_bf16, preferred_element_type=F32)
    out += jnp.dot(mid, w_bf16, preferred_element_type=F32)
    out += jnp.dot(lo, w_bf16, preferred_element_type=F32)
    return out


def _dot_exact_lhs(w_bf16, x):
    hi, mid, lo = _split3(x)
    out = jnp.dot(w_bf16, hi, preferred_element_type=F32)
    out += jnp.dot(w_bf16, mid, preferred_element_type=F32)
    out += jnp.dot(w_bf16, lo, preferred_element_type=F32)
    return out


def _const_spec(shape):
    nd = len(shape)
    return pl.BlockSpec(shape, lambda *_: (0,) * nd, pipeline_mode=pl.Buffered(1))


FF_CHUNK = 256
FF_TILE = 512


def _ffn_kernel(x_ref, gin_ref, wg_ref, wu_ref, wd_ref, gout_ref, o_ref, acc_ref):
    x = x_ref[...]
    xb = _rms(x, gin_ref[...]).astype(BF16)
    for c in range(D_FF // FF_CHUNK):
        sl = slice(c * FF_CHUNK, (c + 1) * FF_CHUNK)
        g = jnp.dot(xb, wg_ref[:, sl], preferred_element_type=F32)
        u = jnp.dot(xb, wu_ref[:, sl], preferred_element_type=F32)
        h = (g * jax.nn.sigmoid(g) * u).astype(BF16)
        part = jnp.dot(h, wd_ref[sl, :], preferred_element_type=F32)
        if c == 0:
            acc_ref[...] = part
        else:
            acc_ref[...] += part
    o_ref[...] = x + 0.5 * _rms(acc_ref[...], gout_ref[...])


def _ffn(x, g_in, wg, wu, wd, g_out):
    t = x.shape[0]
    row = pl.BlockSpec((FF_TILE, D_MODEL), lambda i: (i, 0))
    return pl.pallas_call(
        _ffn_kernel,
        grid=(t // FF_TILE,),
        in_specs=[row, _const_spec((1, D_MODEL)), _const_spec((D_MODEL, D_FF)),
                  _const_spec((D_MODEL, D_FF)), _const_spec((D_FF, D_MODEL)), _const_spec((1, D_MODEL))],
        out_specs=row,
        out_shape=jax.ShapeDtypeStruct((t, D_MODEL), F32),
        scratch_shapes=[pltpu.VMEM((FF_TILE, D_MODEL), F32)],
        compiler_params=_params("parallel"),
        name="ffn_half",
    )(x, g_in.reshape(1, -1), wg, wu, wd, g_out.reshape(1, -1))


PROJ_CHUNK = 512


def _norm_proj_kernel(n_out, x_ref, g_ref, *refs):
    xb = _rms(x_ref[...], g_ref[...]).astype(BF16)
    for w_ref, o_ref in zip(refs[:n_out], refs[n_out:]):
        n = w_ref.shape[1]
        for c in range(0, n, PROJ_CHUNK):
            sl = slice(c, min(c + PROJ_CHUNK, n))
            o_ref[:, sl] = jnp.dot(xb, w_ref[:, sl], preferred_element_type=F32)


def _norm_proj(x, g, ws, name):
    t, d = x.shape
    row = lambda n: pl.BlockSpec((ROW_TILE, n), lambda i: (i, 0))
    return pl.pallas_call(
        functools.partial(_norm_proj_kernel, len(ws)),
        grid=(t // ROW_TILE,),
        in_specs=[row(d), _const_spec((1, d))] + [_const_spec(w.shape) for w in ws],
        out_specs=[row(w.shape[1]) for w in ws],
        out_shape=[jax.ShapeDtypeStruct((t, w.shape[1]), F32) for w in ws],
        compiler_params=_params("parallel"),
        name=name,
    )(x, g.reshape(1, -1), *ws)


RW_TILE = 512


def _shift_rows(x, carry, n):
    rolled = pltpu.roll(x, n, 0)
    row = lax.broadcasted_iota(jnp.int32, x.shape, 0)
    out = rolled
    for i in range(n):
        out = jnp.where(row == i, carry[8 - n + i:8 - n + i + 1, :], out)
    return out


def _ev_in_kernel(x_ref, gn_ref, wrw_ref, wrt_ref, mu_ref, w0_ref, w2_ref, a0_ref, a2_ref, g2_ref, kk_ref, ka_ref,
                  ones_ref, r_ref, k_ref, v_ref, ld_ref, a_ref, b_ref, g_ref, prt_ref, carry_ref, p_ref):
    @pl.when(pl.program_id(1) == 0)
    def _():
        carry_ref[...] = jnp.zeros_like(carry_ref)

    xb = _rms(x_ref[...], gn_ref[...]).astype(BF16)
    for w_ref, o_ref in ((wrw_ref, p_ref), (wrt_ref, prt_ref)):
        n = w_ref.shape[1]
        for c in range(0, n, PROJ_CHUNK):
            sl = slice(c, min(c + PROJ_CHUNK, n))
            o_ref[:, sl] = jnp.dot(xb, w_ref[:, sl], preferred_element_type=F32)

    p = p_ref[...]
    prev = _shift_rows(p, carry_ref[...], 1)
    carry_ref[...] = p[RW_TILE - 8:, :]
    xm = p + (prev - p) * mu_ref[...]
    w = RW_WIDTH
    r, k, v = xm[:, :w], xm[:, w:2 * w], xm[:, 2 * w:3 * w]
    xw = xm[:, 3 * w:3 * w + RW_DECAY_RANK]
    xa = xm[:, 3 * w + RW_DECAY_RANK:3 * w + RW_DECAY_RANK + RW_AAA_RANK]
    xg = xm[:, 3 * w + RW_DECAY_RANK + RW_AAA_RANK:]
    wlog = -jax.nn.softplus(-(w0_ref[...] + _dot(jnp.tanh(xw), w2_ref[...]))) - 0.5
    a = jax.nn.sigmoid(a0_ref[...] + _dot(xa, a2_ref[...]))
    kk = k * kk_ref[...]
    ss = _dot_exact_rhs(kk * kk, ones_ref[...])
    kk = kk / jnp.maximum(jnp.sqrt(ss), 1e-12)
    r_ref[...] = r
    k_ref[...] = k * (1.0 + (a - 1.0) * ka_ref[...])
    v_ref[...] = v
    ld_ref[...] = -jnp.exp(wlog)
    a_ref[...] = -kk
    b_ref[...] = kk * a
    g_ref[...] = _dot(jax.nn.sigmoid(xg), g2_ref[...])


def _ev_in(x, gn, w_rw, w_rt, b_, s_, mu, w0, w2, a0, a2, g2, k_k, k_a, ones_blk):
    t, d = x.shape
    nt = s_ // RW_TILE
    row = lambda n: pl.BlockSpec((RW_TILE, n), lambda b, i: (b * nt + i, 0))
    vec = lambda a: a.reshape(1, -1)
    outs = [jax.ShapeDtypeStruct((t, RW_WIDTH), F32)] * 7 + [jax.ShapeDtypeStruct((t, RT_IN), F32)]
    return pl.pallas_call(
        _ev_in_kernel,
        grid=(b_, nt),
        in_specs=[row(d), _const_spec((1, d)), _const_spec(w_rw.shape), _const_spec(w_rt.shape),
                  _const_spec((1, RW_IN)), _const_spec((1, RW_WIDTH)),
                  _const_spec(w2.shape), _const_spec((1, RW_WIDTH)), _const_spec(a2.shape),
                  _const_spec(g2.shape), _const_spec((1, RW_WIDTH)), _const_spec((1, RW_WIDTH)),
                  _const_spec(ones_blk.shape)],
        out_specs=[row(RW_WIDTH)] * 7 + [row(RT_IN)],
        out_shape=outs,
        scratch_shapes=[pltpu.VMEM((8, RW_IN), F32), pltpu.VMEM((RW_TILE, RW_IN), F32)],
        compiler_params=_params("parallel", "arbitrary"),
        name="ev_in_proj",
    )(x, vec(gn), w_rw, w_rt, vec(mu), vec(w0), w2, vec(a0), a2, g2, vec(k_k), vec(k_a), ones_blk)


def _rw_scan_kernel(r_ref, k_ref, v_ref, ld_ref, a_ref, b_ref, g_ref, tri_ref, ones_ref,
                    rk_ref, lng_ref, lnb_ref, o_ref, state_ref):
    c_ = RW_CHUNK
    n_ = RW_HEAD_DIM

    @pl.when(pl.program_id(1) == 0)
    def _():
        state_ref[...] = jnp.zeros_like(state_ref)

    c2 = 2 * c_
    row = lax.broadcasted_iota(jnp.int32, (c2, c2), 0)
    col = lax.broadcasted_iota(jnp.int32, (c2, c2), 1)
    same_head = (row >= c_) == (col >= c_)
    strict = jnp.logical_and(same_head, row > col)
    incl = jnp.logical_and(same_head, row >= col)
    left = lax.broadcasted_iota(jnp.int32, (c_, LANES), 1) < n_

    def block_diag(x):
        return jnp.concatenate([jnp.where(left, x, 0.0), jnp.where(left, 0.0, x)], axis=0)

    pairs = range(RW_HEADS // 2)
    sls = [slice(p * LANES, (p + 1) * LANES) for p in pairs]
    ones2 = ones_ref[...]

    def head_sums(xs):
        parts = [part[:, sl] for x in xs for part in _split3(x) for sl in sls]
        prod = jnp.dot(jnp.concatenate(parts, axis=0), ones2, preferred_element_type=F32)
        outs = []
        for i in range(len(xs)):
            slabs = []
            for p in pairs:
                rows = [((3 * i + j) * len(sls) + p) * c_ for j in range(3)]
                slabs.append(prod[rows[0]:rows[0] + c_] + prod[rows[1]:rows[1] + c_] + prod[rows[2]:rows[2] + c_])
            outs.append(jnp.concatenate(slabs, axis=1))
        return outs

    def chunk_program(bb):
        ld = ld_ref[bb]
        r = r_ref[bb]
        k = k_ref[bb]
        v = v_ref[bb]
        cum = _dot_exact_lhs(tri_ref[...], ld)
        mid = cum[c_ // 2 - 1:c_ // 2, :]
        e_in = jnp.exp(cum - mid)
        e_out = jnp.exp(mid - cum)
        r_t = r * e_in
        a_t = a_ref[bb] * jnp.exp(cum - ld - mid)
        b_t = b_ref[bb] * e_out
        k_t = k * e_out
        e_mid = jnp.exp(mid)
        w_all = jnp.exp(cum[c_ - 1:c_, :])
        w_tail = jnp.exp(cum[c_ - 1:c_, :] - mid)
        yield
        s0 = [state_ref[bb, p] for p in pairs]
        ar = [jnp.concatenate([block_diag(a_t[:, sl]), block_diag(r_t[:, sl])], axis=0) for sl in sls]
        bk = [jnp.concatenate([block_diag(b_t[:, sl]), block_diag(k_t[:, sl])], axis=0) for sl in sls]
        vb = [block_diag(v[:, sl]) for sl in sls]
        m1 = [_dot_nt(ar[p], bk[p]) for p in pairs]
        m2 = [_dot_nt(ar[p] * e_mid[:, sls[p]], s0[p]) for p in pairs]
        yield
        l_ab = [jnp.where(strict, m[:c2, :c2], 0.0) for m in m1]
        l_ak = [jnp.where(strict, m[:c2, c2:], 0.0) for m in m1]
        l_r = [jnp.concatenate([jnp.where(incl, m[c2:, :c2], 0.0), jnp.where(incl, m[c2:, c2:], 0.0)], axis=1)
               for m in m1]
        u = [m2[p][:c2] + _dot(l_ak[p], vb[p]) for p in pairs]
        pw = l_ab
        yield
        n_steps = int(math.log2(c_))
        for step in range(n_steps):
            if step < n_steps - 1:
                prod = [_dot(pw[p], jnp.concatenate([pw[p], u[p]], axis=1)) for p in pairs]
                pw = [q[:, :c2] for q in prod]
                u = [u[p] + prod[p][:, c2:] for p in pairs]
            else:
                u = [u[p] + _dot(pw[p], u[p]) for p in pairs]
            yield
        uv = [jnp.concatenate([u[p], vb[p]], axis=0) for p in pairs]
        ys = [m2[p][c2:] + _dot(l_r[p], uv[p]) for p in pairs]
        upd = [_dot_tn(uv[p], bk[p]) for p in pairs]
        yield
        for p in pairs:
            state_ref[bb, p] = s0[p] * w_all[:, sls[p]] + upd[p] * w_tail[:, sls[p]]
        y = jnp.concatenate([ys[p][:c_] + ys[p][c_:] for p in pairs], axis=1)
        sum_y, sum_rk = head_sums([y, r * k * rk_ref[...]])
        d = y - sum_y * (1.0 / n_)
        yield
        var = head_sums([d * d])[0] * (1.0 / n_)
        yn = d * lax.rsqrt(var + RW_LN_EPS) * lng_ref[...] + lnb_ref[...]
        o_ref[bb] = ((yn + sum_rk * v) * g_ref[bb]).astype(o_ref.dtype)
        yield

    for _ in zip(*[chunk_program(bb) for bb in range(RW_ROWS)]):
        pass


def _rw_scan(r, k, v, ld, a, b, g, b_, s_, r_k, ln_g, ln_b, ones_pair, tri):
    t = r.shape[0]
    nc = s_ // RW_CHUNK
    row = pl.BlockSpec((RW_ROWS, RW_CHUNK, RW_WIDTH), lambda bi, i: (bi, i, 0))
    vec = lambda x: x.reshape(1, -1)
    seq = lambda x: x.reshape(b_, s_, RW_WIDTH)
    out = pl.pallas_call(
        _rw_scan_kernel,
        grid=(b_ // RW_ROWS, nc),
        in_specs=[row] * 7 + [_const_spec(tri.shape), _const_spec(ones_pair.shape),
                              _const_spec((1, RW_WIDTH)), _const_spec((1, RW_WIDTH)), _const_spec((1, RW_WIDTH))],
        out_specs=row,
        out_shape=jax.ShapeDtypeStruct((b_, s_, RW_WIDTH), BF16),
        scratch_shapes=[pltpu.VMEM((RW_ROWS, RW_HEADS // 2, 2 * RW_HEAD_DIM, 2 * RW_HEAD_DIM), F32)],
        compiler_params=_params("parallel", "arbitrary"),
        name="rwkv_scan",
    )(seq(r), seq(k), seq(v), seq(ld), seq(a), seq(b), seq(g), tri, ones_pair, vec(r_k), vec(ln_g), vec(ln_b))
    return out.reshape(t, RW_WIDTH)


def _rt_log_gamma(h):
    return math.log1p(-(2.0 ** (-5.0 - h)))


def _retention_kernel(p_ref, cos_ref, sin_ref, gng_ref, gnb_ref, o_ref, state_ref):
    c_ = RT_CHUNK

    @pl.when(pl.program_id(1) == 0)
    def _():
        state_ref[...] = jnp.zeros_like(state_ref)

    cos = cos_ref[...]
    sin = sin_ref[...]
    row = lax.broadcasted_iota(jnp.int32, (c_, c_), 0).astype(F32)
    col = lax.broadcasted_iota(jnp.int32, (c_, c_), 1).astype(F32)
    rel = row - col
    heads = range(RT_HEADS)
    lgs = [_rt_log_gamma(h) for h in heads]
    sls = [slice(h * RT_DIM, (h + 1) * RT_DIM) for h in heads]
    decay = [jnp.where(rel >= 0, jnp.exp(jnp.maximum(rel, 0.0) * lg), 0.0) for lg in lgs]
    xi = [jnp.exp((row + 1.0) * lg) for lg in lgs]
    zeta = [jnp.exp((c_ - 1.0 - row) * lg) for lg in lgs]

    def chunk_program(bb):
        col_of = lambda part, h: slice(part * RT_WIDTH + h * RT_DIM, part * RT_WIDTH + (h + 1) * RT_DIM)
        q = [p_ref[bb, :, col_of(0, h)] for h in heads]
        k = [p_ref[bb, :, col_of(1, h)] for h in heads]
        v = [p_ref[bb, :, col_of(2, h)] for h in heads]
        q = [x * cos + pltpu.roll(x, RT_DIM // 2, 1) * sin for x in q]
        k = [(x * cos + pltpu.roll(x, RT_DIM // 2, 1) * sin) * (RT_DIM ** -0.5) for x in k]
        s0 = [state_ref[bb, h] for h in heads]
        yield
        scores = [_dot_nt(q[h], k[h]) * decay[h] for h in heads]
        cross = [_dot(q[h], s0[h]) * xi[h] for h in heads]
        for h in heads:
            state_ref[bb, h] = s0[h] * math.exp(c_ * lgs[h]) + _dot_tn(k[h] * zeta[h], v[h])
        yield
        o = [_dot(scores[h], v[h]) + cross[h] for h in heads]
        yield
        for h in heads:
            mu = jnp.mean(o[h], axis=-1, keepdims=True)
            d = o[h] - mu
            var = jnp.mean(d * d, axis=-1, keepdims=True)
            on = d * lax.rsqrt(var + EPS) * gng_ref[:, sls[h]] + gnb_ref[:, sls[h]]
            gate = p_ref[bb, :, col_of(3, h)]
            o_ref[bb, :, sls[h]] = (gate * jax.nn.sigmoid(gate) * on).astype(o_ref.dtype)
        yield

    for _ in zip(*[chunk_program(bb) for bb in range(RT_ROWS)]):
        pass


def _retention(p_rt, b_, s_, cos2, sin2, gn_g, gn_b):
    t = p_rt.shape[0]
    nc = s_ // RT_CHUNK
    out = pl.pallas_call(
        _retention_kernel,
        grid=(b_ // RT_ROWS, nc),
        in_specs=[pl.BlockSpec((RT_ROWS, RT_CHUNK, RT_IN), lambda bi, i: (bi, i, 0)),
                  pl.BlockSpec((RT_CHUNK, RT_DIM), lambda bi, i: (i, 0)),
                  pl.BlockSpec((RT_CHUNK, RT_DIM), lambda bi, i: (i, 0)),
                  _const_spec((1, RT_WIDTH)), _const_spec((1, RT_WIDTH))],
        out_specs=pl.BlockSpec((RT_ROWS, RT_CHUNK, RT_WIDTH), lambda bi, i: (bi, i, 0)),
        out_shape=jax.ShapeDtypeStruct((b_, s_, RT_WIDTH), BF16),
        scratch_shapes=[pltpu.VMEM((RT_ROWS, RT_HEADS, RT_DIM, RT_DIM), F32)],
        compiler_params=_params("parallel", "arbitrary"),
        name="retention",
    )(p_rt.reshape(b_, s_, RT_IN), cos2, sin2, gn_g.reshape(1, -1), gn_b.reshape(1, -1))
    return out.reshape(t, RT_WIDTH)


DSA_BLOCKS_PER_TILE = ROW_TILE // Q_BLOCK


def _dsa_prep_kernel(cq_ref, ckv_ref, kw_ref, qg_ref, kvg_ref, wuq_ref, wuk_ref, wqi_ref,
                     ckvn_ref, ckvt_ref, kidx_ref, widx_ref, qlat_ref, qidx_ref):
    cq = _rms(cq_ref[...], qg_ref[...]).astype(BF16)
    ckvn = _rms(ckv_ref[...], kvg_ref[...])
    ckvn_ref[...] = ckvn.astype(BF16)
    kw = kw_ref[...]
    kidx_ref[...] = kw[:, :IDX_DIM].astype(BF16)
    q = jnp.dot(cq, wuq_ref[...], preferred_element_type=F32)
    qi = jnp.dot(cq, wqi_ref[...], preferred_element_type=F32)
    for blk in range(DSA_BLOCKS_PER_TILE):
        rows = slice(blk * Q_BLOCK, (blk + 1) * Q_BLOCK)
        ckvt_ref[blk] = ckvn[rows].T.astype(BF16)
        widx_ref[blk] = kw[rows].T[IDX_DIM:IDX_DIM + IDX_HEADS] * ((IDX_HEADS * IDX_DIM) ** -0.5)
    for h in range(DSA_HEADS):
        qh = q[:, h * DSA_HEAD_DIM:(h + 1) * DSA_HEAD_DIM]
        ql = (_dot(qh, wuk_ref[h]) * (DSA_HEAD_DIM ** -0.5)).astype(BF16)
        qih = qi[:, h * IDX_DIM:(h + 1) * IDX_DIM].astype(BF16)
        for blk in range(DSA_BLOCKS_PER_TILE):
            rows = slice(blk * Q_BLOCK, (blk + 1) * Q_BLOCK)
            qlat_ref[blk, h] = ql[rows]
            qidx_ref[blk, h] = qih[rows]


def _dsa_prep(c_q, c_kv, kw, q_g, kv_g, w_uq, w_uk, w_qi):
    t = c_q.shape[0]
    row = lambda n: pl.BlockSpec((ROW_TILE, n), lambda i: (i, 0))
    blk3 = lambda m, n: pl.BlockSpec((DSA_BLOCKS_PER_TILE, m, n), lambda i: (i, 0, 0))
    blk4 = lambda n: pl.BlockSpec((DSA_BLOCKS_PER_TILE, DSA_HEADS, Q_BLOCK, n), lambda i: (i, 0, 0, 0))
    return pl.pallas_call(
        _dsa_prep_kernel,
        grid=(t // ROW_TILE,),
        in_specs=[row(DSA_Q_RANK), row(DSA_KV_RANK), row(LANES), _const_spec((1, DSA_Q_RANK)),
                  _const_spec((1, DSA_KV_RANK)),
                  _const_spec(w_uq.shape), _const_spec(w_uk.shape), _const_spec(w_qi.shape)],
        out_specs=[row(DSA_KV_RANK), blk3(DSA_KV_RANK, Q_BLOCK), row(IDX_DIM), blk3(IDX_HEADS, Q_BLOCK),
                   blk4(DSA_KV_RANK), blk4(IDX_DIM)],
        out_shape=[jax.ShapeDtypeStruct((t, DSA_KV_RANK), BF16),
                   jax.ShapeDtypeStruct((t // Q_BLOCK, DSA_KV_RANK, Q_BLOCK), BF16),
                   jax.ShapeDtypeStruct((t, IDX_DIM), BF16),
                   jax.ShapeDtypeStruct((t // Q_BLOCK, IDX_HEADS, Q_BLOCK), F32),
                   jax.ShapeDtypeStruct((t // Q_BLOCK, DSA_HEADS, Q_BLOCK, DSA_KV_RANK), BF16),
                   jax.ShapeDtypeStruct((t // Q_BLOCK, IDX_HEADS, Q_BLOCK, IDX_DIM), BF16)],
        compiler_params=_params("parallel"),
        name="dsa_prep",
    )(c_q, c_kv, kw, q_g.reshape(1, -1), kv_g.reshape(1, -1), w_uq, w_uk, w_qi)


def _fold_rows(x, op):
    return functools.reduce(op, [x[i:i + 8] for i in range(0, x.shape[0], 8)])


def _bit_transpose32(words):
    a = list(words)
    j, m = 16, 0x0000FFFF
    while j:
        k = 0
        while k < 32:
            t = (a[k] ^ lax.shift_right_logical(a[k + j], jnp.int32(j))) & jnp.int32(m)
            a[k] = a[k] ^ t
            a[k + j] = a[k + j] ^ lax.shift_left(t, jnp.int32(j))
            k = (k + j + 1) & ~j
        j >>= 1
        m ^= m << j
    return a


def _dsa_kernel(top_k, qidx_ref, widx_ref, qlat_ref, kidx_ref, ckv_ref, ckvt_ref, wuv_ref, tril_ref, o_ref,
                key_ref, planes_ref, s_ref, acc_ref):
    qb = Q_BLOCK
    nh = DSA_HEADS
    j = pl.program_id(1)
    n_chunks = j + 1

    keyi = lax.broadcasted_iota(jnp.int32, (qb, qb), 0)
    qryi = lax.broadcasted_iota(jnp.int32, (qb, qb), 1)
    lanes = lambda h: slice(h * qb, (h + 1) * qb)

    def key_chunk(ref, c):
        return ref[pl.ds(pl.multiple_of(c * qb, qb), qb), :]

    w_idx = widx_ref[0]
    q_idx = qidx_ref[0].reshape(IDX_HEADS * qb, IDX_DIM)
    q_lat = qlat_ref[0].reshape(nh * qb, DSA_KV_RANK)

    n_pairs = (n_chunks + 1) // 2

    def score_chunk(c):
        logits = lax.dot_general(key_chunk(kidx_ref, c), q_idx, (((1,), (1,)), ((), ())),
                                 preferred_element_type=F32)
        score = jnp.zeros((qb, qb), F32)
        for h in range(IDX_HEADS):
            score = score + jnp.maximum(logits[:, lanes(h)], 0.0) * w_idx[h:h + 1, :]
        bits = lax.bitcast_convert_type(score, jnp.int32)
        key = jnp.where(bits < 0, bits ^ jnp.int32(0x7FFFFFFF), bits)
        causal = keyi + c * qb <= qryi + j * qb
        key = jnp.where(causal, key, jnp.int32(INT_MIN))
        key_ref[c] = key
        return key

    def score_body(pi, carry):
        keys = [score_chunk(2 * pi), score_chunk(2 * pi + 1)]
        words = [k[r:r + 8] ^ jnp.int32(INT_MIN) for k in keys for r in range(0, qb, 8)]
        for idx, plane in enumerate(_bit_transpose32(words)):
            planes_ref[31 - idx, pi] = plane
        return carry

    lax.fori_loop(0, n_pairs, score_body, 0)

    n_pairs_max = planes_ref.shape[1]
    live = tuple(jnp.where(p < n_pairs, jnp.full((8, qb), -1, jnp.int32), jnp.zeros((8, qb), jnp.int32))
                 for p in range(n_pairs_max))

    def bit_body(i, carry):
        thr_u, above, eq = carry
        plane = planes_ref[31 - i]
        hits = [eq[p] & plane[p] for p in range(n_pairs_max)]
        cnt = functools.reduce(jnp.add, [lax.population_count(h) for h in hits])
        cnt = above + jnp.sum(cnt, axis=0, keepdims=True)
        accept = cnt >= top_k
        thr_u = jnp.where(accept, thr_u | lax.shift_left(jnp.int32(1), 31 - i), thr_u)
        above = jnp.where(accept, above, cnt)
        eq = tuple(jnp.where(accept, h, e ^ h) for e, h in zip(eq, hits))
        return thr_u, above, eq

    zero_row = jnp.zeros((1, qb), jnp.int32)
    thr_u, above, _ = lax.fori_loop(0, 32, bit_body, (zero_row, zero_row, live))
    thr = thr_u ^ jnp.int32(INT_MIN)
    need = (top_k - above).astype(F32)

    def pair_logits(t, taken):
        tops = [None] * nh
        for c in (2 * t, 2 * t + 1):
            key = key_ref[c]
            causal = keyi + c * qb <= qryi + j * qb
            eq = jnp.logical_and(key == thr, causal)
            eq_f = jnp.where(eq, 1.0, 0.0)
            rank = taken + jnp.dot(tril_ref[...], eq_f.astype(BF16), preferred_element_type=F32)
            sel = jnp.logical_or(jnp.logical_and(key > thr, causal), jnp.logical_and(eq, rank <= need))
            s = lax.dot_general(key_chunk(ckv_ref, c), q_lat, (((1,), (1,)), ((), ())),
                                preferred_element_type=F32)
            for h in range(nh):
                sh = jnp.where(sel, s[:, lanes(h)], NEG_BIG)
                s_ref[c, :, lanes(h)] = sh
                top = _fold_rows(sh, jnp.maximum)
                tops[h] = top if tops[h] is None else jnp.maximum(tops[h], top)
            taken = taken + jnp.sum(eq_f, axis=0, keepdims=True)
        return taken, tuple(jnp.max(top, axis=0, keepdims=True) for top in tops)

    def pair_values(t, tops, peak, sums):
        ps, scales, new_peak, new_sums = [], [], [], []
        for h in range(nh):
            pk = jnp.maximum(peak[h], tops[h])
            scale = jnp.exp(peak[h] - pk)
            pa = jnp.exp(s_ref[2 * t, :, lanes(h)] - pk)
            pb = jnp.exp(s_ref[2 * t + 1, :, lanes(h)] - pk)
            new_sums.append(sums[h] * scale + _fold_rows(pa + pb, jnp.add))
            ps.append(jnp.concatenate([pa.astype(BF16), pb.astype(BF16)], axis=0))
            scales.append(scale)
            new_peak.append(pk)
        ckvt_pair = jnp.concatenate([ckvt_ref[2 * t], ckvt_ref[2 * t + 1]], axis=1)
        upd = jnp.dot(ckvt_pair, jnp.concatenate(ps, axis=1), preferred_element_type=F32)
        acc_ref[...] = acc_ref[...] * jnp.concatenate(scales, axis=1) + upd
        return tuple(new_peak), tuple(new_sums)

    acc_ref[...] = jnp.zeros_like(acc_ref)

    def attn_body(t, carry):
        taken, tops, peak, sums = carry
        peak, sums = pair_values(t, tops, peak, sums)
        taken, tops = pair_logits(jnp.minimum(t + 1, n_pairs - 1), taken)
        return taken, tops, peak, sums

    taken, tops = pair_logits(0, jnp.zeros((1, qb), F32))
    init = (taken, tops, tuple(jnp.full((1, qb), NEG_BIG, F32) for _ in range(nh)),
            tuple(jnp.zeros((8, qb), F32) for _ in range(nh)))
    _, _, _, sums = lax.fori_loop(0, n_pairs, attn_body, init)

    outs = []
    for h in range(nh):
        o_lat_t = acc_ref[:, lanes(h)] / jnp.sum(sums[h], axis=0, keepdims=True)
        outs.append(_dot_tn(o_lat_t, wuv_ref[h]))
    o_ref[...] = jnp.concatenate(outs, axis=1).astype(o_ref.dtype)


def _dsa(q_idx, w_idx, q_lat, k_idx, ckv_n, ckv_t, w_uv, tril, b_, s_):
    nb = s_ // Q_BLOCK
    t = b_ * s_
    top_k = min(TOPK_MAX, s_ // 4)
    blk = lambda n: pl.BlockSpec((Q_BLOCK, n), lambda bi, i: (bi * nb + i, 0))
    blk3 = lambda m, n: pl.BlockSpec((1, m, n), lambda bi, i: (bi * nb + i, 0, 0))
    blk4 = lambda n: pl.BlockSpec((1, DSA_HEADS, Q_BLOCK, n), lambda bi, i: (bi * nb + i, 0, 0, 0))
    seq = lambda n: pl.BlockSpec((s_, n), lambda bi, i: (bi, 0))
    stacked = DSA_HEADS * Q_BLOCK
    return pl.pallas_call(
        functools.partial(_dsa_kernel, top_k),
        grid=(b_, nb),
        in_specs=[blk4(IDX_DIM), blk3(IDX_HEADS, Q_BLOCK), blk4(DSA_KV_RANK),
                  seq(IDX_DIM), seq(DSA_KV_RANK),
                  pl.BlockSpec((nb, DSA_KV_RANK, Q_BLOCK), lambda bi, i: (bi, 0, 0)),
                  _const_spec(w_uv.shape), _const_spec(tril.shape)],
        out_specs=blk(DSA_WIDTH),
        out_shape=jax.ShapeDtypeStruct((t, DSA_WIDTH), BF16),
        scratch_shapes=[pltpu.VMEM((nb, Q_BLOCK, Q_BLOCK), jnp.int32),
                        pltpu.VMEM((32, nb // 2, 8, Q_BLOCK), jnp.int32),
                        pltpu.VMEM((nb, Q_BLOCK, stacked), F32),
                        pltpu.VMEM((DSA_KV_RANK, stacked), F32)],
        compiler_params=_params("parallel", "arbitrary"),
        name="dsa_attn",
    )(q_idx, w_idx, q_lat, k_idx, ckv_n, ckv_t, w_uv, tril)


SC_TILE = 512


def _od_in_kernel(x_ref, gn_ref, wq_ref, wkv_ref, wkw_ref, wsc_ref, cw_ref, cb_ref,
                  cq_ref, ckv_ref, kw_ref, yd_ref, carry_ref, p_ref):
    @pl.when(pl.program_id(1) == 0)
    def _():
        carry_ref[...] = jnp.zeros_like(carry_ref)

    xb = _rms(x_ref[...], gn_ref[...]).astype(BF16)
    for w_ref, o_ref in ((wq_ref, cq_ref), (wkv_ref, ckv_ref), (wkw_ref, kw_ref), (wsc_ref, p_ref)):
        n = w_ref.shape[1]
        for c in range(0, n, PROJ_CHUNK):
            sl = slice(c, min(c + PROJ_CHUNK, n))
            o_ref[:, sl] = jnp.dot(xb, w_ref[:, sl], preferred_element_type=F32)

    h = p_ref[:, :SC_WIDTH]
    gate_b = p_ref[:, SC_WIDTH:2 * SC_WIDTH]
    gate_c = p_ref[:, 2 * SC_WIDTH:]
    u = gate_c * h
    carry = carry_ref[...]
    y = u * cw_ref[2:3, :] + _shift_rows(u, carry, 1) * cw_ref[1:2, :] + _shift_rows(u, carry, 2) * cw_ref[0:1, :]
    carry_ref[...] = u[SC_TILE - 8:, :]
    yd_ref[...] = (gate_b * (y + cb_ref[...])).astype(yd_ref.dtype)


def _od_in(x, gn, w_q, w_kv, w_kw, w_sc, b_, s_, conv_w, conv_b):
    t, d = x.shape
    nt = s_ // SC_TILE
    row = lambda n: pl.BlockSpec((SC_TILE, n), lambda bi, i: (bi * nt + i, 0))
    ws = [w_q, w_kv, w_kw, w_sc]
    return pl.pallas_call(
        _od_in_kernel,
        grid=(b_, nt),
        in_specs=[row(d), _const_spec((1, d))] + [_const_spec(w.shape) for w in ws]
                 + [_const_spec((8, SC_WIDTH)), _const_spec((1, SC_WIDTH))],
        out_specs=[row(w.shape[1]) for w in ws[:3]] + [row(SC_WIDTH)],
        out_shape=[jax.ShapeDtypeStruct((t, w.shape[1]), F32) for w in ws[:3]]
                  + [jax.ShapeDtypeStruct((t, SC_WIDTH), BF16)],
        scratch_shapes=[pltpu.VMEM((8, SC_WIDTH), F32), pltpu.VMEM((SC_TILE, 3 * SC_WIDTH), F32)],
        compiler_params=_params("parallel", "arbitrary"),
        name="od_in_proj",
    )(x, gn.reshape(1, -1), *ws, jnp.pad(conv_w, ((0, 8 - SC_KERNEL), (0, 0))), conv_b.reshape(1, -1))


def _xattn_kernel(y1_ref, y2_ref, w1_ref, w2_ref, gm_ref, x_ref, gq_ref, wq_ref, k_ref, v_ref, wo_ref, go_ref,
                  o_ref, att_ref):
    mix = (jnp.dot(y1_ref[...], w1_ref[...], preferred_element_type=F32)
           + jnp.dot(y2_ref[...], w2_ref[...], preferred_element_type=F32))
    x = x_ref[...] + _rms(mix, gm_ref[...])
    q = jnp.dot(_rms(x, gq_ref[...]).astype(BF16), wq_ref[...], preferred_element_type=F32)
    for h in range(XA_HEADS):
        sl = slice(h * XA_HEAD_DIM, (h + 1) * XA_HEAD_DIM)
        s = _dot_nt(q[:, sl], k_ref[:, sl]) * (XA_HEAD_DIM ** -0.5)
        s = s - jnp.max(s, axis=-1, keepdims=True)
        p = jnp.exp(s)
        p = p / jnp.sum(p, axis=-1, keepdims=True)
        att_ref[:, sl] = _dot(p, v_ref[:, sl])
    hout = jnp.dot(att_ref[...].astype(BF16), wo_ref[...], preferred_element_type=F32)
    o_ref[...] = x + _rms(hout, go_ref[...])


def _xattn(y1, y2, w1, w2, g_mix, x, g_q, wq, k_mem, v_mem, wo, g_o, b_, s_):
    t = x.shape[0]
    nt = s_ // ROW_TILE
    row = lambda n: pl.BlockSpec((ROW_TILE, n), lambda bi, i: (bi * nt + i, 0))
    mem = pl.BlockSpec((MEM_LEN, XA_WIDTH), lambda bi, i: (bi, 0))
    vec = lambda g: g.reshape(1, -1)
    return pl.pallas_call(
        _xattn_kernel,
        grid=(b_, nt),
        in_specs=[row(y1.shape[1]), row(y2.shape[1]), _const_spec(w1.shape), _const_spec(w2.shape),
                  _const_spec((1, D_MODEL)), row(D_MODEL), _const_spec((1, D_MODEL)), _const_spec(wq.shape),
                  mem, mem, _const_spec(wo.shape), _const_spec((1, D_MODEL))],
        out_specs=row(D_MODEL),
        out_shape=jax.ShapeDtypeStruct((t, D_MODEL), F32),
        scratch_shapes=[pltpu.VMEM((ROW_TILE, XA_WIDTH), F32)],
        compiler_params=_params("parallel", "parallel"),
        name="mix_out_xattn",
    )(y1, y2, w1, w2, vec(g_mix), x, vec(g_q), wq, k_mem, v_mem, wo, vec(g_o))


def _block_diag(n_blocks, size, value):
    return np.kron(np.eye(n_blocks, dtype=np.float32), np.full((size, size), value, np.float32))


def _rope_tables(s_):
    half = RT_DIM // 2
    inv_freq = RT_ROPE_BASE ** (-jnp.arange(half, dtype=F32) / half)
    ang = jnp.arange(s_).astype(F32)[:, None] * inv_freq[None, :]
    cos, sin = jnp.cos(ang), jnp.sin(ang)
    return jnp.concatenate([cos, cos], axis=-1), jnp.concatenate([-sin, sin], axis=-1)


def kernel(x, mem, norm_g, mem_norm_g, ffn_w_gate, ffn_w_up, ffn_w_down, xa_wq, xa_wk, xa_wv, xa_wo, ev_w_in, ev_w_out, rw_mu, rw_w0, rw_w2, rw_a0, rw_a2, rw_g2, rw_k_k, rw_k_a, rw_r_k, rw_ln_g, rw_ln_b, rt_gn_g, rt_gn_b, od_w_in, od_w_out, dsa_q_norm_g, dsa_kv_norm_g, dsa_w_uq, dsa_w_uk, dsa_w_uv, dsa_w_qi, sc_conv_w, sc_conv_b):
    b_, s_, d_ = x.shape
    depth = norm_g.shape[0]
    t = b_ * s_
    bf = lambda w: w.astype(BF16)

    ones_blk = jnp.asarray(_block_diag(RW_HEADS, RW_HEAD_DIM, 1.0), BF16)
    ones_pair = jnp.asarray(_block_diag(2, RW_HEAD_DIM, 1.0), BF16)
    tri_rw = jnp.asarray(np.tril(np.ones((RW_CHUNK, RW_CHUNK), np.float32)), BF16)
    tril_dsa = jnp.asarray(np.tril(np.ones((Q_BLOCK, Q_BLOCK), np.float32)), BF16)
    cos2, sin2 = _rope_tables(s_)

    xf = x.reshape(t, d_)
    mem_f = mem.reshape(b_ * MEM_LEN, d_)
    for l in range(depth):
        ng = norm_g[l]
        i = l // 2
        xf = _ffn(xf, ng[0], bf(ffn_w_gate[l, 0]), bf(ffn_w_up[l, 0]), bf(ffn_w_down[l, 0]), ng[1])
        if l % 2 == 0:
            w_in = bf(ev_w_in[i])
            r, k, v, ld, a, b, g, p_rt = _ev_in(xf, ng[2], w_in[:, :RW_IN], w_in[:, RW_IN:], b_, s_, rw_mu[i],
                                                rw_w0[i], rw_w2[i], rw_a0[i], rw_a2[i], rw_g2[i], rw_k_k[i],
                                                rw_k_a[i], ones_blk)
            y_a = _rw_scan(r, k, v, ld, a, b, g, b_, s_, rw_r_k[i], rw_ln_g[i], rw_ln_b[i],
                           ones_pair, tri_rw)
            y_b = _retention(p_rt, b_, s_, cos2, sin2, rt_gn_g[i], rt_gn_b[i])
            ys, w_out, split = (y_a, y_b), bf(ev_w_out[i]), RW_WIDTH
        else:
            w_in = od_w_in[i]
            kw_w = jnp.pad(w_in[:, DSA_Q_RANK + DSA_KV_RANK:DSA_IN], ((0, 0), (0, LANES - IDX_DIM - IDX_HEADS)))
            c_q, c_kv, kw, y_d = _od_in(
                xf, ng[2], bf(w_in[:, :DSA_Q_RANK]), bf(w_in[:, DSA_Q_RANK:DSA_Q_RANK + DSA_KV_RANK]),
                bf(kw_w), bf(w_in[:, DSA_IN:]), b_, s_, sc_conv_w[i], sc_conv_b[i])
            ckv_n, ckv_t, k_idx, w_idx, q_lat, q_idx = _dsa_prep(
                c_q, c_kv, kw, dsa_q_norm_g[i], dsa_kv_norm_g[i],
                bf(dsa_w_uq[i].reshape(DSA_Q_RANK, DSA_WIDTH)), bf(dsa_w_uk[i]),
                bf(dsa_w_qi[i].reshape(DSA_Q_RANK, IDX_HEADS * IDX_DIM)))
            y_c = _dsa(q_idx, w_idx, q_lat, k_idx, ckv_n, ckv_t, bf(dsa_w_uv[i]), tril_dsa, b_, s_)
            ys, w_out, split = (y_c, y_d), bf(od_w_out[i]), DSA_WIDTH
        k_mem, v_mem = _norm_proj(mem_f, mem_norm_g, [bf(xa_wk[l]), bf(xa_wv[l])], "mem_kv_proj")
        xf = _xattn(ys[0], ys[1], w_out[:split], w_out[split:], ng[3], xf, ng[4], bf(xa_wq[l]), k_mem, v_mem,
                    bf(xa_wo[l]), ng[5], b_, s_)
        xf = _ffn(xf, ng[6], bf(ffn_w_gate[l, 1]), bf(ffn_w_up[l, 1]), bf(ffn_w_down[l, 1]), ng[7])
    return xf.reshape(b_, s_, d_)
```

```python
import functools
import math

import numpy as np
import jax
import jax.numpy as jnp
from jax import lax
from jax.experimental import pallas as pl
from jax.experimental.pallas import tpu as pltpu

F32 = jnp.float32
BF16 = jnp.bfloat16

D_MODEL = 1024
D_FF = 2816
EPS = 1e-6
MEM_LEN = 256
RW_HEADS = 8
RW_HEAD_DIM = 64
RW_WIDTH = RW_HEADS * RW_HEAD_DIM
RW_DECAY_RANK = 64
RW_AAA_RANK = 64
RW_GATE_RANK = 128
RW_LN_EPS = 64e-5
RW_IN = 3 * RW_WIDTH + RW_DECAY_RANK + RW_AAA_RANK + RW_GATE_RANK
RW_CHUNK = 64
RW_ROWS = 4
RT_HEADS = 4
RT_DIM = 128
RT_WIDTH = RT_HEADS * RT_DIM
RT_CHUNK = 128
RT_ROWS = 4
RT_ROPE_BASE = 10000.0
RT_IN = 4 * RT_WIDTH
DSA_HEADS = 8
DSA_HEAD_DIM = 64
DSA_WIDTH = DSA_HEADS * DSA_HEAD_DIM
DSA_Q_RANK = 256
DSA_KV_RANK = 128
IDX_HEADS = 8
IDX_DIM = 64
TOPK_MAX = 256
Q_BLOCK = 128
DSA_IN = DSA_Q_RANK + DSA_KV_RANK + IDX_DIM + IDX_HEADS
SC_WIDTH = 512
SC_KERNEL = 3
XA_HEADS = 4
XA_HEAD_DIM = 128
XA_WIDTH = XA_HEADS * XA_HEAD_DIM
XA_SUBTILES = 2

LANES = 128
ROW_TILE = 512
VMEM_LIMIT = 56 * 1024 * 1024
SUM_PASSES = 2
INT_MIN = -2 ** 31
NEG_BIG = -1e30


def _params(*sem):
    return pltpu.CompilerParams(dimension_semantics=sem, vmem_limit_bytes=VMEM_LIMIT)


def _rms(x, g):
    return x * lax.rsqrt(jnp.mean(x * x, axis=-1, keepdims=True) + EPS) * g


def _dot(a, b):
    return jnp.dot(a.astype(BF16), b.astype(BF16), preferred_element_type=F32)


def _dot_nt(a, b):
    return lax.dot_general(a.astype(BF16), b.astype(BF16), (((1,), (1,)), ((), ())),
                           preferred_element_type=F32)


def _dot_tn(a, b):
    return lax.dot_general(a.astype(BF16), b.astype(BF16), (((0,), (0,)), ((), ())),
                           preferred_element_type=F32)


def _split(x, n):
    parts = []
    for _ in range(n):
        part = x.astype(BF16)
        parts.append(part)
        x = x - part.astype(F32)
    return parts


def _dot_exact_rhs(x, w_bf16, n):
    return functools.reduce(jnp.add, [jnp.dot(p, w_bf16, preferred_element_type=F32) for p in _split(x, n)])


def _dot_exact_lhs(w_bf16, x, n):
    return functools.reduce(jnp.add, [jnp.dot(w_bf16, p, preferred_element_type=F32) for p in _split(x, n)])


def _const_spec(shape):
    nd = len(shape)
    return pl.BlockSpec(shape, lambda *_: (0,) * nd, pipeline_mode=pl.Buffered(1))


FF_CHUNK = 256
FF_TILE = 512


def _ffn_kernel(x_ref, gin_ref, wg_ref, wu_ref, wd_ref, gout_ref, o_ref, acc_ref):
    x = x_ref[...]
    xb = _rms(x, gin_ref[...]).astype(BF16)
    for c in range(D_FF // FF_CHUNK):
        sl = slice(c * FF_CHUNK, (c + 1) * FF_CHUNK)
        g = jnp.dot(xb, wg_ref[:, sl], preferred_element_type=F32)
        u = jnp.dot(xb, wu_ref[:, sl], preferred_element_type=F32)
        h = (g * jax.nn.sigmoid(g) * u).astype(BF16)
        part = jnp.dot(h, wd_ref[sl, :], preferred_element_type=F32)
        if c == 0:
            acc_ref[...] = part
        else:
            acc_ref[...] += part
    o_ref[...] = x + 0.5 * _rms(acc_ref[...], gout_ref[...])


def _ffn(x, g_in, wg, wu, wd, g_out):
    t = x.shape[0]
    row = pl.BlockSpec((FF_TILE, D_MODEL), lambda i: (i, 0))
    return pl.pallas_call(
        _ffn_kernel,
        grid=(t // FF_TILE,),
        in_specs=[row, _const_spec((1, D_MODEL)), _const_spec((D_MODEL, D_FF)),
                  _const_spec((D_MODEL, D_FF)), _const_spec((D_FF, D_MODEL)), _const_spec((1, D_MODEL))],
        out_specs=row,
        out_shape=jax.ShapeDtypeStruct((t, D_MODEL), F32),
        scratch_shapes=[pltpu.VMEM((FF_TILE, D_MODEL), F32)],
        compiler_params=_params("parallel"),
        name="ffn_half",
    )(x, g_in.reshape(1, -1), wg, wu, wd, g_out.reshape(1, -1))


PROJ_CHUNK = 512


def _norm_proj_kernel(n_out, x_ref, g_ref, *refs):
    xb = _rms(x_ref[...], g_ref[...]).astype(BF16)
    for w_ref, o_ref in zip(refs[:n_out], refs[n_out:]):
        n = w_ref.shape[1]
        for c in range(0, n, PROJ_CHUNK):
            sl = slice(c, min(c + PROJ_CHUNK, n))
            o_ref[:, sl] = jnp.dot(xb, w_ref[:, sl], preferred_element_type=F32)


def _norm_proj(x, g, ws, name):
    t, d = x.shape
    row = lambda n: pl.BlockSpec((ROW_TILE, n), lambda i: (i, 0))
    return pl.pallas_call(
        functools.partial(_norm_proj_kernel, len(ws)),
        grid=(t // ROW_TILE,),
        in_specs=[row(d), _const_spec((1, d))] + [_const_spec(w.shape) for w in ws],
        out_specs=[row(w.shape[1]) for w in ws],
        out_shape=[jax.ShapeDtypeStruct((t, w.shape[1]), F32) for w in ws],
        compiler_params=_params("parallel"),
        name=name,
    )(x, g.reshape(1, -1), *ws)


RW_TILE = 512
EV_SUBTILES = 2


def _shift_rows(x, carry, n):
    rolled = pltpu.roll(x, n, 0)
    row = lax.broadcasted_iota(jnp.int32, x.shape, 0)
    out = rolled
    for i in range(n):
        out = jnp.where(row == i, carry[8 - n + i:8 - n + i + 1, :], out)
    return out


def _ev_in_kernel(x_ref, gn_ref, wrw_ref, wrt_ref, mu_ref, w0_ref, w2_ref, a0_ref, a2_ref, g2_ref, kk_ref, ka_ref,
                  ones_ref, r_ref, k_ref, v_ref, ld_ref, a_ref, b_ref, g_ref, prt_ref, carry_ref, p_ref):
    @pl.when(pl.program_id(1) == 0)
    def _():
        carry_ref[...] = jnp.zeros_like(carry_ref)

    def project(xb, rows, w_ref, o_ref):
        n = w_ref.shape[1]
        for c in range(0, n, PROJ_CHUNK):
            sl = slice(c, min(c + PROJ_CHUNK, n))
            o_ref[rows, sl] = jnp.dot(xb, w_ref[:, sl], preferred_element_type=F32)

    def rows_program(rows):
        xb = _rms(x_ref[rows, :], gn_ref[...]).astype(BF16)
        project(xb, rows, wrw_ref, p_ref)
        yield
        project(xb, rows, wrt_ref, prt_ref)
        p = p_ref[rows, :]
        before = carry_ref[...] if rows.start == 0 else p_ref[rows.start - 8:rows.start, :]
        prev = _shift_rows(p, before, 1)
        xm = p + (prev - p) * mu_ref[...]
        w = RW_WIDTH
        r, k, v = xm[:, :w], xm[:, w:2 * w], xm[:, 2 * w:3 * w]
        xw = xm[:, 3 * w:3 * w + RW_DECAY_RANK]
        xa = xm[:, 3 * w + RW_DECAY_RANK:3 * w + RW_DECAY_RANK + RW_AAA_RANK]
        xg = xm[:, 3 * w + RW_DECAY_RANK + RW_AAA_RANK:]
        yield
        wlog = -jax.nn.softplus(-(w0_ref[...] + _dot(jnp.tanh(xw), w2_ref[...]))) - 0.5
        a = jax.nn.sigmoid(a0_ref[...] + _dot(xa, a2_ref[...]))
        kk = k * kk_ref[...]
        ss = _dot_exact_rhs(kk * kk, ones_ref[...], SUM_PASSES)
        kk = kk / jnp.maximum(jnp.sqrt(ss), 1e-12)
        yield
        r_ref[rows, :] = r
        k_ref[rows, :] = k * (1.0 + (a - 1.0) * ka_ref[...])
        v_ref[rows, :] = v
        ld_ref[rows, :] = -jnp.exp(wlog)
        a_ref[rows, :] = -kk
        b_ref[rows, :] = kk * a
        g_ref[rows, :] = _dot(jax.nn.sigmoid(xg), g2_ref[...])
        yield

    sub = RW_TILE // EV_SUBTILES
    for _ in zip(*[rows_program(slice(i * sub, (i + 1) * sub)) for i in range(EV_SUBTILES)]):
        pass
    carry_ref[...] = p_ref[RW_TILE - 8:, :]


def _ev_in(x, gn, w_rw, w_rt, b_, s_, mu, w0, w2, a0, a2, g2, k_k, k_a, ones_blk):
    t, d = x.shape
    nt = s_ // RW_TILE
    row = lambda n: pl.BlockSpec((RW_TILE, n), lambda b, i: (b * nt + i, 0))
    vec = lambda a: a.reshape(1, -1)
    outs = [jax.ShapeDtypeStruct((t, RW_WIDTH), F32)] * 7 + [jax.ShapeDtypeStruct((t, RT_IN), F32)]
    return pl.pallas_call(
        _ev_in_kernel,
        grid=(b_, nt),
        in_specs=[row(d), _const_spec((1, d)), _const_spec(w_rw.shape), _const_spec(w_rt.shape),
                  _const_spec((1, RW_IN)), _const_spec((1, RW_WIDTH)),
                  _const_spec(w2.shape), _const_spec((1, RW_WIDTH)), _const_spec(a2.shape),
                  _const_spec(g2.shape), _const_spec((1, RW_WIDTH)), _const_spec((1, RW_WIDTH)),
                  _const_spec(ones_blk.shape)],
        out_specs=[row(RW_WIDTH)] * 7 + [row(RT_IN)],
        out_shape=outs,
        scratch_shapes=[pltpu.VMEM((8, RW_IN), F32), pltpu.VMEM((RW_TILE, RW_IN), F32)],
        compiler_params=_params("parallel", "arbitrary"),
        name="ev_in_proj",
    )(x, vec(gn), w_rw, w_rt, vec(mu), vec(w0), w2, vec(a0), a2, g2, vec(k_k), vec(k_a), ones_blk)


def _rw_scan_kernel(r_ref, k_ref, v_ref, ld_ref, a_ref, b_ref, g_ref, tri_ref, ones_ref,
                    rk_ref, lng_ref, lnb_ref, o_ref, state_ref):
    c_ = RW_CHUNK
    n_ = RW_HEAD_DIM

    @pl.when(pl.program_id(1) == 0)
    def _():
        state_ref[...] = jnp.zeros_like(state_ref)

    c2 = 2 * c_
    row = lax.broadcasted_iota(jnp.int32, (c2, c2), 0)
    col = lax.broadcasted_iota(jnp.int32, (c2, c2), 1)
    same_head = (row >= c_) == (col >= c_)
    strict = jnp.logical_and(same_head, row > col)
    incl = jnp.logical_and(same_head, row >= col)
    left = lax.broadcasted_iota(jnp.int32, (c_, LANES), 1) < n_

    def block_diag(x):
        return jnp.concatenate([jnp.where(left, x, 0.0), jnp.where(left, 0.0, x)], axis=0)

    pairs = range(RW_HEADS // 2)
    sls = [slice(p * LANES, (p + 1) * LANES) for p in pairs]
    ones2 = ones_ref[...]

    def head_sums(xs):
        parts = [part[:, sl] for x in xs for part in _split(x, SUM_PASSES) for sl in sls]
        prod = jnp.dot(jnp.concatenate(parts, axis=0), ones2, preferred_element_type=F32)
        outs = []
        for i in range(len(xs)):
            slabs = []
            for p in pairs:
                rows = [((SUM_PASSES * i + j) * len(sls) + p) * c_ for j in range(SUM_PASSES)]
                slabs.append(functools.reduce(jnp.add, [prod[r:r + c_] for r in rows]))
            outs.append(jnp.concatenate(slabs, axis=1))
        return outs

    def chunk_program(bb):
        ld = ld_ref[bb]
        r = r_ref[bb]
        k = k_ref[bb]
        v = v_ref[bb]
        cum = _dot_exact_lhs(tri_ref[...], ld, 3)
        mid = cum[c_ // 2 - 1:c_ // 2, :]
        e_in = jnp.exp(cum - mid)
        e_out = jnp.exp(mid - cum)
        r_t = r * e_in
        a_t = a_ref[bb] * jnp.exp(cum - ld - mid)
        b_t = b_ref[bb] * e_out
        k_t = k * e_out
        e_mid = jnp.exp(mid)
        w_all = jnp.exp(cum[c_ - 1:c_, :])
        w_tail = jnp.exp(cum[c_ - 1:c_, :] - mid)
        yield
        s0 = [state_ref[bb, p] for p in pairs]
        ar = [jnp.concatenate([block_diag(a_t[:, sl]), block_diag(r_t[:, sl])], axis=0) for sl in sls]
        bk = [jnp.concatenate([block_diag(b_t[:, sl]), block_diag(k_t[:, sl])], axis=0) for sl in sls]
        vb = [block_diag(v[:, sl]) for sl in sls]
        m1 = [_dot_nt(ar[p], bk[p]) for p in pairs]
        m2 = [_dot_nt(ar[p] * e_mid[:, sls[p]], s0[p]) for p in pairs]
        yield
        l_ab = [jnp.where(strict, m[:c2, :c2], 0.0) for m in m1]
        l_ak = [jnp.where(strict, m[:c2, c2:], 0.0) for m in m1]
        l_r = [jnp.concatenate([jnp.where(incl, m[c2:, :c2], 0.0), jnp.where(incl, m[c2:, c2:], 0.0)], axis=1)
               for m in m1]
        u = [m2[p][:c2] + _dot(l_ak[p], vb[p]) for p in pairs]
        pw = l_ab
        yield
        n_steps = int(math.log2(c_))
        for step in range(n_steps):
            if step < n_steps - 1:
                prod = [_dot(pw[p], jnp.concatenate([pw[p], u[p]], axis=1)) for p in pairs]
                pw = [q[:, :c2] for q in prod]
                u = [u[p] + prod[p][:, c2:] for p in pairs]
            else:
                u = [u[p] + _dot(pw[p], u[p]) for p in pairs]
            yield
        uv = [jnp.concatenate([u[p], vb[p]], axis=0) for p in pairs]
        ys = [m2[p][c2:] + _dot(l_r[p], uv[p]) for p in pairs]
        upd = [_dot_tn(uv[p], bk[p]) for p in pairs]
        yield
        for p in pairs:
            state_ref[bb, p] = s0[p] * w_all[:, sls[p]] + upd[p] * w_tail[:, sls[p]]
        y = jnp.concatenate([ys[p][:c_] + ys[p][c_:] for p in pairs], axis=1)
        sum_y, sum_rk = head_sums([y, r * k * rk_ref[...]])
        d = y - sum_y * (1.0 / n_)
        yield
        var = head_sums([d * d])[0] * (1.0 / n_)
        yn = d * lax.rsqrt(var + RW_LN_EPS) * lng_ref[...] + lnb_ref[...]
        o_ref[bb] = ((yn + sum_rk * v) * g_ref[bb]).astype(o_ref.dtype)
        yield

    for _ in zip(*[chunk_program(bb) for bb in range(RW_ROWS)]):
        pass


def _rw_scan(r, k, v, ld, a, b, g, b_, s_, r_k, ln_g, ln_b, ones_pair, tri):
    t = r.shape[0]
    nc = s_ // RW_CHUNK
    row = pl.BlockSpec((RW_ROWS, RW_CHUNK, RW_WIDTH), lambda bi, i: (bi, i, 0))
    vec = lambda x: x.reshape(1, -1)
    seq = lambda x: x.reshape(b_, s_, RW_WIDTH)
    out = pl.pallas_call(
        _rw_scan_kernel,
        grid=(b_ // RW_ROWS, nc),
        in_specs=[row] * 7 + [_const_spec(tri.shape), _const_spec(ones_pair.shape),
                              _const_spec((1, RW_WIDTH)), _const_spec((1, RW_WIDTH)), _const_spec((1, RW_WIDTH))],
        out_specs=row,
        out_shape=jax.ShapeDtypeStruct((b_, s_, RW_WIDTH), BF16),
        scratch_shapes=[pltpu.VMEM((RW_ROWS, RW_HEADS // 2, 2 * RW_HEAD_DIM, 2 * RW_HEAD_DIM), F32)],
        compiler_params=_params("parallel", "arbitrary"),
        name="rwkv_scan",
    )(seq(r), seq(k), seq(v), seq(ld), seq(a), seq(b), seq(g), tri, ones_pair, vec(r_k), vec(ln_g), vec(ln_b))
    return out.reshape(t, RW_WIDTH)


def _rt_log_gamma(h):
    return math.log1p(-(2.0 ** (-5.0 - h)))


def _retention_kernel(p_ref, cos_ref, sin_ref, gng_ref, gnb_ref, o_ref, state_ref):
    c_ = RT_CHUNK

    @pl.when(pl.program_id(1) == 0)
    def _():
        state_ref[...] = jnp.zeros_like(state_ref)

    cos = cos_ref[...]
    sin = sin_ref[...]
    row = lax.broadcasted_iota(jnp.int32, (c_, c_), 0).astype(F32)
    col = lax.broadcasted_iota(jnp.int32, (c_, c_), 1).astype(F32)
    rel = row - col
    heads = range(RT_HEADS)
    lgs = [_rt_log_gamma(h) for h in heads]
    sls = [slice(h * RT_DIM, (h + 1) * RT_DIM) for h in heads]
    decay = [jnp.where(rel >= 0, jnp.exp(jnp.maximum(rel, 0.0) * lg), 0.0) for lg in lgs]
    xi = [jnp.exp((row + 1.0) * lg) for lg in lgs]
    zeta = [jnp.exp((c_ - 1.0 - row) * lg) for lg in lgs]

    def chunk_program(bb):
        col_of = lambda part, h: slice(part * RT_WIDTH + h * RT_DIM, part * RT_WIDTH + (h + 1) * RT_DIM)
        q = [p_ref[bb, :, col_of(0, h)] for h in heads]
        k = [p_ref[bb, :, col_of(1, h)] for h in heads]
        v = [p_ref[bb, :, col_of(2, h)] for h in heads]
        q = [x * cos + pltpu.roll(x, RT_DIM // 2, 1) * sin for x in q]
        k = [(x * cos + pltpu.roll(x, RT_DIM // 2, 1) * sin) * (RT_DIM ** -0.5) for x in k]
        s0 = [state_ref[bb, h] for h in heads]
        yield
        scores = [_dot_nt(q[h], k[h]) * decay[h] for h in heads]
        cross = [_dot(q[h], s0[h]) * xi[h] for h in heads]
        for h in heads:
            state_ref[bb, h] = s0[h] * math.exp(c_ * lgs[h]) + _dot_tn(k[h] * zeta[h], v[h])
        yield
        o = [_dot(scores[h], v[h]) + cross[h] for h in heads]
        yield
        for h in heads:
            mu = jnp.mean(o[h], axis=-1, keepdims=True)
            d = o[h] - mu
            var = jnp.mean(d * d, axis=-1, keepdims=True)
            on = d * lax.rsqrt(var + EPS) * gng_ref[:, sls[h]] + gnb_ref[:, sls[h]]
            gate = p_ref[bb, :, col_of(3, h)]
            o_ref[bb, :, sls[h]] = (gate * jax.nn.sigmoid(gate) * on).astype(o_ref.dtype)
        yield

    for _ in zip(*[chunk_program(bb) for bb in range(RT_ROWS)]):
        pass


def _retention(p_rt, b_, s_, cos2, sin2, gn_g, gn_b):
    t = p_rt.shape[0]
    nc = s_ // RT_CHUNK
    out = pl.pallas_call(
        _retention_kernel,
        grid=(b_ // RT_ROWS, nc),
        in_specs=[pl.BlockSpec((RT_ROWS, RT_CHUNK, RT_IN), lambda bi, i: (bi, i, 0)),
                  pl.BlockSpec((RT_CHUNK, RT_DIM), lambda bi, i: (i, 0)),
                  pl.BlockSpec((RT_CHUNK, RT_DIM), lambda bi, i: (i, 0)),
                  _const_spec((1, RT_WIDTH)), _const_spec((1, RT_WIDTH))],
        out_specs=pl.BlockSpec((RT_ROWS, RT_CHUNK, RT_WIDTH), lambda bi, i: (bi, i, 0)),
        out_shape=jax.ShapeDtypeStruct((b_, s_, RT_WIDTH), BF16),
        scratch_shapes=[pltpu.VMEM((RT_ROWS, RT_HEADS, RT_DIM, RT_DIM), F32)],
        compiler_params=_params("parallel", "arbitrary"),
        name="retention",
    )(p_rt.reshape(b_, s_, RT_IN), cos2, sin2, gn_g.reshape(1, -1), gn_b.reshape(1, -1))
    return out.reshape(t, RT_WIDTH)


DSA_BLOCKS_PER_TILE = ROW_TILE // Q_BLOCK


def _dsa_prep_kernel(cq_ref, ckv_ref, kw_ref, qg_ref, kvg_ref, wuq_ref, wuk_ref, wqi_ref,
                     ckvn_ref, ckvt_ref, kidx_ref, widx_ref, qlat_ref, qidx_ref):
    cq = _rms(cq_ref[...], qg_ref[...]).astype(BF16)
    ckvn = _rms(ckv_ref[...], kvg_ref[...])
    ckvn_ref[...] = ckvn.astype(BF16)
    kw = kw_ref[...]
    kidx_ref[...] = kw[:, :IDX_DIM].astype(BF16)
    q = jnp.dot(cq, wuq_ref[...], preferred_element_type=F32)
    qi = jnp.dot(cq, wqi_ref[...], preferred_element_type=F32)
    for blk in range(DSA_BLOCKS_PER_TILE):
        rows = slice(blk * Q_BLOCK, (blk + 1) * Q_BLOCK)
        ckvt_ref[blk] = ckvn[rows].T.astype(BF16)
        widx_ref[blk] = kw[rows].T[IDX_DIM:IDX_DIM + IDX_HEADS] * ((IDX_HEADS * IDX_DIM) ** -0.5)
    for h in range(DSA_HEADS):
        qh = q[:, h * DSA_HEAD_DIM:(h + 1) * DSA_HEAD_DIM]
        ql = (_dot(qh, wuk_ref[h]) * (DSA_HEAD_DIM ** -0.5)).astype(BF16)
        qih = qi[:, h * IDX_DIM:(h + 1) * IDX_DIM].astype(BF16)
        for blk in range(DSA_BLOCKS_PER_TILE):
            rows = slice(blk * Q_BLOCK, (blk + 1) * Q_BLOCK)
            qlat_ref[blk, h] = ql[rows]
            qidx_ref[blk, h] = qih[rows]


def _dsa_prep(c_q, c_kv, kw, q_g, kv_g, w_uq, w_uk, w_qi):
    t = c_q.shape[0]
    row = lambda n: pl.BlockSpec((ROW_TILE, n), lambda i: (i, 0))
    blk3 = lambda m, n: pl.BlockSpec((DSA_BLOCKS_PER_TILE, m, n), lambda i: (i, 0, 0))
    blk4 = lambda n: pl.BlockSpec((DSA_BLOCKS_PER_TILE, DSA_HEADS, Q_BLOCK, n), lambda i: (i, 0, 0, 0))
    return pl.pallas_call(
        _dsa_prep_kernel,
        grid=(t // ROW_TILE,),
        in_specs=[row(DSA_Q_RANK), row(DSA_KV_RANK), row(LANES), _const_spec((1, DSA_Q_RANK)),
                  _const_spec((1, DSA_KV_RANK)),
                  _const_spec(w_uq.shape), _const_spec(w_uk.shape), _const_spec(w_qi.shape)],
        out_specs=[row(DSA_KV_RANK), blk3(DSA_KV_RANK, Q_BLOCK), row(IDX_DIM), blk3(IDX_HEADS, Q_BLOCK),
                   blk4(DSA_KV_RANK), blk4(IDX_DIM)],
        out_shape=[jax.ShapeDtypeStruct((t, DSA_KV_RANK), BF16),
                   jax.ShapeDtypeStruct((t // Q_BLOCK, DSA_KV_RANK, Q_BLOCK), BF16),
                   jax.ShapeDtypeStruct((t, IDX_DIM), BF16),
                   jax.ShapeDtypeStruct((t // Q_BLOCK, IDX_HEADS, Q_BLOCK), F32),
                   jax.ShapeDtypeStruct((t // Q_BLOCK, DSA_HEADS, Q_BLOCK, DSA_KV_RANK), BF16),
                   jax.ShapeDtypeStruct((t // Q_BLOCK, IDX_HEADS, Q_BLOCK, IDX_DIM), BF16)],
        compiler_params=_params("parallel"),
        name="dsa_prep",
    )(c_q, c_kv, kw, q_g.reshape(1, -1), kv_g.reshape(1, -1), w_uq, w_uk, w_qi)


def _fold_rows(x, op):
    return functools.reduce(op, [x[i:i + 8] for i in range(0, x.shape[0], 8)])


def _bit_transpose32(words):
    a = list(words)
    j, m = 16, 0x0000FFFF
    while j:
        k = 0
        while k < 32:
            t = (a[k] ^ lax.shift_right_logical(a[k + j], jnp.int32(j))) & jnp.int32(m)
            a[k] = a[k] ^ t
            a[k + j] = a[k + j] ^ lax.shift_left(t, jnp.int32(j))
            k = (k + j + 1) & ~j
        j >>= 1
        m ^= m << j
    return a


def _dsa_kernel(top_k, qidx_ref, widx_ref, qlat_ref, kidx_ref, ckv_ref, ckvt_ref, wuv_ref, tril_ref, o_ref,
                key_ref, planes_ref, s_ref, acc_ref):
    qb = Q_BLOCK
    nh = DSA_HEADS
    j = pl.program_id(1)
    n_chunks = j + 1

    keyi = lax.broadcasted_iota(jnp.int32, (qb, qb), 0)
    qryi = lax.broadcasted_iota(jnp.int32, (qb, qb), 1)
    lanes = lambda h: slice(h * qb, (h + 1) * qb)

    def key_chunk(ref, c):
        return ref[pl.ds(pl.multiple_of(c * qb, qb), qb), :]

    w_idx = widx_ref[0]
    q_idx = qidx_ref[0].reshape(IDX_HEADS * qb, IDX_DIM)
    q_lat = qlat_ref[0].reshape(nh * qb, DSA_KV_RANK)

    n_pairs = (n_chunks + 1) // 2

    def score_chunk(c):
        logits = lax.dot_general(key_chunk(kidx_ref, c), q_idx, (((1,), (1,)), ((), ())),
                                 preferred_element_type=F32)
        score = jnp.zeros((qb, qb), F32)
        for h in range(IDX_HEADS):
            score = score + jnp.maximum(logits[:, lanes(h)], 0.0) * w_idx[h:h + 1, :]
        bits = lax.bitcast_convert_type(score, jnp.int32)
        key = jnp.where(bits < 0, bits ^ jnp.int32(0x7FFFFFFF), bits)
        causal = keyi + c * qb <= qryi + j * qb
        key = jnp.where(causal, key, jnp.int32(INT_MIN))
        key_ref[c] = key
        return key

    def score_body(pi, carry):
        keys = [score_chunk(2 * pi), score_chunk(2 * pi + 1)]
        words = [k[r:r + 8] ^ jnp.int32(INT_MIN) for k in keys for r in range(0, qb, 8)]
        for idx, plane in enumerate(_bit_transpose32(words)):
            planes_ref[31 - idx, pi] = plane
        return carry

    lax.fori_loop(0, n_pairs, score_body, 0)

    n_pairs_max = planes_ref.shape[1]
    live = tuple(jnp.where(p < n_pairs, jnp.full((8, qb), -1, jnp.int32), jnp.zeros((8, qb), jnp.int32))
                 for p in range(n_pairs_max))

    def bit_body(i, carry):
        thr_u, above, eq = carry
        plane = planes_ref[31 - i]
        hits = [eq[p] & plane[p] for p in range(n_pairs_max)]
        cnt = functools.reduce(jnp.add, [lax.population_count(h) for h in hits])
        cnt = above + jnp.sum(cnt, axis=0, keepdims=True)
        accept = cnt >= top_k
        thr_u = jnp.where(accept, thr_u | lax.shift_left(jnp.int32(1), 31 - i), thr_u)
        above = jnp.where(accept, above, cnt)
        eq = tuple(jnp.where(accept, h, e ^ h) for e, h in zip(eq, hits))
        return thr_u, above, eq

    zero_row = jnp.zeros((1, qb), jnp.int32)
    thr_u, above, _ = lax.fori_loop(0, 32, bit_body, (zero_row, zero_row, live))
    thr = thr_u ^ jnp.int32(INT_MIN)
    need = (top_k - above).astype(F32)

    def pair_logits(t, taken):
        tops = [None] * nh
        for c in (2 * t, 2 * t + 1):
            key = key_ref[c]
            causal = keyi + c * qb <= qryi + j * qb
            eq = jnp.logical_and(key == thr, causal)
            eq_f = jnp.where(eq, 1.0, 0.0)
            rank = taken + jnp.dot(tril_ref[...], eq_f.astype(BF16), preferred_element_type=F32)
            sel = jnp.logical_or(jnp.logical_and(key > thr, causal), jnp.logical_and(eq, rank <= need))
            s = lax.dot_general(key_chunk(ckv_ref, c), q_lat, (((1,), (1,)), ((), ())),
                                preferred_element_type=F32)
            for h in range(nh):
                sh = jnp.where(sel, s[:, lanes(h)], NEG_BIG)
                s_ref[c, :, lanes(h)] = sh
                top = _fold_rows(sh, jnp.maximum)
                tops[h] = top if tops[h] is None else jnp.maximum(tops[h], top)
            taken = taken + jnp.sum(eq_f, axis=0, keepdims=True)
        return taken, tuple(jnp.max(top, axis=0, keepdims=True) for top in tops)

    def pair_values(t, tops, peak, sums):
        ps, scales, new_peak, new_sums = [], [], [], []
        for h in range(nh):
            pk = jnp.maximum(peak[h], tops[h])
            scale = jnp.exp(peak[h] - pk)
            pa = jnp.exp(s_ref[2 * t, :, lanes(h)] - pk)
            pb = jnp.exp(s_ref[2 * t + 1, :, lanes(h)] - pk)
            new_sums.append(sums[h] * scale + _fold_rows(pa + pb, jnp.add))
            ps.append(jnp.concatenate([pa.astype(BF16), pb.astype(BF16)], axis=0))
            scales.append(scale)
            new_peak.append(pk)
        ckvt_pair = jnp.concatenate([ckvt_ref[2 * t], ckvt_ref[2 * t + 1]], axis=1)
        upd = jnp.dot(ckvt_pair, jnp.concatenate(ps, axis=1), preferred_element_type=F32)
        acc_ref[...] = acc_ref[...] * jnp.concatenate(scales, axis=1) + upd
        return tuple(new_peak), tuple(new_sums)

    acc_ref[...] = jnp.zeros_like(acc_ref)

    def attn_body(t, carry):
        taken, tops, peak, sums = carry
        peak, sums = pair_values(t, tops, peak, sums)
        taken, tops = pair_logits(jnp.minimum(t + 1, n_pairs - 1), taken)
        return taken, tops, peak, sums

    taken, tops = pair_logits(0, jnp.zeros((1, qb), F32))
    init = (taken, tops, tuple(jnp.full((1, qb), NEG_BIG, F32) for _ in range(nh)),
            tuple(jnp.zeros((8, qb), F32) for _ in range(nh)))
    _, _, _, sums = lax.fori_loop(0, n_pairs, attn_body, init)

    outs = []
    for h in range(nh):
        o_lat_t = acc_ref[:, lanes(h)] / jnp.sum(sums[h], axis=0, keepdims=True)
        outs.append(_dot_tn(o_lat_t, wuv_ref[h]))
    o_ref[...] = jnp.concatenate(outs, axis=1).astype(o_ref.dtype)


def _dsa(q_idx, w_idx, q_lat, k_idx, ckv_n, ckv_t, w_uv, tril, b_, s_):
    nb = s_ // Q_BLOCK
    t = b_ * s_
    top_k = min(TOPK_MAX, s_ // 4)
    blk = lambda n: pl.BlockSpec((Q_BLOCK, n), lambda bi, i: (bi * nb + i, 0))
    blk3 = lambda m, n: pl.BlockSpec((1, m, n), lambda bi, i: (bi * nb + i, 0, 0))
    blk4 = lambda n: pl.BlockSpec((1, DSA_HEADS, Q_BLOCK, n), lambda bi, i: (bi * nb + i, 0, 0, 0))
    seq = lambda n: pl.BlockSpec((s_, n), lambda bi, i: (bi, 0))
    stacked = DSA_HEADS * Q_BLOCK
    return pl.pallas_call(
        functools.partial(_dsa_kernel, top_k),
        grid=(b_, nb),
        in_specs=[blk4(IDX_DIM), blk3(IDX_HEADS, Q_BLOCK), blk4(DSA_KV_RANK),
                  seq(IDX_DIM), seq(DSA_KV_RANK),
                  pl.BlockSpec((nb, DSA_KV_RANK, Q_BLOCK), lambda bi, i: (bi, 0, 0)),
                  _const_spec(w_uv.shape), _const_spec(tril.shape)],
        out_specs=blk(DSA_WIDTH),
        out_shape=jax.ShapeDtypeStruct((t, DSA_WIDTH), BF16),
        scratch_shapes=[pltpu.VMEM((nb, Q_BLOCK, Q_BLOCK), jnp.int32),
                        pltpu.VMEM((32, nb // 2, 8, Q_BLOCK), jnp.int32),
                        pltpu.VMEM((nb, Q_BLOCK, stacked), F32),
                        pltpu.VMEM((DSA_KV_RANK, stacked), F32)],
        compiler_params=_params("parallel", "arbitrary"),
        name="dsa_attn",
    )(q_idx, w_idx, q_lat, k_idx, ckv_n, ckv_t, w_uv, tril)


SC_TILE = 512


def _od_in_kernel(x_ref, gn_ref, wq_ref, wkv_ref, wkw_ref, wsc_ref, cw_ref, cb_ref,
                  cq_ref, ckv_ref, kw_ref, yd_ref, carry_ref, p_ref):
    @pl.when(pl.program_id(1) == 0)
    def _():
        carry_ref[...] = jnp.zeros_like(carry_ref)

    xb = _rms(x_ref[...], gn_ref[...]).astype(BF16)
    for w_ref, o_ref in ((wq_ref, cq_ref), (wkv_ref, ckv_ref), (wkw_ref, kw_ref), (wsc_ref, p_ref)):
        n = w_ref.shape[1]
        for c in range(0, n, PROJ_CHUNK):
            sl = slice(c, min(c + PROJ_CHUNK, n))
            o_ref[:, sl] = jnp.dot(xb, w_ref[:, sl], preferred_element_type=F32)

    h = p_ref[:, :SC_WIDTH]
    gate_b = p_ref[:, SC_WIDTH:2 * SC_WIDTH]
    gate_c = p_ref[:, 2 * SC_WIDTH:]
    u = gate_c * h
    carry = carry_ref[...]
    y = u * cw_ref[2:3, :] + _shift_rows(u, carry, 1) * cw_ref[1:2, :] + _shift_rows(u, carry, 2) * cw_ref[0:1, :]
    carry_ref[...] = u[SC_TILE - 8:, :]
    yd_ref[...] = (gate_b * (y + cb_ref[...])).astype(yd_ref.dtype)


def _od_in(x, gn, w_q, w_kv, w_kw, w_sc, b_, s_, conv_w, conv_b):
    t, d = x.shape
    nt = s_ // SC_TILE
    row = lambda n: pl.BlockSpec((SC_TILE, n), lambda bi, i: (bi * nt + i, 0))
    ws = [w_q, w_kv, w_kw, w_sc]
    return pl.pallas_call(
        _od_in_kernel,
        grid=(b_, nt),
        in_specs=[row(d), _const_spec((1, d))] + [_const_spec(w.shape) for w in ws]
                 + [_const_spec((8, SC_WIDTH)), _const_spec((1, SC_WIDTH))],
        out_specs=[row(w.shape[1]) for w in ws[:3]] + [row(SC_WIDTH)],
        out_shape=[jax.ShapeDtypeStruct((t, w.shape[1]), F32) for w in ws[:3]]
                  + [jax.ShapeDtypeStruct((t, SC_WIDTH), BF16)],
        scratch_shapes=[pltpu.VMEM((8, SC_WIDTH), F32), pltpu.VMEM((SC_TILE, 3 * SC_WIDTH), F32)],
        compiler_params=_params("parallel", "arbitrary"),
        name="od_in_proj",
    )(x, gn.reshape(1, -1), *ws, jnp.pad(conv_w, ((0, 8 - SC_KERNEL), (0, 0))), conv_b.reshape(1, -1))


def _xattn_kernel(y1_ref, y2_ref, w1_ref, w2_ref, gm_ref, x_ref, gq_ref, wq_ref, k_ref, v_ref, wo_ref, go_ref,
                  o_ref, att_ref):
    def rows_program(rows):
        mix = (jnp.dot(y1_ref[rows, :], w1_ref[...], preferred_element_type=F32)
               + jnp.dot(y2_ref[rows, :], w2_ref[...], preferred_element_type=F32))
        yield
        x = x_ref[rows, :] + _rms(mix, gm_ref[...])
        q = jnp.dot(_rms(x, gq_ref[...]).astype(BF16), wq_ref[...], preferred_element_type=F32)
        yield
        for h in range(XA_HEADS):
            sl = slice(h * XA_HEAD_DIM, (h + 1) * XA_HEAD_DIM)
            s = _dot_nt(q[:, sl], k_ref[:, sl]) * (XA_HEAD_DIM ** -0.5)
            s = s - jnp.max(s, axis=-1, keepdims=True)
            p = jnp.exp(s)
            p = p / jnp.sum(p, axis=-1, keepdims=True)
            att_ref[rows, sl] = _dot(p, v_ref[:, sl])
            yield
        hout = jnp.dot(att_ref[rows, :].astype(BF16), wo_ref[...], preferred_element_type=F32)
        yield
        o_ref[rows, :] = x + _rms(hout, go_ref[...])
        yield

    sub = ROW_TILE // XA_SUBTILES
    for _ in zip(*[rows_program(slice(i * sub, (i + 1) * sub)) for i in range(XA_SUBTILES)]):
        pass


def _xattn(y1, y2, w1, w2, g_mix, x, g_q, wq, k_mem, v_mem, wo, g_o, b_, s_):
    t = x.shape[0]
    nt = s_ // ROW_TILE
    row = lambda n: pl.BlockSpec((ROW_TILE, n), lambda bi, i: (bi * nt + i, 0))
    mem = pl.BlockSpec((MEM_LEN, XA_WIDTH), lambda bi, i: (bi, 0))
    vec = lambda g: g.reshape(1, -1)
    return pl.pallas_call(
        _xattn_kernel,
        grid=(b_, nt),
        in_specs=[row(y1.shape[1]), row(y2.shape[1]), _const_spec(w1.shape), _const_spec(w2.shape),
                  _const_spec((1, D_MODEL)), row(D_MODEL), _const_spec((1, D_MODEL)), _const_spec(wq.shape),
                  mem, mem, _const_spec(wo.shape), _const_spec((1, D_MODEL))],
        out_specs=row(D_MODEL),
        out_shape=jax.ShapeDtypeStruct((t, D_MODEL), F32),
        scratch_shapes=[pltpu.VMEM((ROW_TILE, XA_WIDTH), F32)],
        compiler_params=_params("parallel", "parallel"),
        name="mix_out_xattn",
    )(y1, y2, w1, w2, vec(g_mix), x, vec(g_q), wq, k_mem, v_mem, wo, vec(g_o))


def _block_diag(n_blocks, size, value):
    return np.kron(np.eye(n_blocks, dtype=np.float32), np.full((size, size), value, np.float32))


def _rope_tables(s_):
    half = RT_DIM // 2
    inv_freq = RT_ROPE_BASE ** (-jnp.arange(half, dtype=F32) / half)
    ang = jnp.arange(s_).astype(F32)[:, None] * inv_freq[None, :]
    cos, sin = jnp.cos(ang), jnp.sin(ang)
    return jnp.concatenate([cos, cos], axis=-1), jnp.concatenate([-sin, sin], axis=-1)


def kernel(x, mem, norm_g, mem_norm_g, ffn_w_gate, ffn_w_up, ffn_w_down, xa_wq, xa_wk, xa_wv, xa_wo, ev_w_in, ev_w_out, rw_mu, rw_w0, rw_w2, rw_a0, rw_a2, rw_g2, rw_k_k, rw_k_a, rw_r_k, rw_ln_g, rw_ln_b, rt_gn_g, rt_gn_b, od_w_in, od_w_out, dsa_q_norm_g, dsa_kv_norm_g, dsa_w_uq, dsa_w_uk, dsa_w_uv, dsa_w_qi, sc_conv_w, sc_conv_b):
    b_, s_, d_ = x.shape
    depth = norm_g.shape[0]
    t = b_ * s_
    bf = lambda w: w.astype(BF16)

    ones_blk = jnp.asarray(_block_diag(RW_HEADS, RW_HEAD_DIM, 1.0), BF16)
    ones_pair = jnp.asarray(_block_diag(2, RW_HEAD_DIM, 1.0), BF16)
    tri_rw = jnp.asarray(np.tril(np.ones((RW_CHUNK, RW_CHUNK), np.float32)), BF16)
    tril_dsa = jnp.asarray(np.tril(np.ones((Q_BLOCK, Q_BLOCK), np.float32)), BF16)
    cos2, sin2 = _rope_tables(s_)

    xf = x.reshape(t, d_)
    mem_f = mem.reshape(b_ * MEM_LEN, d_)
    for l in range(depth):
        ng = norm_g[l]
        i = l // 2
        xf = _ffn(xf, ng[0], bf(ffn_w_gate[l, 0]), bf(ffn_w_up[l, 0]), bf(ffn_w_down[l, 0]), ng[1])
        if l % 2 == 0:
            w_in = bf(ev_w_in[i])
            r, k, v, ld, a, b, g, p_rt = _ev_in(xf, ng[2], w_in[:, :RW_IN], w_in[:, RW_IN:], b_, s_, rw_mu[i],
                                                rw_w0[i], rw_w2[i], rw_a0[i], rw_a2[i], rw_g2[i], rw_k_k[i],
                                                rw_k_a[i], ones_blk)
            y_a = _rw_scan(r, k, v, ld, a, b, g, b_, s_, rw_r_k[i], rw_ln_g[i], rw_ln_b[i],
                           ones_pair, tri_rw)
            y_b = _retention(p_rt, b_, s_, cos2, sin2, rt_gn_g[i], rt_gn_b[i])
            ys, w_out, split = (y_a, y_b), bf(ev_w_out[i]), RW_WIDTH
        else:
            w_in = od_w_in[i]
            kw_w = jnp.pad(w_in[:, DSA_Q_RANK + DSA_KV_RANK:DSA_IN], ((0, 0), (0, LANES - IDX_DIM - IDX_HEADS)))
            c_q, c_kv, kw, y_d = _od_in(
                xf, ng[2], bf(w_in[:, :DSA_Q_RANK]), bf(w_in[:, DSA_Q_RANK:DSA_Q_RANK + DSA_KV_RANK]),
                bf(kw_w), bf(w_in[:, DSA_IN:]), b_, s_, sc_conv_w[i], sc_conv_b[i])
            ckv_n, ckv_t, k_idx, w_idx, q_lat, q_idx = _dsa_prep(
                c_q, c_kv, kw, dsa_q_norm_g[i], dsa_kv_norm_g[i],
                bf(dsa_w_uq[i].reshape(DSA_Q_RANK, DSA_WIDTH)), bf(dsa_w_uk[i]),
                bf(dsa_w_qi[i].reshape(DSA_Q_RANK, IDX_HEADS * IDX_DIM)))
            y_c = _dsa(q_idx, w_idx, q_lat, k_idx, ckv_n, ckv_t, bf(dsa_w_uv[i]), tril_dsa, b_, s_)
            ys, w_out, split = (y_c, y_d), bf(od_w_out[i]), DSA_WIDTH
        k_mem, v_mem = _norm_proj(mem_f, mem_norm_g, [bf(xa_wk[l]), bf(xa_wv[l])], "mem_kv_proj")
        xf = _xattn(ys[0], ys[1], w_out[:split], w_out[split:], ng[3], xf, ng[4], bf(xa_wq[l]), k_mem, v_mem,
                    bf(xa_wo[l]), ng[5], b_, s_)
        xf = _ffn(xf, ng[6], bf(ffn_w_gate[l, 1]), bf(ffn_w_up[l, 1]), bf(ffn_w_down[l, 1]), ng[7])
    return xf.reshape(b_, s_, d_)
```

```python
import functools
import math

import numpy as np
import jax
import jax.numpy as jnp
from jax import lax
from jax.experimental import pallas as pl
from jax.experimental.pallas import tpu as pltpu

F32 = jnp.float32
BF16 = jnp.bfloat16

D_MODEL = 1024
D_FF = 2816
EPS = 1e-6
MEM_LEN = 256
RW_HEADS = 8
RW_HEAD_DIM = 64
RW_WIDTH = RW_HEADS * RW_HEAD_DIM
RW_DECAY_RANK = 64
RW_AAA_RANK = 64
RW_GATE_RANK = 128
RW_LN_EPS = 64e-5
RW_IN = 3 * RW_WIDTH + RW_DECAY_RANK + RW_AAA_RANK + RW_GATE_RANK
RW_CHUNK = 64
RW_ROWS = 4
RT_HEADS = 4
RT_DIM = 128
RT_WIDTH = RT_HEADS * RT_DIM
RT_CHUNK = 128
RT_ROWS = 4
RT_ROPE_BASE = 10000.0
RT_IN = 4 * RT_WIDTH
DSA_HEADS = 8
DSA_HEAD_DIM = 64
DSA_WIDTH = DSA_HEADS * DSA_HEAD_DIM
DSA_Q_RANK = 256
DSA_KV_RANK = 128
IDX_HEADS = 8
IDX_DIM = 64
TOPK_MAX = 256
Q_BLOCK = 128
DSA_IN = DSA_Q_RANK + DSA_KV_RANK + IDX_DIM + IDX_HEADS
SC_WIDTH = 512
SC_KERNEL = 3
XA_HEADS = 4
XA_HEAD_DIM = 128
XA_WIDTH = XA_HEADS * XA_HEAD_DIM
XA_SUBTILES = 2

LANES = 128
ROW_TILE = 512
VMEM_LIMIT = 56 * 1024 * 1024
SUM_PASSES = 2
INT_MIN = -2 ** 31
NEG_BIG = -1e30


def _params(*sem):
    return pltpu.CompilerParams(dimension_semantics=sem, vmem_limit_bytes=VMEM_LIMIT)


def _rms(x, g):
    return x * lax.rsqrt(jnp.mean(x * x, axis=-1, keepdims=True) + EPS) * g


def _dot(a, b):
    return jnp.dot(a.astype(BF16), b.astype(BF16), preferred_element_type=F32)


def _dot_nt(a, b):
    return lax.dot_general(a.astype(BF16), b.astype(BF16), (((1,), (1,)), ((), ())),
                           preferred_element_type=F32)


def _dot_tn(a, b):
    return lax.dot_general(a.astype(BF16), b.astype(BF16), (((0,), (0,)), ((), ())),
                           preferred_element_type=F32)


def _split(x, n):
    parts = []
    for _ in range(n):
        part = x.astype(BF16)
        parts.append(part)
        x = x - part.astype(F32)
    return parts


def _dot_exact_rhs(x, w_bf16, n):
    return functools.reduce(jnp.add, [jnp.dot(p, w_bf16, preferred_element_type=F32) for p in _split(x, n)])


def _dot_exact_lhs(w_bf16, x, n):
    return functools.reduce(jnp.add, [jnp.dot(w_bf16, p, preferred_element_type=F32) for p in _split(x, n)])


def _const_spec(shape):
    nd = len(shape)
    return pl.BlockSpec(shape, lambda *_: (0,) * nd, pipeline_mode=pl.Buffered(1))


FF_CHUNK = 256
FF_TILE = 1024
FF_SUBTILES = 2


def _ffn_kernel(x_ref, gin_ref, wg_ref, wu_ref, wd_ref, gout_ref, o_ref, acc_ref):
    def rows_program(rows):
        x = x_ref[rows, :]
        xb = _rms(x, gin_ref[...]).astype(BF16)
        yield
        for c in range(D_FF // FF_CHUNK):
            sl = slice(c * FF_CHUNK, (c + 1) * FF_CHUNK)
            g = jnp.dot(xb, wg_ref[:, sl], preferred_element_type=F32)
            u = jnp.dot(xb, wu_ref[:, sl], preferred_element_type=F32)
            h = (g * jax.nn.sigmoid(g) * u).astype(BF16)
            part = jnp.dot(h, wd_ref[sl, :], preferred_element_type=F32)
            if c == 0:
                acc_ref[rows, :] = part
            else:
                acc_ref[rows, :] += part
            yield
        o_ref[rows, :] = x + 0.5 * _rms(acc_ref[rows, :], gout_ref[...])
        yield

    sub = FF_TILE // FF_SUBTILES
    programs = [rows_program(slice(i * sub, (i + 1) * sub)) for i in range(FF_SUBTILES)]
    n_stages = D_FF // FF_CHUNK + 2
    for step in range(n_stages + FF_SUBTILES - 1):
        for i, prog in enumerate(programs):
            if 0 <= step - i < n_stages:
                next(prog)


def _ffn(x, g_in, wg, wu, wd, layer, half, g_out):
    t = x.shape[0]
    row = pl.BlockSpec((FF_TILE, D_MODEL), lambda i: (i, 0))
    weight = lambda m, n: pl.BlockSpec((None, None, m, n), lambda i: (layer, half, 0, 0),
                                       pipeline_mode=pl.Buffered(1))
    return pl.pallas_call(
        _ffn_kernel,
        grid=(t // FF_TILE,),
        in_specs=[row, _const_spec((1, D_MODEL)), weight(D_MODEL, D_FF), weight(D_MODEL, D_FF),
                  weight(D_FF, D_MODEL), _const_spec((1, D_MODEL))],
        out_specs=row,
        out_shape=jax.ShapeDtypeStruct((t, D_MODEL), F32),
        scratch_shapes=[pltpu.VMEM((FF_TILE, D_MODEL), F32)],
        compiler_params=_params("parallel"),
        name="ffn_half",
    )(x, g_in.reshape(1, -1), wg, wu, wd, g_out.reshape(1, -1))


PROJ_CHUNK = 512


def _norm_proj_kernel(n_out, x_ref, g_ref, *refs):
    xb = _rms(x_ref[...], g_ref[...]).astype(BF16)
    for w_ref, o_ref in zip(refs[:n_out], refs[n_out:]):
        n = w_ref.shape[1]
        for c in range(0, n, PROJ_CHUNK):
            sl = slice(c, min(c + PROJ_CHUNK, n))
            o_ref[:, sl] = jnp.dot(xb, w_ref[:, sl], preferred_element_type=F32)


def _norm_proj(x, g, ws, name):
    t, d = x.shape
    row = lambda n: pl.BlockSpec((ROW_TILE, n), lambda i: (i, 0))
    return pl.pallas_call(
        functools.partial(_norm_proj_kernel, len(ws)),
        grid=(t // ROW_TILE,),
        in_specs=[row(d), _const_spec((1, d))] + [_const_spec(w.shape) for w in ws],
        out_specs=[row(w.shape[1]) for w in ws],
        out_shape=[jax.ShapeDtypeStruct((t, w.shape[1]), F32) for w in ws],
        compiler_params=_params("parallel"),
        name=name,
    )(x, g.reshape(1, -1), *ws)


RW_TILE = 512
EV_SUBTILES = 2


def _shift_rows(x, carry, n):
    rolled = pltpu.roll(x, n, 0)
    row = lax.broadcasted_iota(jnp.int32, x.shape, 0)
    out = rolled
    for i in range(n):
        out = jnp.where(row == i, carry[8 - n + i:8 - n + i + 1, :], out)
    return out


def _ev_in_kernel(x_ref, gn_ref, wrw_ref, wrt_ref, mu_ref, w0_ref, w2_ref, a0_ref, a2_ref, g2_ref, kk_ref, ka_ref,
                  ones_ref, r_ref, k_ref, v_ref, ld_ref, a_ref, b_ref, g_ref, prt_ref, carry_ref, p_ref):
    @pl.when(pl.program_id(1) == 0)
    def _():
        carry_ref[...] = jnp.zeros_like(carry_ref)

    def project(xb, rows, w_ref, o_ref):
        n = w_ref.shape[1]
        for c in range(0, n, PROJ_CHUNK):
            sl = slice(c, min(c + PROJ_CHUNK, n))
            o_ref[rows, sl] = jnp.dot(xb, w_ref[:, sl], preferred_element_type=F32)

    def rows_program(rows):
        xb = _rms(x_ref[rows, :], gn_ref[...]).astype(BF16)
        project(xb, rows, wrw_ref, p_ref)
        yield
        project(xb, rows, wrt_ref, prt_ref)
        p = p_ref[rows, :]
        before = carry_ref[...] if rows.start == 0 else p_ref[rows.start - 8:rows.start, :]
        prev = _shift_rows(p, before, 1)
        xm = p + (prev - p) * mu_ref[...]
        w = RW_WIDTH
        r, k, v = xm[:, :w], xm[:, w:2 * w], xm[:, 2 * w:3 * w]
        xw = xm[:, 3 * w:3 * w + RW_DECAY_RANK]
        xa = xm[:, 3 * w + RW_DECAY_RANK:3 * w + RW_DECAY_RANK + RW_AAA_RANK]
        xg = xm[:, 3 * w + RW_DECAY_RANK + RW_AAA_RANK:]
        yield
        wlog = -jax.nn.softplus(-(w0_ref[...] + _dot(jnp.tanh(xw), w2_ref[...]))) - 0.5
        a = jax.nn.sigmoid(a0_ref[...] + _dot(xa, a2_ref[...]))
        kk = k * kk_ref[...]
        ss = _dot_exact_rhs(kk * kk, ones_ref[...], SUM_PASSES)
        kk = kk / jnp.maximum(jnp.sqrt(ss), 1e-12)
        yield
        r_ref[rows, :] = r
        k_ref[rows, :] = k * (1.0 + (a - 1.0) * ka_ref[...])
        v_ref[rows, :] = v
        ld_ref[rows, :] = -jnp.exp(wlog)
        a_ref[rows, :] = -kk
        b_ref[rows, :] = kk * a
        g_ref[rows, :] = _dot(jax.nn.sigmoid(xg), g2_ref[...])
        yield

    sub = RW_TILE // EV_SUBTILES
    for _ in zip(*[rows_program(slice(i * sub, (i + 1) * sub)) for i in range(EV_SUBTILES)]):
        pass
    carry_ref[...] = p_ref[RW_TILE - 8:, :]


def _ev_in(x, gn, w_rw, w_rt, b_, s_, mu, w0, w2, a0, a2, g2, k_k, k_a, ones_blk):
    t, d = x.shape
    nt = s_ // RW_TILE
    row = lambda n: pl.BlockSpec((RW_TILE, n), lambda b, i: (b * nt + i, 0))
    vec = lambda a: a.reshape(1, -1)
    outs = [jax.ShapeDtypeStruct((t, RW_WIDTH), F32)] * 7 + [jax.ShapeDtypeStruct((t, RT_IN), F32)]
    return pl.pallas_call(
        _ev_in_kernel,
        grid=(b_, nt),
        in_specs=[row(d), _const_spec((1, d)), _const_spec(w_rw.shape), _const_spec(w_rt.shape),
                  _const_spec((1, RW_IN)), _const_spec((1, RW_WIDTH)),
                  _const_spec(w2.shape), _const_spec((1, RW_WIDTH)), _const_spec(a2.shape),
                  _const_spec(g2.shape), _const_spec((1, RW_WIDTH)), _const_spec((1, RW_WIDTH)),
                  _const_spec(ones_blk.shape)],
        out_specs=[row(RW_WIDTH)] * 7 + [row(RT_IN)],
        out_shape=outs,
        scratch_shapes=[pltpu.VMEM((8, RW_IN), F32), pltpu.VMEM((RW_TILE, RW_IN), F32)],
        compiler_params=_params("parallel", "arbitrary"),
        name="ev_in_proj",
    )(x, vec(gn), w_rw, w_rt, vec(mu), vec(w0), w2, vec(a0), a2, g2, vec(k_k), vec(k_a), ones_blk)


def _rw_scan_kernel(r_ref, k_ref, v_ref, ld_ref, a_ref, b_ref, g_ref, tri_ref, ones_ref,
                    rk_ref, lng_ref, lnb_ref, o_ref, state_ref):
    c_ = RW_CHUNK
    n_ = RW_HEAD_DIM

    @pl.when(pl.program_id(1) == 0)
    def _():
        state_ref[...] = jnp.zeros_like(state_ref)

    c2 = 2 * c_
    row = lax.broadcasted_iota(jnp.int32, (c2, c2), 0)
    col = lax.broadcasted_iota(jnp.int32, (c2, c2), 1)
    same_head = (row >= c_) == (col >= c_)
    strict = jnp.logical_and(same_head, row > col)
    incl = jnp.logical_and(same_head, row >= col)
    left = lax.broadcasted_iota(jnp.int32, (c_, LANES), 1) < n_

    def block_diag(x):
        return jnp.concatenate([jnp.where(left, x, 0.0), jnp.where(left, 0.0, x)], axis=0)

    pairs = range(RW_HEADS // 2)
    sls = [slice(p * LANES, (p + 1) * LANES) for p in pairs]
    ones2 = ones_ref[...]

    def head_sums(xs):
        parts = [part[:, sl] for x in xs for part in _split(x, SUM_PASSES) for sl in sls]
        prod = jnp.dot(jnp.concatenate(parts, axis=0), ones2, preferred_element_type=F32)
        outs = []
        for i in range(len(xs)):
            slabs = []
            for p in pairs:
                rows = [((SUM_PASSES * i + j) * len(sls) + p) * c_ for j in range(SUM_PASSES)]
                slabs.append(functools.reduce(jnp.add, [prod[r:r + c_] for r in rows]))
            outs.append(jnp.concatenate(slabs, axis=1))
        return outs

    def chunk_program(bb):
        ld = ld_ref[bb]
        r = r_ref[bb]
        k = k_ref[bb]
        v = v_ref[bb]
        cum = _dot_exact_lhs(tri_ref[...], ld, 3)
        mid = cum[c_ // 2 - 1:c_ // 2, :]
        e_in = jnp.exp(cum - mid)
        e_out = jnp.exp(mid - cum)
        r_t = r * e_in
        a_t = a_ref[bb] * jnp.exp(cum - ld - mid)
        b_t = b_ref[bb] * e_out
        k_t = k * e_out
        e_mid = jnp.exp(mid)
        w_all = jnp.exp(cum[c_ - 1:c_, :])
        w_tail = jnp.exp(cum[c_ - 1:c_, :] - mid)
        yield
        s0 = [state_ref[bb, p] for p in pairs]
        ar = [jnp.concatenate([block_diag(a_t[:, sl]), block_diag(r_t[:, sl])], axis=0) for sl in sls]
        bk = [jnp.concatenate([block_diag(b_t[:, sl]), block_diag(k_t[:, sl])], axis=0) for sl in sls]
        vb = [block_diag(v[:, sl]) for sl in sls]
        m1 = [_dot_nt(ar[p], bk[p]) for p in pairs]
        m2 = [_dot_nt(ar[p] * e_mid[:, sls[p]], s0[p]) for p in pairs]
        yield
        l_ab = [jnp.where(strict, m[:c2, :c2], 0.0) for m in m1]
        l_ak = [jnp.where(strict, m[:c2, c2:], 0.0) for m in m1]
        l_r = [jnp.concatenate([jnp.where(incl, m[c2:, :c2], 0.0), jnp.where(incl, m[c2:, c2:], 0.0)], axis=1)
               for m in m1]
        u = [m2[p][:c2] + _dot(l_ak[p], vb[p]) for p in pairs]
        pw = l_ab
        yield
        n_steps = int(math.log2(c_))
        for step in range(n_steps):
            if step < n_steps - 1:
                prod = [_dot(pw[p], jnp.concatenate([pw[p], u[p]], axis=1)) for p in pairs]
                pw = [q[:, :c2] for q in prod]
                u = [u[p] + prod[p][:, c2:] for p in pairs]
            else:
                u = [u[p] + _dot(pw[p], u[p]) for p in pairs]
            yield
        uv = [jnp.concatenate([u[p], vb[p]], axis=0) for p in pairs]
        ys = [m2[p][c2:] + _dot(l_r[p], uv[p]) for p in pairs]
        upd = [_dot_tn(uv[p], bk[p]) for p in pairs]
        yield
        for p in pairs:
            state_ref[bb, p] = s0[p] * w_all[:, sls[p]] + upd[p] * w_tail[:, sls[p]]
        y = jnp.concatenate([ys[p][:c_] + ys[p][c_:] for p in pairs], axis=1)
        sum_y, sum_rk = head_sums([y, r * k * rk_ref[...]])
        d = y - sum_y * (1.0 / n_)
        yield
        var = head_sums([d * d])[0] * (1.0 / n_)
        yn = d * lax.rsqrt(var + RW_LN_EPS) * lng_ref[...] + lnb_ref[...]
        o_ref[bb] = ((yn + sum_rk * v) * g_ref[bb]).astype(o_ref.dtype)
        yield

    for _ in zip(*[chunk_program(bb) for bb in range(RW_ROWS)]):
        pass


def _rw_scan(r, k, v, ld, a, b, g, b_, s_, r_k, ln_g, ln_b, ones_pair, tri):
    t = r.shape[0]
    nc = s_ // RW_CHUNK
    row = pl.BlockSpec((RW_ROWS, RW_CHUNK, RW_WIDTH), lambda bi, i: (bi, i, 0))
    vec = lambda x: x.reshape(1, -1)
    seq = lambda x: x.reshape(b_, s_, RW_WIDTH)
    out = pl.pallas_call(
        _rw_scan_kernel,
        grid=(b_ // RW_ROWS, nc),
        in_specs=[row] * 7 + [_const_spec(tri.shape), _const_spec(ones_pair.shape),
                              _const_spec((1, RW_WIDTH)), _const_spec((1, RW_WIDTH)), _const_spec((1, RW_WIDTH))],
        out_specs=row,
        out_shape=jax.ShapeDtypeStruct((b_, s_, RW_WIDTH), BF16),
        scratch_shapes=[pltpu.VMEM((RW_ROWS, RW_HEADS // 2, 2 * RW_HEAD_DIM, 2 * RW_HEAD_DIM), F32)],
        compiler_params=_params("parallel", "arbitrary"),
        name="rwkv_scan",
    )(seq(r), seq(k), seq(v), seq(ld), seq(a), seq(b), seq(g), tri, ones_pair, vec(r_k), vec(ln_g), vec(ln_b))
    return out.reshape(t, RW_WIDTH)


def _rt_log_gamma(h):
    return math.log1p(-(2.0 ** (-5.0 - h)))


def _retention_kernel(p_ref, cos_ref, sin_ref, gng_ref, gnb_ref, o_ref, state_ref):
    c_ = RT_CHUNK

    @pl.when(pl.program_id(1) == 0)
    def _():
        state_ref[...] = jnp.zeros_like(state_ref)

    cos = cos_ref[...]
    sin = sin_ref[...]
    row = lax.broadcasted_iota(jnp.int32, (c_, c_), 0).astype(F32)
    col = lax.broadcasted_iota(jnp.int32, (c_, c_), 1).astype(F32)
    rel = row - col
    heads = range(RT_HEADS)
    lgs = [_rt_log_gamma(h) for h in heads]
    sls = [slice(h * RT_DIM, (h + 1) * RT_DIM) for h in heads]
    decay = [jnp.where(rel >= 0, jnp.exp(jnp.maximum(rel, 0.0) * lg), 0.0) for lg in lgs]
    xi = [jnp.exp((row + 1.0) * lg) for lg in lgs]
    zeta = [jnp.exp((c_ - 1.0 - row) * lg) for lg in lgs]

    def chunk_program(bb):
        col_of = lambda part, h: slice(part * RT_WIDTH + h * RT_DIM, part * RT_WIDTH + (h + 1) * RT_DIM)
        q = [p_ref[bb, :, col_of(0, h)] for h in heads]
        k = [p_ref[bb, :, col_of(1, h)] for h in heads]
        v = [p_ref[bb, :, col_of(2, h)] for h in heads]
        q = [x * cos + pltpu.roll(x, RT_DIM // 2, 1) * sin for x in q]
        k = [(x * cos + pltpu.roll(x, RT_DIM // 2, 1) * sin) * (RT_DIM ** -0.5) for x in k]
        s0 = [state_ref[bb, h] for h in heads]
        yield
        scores = [_dot_nt(q[h], k[h]) * decay[h] for h in heads]
        cross = [_dot(q[h], s0[h]) * xi[h] for h in heads]
        for h in heads:
            state_ref[bb, h] = s0[h] * math.exp(c_ * lgs[h]) + _dot_tn(k[h] * zeta[h], v[h])
        yield
        o = [_dot(scores[h], v[h]) + cross[h] for h in heads]
        yield
        for h in heads:
            mu = jnp.mean(o[h], axis=-1, keepdims=True)
            d = o[h] - mu
            var = jnp.mean(d * d, axis=-1, keepdims=True)
            on = d * lax.rsqrt(var + EPS) * gng_ref[:, sls[h]] + gnb_ref[:, sls[h]]
            gate = p_ref[bb, :, col_of(3, h)]
            o_ref[bb, :, sls[h]] = (gate * jax.nn.sigmoid(gate) * on).astype(o_ref.dtype)
        yield

    for _ in zip(*[chunk_program(bb) for bb in range(RT_ROWS)]):
        pass


def _retention(p_rt, b_, s_, cos2, sin2, gn_g, gn_b):
    t = p_rt.shape[0]
    nc = s_ // RT_CHUNK
    out = pl.pallas_call(
        _retention_kernel,
        grid=(b_ // RT_ROWS, nc),
        in_specs=[pl.BlockSpec((RT_ROWS, RT_CHUNK, RT_IN), lambda bi, i: (bi, i, 0)),
                  pl.BlockSpec((RT_CHUNK, RT_DIM), lambda bi, i: (i, 0)),
                  pl.BlockSpec((RT_CHUNK, RT_DIM), lambda bi, i: (i, 0)),
                  _const_spec((1, RT_WIDTH)), _const_spec((1, RT_WIDTH))],
        out_specs=pl.BlockSpec((RT_ROWS, RT_CHUNK, RT_WIDTH), lambda bi, i: (bi, i, 0)),
        out_shape=jax.ShapeDtypeStruct((b_, s_, RT_WIDTH), BF16),
        scratch_shapes=[pltpu.VMEM((RT_ROWS, RT_HEADS, RT_DIM, RT_DIM), F32)],
        compiler_params=_params("parallel", "arbitrary"),
        name="retention",
    )(p_rt.reshape(b_, s_, RT_IN), cos2, sin2, gn_g.reshape(1, -1), gn_b.reshape(1, -1))
    return out.reshape(t, RT_WIDTH)


DSA_BLOCKS_PER_TILE = ROW_TILE // Q_BLOCK


def _dsa_prep(c_q, c_kv, kw, qg_ref, kvg_ref, wuq_ref, wuk_ref, wqi_ref,
              ckvn_ref, ckvt_ref, kidx_ref, widx_ref, qlat_ref, qidx_ref):
    cq = _rms(c_q, qg_ref[...]).astype(BF16)
    ckvn = _rms(c_kv, kvg_ref[...])
    ckvn_ref[...] = ckvn.astype(BF16)
    kidx_ref[...] = kw[:, :IDX_DIM].astype(BF16)
    q = jnp.dot(cq, wuq_ref[...], preferred_element_type=F32)
    qi = jnp.dot(cq, wqi_ref[...], preferred_element_type=F32)
    for blk in range(DSA_BLOCKS_PER_TILE):
        rows = slice(blk * Q_BLOCK, (blk + 1) * Q_BLOCK)
        ckvt_ref[blk] = ckvn[rows].T.astype(BF16)
        widx_ref[blk] = kw[rows].T[IDX_DIM:IDX_DIM + IDX_HEADS] * ((IDX_HEADS * IDX_DIM) ** -0.5)
    for h in range(DSA_HEADS):
        qh = q[:, h * DSA_HEAD_DIM:(h + 1) * DSA_HEAD_DIM]
        ql = (_dot(qh, wuk_ref[h]) * (DSA_HEAD_DIM ** -0.5)).astype(BF16)
        qih = qi[:, h * IDX_DIM:(h + 1) * IDX_DIM].astype(BF16)
        for blk in range(DSA_BLOCKS_PER_TILE):
            rows = slice(blk * Q_BLOCK, (blk + 1) * Q_BLOCK)
            qlat_ref[blk, h] = ql[rows]
            qidx_ref[blk, h] = qih[rows]


def _fold_rows(x, op):
    return functools.reduce(op, [x[i:i + 8] for i in range(0, x.shape[0], 8)])


def _bit_transpose32(words):
    a = list(words)
    j, m = 16, 0x0000FFFF
    while j:
        k = 0
        while k < 32:
            t = (a[k] ^ lax.shift_right_logical(a[k + j], jnp.int32(j))) & jnp.int32(m)
            a[k] = a[k] ^ t
            a[k + j] = a[k + j] ^ lax.shift_left(t, jnp.int32(j))
            k = (k + j + 1) & ~j
        j >>= 1
        m ^= m << j
    return a


def _dsa_kernel(top_k, qidx_ref, widx_ref, qlat_ref, kidx_ref, ckv_ref, ckvt_ref, wuv_ref, tril_ref, o_ref,
                key_ref, planes_ref, s_ref, acc_ref):
    qb = Q_BLOCK
    nh = DSA_HEADS
    j = pl.program_id(1)
    n_chunks = j + 1

    keyi = lax.broadcasted_iota(jnp.int32, (qb, qb), 0)
    qryi = lax.broadcasted_iota(jnp.int32, (qb, qb), 1)
    lanes = lambda h: slice(h * qb, (h + 1) * qb)

    def key_chunk(ref, c):
        return ref[pl.ds(pl.multiple_of(c * qb, qb), qb), :]

    w_idx = widx_ref[0]
    q_idx = qidx_ref[0].reshape(IDX_HEADS * qb, IDX_DIM)
    q_lat = qlat_ref[0].reshape(nh * qb, DSA_KV_RANK)

    n_pairs = (n_chunks + 1) // 2

    def score_chunk(c):
        logits = lax.dot_general(key_chunk(kidx_ref, c), q_idx, (((1,), (1,)), ((), ())),
                                 preferred_element_type=F32)
        score = jnp.zeros((qb, qb), F32)
        for h in range(IDX_HEADS):
            score = score + jnp.maximum(logits[:, lanes(h)], 0.0) * w_idx[h:h + 1, :]
        bits = lax.bitcast_convert_type(score, jnp.int32)
        key = jnp.where(bits < 0, bits ^ jnp.int32(0x7FFFFFFF), bits)
        causal = keyi + c * qb <= qryi + j * qb
        key = jnp.where(causal, key, jnp.int32(INT_MIN))
        key_ref[c] = key
        return key

    def score_body(pi, carry):
        keys = [score_chunk(2 * pi), score_chunk(2 * pi + 1)]
        words = [k[r:r + 8] ^ jnp.int32(INT_MIN) for k in keys for r in range(0, qb, 8)]
        for idx, plane in enumerate(_bit_transpose32(words)):
            planes_ref[31 - idx, pi] = plane
        return carry

    lax.fori_loop(0, n_pairs, score_body, 0)

    n_pairs_max = planes_ref.shape[1]
    live = tuple(jnp.where(p < n_pairs, jnp.full((8, qb), -1, jnp.int32), jnp.zeros((8, qb), jnp.int32))
                 for p in range(n_pairs_max))

    def bit_body(i, carry):
        thr_u, above, eq = carry
        plane = planes_ref[31 - i]
        hits = [eq[p] & plane[p] for p in range(n_pairs_max)]
        cnt = functools.reduce(jnp.add, [lax.population_count(h) for h in hits])
        cnt = above + jnp.sum(cnt, axis=0, keepdims=True)
        accept = cnt >= top_k
        thr_u = jnp.where(accept, thr_u | lax.shift_left(jnp.int32(1), 31 - i), thr_u)
        above = jnp.where(accept, above, cnt)
        eq = tuple(jnp.where(accept, h, e ^ h) for e, h in zip(eq, hits))
        return thr_u, above, eq

    zero_row = jnp.zeros((1, qb), jnp.int32)
    thr_u, above, _ = lax.fori_loop(0, 32, bit_body, (zero_row, zero_row, live))
    thr = thr_u ^ jnp.int32(INT_MIN)
    need = (top_k - above).astype(F32)

    def pair_logits(t, taken):
        tops = [None] * nh
        for c in (2 * t, 2 * t + 1):
            key = key_ref[c]
            causal = keyi + c * qb <= qryi + j * qb
            eq = jnp.logical_and(key == thr, causal)
            eq_f = jnp.where(eq, 1.0, 0.0)
            rank = taken + jnp.dot(tril_ref[...], eq_f.astype(BF16), preferred_element_type=F32)
            sel = jnp.logical_or(jnp.logical_and(key > thr, causal), jnp.logical_and(eq, rank <= need))
            s = lax.dot_general(key_chunk(ckv_ref, c), q_lat, (((1,), (1,)), ((), ())),
                                preferred_element_type=F32)
            for h in range(nh):
                sh = jnp.where(sel, s[:, lanes(h)], NEG_BIG)
                s_ref[c, :, lanes(h)] = sh
                top = _fold_rows(sh, jnp.maximum)
                tops[h] = top if tops[h] is None else jnp.maximum(tops[h], top)
            taken = taken + jnp.sum(eq_f, axis=0, keepdims=True)
        return taken, tuple(jnp.max(top, axis=0, keepdims=True) for top in tops)

    def pair_values(t, tops, peak, sums):
        ps, scales, new_peak, new_sums = [], [], [], []
        for h in range(nh):
            pk = jnp.maximum(peak[h], tops[h])
            scale = jnp.exp(peak[h] - pk)
            pa = jnp.exp(s_ref[2 * t, :, lanes(h)] - pk)
            pb = jnp.exp(s_ref[2 * t + 1, :, lanes(h)] - pk)
            new_sums.append(sums[h] * scale + _fold_rows(pa + pb, jnp.add))
            ps.append(jnp.concatenate([pa.astype(BF16), pb.astype(BF16)], axis=0))
            scales.append(scale)
            new_peak.append(pk)
        ckvt_pair = jnp.concatenate([ckvt_ref[2 * t], ckvt_ref[2 * t + 1]], axis=1)
        upd = jnp.dot(ckvt_pair, jnp.concatenate(ps, axis=1), preferred_element_type=F32)
        acc_ref[...] = acc_ref[...] * jnp.concatenate(scales, axis=1) + upd
        return tuple(new_peak), tuple(new_sums)

    acc_ref[...] = jnp.zeros_like(acc_ref)

    def attn_body(t, carry):
        taken, tops, peak, sums = carry
        peak, sums = pair_values(t, tops, peak, sums)
        taken, tops = pair_logits(jnp.minimum(t + 1, n_pairs - 1), taken)
        return taken, tops, peak, sums

    taken, tops = pair_logits(0, jnp.zeros((1, qb), F32))
    init = (taken, tops, tuple(jnp.full((1, qb), NEG_BIG, F32) for _ in range(nh)),
            tuple(jnp.zeros((8, qb), F32) for _ in range(nh)))
    _, _, _, sums = lax.fori_loop(0, n_pairs, attn_body, init)

    outs = []
    for h in range(nh):
        o_lat_t = acc_ref[:, lanes(h)] / jnp.sum(sums[h], axis=0, keepdims=True)
        outs.append(_dot_tn(o_lat_t, wuv_ref[h]))
    o_ref[...] = jnp.concatenate(outs, axis=1).astype(o_ref.dtype)


def _dsa(q_idx, w_idx, q_lat, k_idx, ckv_n, ckv_t, w_uv, tril, b_, s_):
    nb = s_ // Q_BLOCK
    t = b_ * s_
    top_k = min(TOPK_MAX, s_ // 4)
    blk = lambda n: pl.BlockSpec((Q_BLOCK, n), lambda bi, i: (bi * nb + i, 0))
    blk3 = lambda m, n: pl.BlockSpec((1, m, n), lambda bi, i: (bi * nb + i, 0, 0))
    blk4 = lambda n: pl.BlockSpec((1, DSA_HEADS, Q_BLOCK, n), lambda bi, i: (bi * nb + i, 0, 0, 0))
    seq = lambda n: pl.BlockSpec((s_, n), lambda bi, i: (bi, 0))
    stacked = DSA_HEADS * Q_BLOCK
    return pl.pallas_call(
        functools.partial(_dsa_kernel, top_k),
        grid=(b_, nb),
        in_specs=[blk4(IDX_DIM), blk3(IDX_HEADS, Q_BLOCK), blk4(DSA_KV_RANK),
                  seq(IDX_DIM), seq(DSA_KV_RANK),
                  pl.BlockSpec((nb, DSA_KV_RANK, Q_BLOCK), lambda bi, i: (bi, 0, 0)),
                  _const_spec(w_uv.shape), _const_spec(tril.shape)],
        out_specs=blk(DSA_WIDTH),
        out_shape=jax.ShapeDtypeStruct((t, DSA_WIDTH), BF16),
        scratch_shapes=[pltpu.VMEM((nb, Q_BLOCK, Q_BLOCK), jnp.int32),
                        pltpu.VMEM((32, nb // 2, 8, Q_BLOCK), jnp.int32),
                        pltpu.VMEM((nb, Q_BLOCK, stacked), F32),
                        pltpu.VMEM((DSA_KV_RANK, stacked), F32)],
        compiler_params=_params("parallel", "arbitrary"),
        name="dsa_attn",
    )(q_idx, w_idx, q_lat, k_idx, ckv_n, ckv_t, w_uv, tril)


SC_TILE = 512


def _od_in_kernel(x_ref, gn_ref, wq_ref, wkv_ref, wkw_ref, wsc_ref, cw_ref, cb_ref,
                  qg_ref, kvg_ref, wuq_ref, wuk_ref, wqi_ref,
                  ckvn_ref, ckvt_ref, kidx_ref, widx_ref, qlat_ref, qidx_ref, yd_ref, carry_ref, p_ref):
    @pl.when(pl.program_id(1) == 0)
    def _():
        carry_ref[...] = jnp.zeros_like(carry_ref)

    xb = _rms(x_ref[...], gn_ref[...]).astype(BF16)
    c_q, c_kv, kw = (jnp.dot(xb, w_ref[...], preferred_element_type=F32) for w_ref in (wq_ref, wkv_ref, wkw_ref))
    for c in range(0, 3 * SC_WIDTH, PROJ_CHUNK):
        sl = slice(c, c + PROJ_CHUNK)
        p_ref[:, sl] = jnp.dot(xb, wsc_ref[:, sl], preferred_element_type=F32)
    _dsa_prep(c_q, c_kv, kw, qg_ref, kvg_ref, wuq_ref, wuk_ref, wqi_ref,
              ckvn_ref, ckvt_ref, kidx_ref, widx_ref, qlat_ref, qidx_ref)

    h = p_ref[:, :SC_WIDTH]
    gate_b = p_ref[:, SC_WIDTH:2 * SC_WIDTH]
    gate_c = p_ref[:, 2 * SC_WIDTH:]
    u = gate_c * h
    carry = carry_ref[...]
    y = u * cw_ref[2:3, :] + _shift_rows(u, carry, 1) * cw_ref[1:2, :] + _shift_rows(u, carry, 2) * cw_ref[0:1, :]
    carry_ref[...] = u[SC_TILE - 8:, :]
    yd_ref[...] = (gate_b * (y + cb_ref[...])).astype(yd_ref.dtype)


def _od_in(x, gn, w_q, w_kv, w_kw, w_sc, b_, s_, conv_w, conv_b, q_g, kv_g, w_uq, w_uk, w_qi):
    t, d = x.shape
    nt = s_ // SC_TILE
    row = lambda n: pl.BlockSpec((SC_TILE, n), lambda bi, i: (bi * nt + i, 0))
    blk3 = lambda m, n: pl.BlockSpec((DSA_BLOCKS_PER_TILE, m, n), lambda bi, i: (bi * nt + i, 0, 0))
    blk4 = lambda n: pl.BlockSpec((DSA_BLOCKS_PER_TILE, DSA_HEADS, Q_BLOCK, n), lambda bi, i: (bi * nt + i, 0, 0, 0))
    consts = [w_q, w_kv, w_kw, w_sc, jnp.pad(conv_w, ((0, 8 - SC_KERNEL), (0, 0))), conv_b.reshape(1, -1),
              q_g.reshape(1, -1), kv_g.reshape(1, -1), w_uq, w_uk, w_qi]
    return pl.pallas_call(
        _od_in_kernel,
        grid=(b_, nt),
        in_specs=[row(d), _const_spec((1, d))] + [_const_spec(c.shape) for c in consts],
        out_specs=[row(DSA_KV_RANK), blk3(DSA_KV_RANK, Q_BLOCK), row(IDX_DIM), blk3(IDX_HEADS, Q_BLOCK),
                   blk4(DSA_KV_RANK), blk4(IDX_DIM), row(SC_WIDTH)],
        out_shape=[jax.ShapeDtypeStruct((t, DSA_KV_RANK), BF16),
                   jax.ShapeDtypeStruct((t // Q_BLOCK, DSA_KV_RANK, Q_BLOCK), BF16),
                   jax.ShapeDtypeStruct((t, IDX_DIM), BF16),
                   jax.ShapeDtypeStruct((t // Q_BLOCK, IDX_HEADS, Q_BLOCK), F32),
                   jax.ShapeDtypeStruct((t // Q_BLOCK, DSA_HEADS, Q_BLOCK, DSA_KV_RANK), BF16),
                   jax.ShapeDtypeStruct((t // Q_BLOCK, IDX_HEADS, Q_BLOCK, IDX_DIM), BF16),
                   jax.ShapeDtypeStruct((t, SC_WIDTH), BF16)],
        scratch_shapes=[pltpu.VMEM((8, SC_WIDTH), F32), pltpu.VMEM((SC_TILE, 3 * SC_WIDTH), F32)],
        compiler_params=_params("parallel", "arbitrary"),
        name="od_in_proj",
    )(x, gn.reshape(1, -1), *consts)


def _xattn_kernel(y1_ref, y2_ref, w1_ref, w2_ref, gm_ref, x_ref, gq_ref, wq_ref, k_ref, v_ref, wo_ref, go_ref,
                  o_ref, att_ref):
    def rows_program(rows):
        mix = (jnp.dot(y1_ref[rows, :], w1_ref[...], preferred_element_type=F32)
               + jnp.dot(y2_ref[rows, :], w2_ref[...], preferred_element_type=F32))
        yield
        x = x_ref[rows, :] + _rms(mix, gm_ref[...])
        q = jnp.dot(_rms(x, gq_ref[...]).astype(BF16), wq_ref[...], preferred_element_type=F32)
        yield
        for h in range(XA_HEADS):
            sl = slice(h * XA_HEAD_DIM, (h + 1) * XA_HEAD_DIM)
            s = _dot_nt(q[:, sl], k_ref[:, sl]) * (XA_HEAD_DIM ** -0.5)
            s = s - jnp.max(s, axis=-1, keepdims=True)
            p = jnp.exp(s)
            p = p / jnp.sum(p, axis=-1, keepdims=True)
            att_ref[rows, sl] = _dot(p, v_ref[:, sl])
            yield
        hout = jnp.dot(att_ref[rows, :].astype(BF16), wo_ref[...], preferred_element_type=F32)
        yield
        o_ref[rows, :] = x + _rms(hout, go_ref[...])
        yield

    sub = ROW_TILE // XA_SUBTILES
    for _ in zip(*[rows_program(slice(i * sub, (i + 1) * sub)) for i in range(XA_SUBTILES)]):
        pass


def _xattn(y1, y2, w1, w2, g_mix, x, g_q, wq, k_mem, v_mem, wo, g_o, b_, s_):
    t = x.shape[0]
    nt = s_ // ROW_TILE
    row = lambda n: pl.BlockSpec((ROW_TILE, n), lambda bi, i: (bi * nt + i, 0))
    mem = pl.BlockSpec((MEM_LEN, XA_WIDTH), lambda bi, i: (bi, 0))
    vec = lambda g: g.reshape(1, -1)
    return pl.pallas_call(
        _xattn_kernel,
        grid=(b_, nt),
        in_specs=[row(y1.shape[1]), row(y2.shape[1]), _const_spec(w1.shape), _const_spec(w2.shape),
                  _const_spec((1, D_MODEL)), row(D_MODEL), _const_spec((1, D_MODEL)), _const_spec(wq.shape),
                  mem, mem, _const_spec(wo.shape), _const_spec((1, D_MODEL))],
        out_specs=row(D_MODEL),
        out_shape=jax.ShapeDtypeStruct((t, D_MODEL), F32),
        scratch_shapes=[pltpu.VMEM((ROW_TILE, XA_WIDTH), F32)],
        compiler_params=_params("parallel", "parallel"),
        name="mix_out_xattn",
    )(y1, y2, w1, w2, vec(g_mix), x, vec(g_q), wq, k_mem, v_mem, wo, vec(g_o))


def _block_diag(n_blocks, size, value):
    return np.kron(np.eye(n_blocks, dtype=np.float32), np.full((size, size), value, np.float32))


def _rope_tables(s_):
    half = RT_DIM // 2
    inv_freq = RT_ROPE_BASE ** (-jnp.arange(half, dtype=F32) / half)
    ang = jnp.arange(s_).astype(F32)[:, None] * inv_freq[None, :]
    cos, sin = jnp.cos(ang), jnp.sin(ang)
    return jnp.concatenate([cos, cos], axis=-1), jnp.concatenate([-sin, sin], axis=-1)


def kernel(x, mem, norm_g, mem_norm_g, ffn_w_gate, ffn_w_up, ffn_w_down, xa_wq, xa_wk, xa_wv, xa_wo, ev_w_in, ev_w_out, rw_mu, rw_w0, rw_w2, rw_a0, rw_a2, rw_g2, rw_k_k, rw_k_a, rw_r_k, rw_ln_g, rw_ln_b, rt_gn_g, rt_gn_b, od_w_in, od_w_out, dsa_q_norm_g, dsa_kv_norm_g, dsa_w_uq, dsa_w_uk, dsa_w_uv, dsa_w_qi, sc_conv_w, sc_conv_b):
    b_, s_, d_ = x.shape
    depth = norm_g.shape[0]
    t = b_ * s_
    bf = lambda w: w.astype(BF16)

    ones_blk = jnp.asarray(_block_diag(RW_HEADS, RW_HEAD_DIM, 1.0), BF16)
    ones_pair = jnp.asarray(_block_diag(2, RW_HEAD_DIM, 1.0), BF16)
    tri_rw = jnp.asarray(np.tril(np.ones((RW_CHUNK, RW_CHUNK), np.float32)), BF16)
    tril_dsa = jnp.asarray(np.tril(np.ones((Q_BLOCK, Q_BLOCK), np.float32)), BF16)
    cos2, sin2 = _rope_tables(s_)

    w_gate, w_up, w_down = bf(ffn_w_gate), bf(ffn_w_up), bf(ffn_w_down)
    xf = x.reshape(t, d_)
    mem_f = mem.reshape(b_ * MEM_LEN, d_)
    for l in range(depth):
        ng = norm_g[l]
        i = l // 2
        xf = _ffn(xf, ng[0], w_gate, w_up, w_down, l, 0, ng[1])
        if l % 2 == 0:
            w_in = bf(ev_w_in[i])
            r, k, v, ld, a, b, g, p_rt = _ev_in(xf, ng[2], w_in[:, :RW_IN], w_in[:, RW_IN:], b_, s_, rw_mu[i],
                                                rw_w0[i], rw_w2[i], rw_a0[i], rw_a2[i], rw_g2[i], rw_k_k[i],
                                                rw_k_a[i], ones_blk)
            y_a = _rw_scan(r, k, v, ld, a, b, g, b_, s_, rw_r_k[i], rw_ln_g[i], rw_ln_b[i],
                           ones_pair, tri_rw)
            y_b = _retention(p_rt, b_, s_, cos2, sin2, rt_gn_g[i], rt_gn_b[i])
            ys, w_out, split = (y_a, y_b), bf(ev_w_out[i]), RW_WIDTH
        else:
            w_in = od_w_in[i]
            kw_w = jnp.pad(w_in[:, DSA_Q_RANK + DSA_KV_RANK:DSA_IN], ((0, 0), (0, LANES - IDX_DIM - IDX_HEADS)))
            ckv_n, ckv_t, k_idx, w_idx, q_lat, q_idx, y_d = _od_in(
                xf, ng[2], bf(w_in[:, :DSA_Q_RANK]), bf(w_in[:, DSA_Q_RANK:DSA_Q_RANK + DSA_KV_RANK]),
                bf(kw_w), bf(w_in[:, DSA_IN:]), b_, s_, sc_conv_w[i], sc_conv_b[i],
                dsa_q_norm_g[i], dsa_kv_norm_g[i],
                bf(dsa_w_uq[i].reshape(DSA_Q_RANK, DSA_WIDTH)), bf(dsa_w_uk[i]),
                bf(dsa_w_qi[i].reshape(DSA_Q_RANK, IDX_HEADS * IDX_DIM)))
            y_c = _dsa(q_idx, w_idx, q_lat, k_idx, ckv_n, ckv_t, bf(dsa_w_uv[i]), tril_dsa, b_, s_)
            ys, w_out, split = (y_c, y_d), bf(od_w_out[i]), DSA_WIDTH
        k_mem, v_mem = _norm_proj(mem_f, mem_norm_g, [bf(xa_wk[l]), bf(xa_wv[l])], "mem_kv_proj")
        xf = _xattn(ys[0], ys[1], w_out[:split], w_out[split:], ng[3], xf, ng[4], bf(xa_wq[l]), k_mem, v_mem,
                    bf(xa_wo[l]), ng[5], b_, s_)
        xf = _ffn(xf, ng[6], w_gate, w_up, w_down, l, 1, ng[7])
    return xf.reshape(b_, s_, d_)
```

```python
import functools
import math

import numpy as np
import jax
import jax.numpy as jnp
from jax import lax
from jax.experimental import pallas as pl
from jax.experimental.pallas import tpu as pltpu

F32 = jnp.float32
BF16 = jnp.bfloat16

D_MODEL = 1024
D_FF = 2816
EPS = 1e-6
MEM_LEN = 256
RW_HEADS = 8
RW_HEAD_DIM = 64
RW_WIDTH = RW_HEADS * RW_HEAD_DIM
RW_DECAY_RANK = 64
RW_AAA_RANK = 64
RW_GATE_RANK = 128
RW_LN_EPS = 64e-5
RW_IN = 3 * RW_WIDTH + RW_DECAY_RANK + RW_AAA_RANK + RW_GATE_RANK
RW_CHUNK = 64
RW_ROWS = 4
RT_HEADS = 4
RT_DIM = 128
RT_WIDTH = RT_HEADS * RT_DIM
RT_CHUNK = 128
RT_ROWS = 4
RT_ROPE_BASE = 10000.0
RT_IN = 4 * RT_WIDTH
DSA_HEADS = 8
DSA_HEAD_DIM = 64
DSA_WIDTH = DSA_HEADS * DSA_HEAD_DIM
DSA_Q_RANK = 256
DSA_KV_RANK = 128
IDX_HEADS = 8
IDX_DIM = 64
TOPK_MAX = 256
Q_BLOCK = 128
DSA_IN = DSA_Q_RANK + DSA_KV_RANK + IDX_DIM + IDX_HEADS
SC_WIDTH = 512
SC_KERNEL = 3
XA_HEADS = 4
XA_HEAD_DIM = 128
XA_WIDTH = XA_HEADS * XA_HEAD_DIM
XA_SUBTILES = 2

LANES = 128
ROW_TILE = 512
VMEM_LIMIT = 56 * 1024 * 1024
SUM_PASSES = 2
INT_MIN = -2 ** 31
NEG_BIG = -1e30


def _params(*sem):
    return pltpu.CompilerParams(dimension_semantics=sem, vmem_limit_bytes=VMEM_LIMIT)


def _rms(x, g):
    return x * lax.rsqrt(jnp.mean(x * x, axis=-1, keepdims=True) + EPS) * g


def _dot(a, b):
    return jnp.dot(a.astype(BF16), b.astype(BF16), preferred_element_type=F32)


def _dot_nt(a, b):
    return lax.dot_general(a.astype(BF16), b.astype(BF16), (((1,), (1,)), ((), ())),
                           preferred_element_type=F32)


def _dot_tn(a, b):
    return lax.dot_general(a.astype(BF16), b.astype(BF16), (((0,), (0,)), ((), ())),
                           preferred_element_type=F32)


def _split(x, n):
    parts = []
    for _ in range(n):
        part = x.astype(BF16)
        parts.append(part)
        x = x - part.astype(F32)
    return parts


def _dot_exact_rhs(x, w_bf16, n):
    return functools.reduce(jnp.add, [jnp.dot(p, w_bf16, preferred_element_type=F32) for p in _split(x, n)])


def _dot_exact_lhs(w_bf16, x, n):
    return functools.reduce(jnp.add, [jnp.dot(w_bf16, p, preferred_element_type=F32) for p in _split(x, n)])


def _const_spec(shape):
    nd = len(shape)
    return pl.BlockSpec(shape, lambda *_: (0,) * nd, pipeline_mode=pl.Buffered(1))


FF_CHUNK = 256
FF_TILE = 1024
FF_SUBTILES = 2


def _ffn_kernel(x_ref, gin_ref, wg_ref, wu_ref, wd_ref, gout_ref, o_ref, acc_ref):
    def rows_program(rows):
        x = x_ref[rows, :]
        xb = _rms(x, gin_ref[...]).astype(BF16)
        yield
        for c in range(D_FF // FF_CHUNK):
            sl = slice(c * FF_CHUNK, (c + 1) * FF_CHUNK)
            g = jnp.dot(xb, wg_ref[:, sl], preferred_element_type=F32)
            u = jnp.dot(xb, wu_ref[:, sl], preferred_element_type=F32)
            h = (g * jax.nn.sigmoid(g) * u).astype(BF16)
            part = jnp.dot(h, wd_ref[sl, :], preferred_element_type=F32)
            if c == 0:
                acc_ref[rows, :] = part
            else:
                acc_ref[rows, :] += part
            yield
        o_ref[rows, :] = x + 0.5 * _rms(acc_ref[rows, :], gout_ref[...])
        yield

    sub = FF_TILE // FF_SUBTILES
    programs = [rows_program(slice(i * sub, (i + 1) * sub)) for i in range(FF_SUBTILES)]
    n_stages = D_FF // FF_CHUNK + 2
    for step in range(n_stages + FF_SUBTILES - 1):
        for i, prog in enumerate(programs):
            if 0 <= step - i < n_stages:
                next(prog)


def _ffn(x, g_in, wg, wu, wd, layer, half, g_out):
    t = x.shape[0]
    row = pl.BlockSpec((FF_TILE, D_MODEL), lambda i: (i, 0))
    weight = lambda m, n: pl.BlockSpec((None, None, m, n), lambda i: (layer, half, 0, 0),
                                       pipeline_mode=pl.Buffered(1))
    return pl.pallas_call(
        _ffn_kernel,
        grid=(t // FF_TILE,),
        in_specs=[row, _const_spec((1, D_MODEL)), weight(D_MODEL, D_FF), weight(D_MODEL, D_FF),
                  weight(D_FF, D_MODEL), _const_spec((1, D_MODEL))],
        out_specs=row,
        out_shape=jax.ShapeDtypeStruct((t, D_MODEL), F32),
        scratch_shapes=[pltpu.VMEM((FF_TILE, D_MODEL), F32)],
        compiler_params=_params("parallel"),
        name="ffn_half",
    )(x, g_in.reshape(1, -1), wg, wu, wd, g_out.reshape(1, -1))


PROJ_CHUNK = 512


def _norm_proj_kernel(n_out, x_ref, g_ref, *refs):
    xb = _rms(x_ref[...], g_ref[...]).astype(BF16)
    for w_ref, o_ref in zip(refs[:n_out], refs[n_out:]):
        n = w_ref.shape[1]
        for c in range(0, n, PROJ_CHUNK):
            sl = slice(c, min(c + PROJ_CHUNK, n))
            o_ref[:, sl] = jnp.dot(xb, w_ref[:, sl], preferred_element_type=F32)


def _norm_proj(x, g, ws, name):
    t, d = x.shape
    row = lambda n: pl.BlockSpec((ROW_TILE, n), lambda i: (i, 0))
    return pl.pallas_call(
        functools.partial(_norm_proj_kernel, len(ws)),
        grid=(t // ROW_TILE,),
        in_specs=[row(d), _const_spec((1, d))] + [_const_spec(w.shape) for w in ws],
        out_specs=[row(w.shape[1]) for w in ws],
        out_shape=[jax.ShapeDtypeStruct((t, w.shape[1]), F32) for w in ws],
        compiler_params=_params("parallel"),
        name=name,
    )(x, g.reshape(1, -1), *ws)


RW_TILE = 512
EV_SUBTILES = 2


def _shift_rows(x, carry, n):
    rolled = pltpu.roll(x, n, 0)
    row = lax.broadcasted_iota(jnp.int32, x.shape, 0)
    out = rolled
    for i in range(n):
        out = jnp.where(row == i, carry[8 - n + i:8 - n + i + 1, :], out)
    return out


def _ev_in_kernel(x_ref, gn_ref, wrw_ref, wrt_ref, mu_ref, w0_ref, w2_ref, a0_ref, a2_ref, g2_ref, kk_ref, ka_ref,
                  ones_ref, r_ref, k_ref, v_ref, ld_ref, a_ref, b_ref, g_ref, prt_ref, carry_ref, p_ref):
    @pl.when(pl.program_id(1) == 0)
    def _():
        carry_ref[...] = jnp.zeros_like(carry_ref)

    def project(xb, rows, w_ref, o_ref):
        n = w_ref.shape[1]
        for c in range(0, n, PROJ_CHUNK):
            sl = slice(c, min(c + PROJ_CHUNK, n))
            o_ref[rows, sl] = jnp.dot(xb, w_ref[:, sl], preferred_element_type=F32)

    def rows_program(rows):
        xb = _rms(x_ref[rows, :], gn_ref[...]).astype(BF16)
        project(xb, rows, wrw_ref, p_ref)
        yield
        project(xb, rows, wrt_ref, prt_ref)
        p = p_ref[rows, :]
        before = carry_ref[...] if rows.start == 0 else p_ref[rows.start - 8:rows.start, :]
        prev = _shift_rows(p, before, 1)
        xm = p + (prev - p) * mu_ref[...]
        w = RW_WIDTH
        r, k, v = xm[:, :w], xm[:, w:2 * w], xm[:, 2 * w:3 * w]
        xw = xm[:, 3 * w:3 * w + RW_DECAY_RANK]
        xa = xm[:, 3 * w + RW_DECAY_RANK:3 * w + RW_DECAY_RANK + RW_AAA_RANK]
        xg = xm[:, 3 * w + RW_DECAY_RANK + RW_AAA_RANK:]
        yield
        wlog = -jax.nn.softplus(-(w0_ref[...] + _dot(jnp.tanh(xw), w2_ref[...]))) - 0.5
        a = jax.nn.sigmoid(a0_ref[...] + _dot(xa, a2_ref[...]))
        kk = k * kk_ref[...]
        ss = _dot_exact_rhs(kk * kk, ones_ref[...], SUM_PASSES)
        kk = kk / jnp.maximum(jnp.sqrt(ss), 1e-12)
        yield
        r_ref[rows, :] = r
        k_ref[rows, :] = k * (1.0 + (a - 1.0) * ka_ref[...])
        v_ref[rows, :] = v
        ld_ref[rows, :] = -jnp.exp(wlog)
        a_ref[rows, :] = -kk
        b_ref[rows, :] = kk * a
        g_ref[rows, :] = _dot(jax.nn.sigmoid(xg), g2_ref[...])
        yield

    sub = RW_TILE // EV_SUBTILES
    for _ in zip(*[rows_program(slice(i * sub, (i + 1) * sub)) for i in range(EV_SUBTILES)]):
        pass
    carry_ref[...] = p_ref[RW_TILE - 8:, :]


def _ev_in(x, gn, w_rw, w_rt, b_, s_, mu, w0, w2, a0, a2, g2, k_k, k_a, ones_blk):
    t, d = x.shape
    nt = s_ // RW_TILE
    row = lambda n: pl.BlockSpec((RW_TILE, n), lambda b, i: (b * nt + i, 0))
    vec = lambda a: a.reshape(1, -1)
    outs = [jax.ShapeDtypeStruct((t, RW_WIDTH), F32)] * 7 + [jax.ShapeDtypeStruct((t, RT_IN), F32)]
    return pl.pallas_call(
        _ev_in_kernel,
        grid=(b_, nt),
        in_specs=[row(d), _const_spec((1, d)), _const_spec(w_rw.shape), _const_spec(w_rt.shape),
                  _const_spec((1, RW_IN)), _const_spec((1, RW_WIDTH)),
                  _const_spec(w2.shape), _const_spec((1, RW_WIDTH)), _const_spec(a2.shape),
                  _const_spec(g2.shape), _const_spec((1, RW_WIDTH)), _const_spec((1, RW_WIDTH)),
                  _const_spec(ones_blk.shape)],
        out_specs=[row(RW_WIDTH)] * 7 + [row(RT_IN)],
        out_shape=outs,
        scratch_shapes=[pltpu.VMEM((8, RW_IN), F32), pltpu.VMEM((RW_TILE, RW_IN), F32)],
        compiler_params=_params("parallel", "arbitrary"),
        name="ev_in_proj",
    )(x, vec(gn), w_rw, w_rt, vec(mu), vec(w0), w2, vec(a0), a2, g2, vec(k_k), vec(k_a), ones_blk)


def _rw_scan_kernel(r_ref, k_ref, v_ref, ld_ref, a_ref, b_ref, g_ref, tri_ref, ones_ref,
                    rk_ref, lng_ref, lnb_ref, o_ref, state_ref):
    c_ = RW_CHUNK
    n_ = RW_HEAD_DIM

    @pl.when(pl.program_id(1) == 0)
    def _():
        state_ref[...] = jnp.zeros_like(state_ref)

    c2 = 2 * c_
    row = lax.broadcasted_iota(jnp.int32, (c2, c2), 0)
    col = lax.broadcasted_iota(jnp.int32, (c2, c2), 1)
    same_head = (row >= c_) == (col >= c_)
    strict = jnp.logical_and(same_head, row > col)
    incl = jnp.logical_and(same_head, row >= col)
    left = lax.broadcasted_iota(jnp.int32, (c_, LANES), 1) < n_

    def block_diag(x):
        return jnp.concatenate([jnp.where(left, x, 0.0), jnp.where(left, 0.0, x)], axis=0)

    pairs = range(RW_HEADS // 2)
    sls = [slice(p * LANES, (p + 1) * LANES) for p in pairs]
    ones2 = ones_ref[...]

    def head_sums(xs):
        parts = [part[:, sl] for x in xs for part in _split(x, SUM_PASSES) for sl in sls]
        prod = jnp.dot(jnp.concatenate(parts, axis=0), ones2, preferred_element_type=F32)
        outs = []
        for i in range(len(xs)):
            slabs = []
            for p in pairs:
                rows = [((SUM_PASSES * i + j) * len(sls) + p) * c_ for j in range(SUM_PASSES)]
                slabs.append(functools.reduce(jnp.add, [prod[r:r + c_] for r in rows]))
            outs.append(jnp.concatenate(slabs, axis=1))
        return outs

    def chunk_program(bb):
        ld = ld_ref[bb]
        r = r_ref[bb]
        k = k_ref[bb]
        v = v_ref[bb]
        cum = _dot_exact_lhs(tri_ref[...], ld, 3)
        mid = cum[c_ // 2 - 1:c_ // 2, :]
        e_in = jnp.exp(cum - mid)
        e_out = jnp.exp(mid - cum)
        r_t = r * e_in
        a_t = a_ref[bb] * jnp.exp(cum - ld - mid)
        b_t = b_ref[bb] * e_out
        k_t = k * e_out
        e_mid = jnp.exp(mid)
        w_all = jnp.exp(cum[c_ - 1:c_, :])
        w_tail = jnp.exp(cum[c_ - 1:c_, :] - mid)
        yield
        s0 = [state_ref[bb, p] for p in pairs]
        ar = [jnp.concatenate([block_diag(a_t[:, sl]), block_diag(r_t[:, sl])], axis=0) for sl in sls]
        bk = [jnp.concatenate([block_diag(b_t[:, sl]), block_diag(k_t[:, sl])], axis=0) for sl in sls]
        vb = [block_diag(v[:, sl]) for sl in sls]
        m1 = [_dot_nt(ar[p], bk[p]) for p in pairs]
        m2 = [_dot_nt(ar[p] * e_mid[:, sls[p]], s0[p]) for p in pairs]
        yield
        l_ab = [jnp.where(strict, m[:c2, :c2], 0.0) for m in m1]
        l_ak = [jnp.where(strict, m[:c2, c2:], 0.0) for m in m1]
        l_r = [jnp.concatenate([jnp.where(incl, m[c2:, :c2], 0.0), jnp.where(incl, m[c2:, c2:], 0.0)], axis=1)
               for m in m1]
        u = [m2[p][:c2] + _dot(l_ak[p], vb[p]) for p in pairs]
        pw = l_ab
        yield
        n_steps = int(math.log2(c_))
        for step in range(n_steps):
            if step < n_steps - 1:
                prod = [_dot(pw[p], jnp.concatenate([pw[p], u[p]], axis=1)) for p in pairs]
                pw = [q[:, :c2] for q in prod]
                u = [u[p] + prod[p][:, c2:] for p in pairs]
            else:
                u = [u[p] + _dot(pw[p], u[p]) for p in pairs]
            yield
        uv = [jnp.concatenate([u[p], vb[p]], axis=0) for p in pairs]
        ys = [m2[p][c2:] + _dot(l_r[p], uv[p]) for p in pairs]
        upd = [_dot_tn(uv[p], bk[p]) for p in pairs]
        yield
        for p in pairs:
            state_ref[bb, p] = s0[p] * w_all[:, sls[p]] + upd[p] * w_tail[:, sls[p]]
        y = jnp.concatenate([ys[p][:c_] + ys[p][c_:] for p in pairs], axis=1)
        sum_y, sum_rk = head_sums([y, r * k * rk_ref[...]])
        d = y - sum_y * (1.0 / n_)
        yield
        var = head_sums([d * d])[0] * (1.0 / n_)
        yn = d * lax.rsqrt(var + RW_LN_EPS) * lng_ref[...] + lnb_ref[...]
        o_ref[bb] = ((yn + sum_rk * v) * g_ref[bb]).astype(o_ref.dtype)
        yield

    for _ in zip(*[chunk_program(bb) for bb in range(RW_ROWS)]):
        pass


def _rw_scan(r, k, v, ld, a, b, g, b_, s_, r_k, ln_g, ln_b, ones_pair, tri):
    t = r.shape[0]
    nc = s_ // RW_CHUNK
    row = pl.BlockSpec((RW_ROWS, RW_CHUNK, RW_WIDTH), lambda bi, i: (bi, i, 0))
    vec = lambda x: x.reshape(1, -1)
    seq = lambda x: x.reshape(b_, s_, RW_WIDTH)
    out = pl.pallas_call(
        _rw_scan_kernel,
        grid=(b_ // RW_ROWS, nc),
        in_specs=[row] * 7 + [_const_spec(tri.shape), _const_spec(ones_pair.shape),
                              _const_spec((1, RW_WIDTH)), _const_spec((1, RW_WIDTH)), _const_spec((1, RW_WIDTH))],
        out_specs=row,
        out_shape=jax.ShapeDtypeStruct((b_, s_, RW_WIDTH), BF16),
        scratch_shapes=[pltpu.VMEM((RW_ROWS, RW_HEADS // 2, 2 * RW_HEAD_DIM, 2 * RW_HEAD_DIM), F32)],
        compiler_params=_params("parallel", "arbitrary"),
        name="rwkv_scan",
    )(seq(r), seq(k), seq(v), seq(ld), seq(a), seq(b), seq(g), tri, ones_pair, vec(r_k), vec(ln_g), vec(ln_b))
    return out.reshape(t, RW_WIDTH)


def _rt_log_gamma(h):
    return math.log1p(-(2.0 ** (-5.0 - h)))


def _retention_kernel(p_ref, cos_ref, sin_ref, gng_ref, gnb_ref, o_ref, state_ref):
    c_ = RT_CHUNK

    @pl.when(pl.program_id(1) == 0)
    def _():
        state_ref[...] = jnp.zeros_like(state_ref)

    cos = cos_ref[...]
    sin = sin_ref[...]
    row = lax.broadcasted_iota(jnp.int32, (c_, c_), 0).astype(F32)
    col = lax.broadcasted_iota(jnp.int32, (c_, c_), 1).astype(F32)
    rel = row - col
    heads = range(RT_HEADS)
    lgs = [_rt_log_gamma(h) for h in heads]
    sls = [slice(h * RT_DIM, (h + 1) * RT_DIM) for h in heads]
    decay = [jnp.where(rel >= 0, jnp.exp(jnp.maximum(rel, 0.0) * lg), 0.0) for lg in lgs]
    xi = [jnp.exp((row + 1.0) * lg) for lg in lgs]
    zeta = [jnp.exp((c_ - 1.0 - row) * lg) for lg in lgs]

    def chunk_program(bb):
        col_of = lambda part, h: slice(part * RT_WIDTH + h * RT_DIM, part * RT_WIDTH + (h + 1) * RT_DIM)
        q = [p_ref[bb, :, col_of(0, h)] for h in heads]
        k = [p_ref[bb, :, col_of(1, h)] for h in heads]
        v = [p_ref[bb, :, col_of(2, h)] for h in heads]
        q = [x * cos + pltpu.roll(x, RT_DIM // 2, 1) * sin for x in q]
        k = [(x * cos + pltpu.roll(x, RT_DIM // 2, 1) * sin) * (RT_DIM ** -0.5) for x in k]
        s0 = [state_ref[bb, h] for h in heads]
        yield
        scores = [_dot_nt(q[h], k[h]) * decay[h] for h in heads]
        cross = [_dot(q[h], s0[h]) * xi[h] for h in heads]
        for h in heads:
            state_ref[bb, h] = s0[h] * math.exp(c_ * lgs[h]) + _dot_tn(k[h] * zeta[h], v[h])
        yield
        o = [_dot(scores[h], v[h]) + cross[h] for h in heads]
        yield
        for h in heads:
            mu = jnp.mean(o[h], axis=-1, keepdims=True)
            d = o[h] - mu
            var = jnp.mean(d * d, axis=-1, keepdims=True)
            on = d * lax.rsqrt(var + EPS) * gng_ref[:, sls[h]] + gnb_ref[:, sls[h]]
            gate = p_ref[bb, :, col_of(3, h)]
            o_ref[bb, :, sls[h]] = (gate * jax.nn.sigmoid(gate) * on).astype(o_ref.dtype)
        yield

    for _ in zip(*[chunk_program(bb) for bb in range(RT_ROWS)]):
        pass


def _retention(p_rt, b_, s_, cos2, sin2, gn_g, gn_b):
    t = p_rt.shape[0]
    nc = s_ // RT_CHUNK
    out = pl.pallas_call(
        _retention_kernel,
        grid=(b_ // RT_ROWS, nc),
        in_specs=[pl.BlockSpec((RT_ROWS, RT_CHUNK, RT_IN), lambda bi, i: (bi, i, 0)),
                  pl.BlockSpec((RT_CHUNK, RT_DIM), lambda bi, i: (i, 0)),
                  pl.BlockSpec((RT_CHUNK, RT_DIM), lambda bi, i: (i, 0)),
                  _const_spec((1, RT_WIDTH)), _const_spec((1, RT_WIDTH))],
        out_specs=pl.BlockSpec((RT_ROWS, RT_CHUNK, RT_WIDTH), lambda bi, i: (bi, i, 0)),
        out_shape=jax.ShapeDtypeStruct((b_, s_, RT_WIDTH), BF16),
        scratch_shapes=[pltpu.VMEM((RT_ROWS, RT_HEADS, RT_DIM, RT_DIM), F32)],
        compiler_params=_params("parallel", "arbitrary"),
        name="retention",
    )(p_rt.reshape(b_, s_, RT_IN), cos2, sin2, gn_g.reshape(1, -1), gn_b.reshape(1, -1))
    return out.reshape(t, RT_WIDTH)


DSA_BLOCKS_PER_TILE = ROW_TILE // Q_BLOCK


def _dsa_prep(c_q, c_kv, kw, qg_ref, kvg_ref, wuq_ref, wuk_ref, wqi_ref,
              ckvn_ref, ckvt_ref, kidx_ref, widx_ref, qlat_ref, qidx_ref):
    cq = _rms(c_q, qg_ref[...]).astype(BF16)
    ckvn = _rms(c_kv, kvg_ref[...])
    ckvn_ref[...] = ckvn.astype(BF16)
    kidx_ref[...] = kw[:, :IDX_DIM].astype(BF16)
    q = jnp.dot(cq, wuq_ref[...], preferred_element_type=F32)
    qi = jnp.dot(cq, wqi_ref[...], preferred_element_type=F32)
    for blk in range(DSA_BLOCKS_PER_TILE):
        rows = slice(blk * Q_BLOCK, (blk + 1) * Q_BLOCK)
        ckvt_ref[blk] = ckvn[rows].T.astype(BF16)
        widx_ref[blk] = kw[rows].T[IDX_DIM:IDX_DIM + IDX_HEADS] * ((IDX_HEADS * IDX_DIM) ** -0.5)
    for h in range(DSA_HEADS):
        qh = q[:, h * DSA_HEAD_DIM:(h + 1) * DSA_HEAD_DIM]
        ql = (_dot(qh, wuk_ref[h]) * (DSA_HEAD_DIM ** -0.5)).astype(BF16)
        qih = qi[:, h * IDX_DIM:(h + 1) * IDX_DIM].astype(BF16)
        for blk in range(DSA_BLOCKS_PER_TILE):
            rows = slice(blk * Q_BLOCK, (blk + 1) * Q_BLOCK)
            qlat_ref[blk, h] = ql[rows]
            qidx_ref[blk, h] = qih[rows]


def _fold_rows(x, op):
    return functools.reduce(op, [x[i:i + 8] for i in range(0, x.shape[0], 8)])


def _bit_transpose32(words):
    a = list(words)
    j, m = 16, 0x0000FFFF
    while j:
        k = 0
        while k < 32:
            t = (a[k] ^ lax.shift_right_logical(a[k + j], jnp.int32(j))) & jnp.int32(m)
            a[k] = a[k] ^ t
            a[k + j] = a[k + j] ^ lax.shift_left(t, jnp.int32(j))
            k = (k + j + 1) & ~j
        j >>= 1
        m ^= m << j
    return a


def _dsa_kernel(top_k, qidx_ref, widx_ref, qlat_ref, kidx_ref, ckv_ref, ckvt_ref, wuv_ref, tril_ref, o_ref,
                key_ref, planes_ref, s_ref, acc_ref):
    qb = Q_BLOCK
    nh = DSA_HEADS
    j = pl.program_id(1)
    n_chunks = j + 1

    keyi = lax.broadcasted_iota(jnp.int32, (qb, qb), 0)
    qryi = lax.broadcasted_iota(jnp.int32, (qb, qb), 1)
    lanes = lambda h: slice(h * qb, (h + 1) * qb)

    def key_chunk(ref, c):
        return ref[pl.ds(pl.multiple_of(c * qb, qb), qb), :]

    @pl.when(j == 0)
    def _():
        planes_ref[...] = jnp.zeros_like(planes_ref)

    w_idx = widx_ref[0]
    q_idx = qidx_ref[0].reshape(IDX_HEADS * qb, IDX_DIM)
    q_lat = qlat_ref[0].reshape(nh * qb, DSA_KV_RANK)

    n_pairs = (n_chunks + 1) // 2

    def score_chunk(c):
        logits = lax.dot_general(key_chunk(kidx_ref, c), q_idx, (((1,), (1,)), ((), ())),
                                 preferred_element_type=F32)
        score = jnp.zeros((qb, qb), F32)
        for h in range(IDX_HEADS):
            score = score + jnp.maximum(logits[:, lanes(h)], 0.0) * w_idx[h:h + 1, :]
        bits = lax.bitcast_convert_type(score, jnp.int32)
        key = jnp.where(bits < 0, bits ^ jnp.int32(0x7FFFFFFF), bits)
        causal = keyi + c * qb <= qryi + j * qb
        key = jnp.where(causal, key, jnp.int32(INT_MIN))
        key_ref[c] = key
        return key

    def score_body(pi, carry):
        keys = [score_chunk(2 * pi), score_chunk(2 * pi + 1)]
        words = [k[r:r + 8] ^ jnp.int32(INT_MIN) for k in keys for r in range(0, qb, 8)]
        for idx, plane in enumerate(_bit_transpose32(words)):
            planes_ref[31 - idx, pi] = plane
        return carry

    lax.fori_loop(0, n_pairs, score_body, 0)

    n_pairs_max = planes_ref.shape[1]
    live = tuple(jnp.where(p < n_pairs, jnp.full((8, qb), -1, jnp.int32), jnp.zeros((8, qb), jnp.int32))
                 for p in range(n_pairs_max))

    def bit_body(i, carry):
        thr_u, above, eq = carry
        plane = planes_ref[31 - i]
        hits = [eq[p] & plane[p] for p in range(n_pairs_max)]
        cnt = functools.reduce(jnp.add, [lax.population_count(h) for h in hits])
        cnt = above + jnp.sum(cnt, axis=0, keepdims=True)
        accept = cnt >= top_k
        thr_u = jnp.where(accept, thr_u | lax.shift_left(jnp.int32(1), 31 - i), thr_u)
        above = jnp.where(accept, above, cnt)
        eq = tuple(jnp.where(accept, h, e ^ h) for e, h in zip(eq, hits))
        return thr_u, above, eq

    zero_row = jnp.zeros((1, qb), jnp.int32)
    thr_u, above, _ = lax.fori_loop(0, 32, bit_body, (zero_row, zero_row, live))
    thr = thr_u ^ jnp.int32(INT_MIN)
    need = (top_k - above).astype(F32)

    def pair_logits(t, taken):
        tops = [None] * nh
        for c in (2 * t, 2 * t + 1):
            key = key_ref[c]
            causal = keyi + c * qb <= qryi + j * qb
            eq = jnp.logical_and(key == thr, causal)
            eq_f = jnp.where(eq, 1.0, 0.0)
            rank = taken + jnp.dot(tril_ref[...], eq_f.astype(BF16), preferred_element_type=F32)
            sel = jnp.logical_or(jnp.logical_and(key > thr, causal), jnp.logical_and(eq, rank <= need))
            s = lax.dot_general(key_chunk(ckv_ref, c), q_lat, (((1,), (1,)), ((), ())),
                                preferred_element_type=F32)
            for h in range(nh):
                sh = jnp.where(sel, s[:, lanes(h)], NEG_BIG)
                s_ref[c, :, lanes(h)] = sh
                top = _fold_rows(sh, jnp.maximum)
                tops[h] = top if tops[h] is None else jnp.maximum(tops[h], top)
            taken = taken + jnp.sum(eq_f, axis=0, keepdims=True)
        return taken, tuple(jnp.max(top, axis=0, keepdims=True) for top in tops)

    def pair_values(t, tops, peak, sums):
        ps, scales, new_peak, new_sums = [], [], [], []
        for h in range(nh):
            pk = jnp.maximum(peak[h], tops[h])
            scale = jnp.exp(peak[h] - pk)
            pa = jnp.exp(s_ref[2 * t, :, lanes(h)] - pk)
            pb = jnp.exp(s_ref[2 * t + 1, :, lanes(h)] - pk)
            new_sums.append(sums[h] * scale + _fold_rows(pa + pb, jnp.add))
            ps.append(jnp.concatenate([pa.astype(BF16), pb.astype(BF16)], axis=0))
            scales.append(scale)
            new_peak.append(pk)
        ckvt_pair = jnp.concatenate([ckvt_ref[2 * t], ckvt_ref[2 * t + 1]], axis=1)
        upd = jnp.dot(ckvt_pair, jnp.concatenate(ps, axis=1), preferred_element_type=F32)
        acc_ref[...] = acc_ref[...] * jnp.concatenate(scales, axis=1) + upd
        return tuple(new_peak), tuple(new_sums)

    acc_ref[...] = jnp.zeros_like(acc_ref)

    def attn_body(t, carry):
        taken, tops, peak, sums = carry
        peak, sums = pair_values(t, tops, peak, sums)
        taken, tops = pair_logits(jnp.minimum(t + 1, n_pairs - 1), taken)
        return taken, tops, peak, sums

    taken, tops = pair_logits(0, jnp.zeros((1, qb), F32))
    init = (taken, tops, tuple(jnp.full((1, qb), NEG_BIG, F32) for _ in range(nh)),
            tuple(jnp.zeros((8, qb), F32) for _ in range(nh)))
    _, _, _, sums = lax.fori_loop(0, n_pairs, attn_body, init)

    outs = []
    for h in range(nh):
        o_lat_t = acc_ref[:, lanes(h)] / jnp.sum(sums[h], axis=0, keepdims=True)
        outs.append(_dot_tn(o_lat_t, wuv_ref[h]))
    o_ref[...] = jnp.concatenate(outs, axis=1).astype(o_ref.dtype)


def _dsa(q_idx, w_idx, q_lat, k_idx, ckv_n, ckv_t, w_uv, tril, b_, s_):
    nb = s_ // Q_BLOCK
    t = b_ * s_
    top_k = min(TOPK_MAX, s_ // 4)
    blk = lambda n: pl.BlockSpec((Q_BLOCK, n), lambda bi, i: (bi * nb + i, 0))
    blk3 = lambda m, n: pl.BlockSpec((1, m, n), lambda bi, i: (bi * nb + i, 0, 0))
    blk4 = lambda n: pl.BlockSpec((1, DSA_HEADS, Q_BLOCK, n), lambda bi, i: (bi * nb + i, 0, 0, 0))
    seq = lambda n: pl.BlockSpec((s_, n), lambda bi, i: (bi, 0))
    stacked = DSA_HEADS * Q_BLOCK
    return pl.pallas_call(
        functools.partial(_dsa_kernel, top_k),
        grid=(b_, nb),
        in_specs=[blk4(IDX_DIM), blk3(IDX_HEADS, Q_BLOCK), blk4(DSA_KV_RANK),
                  seq(IDX_DIM), seq(DSA_KV_RANK),
                  pl.BlockSpec((nb, DSA_KV_RANK, Q_BLOCK), lambda bi, i: (bi, 0, 0)),
                  _const_spec(w_uv.shape), _const_spec(tril.shape)],
        out_specs=blk(DSA_WIDTH),
        out_shape=jax.ShapeDtypeStruct((t, DSA_WIDTH), BF16),
        scratch_shapes=[pltpu.VMEM((nb, Q_BLOCK, Q_BLOCK), jnp.int32),
                        pltpu.VMEM((32, nb // 2, 8, Q_BLOCK), jnp.int32),
                        pltpu.VMEM((nb, Q_BLOCK, stacked), F32),
                        pltpu.VMEM((DSA_KV_RANK, stacked), F32)],
        compiler_params=_params("parallel", "arbitrary"),
        name="dsa_attn",
    )(q_idx, w_idx, q_lat, k_idx, ckv_n, ckv_t, w_uv, tril)


SC_TILE = 512


def _od_in_kernel(x_ref, gn_ref, wq_ref, wkv_ref, wkw_ref, wsc_ref, cw_ref, cb_ref,
                  qg_ref, kvg_ref, wuq_ref, wuk_ref, wqi_ref,
                  ckvn_ref, ckvt_ref, kidx_ref, widx_ref, qlat_ref, qidx_ref, yd_ref, carry_ref, p_ref):
    @pl.when(pl.program_id(1) == 0)
    def _():
        carry_ref[...] = jnp.zeros_like(carry_ref)

    xb = _rms(x_ref[...], gn_ref[...]).astype(BF16)
    c_q, c_kv, kw = (jnp.dot(xb, w_ref[...], preferred_element_type=F32) for w_ref in (wq_ref, wkv_ref, wkw_ref))
    for c in range(0, 3 * SC_WIDTH, PROJ_CHUNK):
        sl = slice(c, c + PROJ_CHUNK)
        p_ref[:, sl] = jnp.dot(xb, wsc_ref[:, sl], preferred_element_type=F32)
    _dsa_prep(c_q, c_kv, kw, qg_ref, kvg_ref, wuq_ref, wuk_ref, wqi_ref,
              ckvn_ref, ckvt_ref, kidx_ref, widx_ref, qlat_ref, qidx_ref)

    h = p_ref[:, :SC_WIDTH]
    gate_b = p_ref[:, SC_WIDTH:2 * SC_WIDTH]
    gate_c = p_ref[:, 2 * SC_WIDTH:]
    u = gate_c * h
    carry = carry_ref[...]
    y = u * cw_ref[2:3, :] + _shift_rows(u, carry, 1) * cw_ref[1:2, :] + _shift_rows(u, carry, 2) * cw_ref[0:1, :]
    carry_ref[...] = u[SC_TILE - 8:, :]
    yd_ref[...] = (gate_b * (y + cb_ref[...])).astype(yd_ref.dtype)


def _od_in(x, gn, w_q, w_kv, w_kw, w_sc, b_, s_, conv_w, conv_b, q_g, kv_g, w_uq, w_uk, w_qi):
    t, d = x.shape
    nt = s_ // SC_TILE
    row = lambda n: pl.BlockSpec((SC_TILE, n), lambda bi, i: (bi * nt + i, 0))
    blk3 = lambda m, n: pl.BlockSpec((DSA_BLOCKS_PER_TILE, m, n), lambda bi, i: (bi * nt + i, 0, 0))
    blk4 = lambda n: pl.BlockSpec((DSA_BLOCKS_PER_TILE, DSA_HEADS, Q_BLOCK, n), lambda bi, i: (bi * nt + i, 0, 0, 0))
    consts = [w_q, w_kv, w_kw, w_sc, jnp.pad(conv_w, ((0, 8 - SC_KERNEL), (0, 0))), conv_b.reshape(1, -1),
              q_g.reshape(1, -1), kv_g.reshape(1, -1), w_uq, w_uk, w_qi]
    return pl.pallas_call(
        _od_in_kernel,
        grid=(b_, nt),
        in_specs=[row(d), _const_spec((1, d))] + [_const_spec(c.shape) for c in consts],
        out_specs=[row(DSA_KV_RANK), blk3(DSA_KV_RANK, Q_BLOCK), row(IDX_DIM), blk3(IDX_HEADS, Q_BLOCK),
                   blk4(DSA_KV_RANK), blk4(IDX_DIM), row(SC_WIDTH)],
        out_shape=[jax.ShapeDtypeStruct((t, DSA_KV_RANK), BF16),
                   jax.ShapeDtypeStruct((t // Q_BLOCK, DSA_KV_RANK, Q_BLOCK), BF16),
                   jax.ShapeDtypeStruct((t, IDX_DIM), BF16),
                   jax.ShapeDtypeStruct((t // Q_BLOCK, IDX_HEADS, Q_BLOCK), F32),
                   jax.ShapeDtypeStruct((t // Q_BLOCK, DSA_HEADS, Q_BLOCK, DSA_KV_RANK), BF16),
                   jax.ShapeDtypeStruct((t // Q_BLOCK, IDX_HEADS, Q_BLOCK, IDX_DIM), BF16),
                   jax.ShapeDtypeStruct((t, SC_WIDTH), BF16)],
        scratch_shapes=[pltpu.VMEM((8, SC_WIDTH), F32), pltpu.VMEM((SC_TILE, 3 * SC_WIDTH), F32)],
        compiler_params=_params("parallel", "arbitrary"),
        name="od_in_proj",
    )(x, gn.reshape(1, -1), *consts)


def _xattn_kernel(y1_ref, y2_ref, w1_ref, w2_ref, gm_ref, x_ref, gq_ref, wq_ref, k_ref, v_ref, wo_ref, go_ref,
                  o_ref, att_ref):
    def rows_program(rows):
        mix = (jnp.dot(y1_ref[rows, :], w1_ref[...], preferred_element_type=F32)
               + jnp.dot(y2_ref[rows, :], w2_ref[...], preferred_element_type=F32))
        yield
        x = x_ref[rows, :] + _rms(mix, gm_ref[...])
        q = jnp.dot(_rms(x, gq_ref[...]).astype(BF16), wq_ref[...], preferred_element_type=F32)
        yield
        for h in range(XA_HEADS):
            sl = slice(h * XA_HEAD_DIM, (h + 1) * XA_HEAD_DIM)
            s = _dot_nt(q[:, sl], k_ref[:, sl]) * (XA_HEAD_DIM ** -0.5)
            s = s - jnp.max(s, axis=-1, keepdims=True)
            p = jnp.exp(s)
            p = p / jnp.sum(p, axis=-1, keepdims=True)
            att_ref[rows, sl] = _dot(p, v_ref[:, sl])
            yield
        hout = jnp.dot(att_ref[rows, :].astype(BF16), wo_ref[...], preferred_element_type=F32)
        yield
        o_ref[rows, :] = x + _rms(hout, go_ref[...])
        yield

    sub = ROW_TILE // XA_SUBTILES
    for _ in zip(*[rows_program(slice(i * sub, (i + 1) * sub)) for i in range(XA_SUBTILES)]):
        pass


def _xattn(y1, y2, w1, w2, g_mix, x, g_q, wq, k_mem, v_mem, wo, g_o, b_, s_):
    t = x.shape[0]
    nt = s_ // ROW_TILE
    row = lambda n: pl.BlockSpec((ROW_TILE, n), lambda bi, i: (bi * nt + i, 0))
    mem = pl.BlockSpec((MEM_LEN, XA_WIDTH), lambda bi, i: (bi, 0))
    vec = lambda g: g.reshape(1, -1)
    return pl.pallas_call(
        _xattn_kernel,
        grid=(b_, nt),
        in_specs=[row(y1.shape[1]), row(y2.shape[1]), _const_spec(w1.shape), _const_spec(w2.shape),
                  _const_spec((1, D_MODEL)), row(D_MODEL), _const_spec((1, D_MODEL)), _const_spec(wq.shape),
                  mem, mem, _const_spec(wo.shape), _const_spec((1, D_MODEL))],
        out_specs=row(D_MODEL),
        out_shape=jax.ShapeDtypeStruct((t, D_MODEL), F32),
        scratch_shapes=[pltpu.VMEM((ROW_TILE, XA_WIDTH), F32)],
        compiler_params=_params("parallel", "parallel"),
        name="mix_out_xattn",
    )(y1, y2, w1, w2, vec(g_mix), x, vec(g_q), wq, k_mem, v_mem, wo, vec(g_o))


def _block_diag(n_blocks, size, value):
    return np.kron(np.eye(n_blocks, dtype=np.float32), np.full((size, size), value, np.float32))


def _rope_tables(s_):
    half = RT_DIM // 2
    inv_freq = RT_ROPE_BASE ** (-jnp.arange(half, dtype=F32) / half)
    ang = jnp.arange(s_).astype(F32)[:, None] * inv_freq[None, :]
    cos, sin = jnp.cos(ang), jnp.sin(ang)
    return jnp.concatenate([cos, cos], axis=-1), jnp.concatenate([-sin, sin], axis=-1)


def kernel(x, mem, norm_g, mem_norm_g, ffn_w_gate, ffn_w_up, ffn_w_down, xa_wq, xa_wk, xa_wv, xa_wo, ev_w_in, ev_w_out, rw_mu, rw_w0, rw_w2, rw_a0, rw_a2, rw_g2, rw_k_k, rw_k_a, rw_r_k, rw_ln_g, rw_ln_b, rt_gn_g, rt_gn_b, od_w_in, od_w_out, dsa_q_norm_g, dsa_kv_norm_g, dsa_w_uq, dsa_w_uk, dsa_w_uv, dsa_w_qi, sc_conv_w, sc_conv_b):
    b_, s_, d_ = x.shape
    depth = norm_g.shape[0]
    t = b_ * s_
    bf = lambda w: w.astype(BF16)

    ones_blk = jnp.asarray(_block_diag(RW_HEADS, RW_HEAD_DIM, 1.0), BF16)
    ones_pair = jnp.asarray(_block_diag(2, RW_HEAD_DIM, 1.0), BF16)
    tri_rw = jnp.asarray(np.tril(np.ones((RW_CHUNK, RW_CHUNK), np.float32)), BF16)
    tril_dsa = jnp.asarray(np.tril(np.ones((Q_BLOCK, Q_BLOCK), np.float32)), BF16)
    cos2, sin2 = _rope_tables(s_)

    w_gate, w_up, w_down = bf(ffn_w_gate), bf(ffn_w_up), bf(ffn_w_down)
    xf = x.reshape(t, d_)
    mem_f = mem.reshape(b_ * MEM_LEN, d_)
    for l in range(depth):
        ng = norm_g[l]
        i = l // 2
        xf = _ffn(xf, ng[0], w_gate, w_up, w_down, l, 0, ng[1])
        if l % 2 == 0:
            w_in = bf(ev_w_in[i])
            r, k, v, ld, a, b, g, p_rt = _ev_in(xf, ng[2], w_in[:, :RW_IN], w_in[:, RW_IN:], b_, s_, rw_mu[i],
                                                rw_w0[i], rw_w2[i], rw_a0[i], rw_a2[i], rw_g2[i], rw_k_k[i],
                                                rw_k_a[i], ones_blk)
            y_a = _rw_scan(r, k, v, ld, a, b, g, b_, s_, rw_r_k[i], rw_ln_g[i], rw_ln_b[i],
                           ones_pair, tri_rw)
            y_b = _retention(p_rt, b_, s_, cos2, sin2, rt_gn_g[i], rt_gn_b[i])
            ys, w_out, split = (y_a, y_b), bf(ev_w_out[i]), RW_WIDTH
        else:
            w_in = od_w_in[i]
            kw_w = jnp.pad(w_in[:, DSA_Q_RANK + DSA_KV_RANK:DSA_IN], ((0, 0), (0, LANES - IDX_DIM - IDX_HEADS)))
            ckv_n, ckv_t, k_idx, w_idx, q_lat, q_idx, y_d = _od_in(
                xf, ng[2], bf(w_in[:, :DSA_Q_RANK]), bf(w_in[:, DSA_Q_RANK:DSA_Q_RANK + DSA_KV_RANK]),
                bf(kw_w), bf(w_in[:, DSA_IN:]), b_, s_, sc_conv_w[i], sc_conv_b[i],
                dsa_q_norm_g[i], dsa_kv_norm_g[i],
                bf(dsa_w_uq[i].reshape(DSA_Q_RANK, DSA_WIDTH)), bf(dsa_w_uk[i]),
                bf(dsa_w_qi[i].reshape(DSA_Q_RANK, IDX_HEADS * IDX_DIM)))
            y_c = _dsa(q_idx, w_idx, q_lat, k_idx, ckv_n, ckv_t, bf(dsa_w_uv[i]), tril_dsa, b_, s_)
            ys, w_out, split = (y_c, y_d), bf(od_w_out[i]), DSA_WIDTH
        k_mem, v_mem = _norm_proj(mem_f, mem_norm_g, [bf(xa_wk[l]), bf(xa_wv[l])], "mem_kv_proj")
        xf = _xattn(ys[0], ys[1], w_out[:split], w_out[split:], ng[3], xf, ng[4], bf(xa_wq[l]), k_mem, v_mem,
                    bf(xa_wo[l]), ng[5], b_, s_)
        xf = _ffn(xf, ng[6], w_gate, w_up, w_down, l, 1, ng[7])
    return xf.reshape(b_, s_, d_)
```

```python
import functools
import math

import numpy as np
import jax
import jax.numpy as jnp
from jax import lax
from jax.experimental import pallas as pl
from jax.experimental.pallas import tpu as pltpu

F32 = jnp.float32
BF16 = jnp.bfloat16

D_MODEL = 1024
D_FF = 2816
EPS = 1e-6
MEM_LEN = 256
RW_HEADS = 8
RW_HEAD_DIM = 64
RW_WIDTH = RW_HEADS * RW_HEAD_DIM
RW_DECAY_RANK = 64
RW_AAA_RANK = 64
RW_GATE_RANK = 128
RW_LN_EPS = 64e-5
RW_IN = 3 * RW_WIDTH + RW_DECAY_RANK + RW_AAA_RANK + RW_GATE_RANK
RW_CHUNK = 64
RW_ROWS = 4
RT_HEADS = 4
RT_DIM = 128
RT_WIDTH = RT_HEADS * RT_DIM
RT_CHUNK = 128
RT_ROWS = 4
RT_ROPE_BASE = 10000.0
RT_IN = 4 * RT_WIDTH
DSA_HEADS = 8
DSA_HEAD_DIM = 64
DSA_WIDTH = DSA_HEADS * DSA_HEAD_DIM
DSA_Q_RANK = 256
DSA_KV_RANK = 128
IDX_HEADS = 8
IDX_DIM = 64
TOPK_MAX = 256
Q_BLOCK = 128
DSA_IN = DSA_Q_RANK + DSA_KV_RANK + IDX_DIM + IDX_HEADS
SC_WIDTH = 512
SC_KERNEL = 3
XA_HEADS = 4
XA_HEAD_DIM = 128
XA_WIDTH = XA_HEADS * XA_HEAD_DIM
XA_SUBTILES = 2

LANES = 128
ROW_TILE = 512
VMEM_LIMIT = 56 * 1024 * 1024
SUM_PASSES = 2
INT_MIN = -2 ** 31
NEG_BIG = -1e30


def _params(*sem):
    return pltpu.CompilerParams(dimension_semantics=sem, vmem_limit_bytes=VMEM_LIMIT)


def _rms(x, g):
    return x * lax.rsqrt(jnp.mean(x * x, axis=-1, keepdims=True) + EPS) * g


def _dot(a, b):
    return jnp.dot(a.astype(BF16), b.astype(BF16), preferred_element_type=F32)


def _dot_nt(a, b):
    return lax.dot_general(a.astype(BF16), b.astype(BF16), (((1,), (1,)), ((), ())),
                           preferred_element_type=F32)


def _dot_tn(a, b):
    return lax.dot_general(a.astype(BF16), b.astype(BF16), (((0,), (0,)), ((), ())),
                           preferred_element_type=F32)


def _split(x, n):
    parts = []
    for _ in range(n):
        part = x.astype(BF16)
        parts.append(part)
        x = x - part.astype(F32)
    return parts


def _dot_exact_rhs(x, w_bf16, n):
    return functools.reduce(jnp.add, [jnp.dot(p, w_bf16, preferred_element_type=F32) for p in _split(x, n)])


def _dot_exact_lhs(w_bf16, x, n):
    return functools.reduce(jnp.add, [jnp.dot(w_bf16, p, preferred_element_type=F32) for p in _split(x, n)])


def _const_spec(shape):
    nd = len(shape)
    return pl.BlockSpec(shape, lambda *_: (0,) * nd, pipeline_mode=pl.Buffered(1))


FF_CHUNK = 256
FF_TILE = 1024
FF_SUBTILES = 2


def _ffn_kernel(x_ref, gin_ref, wg_ref, wu_ref, wd_ref, gout_ref, o_ref, acc_ref):
    def rows_program(rows):
        x = x_ref[rows, :]
        xb = _rms(x, gin_ref[...]).astype(BF16)
        yield
        for c in range(D_FF // FF_CHUNK):
            sl = slice(c * FF_CHUNK, (c + 1) * FF_CHUNK)
            g = jnp.dot(xb, wg_ref[:, sl], preferred_element_type=F32)
            u = jnp.dot(xb, wu_ref[:, sl], preferred_element_type=F32)
            h = (g * jax.nn.sigmoid(g) * u).astype(BF16)
            part = jnp.dot(h, wd_ref[sl, :], preferred_element_type=F32)
            if c == 0:
                acc_ref[rows, :] = part
            else:
                acc_ref[rows, :] += part
            yield
        o_ref[rows, :] = x + 0.5 * _rms(acc_ref[rows, :], gout_ref[...])
        yield

    sub = FF_TILE // FF_SUBTILES
    programs = [rows_program(slice(i * sub, (i + 1) * sub)) for i in range(FF_SUBTILES)]
    n_stages = D_FF // FF_CHUNK + 2
    for step in range(n_stages + FF_SUBTILES - 1):
        for i, prog in enumerate(programs):
            if 0 <= step - i < n_stages:
                next(prog)


def _ffn(x, g_in, wg, wu, wd, layer, half, g_out):
    t = x.shape[0]
    row = pl.BlockSpec((FF_TILE, D_MODEL), lambda i: (i, 0))
    weight = lambda m, n: pl.BlockSpec((None, None, m, n), lambda i: (layer, half, 0, 0),
                                       pipeline_mode=pl.Buffered(1))
    return pl.pallas_call(
        _ffn_kernel,
        grid=(t // FF_TILE,),
        in_specs=[row, _const_spec((1, D_MODEL)), weight(D_MODEL, D_FF), weight(D_MODEL, D_FF),
                  weight(D_FF, D_MODEL), _const_spec((1, D_MODEL))],
        out_specs=row,
        out_shape=jax.ShapeDtypeStruct((t, D_MODEL), F32),
        scratch_shapes=[pltpu.VMEM((FF_TILE, D_MODEL), F32)],
        compiler_params=_params("parallel"),
        name="ffn_half",
    )(x, g_in.reshape(1, -1), wg, wu, wd, g_out.reshape(1, -1))


PROJ_CHUNK = 512


def _norm_proj_kernel(n_out, x_ref, g_ref, *refs):
    xb = _rms(x_ref[...], g_ref[...]).astype(BF16)
    for w_ref, o_ref in zip(refs[:n_out], refs[n_out:]):
        n = w_ref.shape[1]
        for c in range(0, n, PROJ_CHUNK):
            sl = slice(c, min(c + PROJ_CHUNK, n))
            o_ref[:, sl] = jnp.dot(xb, w_ref[:, sl], preferred_element_type=F32)


def _norm_proj(x, g, ws, name):
    t, d = x.shape
    row = lambda n: pl.BlockSpec((ROW_TILE, n), lambda i: (i, 0))
    return pl.pallas_call(
        functools.partial(_norm_proj_kernel, len(ws)),
        grid=(t // ROW_TILE,),
        in_specs=[row(d), _const_spec((1, d))] + [_const_spec(w.shape) for w in ws],
        out_specs=[row(w.shape[1]) for w in ws],
        out_shape=[jax.ShapeDtypeStruct((t, w.shape[1]), F32) for w in ws],
        compiler_params=_params("parallel"),
        name=name,
    )(x, g.reshape(1, -1), *ws)


RW_TILE = 512
EV_SUBTILES = 2


def _shift_rows(x, carry, n):
    rolled = pltpu.roll(x, n, 0)
    row = lax.broadcasted_iota(jnp.int32, x.shape, 0)
    out = rolled
    for i in range(n):
        out = jnp.where(row == i, carry[8 - n + i:8 - n + i + 1, :], out)
    return out


def _ev_in_kernel(x_ref, gn_ref, wrw_ref, wrt_ref, mu_ref, w0_ref, w2_ref, a0_ref, a2_ref, g2_ref, kk_ref, ka_ref,
                  ones_ref, r_ref, k_ref, v_ref, ld_ref, a_ref, b_ref, g_ref, prt_ref, carry_ref, p_ref):
    @pl.when(pl.program_id(1) == 0)
    def _():
        carry_ref[...] = jnp.zeros_like(carry_ref)

    def project(xb, rows, w_ref, o_ref):
        n = w_ref.shape[1]
        for c in range(0, n, PROJ_CHUNK):
            sl = slice(c, min(c + PROJ_CHUNK, n))
            o_ref[rows, sl] = jnp.dot(xb, w_ref[:, sl], preferred_element_type=F32)

    def rows_program(rows):
        xb = _rms(x_ref[rows, :], gn_ref[...]).astype(BF16)
        project(xb, rows, wrw_ref, p_ref)
        yield
        project(xb, rows, wrt_ref, prt_ref)
        p = p_ref[rows, :]
        before = carry_ref[...] if rows.start == 0 else p_ref[rows.start - 8:rows.start, :]
        prev = _shift_rows(p, before, 1)
        xm = p + (prev - p) * mu_ref[...]
        w = RW_WIDTH
        r, k, v = xm[:, :w], xm[:, w:2 * w], xm[:, 2 * w:3 * w]
        xw = xm[:, 3 * w:3 * w + RW_DECAY_RANK]
        xa = xm[:, 3 * w + RW_DECAY_RANK:3 * w + RW_DECAY_RANK + RW_AAA_RANK]
        xg = xm[:, 3 * w + RW_DECAY_RANK + RW_AAA_RANK:]
        yield
        wlog = -jax.nn.softplus(-(w0_ref[...] + _dot(jnp.tanh(xw), w2_ref[...]))) - 0.5
        a = jax.nn.sigmoid(a0_ref[...] + _dot(xa, a2_ref[...]))
        kk = k * kk_ref[...]
        ss = _dot_exact_rhs(kk * kk, ones_ref[...], SUM_PASSES)
        kk = kk / jnp.maximum(jnp.sqrt(ss), 1e-12)
        yield
        r_ref[rows, :] = r
        k_ref[rows, :] = k * (1.0 + (a - 1.0) * ka_ref[...])
        v_ref[rows, :] = v
        ld_ref[rows, :] = -jnp.exp(wlog)
        a_ref[rows, :] = -kk
        b_ref[rows, :] = kk * a
        g_ref[rows, :] = _dot(jax.nn.sigmoid(xg), g2_ref[...])
        yield

    sub = RW_TILE // EV_SUBTILES
    for _ in zip(*[rows_program(slice(i * sub, (i + 1) * sub)) for i in range(EV_SUBTILES)]):
        pass
    carry_ref[...] = p_ref[RW_TILE - 8:, :]


def _ev_in(x, gn, w_rw, w_rt, b_, s_, mu, w0, w2, a0, a2, g2, k_k, k_a, ones_blk):
    t, d = x.shape
    nt = s_ // RW_TILE
    row = lambda n: pl.BlockSpec((RW_TILE, n), lambda b, i: (b * nt + i, 0))
    vec = lambda a: a.reshape(1, -1)
    outs = [jax.ShapeDtypeStruct((t, RW_WIDTH), F32)] * 7 + [jax.ShapeDtypeStruct((t, RT_IN), F32)]
    return pl.pallas_call(
        _ev_in_kernel,
        grid=(b_, nt),
        in_specs=[row(d), _const_spec((1, d)), _const_spec(w_rw.shape), _const_spec(w_rt.shape),
                  _const_spec((1, RW_IN)), _const_spec((1, RW_WIDTH)),
                  _const_spec(w2.shape), _const_spec((1, RW_WIDTH)), _const_spec(a2.shape),
                  _const_spec(g2.shape), _const_spec((1, RW_WIDTH)), _const_spec((1, RW_WIDTH)),
                  _const_spec(ones_blk.shape)],
        out_specs=[row(RW_WIDTH)] * 7 + [row(RT_IN)],
        out_shape=outs,
        scratch_shapes=[pltpu.VMEM((8, RW_IN), F32), pltpu.VMEM((RW_TILE, RW_IN), F32)],
        compiler_params=_params("parallel", "arbitrary"),
        name="ev_in_proj",
    )(x, vec(gn), w_rw, w_rt, vec(mu), vec(w0), w2, vec(a0), a2, g2, vec(k_k), vec(k_a), ones_blk)


def _rw_scan_kernel(r_ref, k_ref, v_ref, ld_ref, a_ref, b_ref, g_ref, tri_ref, ones_ref,
                    rk_ref, lng_ref, lnb_ref, o_ref, state_ref):
    c_ = RW_CHUNK
    n_ = RW_HEAD_DIM

    @pl.when(pl.program_id(1) == 0)
    def _():
        state_ref[...] = jnp.zeros_like(state_ref)

    c2 = 2 * c_
    row = lax.broadcasted_iota(jnp.int32, (c2, c2), 0)
    col = lax.broadcasted_iota(jnp.int32, (c2, c2), 1)
    same_head = (row >= c_) == (col >= c_)
    strict = jnp.logical_and(same_head, row > col)
    incl = jnp.logical_and(same_head, row >= col)
    left = lax.broadcasted_iota(jnp.int32, (c_, LANES), 1) < n_

    def block_diag(x):
        return jnp.concatenate([jnp.where(left, x, 0.0), jnp.where(left, 0.0, x)], axis=0)

    pairs = range(RW_HEADS // 2)
    sls = [slice(p * LANES, (p + 1) * LANES) for p in pairs]
    ones2 = ones_ref[...]

    def head_sums(xs):
        parts = [part[:, sl] for x in xs for part in _split(x, SUM_PASSES) for sl in sls]
        prod = jnp.dot(jnp.concatenate(parts, axis=0), ones2, preferred_element_type=F32)
        outs = []
        for i in range(len(xs)):
            slabs = []
            for p in pairs:
                rows = [((SUM_PASSES * i + j) * len(sls) + p) * c_ for j in range(SUM_PASSES)]
                slabs.append(functools.reduce(jnp.add, [prod[r:r + c_] for r in rows]))
            outs.append(jnp.concatenate(slabs, axis=1))
        return outs

    def chunk_program(bb):
        ld = ld_ref[bb]
        r = r_ref[bb]
        k = k_ref[bb]
        v = v_ref[bb]
        cum = _dot_exact_lhs(tri_ref[...], ld, 3)
        mid = cum[c_ // 2 - 1:c_ // 2, :]
        e_in = jnp.exp(cum - mid)
        e_out = jnp.exp(mid - cum)
        r_t = r * e_in
        a_t = a_ref[bb] * jnp.exp(cum - ld - mid)
        b_t = b_ref[bb] * e_out
        k_t = k * e_out
        e_mid = jnp.exp(mid)
        w_all = jnp.exp(cum[c_ - 1:c_, :])
        w_tail = jnp.exp(cum[c_ - 1:c_, :] - mid)
        yield
        s0 = [state_ref[bb, p] for p in pairs]
        ar = [jnp.concatenate([block_diag(a_t[:, sl]), block_diag(r_t[:, sl])], axis=0) for sl in sls]
        bk = [jnp.concatenate([block_diag(b_t[:, sl]), block_diag(k_t[:, sl])], axis=0) for sl in sls]
        vb = [block_diag(v[:, sl]) for sl in sls]
        m1 = [_dot_nt(ar[p], bk[p]) for p in pairs]
        m2 = [_dot_nt(ar[p] * e_mid[:, sls[p]], s0[p]) for p in pairs]
        yield
        l_ab = [jnp.where(strict, m[:c2, :c2], 0.0) for m in m1]
        l_ak = [jnp.where(strict, m[:c2, c2:], 0.0) for m in m1]
        l_r = [jnp.concatenate([jnp.where(incl, m[c2:, :c2], 0.0), jnp.where(incl, m[c2:, c2:], 0.0)], axis=1)
               for m in m1]
        u = [m2[p][:c2] + _dot(l_ak[p], vb[p]) for p in pairs]
        pw = l_ab
        yield
        n_steps = int(math.log2(c_))
        for step in range(n_steps):
            if step < n_steps - 1:
                prod = [_dot(pw[p], jnp.concatenate([pw[p], u[p]], axis=1)) for p in pairs]
                pw = [q[:, :c2] for q in prod]
                u = [u[p] + prod[p][:, c2:] for p in pairs]
            else:
                u = [u[p] + _dot(pw[p], u[p]) for p in pairs]
            yield
        uv = [jnp.concatenate([u[p], vb[p]], axis=0) for p in pairs]
        ys = [m2[p][c2:] + _dot(l_r[p], uv[p]) for p in pairs]
        upd = [_dot_tn(uv[p], bk[p]) for p in pairs]
        yield
        for p in pairs:
            state_ref[bb, p] = s0[p] * w_all[:, sls[p]] + upd[p] * w_tail[:, sls[p]]
        y = jnp.concatenate([ys[p][:c_] + ys[p][c_:] for p in pairs], axis=1)
        sum_y, sum_rk = head_sums([y, r * k * rk_ref[...]])
        d = y - sum_y * (1.0 / n_)
        yield
        var = head_sums([d * d])[0] * (1.0 / n_)
        yn = d * lax.rsqrt(var + RW_LN_EPS) * lng_ref[...] + lnb_ref[...]
        o_ref[bb] = ((yn + sum_rk * v) * g_ref[bb]).astype(o_ref.dtype)
        yield

    for _ in zip(*[chunk_program(bb) for bb in range(RW_ROWS)]):
        pass


def _rw_scan(r, k, v, ld, a, b, g, b_, s_, r_k, ln_g, ln_b, ones_pair, tri):
    t = r.shape[0]
    nc = s_ // RW_CHUNK
    row = pl.BlockSpec((RW_ROWS, RW_CHUNK, RW_WIDTH), lambda bi, i: (bi, i, 0))
    vec = lambda x: x.reshape(1, -1)
    seq = lambda x: x.reshape(b_, s_, RW_WIDTH)
    out = pl.pallas_call(
        _rw_scan_kernel,
        grid=(b_ // RW_ROWS, nc),
        in_specs=[row] * 7 + [_const_spec(tri.shape), _const_spec(ones_pair.shape),
                              _const_spec((1, RW_WIDTH)), _const_spec((1, RW_WIDTH)), _const_spec((1, RW_WIDTH))],
        out_specs=row,
        out_shape=jax.ShapeDtypeStruct((b_, s_, RW_WIDTH), BF16),
        scratch_shapes=[pltpu.VMEM((RW_ROWS, RW_HEADS // 2, 2 * RW_HEAD_DIM, 2 * RW_HEAD_DIM), F32)],
        compiler_params=_params("parallel", "arbitrary"),
        name="rwkv_scan",
    )(seq(r), seq(k), seq(v), seq(ld), seq(a), seq(b), seq(g), tri, ones_pair, vec(r_k), vec(ln_g), vec(ln_b))
    return out.reshape(t, RW_WIDTH)


def _rt_log_gamma(h):
    return math.log1p(-(2.0 ** (-5.0 - h)))


def _retention_kernel(p_ref, cos_ref, sin_ref, gng_ref, gnb_ref, o_ref, state_ref):
    c_ = RT_CHUNK

    @pl.when(pl.program_id(1) == 0)
    def _():
        state_ref[...] = jnp.zeros_like(state_ref)

    cos = cos_ref[...]
    sin = sin_ref[...]
    row = lax.broadcasted_iota(jnp.int32, (c_, c_), 0).astype(F32)
    col = lax.broadcasted_iota(jnp.int32, (c_, c_), 1).astype(F32)
    rel = row - col
    heads = range(RT_HEADS)
    lgs = [_rt_log_gamma(h) for h in heads]
    sls = [slice(h * RT_DIM, (h + 1) * RT_DIM) for h in heads]
    decay = [jnp.where(rel >= 0, jnp.exp(jnp.maximum(rel, 0.0) * lg), 0.0) for lg in lgs]
    xi = [jnp.exp((row + 1.0) * lg) for lg in lgs]
    zeta = [jnp.exp((c_ - 1.0 - row) * lg) for lg in lgs]

    def chunk_program(bb):
        col_of = lambda part, h: slice(part * RT_WIDTH + h * RT_DIM, part * RT_WIDTH + (h + 1) * RT_DIM)
        q = [p_ref[bb, :, col_of(0, h)] for h in heads]
        k = [p_ref[bb, :, col_of(1, h)] for h in heads]
        v = [p_ref[bb, :, col_of(2, h)] for h in heads]
        q = [x * cos + pltpu.roll(x, RT_DIM // 2, 1) * sin for x in q]
        k = [(x * cos + pltpu.roll(x, RT_DIM // 2, 1) * sin) * (RT_DIM ** -0.5) for x in k]
        s0 = [state_ref[bb, h] for h in heads]
        yield
        scores = [_dot_nt(q[h], k[h]) * decay[h] for h in heads]
        cross = [_dot(q[h], s0[h]) * xi[h] for h in heads]
        for h in heads:
            state_ref[bb, h] = s0[h] * math.exp(c_ * lgs[h]) + _dot_tn(k[h] * zeta[h], v[h])
        yield
        o = [_dot(scores[h], v[h]) + cross[h] for h in heads]
        yield
        for h in heads:
            mu = jnp.mean(o[h], axis=-1, keepdims=True)
            d = o[h] - mu
            var = jnp.mean(d * d, axis=-1, keepdims=True)
            on = d * lax.rsqrt(var + EPS) * gng_ref[:, sls[h]] + gnb_ref[:, sls[h]]
            gate = p_ref[bb, :, col_of(3, h)]
            o_ref[bb, :, sls[h]] = (gate * jax.nn.sigmoid(gate) * on).astype(o_ref.dtype)
        yield

    for _ in zip(*[chunk_program(bb) for bb in range(RT_ROWS)]):
        pass


def _retention(p_rt, b_, s_, cos2, sin2, gn_g, gn_b):
    t = p_rt.shape[0]
    nc = s_ // RT_CHUNK
    out = pl.pallas_call(
        _retention_kernel,
        grid=(b_ // RT_ROWS, nc),
        in_specs=[pl.BlockSpec((RT_ROWS, RT_CHUNK, RT_IN), lambda bi, i: (bi, i, 0)),
                  pl.BlockSpec((RT_CHUNK, RT_DIM), lambda bi, i: (i, 0)),
                  pl.BlockSpec((RT_CHUNK, RT_DIM), lambda bi, i: (i, 0)),
                  _const_spec((1, RT_WIDTH)), _const_spec((1, RT_WIDTH))],
        out_specs=pl.BlockSpec((RT_ROWS, RT_CHUNK, RT_WIDTH), lambda bi, i: (bi, i, 0)),
        out_shape=jax.ShapeDtypeStruct((b_, s_, RT_WIDTH), BF16),
        scratch_shapes=[pltpu.VMEM((RT_ROWS, RT_HEADS, RT_DIM, RT_DIM), F32)],
        compiler_params=_params("parallel", "arbitrary"),
        name="retention",
    )(p_rt.reshape(b_, s_, RT_IN), cos2, sin2, gn_g.reshape(1, -1), gn_b.reshape(1, -1))
    return out.reshape(t, RT_WIDTH)


DSA_BLOCKS_PER_TILE = ROW_TILE // Q_BLOCK


def _dsa_prep(c_q, c_kv, kw, qg_ref, kvg_ref, wuq_ref, wuk_ref, wqi_ref,
              ckvn_ref, ckvt_ref, kidx_ref, widx_ref, qlat_ref, qidx_ref):
    cq = _rms(c_q, qg_ref[...]).astype(BF16)
    ckvn = _rms(c_kv, kvg_ref[...])
    ckvn_ref[...] = ckvn.astype(BF16)
    kidx_ref[...] = kw[:, :IDX_DIM].astype(BF16)
    q = jnp.dot(cq, wuq_ref[...], preferred_element_type=F32)
    qi = jnp.dot(cq, wqi_ref[...], preferred_element_type=F32)
    for blk in range(DSA_BLOCKS_PER_TILE):
        rows = slice(blk * Q_BLOCK, (blk + 1) * Q_BLOCK)
        ckvt_ref[blk] = ckvn[rows].T.astype(BF16)
        widx_ref[blk] = kw[rows].T[IDX_DIM:IDX_DIM + IDX_HEADS] * ((IDX_HEADS * IDX_DIM) ** -0.5)
    for h in range(DSA_HEADS):
        qh = q[:, h * DSA_HEAD_DIM:(h + 1) * DSA_HEAD_DIM]
        ql = (_dot(qh, wuk_ref[h]) * (DSA_HEAD_DIM ** -0.5)).astype(BF16)
        qih = qi[:, h * IDX_DIM:(h + 1) * IDX_DIM].astype(BF16)
        for blk in range(DSA_BLOCKS_PER_TILE):
            rows = slice(blk * Q_BLOCK, (blk + 1) * Q_BLOCK)
            qlat_ref[blk, h] = ql[rows]
            qidx_ref[blk, h] = qih[rows]


def _fold_rows(x, op):
    return functools.reduce(op, [x[i:i + 8] for i in range(0, x.shape[0], 8)])


def _bit_transpose32(words):
    a = list(words)
    j, m = 16, 0x0000FFFF
    while j:
        k = 0
        while k < 32:
            t = (a[k] ^ lax.shift_right_logical(a[k + j], jnp.int32(j))) & jnp.int32(m)
            a[k] = a[k] ^ t
            a[k + j] = a[k + j] ^ lax.shift_left(t, jnp.int32(j))
            k = (k + j + 1) & ~j
        j >>= 1
        m ^= m << j
    return a


def _dsa_kernel(top_k, qidx_ref, widx_ref, qlat_ref, kidx_ref, ckv_ref, ckvt_ref, wuv_ref, tril_ref, o_ref,
                key_ref, planes_ref, s_ref, acc_ref):
    qb = Q_BLOCK
    nh = DSA_HEADS
    j = pl.program_id(1)
    n_chunks = j + 1

    keyi = lax.broadcasted_iota(jnp.int32, (qb, qb), 0)
    qryi = lax.broadcasted_iota(jnp.int32, (qb, qb), 1)
    lanes = lambda h: slice(h * qb, (h + 1) * qb)

    def key_chunk(ref, c):
        return ref[pl.ds(pl.multiple_of(c * qb, qb), qb), :]

    @pl.when(j == 0)
    def _():
        planes_ref[...] = jnp.zeros_like(planes_ref)

    w_idx = widx_ref[0]
    q_idx = qidx_ref[0].reshape(IDX_HEADS * qb, IDX_DIM)
    q_lat = qlat_ref[0].reshape(nh * qb, DSA_KV_RANK)

    n_pairs = (n_chunks + 1) // 2

    def score_chunk(c):
        logits = lax.dot_general(key_chunk(kidx_ref, c), q_idx, (((1,), (1,)), ((), ())),
                                 preferred_element_type=F32)
        score = jnp.zeros((qb, qb), F32)
        for h in range(IDX_HEADS):
            score = score + jnp.maximum(logits[:, lanes(h)], 0.0) * w_idx[h:h + 1, :]
        bits = lax.bitcast_convert_type(score, jnp.int32)
        key = jnp.where(bits < 0, bits ^ jnp.int32(0x7FFFFFFF), bits)
        causal = keyi + c * qb <= qryi + j * qb
        key = jnp.where(causal, key, jnp.int32(INT_MIN))
        key_ref[c] = key
        return key

    def score_body(qi, carry):
        for pi in (2 * qi, 2 * qi + 1):
            keys = [score_chunk(2 * pi), score_chunk(2 * pi + 1)]
            words = [k[r:r + 8] ^ jnp.int32(INT_MIN) for k in keys for r in range(0, qb, 8)]
            for idx, plane in enumerate(_bit_transpose32(words)):
                planes_ref[31 - idx, pi] = plane
        return carry

    lax.fori_loop(0, (n_chunks + 3) // 4, score_body, 0)

    n_pairs_max = planes_ref.shape[1]
    live = tuple(jnp.where(p < n_pairs, jnp.full((8, qb), -1, jnp.int32), jnp.zeros((8, qb), jnp.int32))
                 for p in range(n_pairs_max))

    def bit_body(i, carry):
        thr_u, above, eq = carry
        plane = planes_ref[31 - i]
        hits = [eq[p] & plane[p] for p in range(n_pairs_max)]
        cnt = functools.reduce(jnp.add, [lax.population_count(h) for h in hits])
        cnt = above + jnp.sum(cnt, axis=0, keepdims=True)
        accept = cnt >= top_k
        thr_u = jnp.where(accept, thr_u | lax.shift_left(jnp.int32(1), 31 - i), thr_u)
        above = jnp.where(accept, above, cnt)
        eq = tuple(jnp.where(accept, h, e ^ h) for e, h in zip(eq, hits))
        return thr_u, above, eq

    zero_row = jnp.zeros((1, qb), jnp.int32)
    thr_u, above, _ = lax.fori_loop(0, 32, bit_body, (zero_row, zero_row, live))
    thr = thr_u ^ jnp.int32(INT_MIN)
    need = (top_k - above).astype(F32)

    def pair_logits(t, taken):
        tops = [None] * nh
        for c in (2 * t, 2 * t + 1):
            key = key_ref[c]
            causal = keyi + c * qb <= qryi + j * qb
            eq = jnp.logical_and(key == thr, causal)
            eq_f = jnp.where(eq, 1.0, 0.0)
            rank = taken + jnp.dot(tril_ref[...], eq_f.astype(BF16), preferred_element_type=F32)
            sel = jnp.logical_or(jnp.logical_and(key > thr, causal), jnp.logical_and(eq, rank <= need))
            s = lax.dot_general(key_chunk(ckv_ref, c), q_lat, (((1,), (1,)), ((), ())),
                                preferred_element_type=F32)
            for h in range(nh):
                sh = jnp.where(sel, s[:, lanes(h)], NEG_BIG)
                s_ref[c, :, lanes(h)] = sh
                top = _fold_rows(sh, jnp.maximum)
                tops[h] = top if tops[h] is None else jnp.maximum(tops[h], top)
            taken = taken + jnp.sum(eq_f, axis=0, keepdims=True)
        return taken, tuple(jnp.max(top, axis=0, keepdims=True) for top in tops)

    def pair_values(t, tops, peak, sums):
        ps, scales, new_peak, new_sums = [], [], [], []
        for h in range(nh):
            pk = jnp.maximum(peak[h], tops[h])
            scale = jnp.exp(peak[h] - pk)
            pa = jnp.exp(s_ref[2 * t, :, lanes(h)] - pk)
            pb = jnp.exp(s_ref[2 * t + 1, :, lanes(h)] - pk)
            new_sums.append(sums[h] * scale + _fold_rows(pa + pb, jnp.add))
            ps.append(jnp.concatenate([pa.astype(BF16), pb.astype(BF16)], axis=0))
            scales.append(scale)
            new_peak.append(pk)
        ckvt_pair = jnp.concatenate([ckvt_ref[2 * t], ckvt_ref[2 * t + 1]], axis=1)
        upd = jnp.dot(ckvt_pair, jnp.concatenate(ps, axis=1), preferred_element_type=F32)
        acc_ref[...] = acc_ref[...] * jnp.concatenate(scales, axis=1) + upd
        return tuple(new_peak), tuple(new_sums)

    acc_ref[...] = jnp.zeros_like(acc_ref)

    def attn_body(t, carry):
        taken, tops, peak, sums = carry
        peak, sums = pair_values(t, tops, peak, sums)
        taken, tops = pair_logits(jnp.minimum(t + 1, n_pairs - 1), taken)
        return taken, tops, peak, sums

    taken, tops = pair_logits(0, jnp.zeros((1, qb), F32))
    init = (taken, tops, tuple(jnp.full((1, qb), NEG_BIG, F32) for _ in range(nh)),
            tuple(jnp.zeros((8, qb), F32) for _ in range(nh)))
    _, _, _, sums = lax.fori_loop(0, n_pairs, attn_body, init)

    outs = []
    for h in range(nh):
        o_lat_t = acc_ref[:, lanes(h)] / jnp.sum(sums[h], axis=0, keepdims=True)
        outs.append(_dot_tn(o_lat_t, wuv_ref[h]))
    o_ref[...] = jnp.concatenate(outs, axis=1).astype(o_ref.dtype)


def _dsa(q_idx, w_idx, q_lat, k_idx, ckv_n, ckv_t, w_uv, tril, b_, s_):
    nb = s_ // Q_BLOCK
    t = b_ * s_
    top_k = min(TOPK_MAX, s_ // 4)
    blk = lambda n: pl.BlockSpec((Q_BLOCK, n), lambda bi, i: (bi * nb + i, 0))
    blk3 = lambda m, n: pl.BlockSpec((1, m, n), lambda bi, i: (bi * nb + i, 0, 0))
    blk4 = lambda n: pl.BlockSpec((1, DSA_HEADS, Q_BLOCK, n), lambda bi, i: (bi * nb + i, 0, 0, 0))
    seq = lambda n: pl.BlockSpec((s_, n), lambda bi, i: (bi, 0))
    stacked = DSA_HEADS * Q_BLOCK
    return pl.pallas_call(
        functools.partial(_dsa_kernel, top_k),
        grid=(b_, nb),
        in_specs=[blk4(IDX_DIM), blk3(IDX_HEADS, Q_BLOCK), blk4(DSA_KV_RANK),
                  seq(IDX_DIM), seq(DSA_KV_RANK),
                  pl.BlockSpec((nb, DSA_KV_RANK, Q_BLOCK), lambda bi, i: (bi, 0, 0)),
                  _const_spec(w_uv.shape), _const_spec(tril.shape)],
        out_specs=blk(DSA_WIDTH),
        out_shape=jax.ShapeDtypeStruct((t, DSA_WIDTH), BF16),
        scratch_shapes=[pltpu.VMEM((nb, Q_BLOCK, Q_BLOCK), jnp.int32),
                        pltpu.VMEM((32, nb // 2, 8, Q_BLOCK), jnp.int32),
                        pltpu.VMEM((nb, Q_BLOCK, stacked), F32),
                        pltpu.VMEM((DSA_KV_RANK, stacked), F32)],
        compiler_params=_params("parallel", "arbitrary"),
        name="dsa_attn",
    )(q_idx, w_idx, q_lat, k_idx, ckv_n, ckv_t, w_uv, tril)


SC_TILE = 512


def _od_in_kernel(x_ref, gn_ref, wq_ref, wkv_ref, wkw_ref, wsc_ref, cw_ref, cb_ref,
                  qg_ref, kvg_ref, wuq_ref, wuk_ref, wqi_ref,
                  ckvn_ref, ckvt_ref, kidx_ref, widx_ref, qlat_ref, qidx_ref, yd_ref, carry_ref, p_ref):
    @pl.when(pl.program_id(1) == 0)
    def _():
        carry_ref[...] = jnp.zeros_like(carry_ref)

    xb = _rms(x_ref[...], gn_ref[...]).astype(BF16)
    c_q, c_kv, kw = (jnp.dot(xb, w_ref[...], preferred_element_type=F32) for w_ref in (wq_ref, wkv_ref, wkw_ref))
    for c in range(0, 3 * SC_WIDTH, PROJ_CHUNK):
        sl = slice(c, c + PROJ_CHUNK)
        p_ref[:, sl] = jnp.dot(xb, wsc_ref[:, sl], preferred_element_type=F32)
    _dsa_prep(c_q, c_kv, kw, qg_ref, kvg_ref, wuq_ref, wuk_ref, wqi_ref,
              ckvn_ref, ckvt_ref, kidx_ref, widx_ref, qlat_ref, qidx_ref)

    h = p_ref[:, :SC_WIDTH]
    gate_b = p_ref[:, SC_WIDTH:2 * SC_WIDTH]
    gate_c = p_ref[:, 2 * SC_WIDTH:]
    u = gate_c * h
    carry = carry_ref[...]
    y = u * cw_ref[2:3, :] + _shift_rows(u, carry, 1) * cw_ref[1:2, :] + _shift_rows(u, carry, 2) * cw_ref[0:1, :]
    carry_ref[...] = u[SC_TILE - 8:, :]
    yd_ref[...] = (gate_b * (y + cb_ref[...])).astype(yd_ref.dtype)


def _od_in(x, gn, w_q, w_kv, w_kw, w_sc, b_, s_, conv_w, conv_b, q_g, kv_g, w_uq, w_uk, w_qi):
    t, d = x.shape
    nt = s_ // SC_TILE
    row = lambda n: pl.BlockSpec((SC_TILE, n), lambda bi, i: (bi * nt + i, 0))
    blk3 = lambda m, n: pl.BlockSpec((DSA_BLOCKS_PER_TILE, m, n), lambda bi, i: (bi * nt + i, 0, 0))
    blk4 = lambda n: pl.BlockSpec((DSA_BLOCKS_PER_TILE, DSA_HEADS, Q_BLOCK, n), lambda bi, i: (bi * nt + i, 0, 0, 0))
    consts = [w_q, w_kv, w_kw, w_sc, jnp.pad(conv_w, ((0, 8 - SC_KERNEL), (0, 0))), conv_b.reshape(1, -1),
              q_g.reshape(1, -1), kv_g.reshape(1, -1), w_uq, w_uk, w_qi]
    return pl.pallas_call(
        _od_in_kernel,
        grid=(b_, nt),
        in_specs=[row(d), _const_spec((1, d))] + [_const_spec(c.shape) for c in consts],
        out_specs=[row(DSA_KV_RANK), blk3(DSA_KV_RANK, Q_BLOCK), row(IDX_DIM), blk3(IDX_HEADS, Q_BLOCK),
                   blk4(DSA_KV_RANK), blk4(IDX_DIM), row(SC_WIDTH)],
        out_shape=[jax.ShapeDtypeStruct((t, DSA_KV_RANK), BF16),
                   jax.ShapeDtypeStruct((t // Q_BLOCK, DSA_KV_RANK, Q_BLOCK), BF16),
                   jax.ShapeDtypeStruct((t, IDX_DIM), BF16),
                   jax.ShapeDtypeStruct((t // Q_BLOCK, IDX_HEADS, Q_BLOCK), F32),
                   jax.ShapeDtypeStruct((t // Q_BLOCK, DSA_HEADS, Q_BLOCK, DSA_KV_RANK), BF16),
                   jax.ShapeDtypeStruct((t // Q_BLOCK, IDX_HEADS, Q_BLOCK, IDX_DIM), BF16),
                   jax.ShapeDtypeStruct((t, SC_WIDTH), BF16)],
        scratch_shapes=[pltpu.VMEM((8, SC_WIDTH), F32), pltpu.VMEM((SC_TILE, 3 * SC_WIDTH), F32)],
        compiler_params=_params("parallel", "arbitrary"),
        name="od_in_proj",
    )(x, gn.reshape(1, -1), *consts)


def _xattn_kernel(y1_ref, y2_ref, w1_ref, w2_ref, gm_ref, x_ref, gq_ref, wq_ref, k_ref, v_ref, wo_ref, go_ref,
                  o_ref, att_ref):
    def rows_program(rows):
        mix = (jnp.dot(y1_ref[rows, :], w1_ref[...], preferred_element_type=F32)
               + jnp.dot(y2_ref[rows, :], w2_ref[...], preferred_element_type=F32))
        yield
        x = x_ref[rows, :] + _rms(mix, gm_ref[...])
        q = jnp.dot(_rms(x, gq_ref[...]).astype(BF16), wq_ref[...], preferred_element_type=F32)
        yield
        for h in range(XA_HEADS):
            sl = slice(h * XA_HEAD_DIM, (h + 1) * XA_HEAD_DIM)
            s = _dot_nt(q[:, sl], k_ref[:, sl]) * (XA_HEAD_DIM ** -0.5)
            s = s - jnp.max(s, axis=-1, keepdims=True)
            p = jnp.exp(s)
            p = p / jnp.sum(p, axis=-1, keepdims=True)
            att_ref[rows, sl] = _dot(p, v_ref[:, sl])
            yield
        hout = jnp.dot(att_ref[rows, :].astype(BF16), wo_ref[...], preferred_element_type=F32)
        yield
        o_ref[rows, :] = x + _rms(hout, go_ref[...])
        yield

    sub = ROW_TILE // XA_SUBTILES
    for _ in zip(*[rows_program(slice(i * sub, (i + 1) * sub)) for i in range(XA_SUBTILES)]):
        pass


def _xattn(y1, y2, w1, w2, g_mix, x, g_q, wq, k_mem, v_mem, wo, g_o, b_, s_):
    t = x.shape[0]
    nt = s_ // ROW_TILE
    row = lambda n: pl.BlockSpec((ROW_TILE, n), lambda bi, i: (bi * nt + i, 0))
    mem = pl.BlockSpec((MEM_LEN, XA_WIDTH), lambda bi, i: (bi, 0))
    vec = lambda g: g.reshape(1, -1)
    return pl.pallas_call(
        _xattn_kernel,
        grid=(b_, nt),
        in_specs=[row(y1.shape[1]), row(y2.shape[1]), _const_spec(w1.shape), _const_spec(w2.shape),
                  _const_spec((1, D_MODEL)), row(D_MODEL), _const_spec((1, D_MODEL)), _const_spec(wq.shape),
                  mem, mem, _const_spec(wo.shape), _const_spec((1, D_MODEL))],
        out_specs=row(D_MODEL),
        out_shape=jax.ShapeDtypeStruct((t, D_MODEL), F32),
        scratch_shapes=[pltpu.VMEM((ROW_TILE, XA_WIDTH), F32)],
        compiler_params=_params("parallel", "parallel"),
        name="mix_out_xattn",
    )(y1, y2, w1, w2, vec(g_mix), x, vec(g_q), wq, k_mem, v_mem, wo, vec(g_o))


def _block_diag(n_blocks, size, value):
    return np.kron(np.eye(n_blocks, dtype=np.float32), np.full((size, size), value, np.float32))


def _rope_tables(s_):
    half = RT_DIM // 2
    inv_freq = RT_ROPE_BASE ** (-jnp.arange(half, dtype=F32) / half)
    ang = jnp.arange(s_).astype(F32)[:, None] * inv_freq[None, :]
    cos, sin = jnp.cos(ang), jnp.sin(ang)
    return jnp.concatenate([cos, cos], axis=-1), jnp.concatenate([-sin, sin], axis=-1)


def kernel(x, mem, norm_g, mem_norm_g, ffn_w_gate, ffn_w_up, ffn_w_down, xa_wq, xa_wk, xa_wv, xa_wo, ev_w_in, ev_w_out, rw_mu, rw_w0, rw_w2, rw_a0, rw_a2, rw_g2, rw_k_k, rw_k_a, rw_r_k, rw_ln_g, rw_ln_b, rt_gn_g, rt_gn_b, od_w_in, od_w_out, dsa_q_norm_g, dsa_kv_norm_g, dsa_w_uq, dsa_w_uk, dsa_w_uv, dsa_w_qi, sc_conv_w, sc_conv_b):
    b_, s_, d_ = x.shape
    depth = norm_g.shape[0]
    t = b_ * s_
    bf = lambda w: w.astype(BF16)

    ones_blk = jnp.asarray(_block_diag(RW_HEADS, RW_HEAD_DIM, 1.0), BF16)
    ones_pair = jnp.asarray(_block_diag(2, RW_HEAD_DIM, 1.0), BF16)
    tri_rw = jnp.asarray(np.tril(np.ones((RW_CHUNK, RW_CHUNK), np.float32)), BF16)
    tril_dsa = jnp.asarray(np.tril(np.ones((Q_BLOCK, Q_BLOCK), np.float32)), BF16)
    cos2, sin2 = _rope_tables(s_)

    w_gate, w_up, w_down = bf(ffn_w_gate), bf(ffn_w_up), bf(ffn_w_down)
    xf = x.reshape(t, d_)
    mem_f = mem.reshape(b_ * MEM_LEN, d_)
    for l in range(depth):
        ng = norm_g[l]
        i = l // 2
        xf = _ffn(xf, ng[0], w_gate, w_up, w_down, l, 0, ng[1])
        if l % 2 == 0:
            w_in = bf(ev_w_in[i])
            r, k, v, ld, a, b, g, p_rt = _ev_in(xf, ng[2], w_in[:, :RW_IN], w_in[:, RW_IN:], b_, s_, rw_mu[i],
                                                rw_w0[i], rw_w2[i], rw_a0[i], rw_a2[i], rw_g2[i], rw_k_k[i],
                                                rw_k_a[i], ones_blk)
            y_a = _rw_scan(r, k, v, ld, a, b, g, b_, s_, rw_r_k[i], rw_ln_g[i], rw_ln_b[i],
                           ones_pair, tri_rw)
            y_b = _retention(p_rt, b_, s_, cos2, sin2, rt_gn_g[i], rt_gn_b[i])
            ys, w_out, split = (y_a, y_b), bf(ev_w_out[i]), RW_WIDTH
        else:
            w_in = od_w_in[i]
            kw_w = jnp.pad(w_in[:, DSA_Q_RANK + DSA_KV_RANK:DSA_IN], ((0, 0), (0, LANES - IDX_DIM - IDX_HEADS)))
            ckv_n, ckv_t, k_idx, w_idx, q_lat, q_idx, y_d = _od_in(
                xf, ng[2], bf(w_in[:, :DSA_Q_RANK]), bf(w_in[:, DSA_Q_RANK:DSA_Q_RANK + DSA_KV_RANK]),
                bf(kw_w), bf(w_in[:, DSA_IN:]), b_, s_, sc_conv_w[i], sc_conv_b[i],
                dsa_q_norm_g[i], dsa_kv_norm_g[i],
                bf(dsa_w_uq[i].reshape(DSA_Q_RANK, DSA_WIDTH)), bf(dsa_w_uk[i]),
                bf(dsa_w_qi[i].reshape(DSA_Q_RANK, IDX_HEADS * IDX_DIM)))
            y_c = _dsa(q_idx, w_idx, q_lat, k_idx, ckv_n, ckv_t, bf(dsa_w_uv[i]), tril_dsa, b_, s_)
            ys, w_out, split = (y_c, y_d), bf(od_w_out[i]), DSA_WIDTH
        k_mem, v_mem = _norm_proj(mem_f, mem_norm_g, [bf(xa_wk[l]), bf(xa_wv[l])], "mem_kv_proj")
        xf = _xattn(ys[0], ys[1], w_out[:split], w_out[split:], ng[3], xf, ng[4], bf(xa_wq[l]), k_mem, v_mem,
                    bf(xa_wo[l]), ng[5], b_, s_)
        xf = _ffn(xf, ng[6], w_gate, w_up, w_down, l, 1, ng[7])
    return xf.reshape(b_, s_, d_)
```

```python
import functools
import math

import numpy as np
import jax
import jax.numpy as jnp
from jax import lax
from jax.experimental import pallas as pl
from jax.experimental.pallas import tpu as pltpu

F32 = jnp.float32
BF16 = jnp.bfloat16

D_MODEL = 1024
D_FF = 2816
EPS = 1e-6
MEM_LEN = 256
RW_HEADS = 8
RW_HEAD_DIM = 64
RW_WIDTH = RW_HEADS * RW_HEAD_DIM
RW_DECAY_RANK = 64
RW_AAA_RANK = 64
RW_GATE_RANK = 128
RW_LN_EPS = 64e-5
RW_IN = 3 * RW_WIDTH + RW_DECAY_RANK + RW_AAA_RANK + RW_GATE_RANK
RW_CHUNK = 64
RW_ROWS = 4
RT_HEADS = 4
RT_DIM = 128
RT_WIDTH = RT_HEADS * RT_DIM
RT_CHUNK = 128
RT_ROWS = 4
RT_ROPE_BASE = 10000.0
RT_IN = 4 * RT_WIDTH
DSA_HEADS = 8
DSA_HEAD_DIM = 64
DSA_WIDTH = DSA_HEADS * DSA_HEAD_DIM
DSA_Q_RANK = 256
DSA_KV_RANK = 128
IDX_HEADS = 8
IDX_DIM = 64
TOPK_MAX = 256
Q_BLOCK = 128
DSA_IN = DSA_Q_RANK + DSA_KV_RANK + IDX_DIM + IDX_HEADS
SC_WIDTH = 512
SC_KERNEL = 3
XA_HEADS = 4
XA_HEAD_DIM = 128
XA_WIDTH = XA_HEADS * XA_HEAD_DIM
XA_SUBTILES = 2

LANES = 128
ROW_TILE = 512
VMEM_LIMIT = 56 * 1024 * 1024
SUM_PASSES = 2
INT_MIN = -2 ** 31
NEG_BIG = -1e30


def _params(*sem):
    return pltpu.CompilerParams(dimension_semantics=sem, vmem_limit_bytes=VMEM_LIMIT)


def _rms(x, g):
    return x * lax.rsqrt(jnp.mean(x * x, axis=-1, keepdims=True) + EPS) * g


def _dot(a, b):
    return jnp.dot(a.astype(BF16), b.astype(BF16), preferred_element_type=F32)


def _dot_nt(a, b):
    return lax.dot_general(a.astype(BF16), b.astype(BF16), (((1,), (1,)), ((), ())),
                           preferred_element_type=F32)


def _dot_tn(a, b):
    return lax.dot_general(a.astype(BF16), b.astype(BF16), (((0,), (0,)), ((), ())),
                           preferred_element_type=F32)


def _split(x, n):
    parts = []
    for _ in range(n):
        part = x.astype(BF16)
        parts.append(part)
        x = x - part.astype(F32)
    return parts


def _dot_exact_rhs(x, w_bf16, n):
    return functools.reduce(jnp.add, [jnp.dot(p, w_bf16, preferred_element_type=F32) for p in _split(x, n)])


def _dot_exact_lhs(w_bf16, x, n):
    return functools.reduce(jnp.add, [jnp.dot(w_bf16, p, preferred_element_type=F32) for p in _split(x, n)])


def _const_spec(shape):
    nd = len(shape)
    return pl.BlockSpec(shape, lambda *_: (0,) * nd, pipeline_mode=pl.Buffered(1))


FF_CHUNK = 256
FF_TILE = 1024
FF_SUBTILES = 2


def _ffn_kernel(x_ref, gin_ref, wg_ref, wu_ref, wd_ref, gout_ref, o_ref, acc_ref):
    def rows_program(rows):
        x = x_ref[rows, :]
        xb = _rms(x, gin_ref[...]).astype(BF16)
        yield
        for c in range(D_FF // FF_CHUNK):
            sl = slice(c * FF_CHUNK, (c + 1) * FF_CHUNK)
            g = jnp.dot(xb, wg_ref[:, sl], preferred_element_type=F32)
            u = jnp.dot(xb, wu_ref[:, sl], preferred_element_type=F32)
            h = (g * jax.nn.sigmoid(g) * u).astype(BF16)
            part = jnp.dot(h, wd_ref[sl, :], preferred_element_type=F32)
            if c == 0:
                acc_ref[rows, :] = part
            else:
                acc_ref[rows, :] += part
            yield
        o_ref[rows, :] = x + 0.5 * _rms(acc_ref[rows, :], gout_ref[...])
        yield

    sub = FF_TILE // FF_SUBTILES
    programs = [rows_program(slice(i * sub, (i + 1) * sub)) for i in range(FF_SUBTILES)]
    n_stages = D_FF // FF_CHUNK + 2
    for step in range(n_stages + FF_SUBTILES - 1):
        for i, prog in enumerate(programs):
            if 0 <= step - i < n_stages:
                next(prog)


def _ffn(x, g_in, wg, wu, wd, layer, half, g_out):
    t = x.shape[0]
    row = pl.BlockSpec((FF_TILE, D_MODEL), lambda i: (i, 0))
    weight = lambda m, n: pl.BlockSpec((None, None, m, n), lambda i: (layer, half, 0, 0),
                                       pipeline_mode=pl.Buffered(1))
    return pl.pallas_call(
        _ffn_kernel,
        grid=(t // FF_TILE,),
        in_specs=[row, _const_spec((1, D_MODEL)), weight(D_MODEL, D_FF), weight(D_MODEL, D_FF),
                  weight(D_FF, D_MODEL), _const_spec((1, D_MODEL))],
        out_specs=row,
        out_shape=jax.ShapeDtypeStruct((t, D_MODEL), F32),
        scratch_shapes=[pltpu.VMEM((FF_TILE, D_MODEL), F32)],
        compiler_params=_params("parallel"),
        name="ffn_half",
    )(x, g_in.reshape(1, -1), wg, wu, wd, g_out.reshape(1, -1))


PROJ_CHUNK = 512


def _norm_proj_kernel(n_out, x_ref, g_ref, *refs):
    xb = _rms(x_ref[...], g_ref[...]).astype(BF16)
    for w_ref, o_ref in zip(refs[:n_out], refs[n_out:]):
        n = w_ref.shape[1]
        for c in range(0, n, PROJ_CHUNK):
            sl = slice(c, min(c + PROJ_CHUNK, n))
            o_ref[:, sl] = jnp.dot(xb, w_ref[:, sl], preferred_element_type=F32)


def _norm_proj(x, g, ws, name):
    t, d = x.shape
    row = lambda n: pl.BlockSpec((ROW_TILE, n), lambda i: (i, 0))
    return pl.pallas_call(
        functools.partial(_norm_proj_kernel, len(ws)),
        grid=(t // ROW_TILE,),
        in_specs=[row(d), _const_spec((1, d))] + [_const_spec(w.shape) for w in ws],
        out_specs=[row(w.shape[1]) for w in ws],
        out_shape=[jax.ShapeDtypeStruct((t, w.shape[1]), F32) for w in ws],
        compiler_params=_params("parallel"),
        name=name,
    )(x, g.reshape(1, -1), *ws)


RW_TILE = ROW_TILE
EV_SUBTILES = 2


def _shift_rows(x, carry, n):
    rolled = pltpu.roll(x, n, 0)
    row = lax.broadcasted_iota(jnp.int32, x.shape, 0)
    out = rolled
    for i in range(n):
        out = jnp.where(row == i, carry[8 - n + i:8 - n + i + 1, :], out)
    return out


def _ev_in_kernel(x_ref, gn_ref, wrw_ref, wrt_ref, mu_ref, w0_ref, w2_ref, a0_ref, a2_ref, g2_ref, kk_ref, ka_ref,
                  ones_ref, r_ref, k_ref, v_ref, ld_ref, a_ref, b_ref, g_ref, prt_ref, carry_ref, p_ref):
    @pl.when(pl.program_id(1) == 0)
    def _():
        carry_ref[...] = jnp.zeros_like(carry_ref)

    def project(xb, rows, w_ref, o_ref):
        n = w_ref.shape[1]
        for c in range(0, n, PROJ_CHUNK):
            sl = slice(c, min(c + PROJ_CHUNK, n))
            o_ref[rows, sl] = jnp.dot(xb, w_ref[:, sl], preferred_element_type=F32)

    def rows_program(rows):
        xb = _rms(x_ref[rows, :], gn_ref[...]).astype(BF16)
        project(xb, rows, wrw_ref, p_ref)
        yield
        project(xb, rows, wrt_ref, prt_ref)
        p = p_ref[rows, :]
        before = carry_ref[...] if rows.start == 0 else p_ref[rows.start - 8:rows.start, :]
        prev = _shift_rows(p, before, 1)
        xm = p + (prev - p) * mu_ref[...]
        w = RW_WIDTH
        r, k, v = xm[:, :w], xm[:, w:2 * w], xm[:, 2 * w:3 * w]
        xw = xm[:, 3 * w:3 * w + RW_DECAY_RANK]
        xa = xm[:, 3 * w + RW_DECAY_RANK:3 * w + RW_DECAY_RANK + RW_AAA_RANK]
        xg = xm[:, 3 * w + RW_DECAY_RANK + RW_AAA_RANK:]
        yield
        wlog = -jax.nn.softplus(-(w0_ref[...] + _dot(jnp.tanh(xw), w2_ref[...]))) - 0.5
        a = jax.nn.sigmoid(a0_ref[...] + _dot(xa, a2_ref[...]))
        kk = k * kk_ref[...]
        ss = _dot_exact_rhs(kk * kk, ones_ref[...], SUM_PASSES)
        kk = kk / jnp.maximum(jnp.sqrt(ss), 1e-12)
        yield
        r_ref[rows, :] = r
        k_ref[rows, :] = k * (1.0 + (a - 1.0) * ka_ref[...])
        v_ref[rows, :] = v
        ld_ref[rows, :] = -jnp.exp(wlog)
        a_ref[rows, :] = -kk
        b_ref[rows, :] = kk * a
        g_ref[rows, :] = _dot(jax.nn.sigmoid(xg), g2_ref[...])
        yield

    sub = RW_TILE // EV_SUBTILES
    for _ in zip(*[rows_program(slice(i * sub, (i + 1) * sub)) for i in range(EV_SUBTILES)]):
        pass
    carry_ref[...] = p_ref[RW_TILE - 8:, :]


def _ev_in(x, gn, w_rw, w_rt, b_, s_, mu, w0, w2, a0, a2, g2, k_k, k_a, ones_blk):
    t, d = x.shape
    nt = s_ // RW_TILE
    row = lambda n: pl.BlockSpec((RW_TILE, n), lambda b, i: (b * nt + i, 0))
    vec = lambda a: a.reshape(1, -1)
    outs = [jax.ShapeDtypeStruct((t, RW_WIDTH), F32)] * 7 + [jax.ShapeDtypeStruct((t, RT_IN), F32)]
    return pl.pallas_call(
        _ev_in_kernel,
        grid=(b_, nt),
        in_specs=[row(d), _const_spec((1, d)), _const_spec(w_rw.shape), _const_spec(w_rt.shape),
                  _const_spec((1, RW_IN)), _const_spec((1, RW_WIDTH)),
                  _const_spec(w2.shape), _const_spec((1, RW_WIDTH)), _const_spec(a2.shape),
                  _const_spec(g2.shape), _const_spec((1, RW_WIDTH)), _const_spec((1, RW_WIDTH)),
                  _const_spec(ones_blk.shape)],
        out_specs=[row(RW_WIDTH)] * 7 + [row(RT_IN)],
        out_shape=outs,
        scratch_shapes=[pltpu.VMEM((8, RW_IN), F32), pltpu.VMEM((RW_TILE, RW_IN), F32)],
        compiler_params=_params("parallel", "arbitrary"),
        name="ev_in_proj",
    )(x, vec(gn), w_rw, w_rt, vec(mu), vec(w0), w2, vec(a0), a2, g2, vec(k_k), vec(k_a), ones_blk)


def _rw_scan_kernel(r_ref, k_ref, v_ref, ld_ref, a_ref, b_ref, g_ref, tri_ref, ones_ref,
                    rk_ref, lng_ref, lnb_ref, o_ref, state_ref):
    c_ = RW_CHUNK
    n_ = RW_HEAD_DIM

    @pl.when(pl.program_id(1) == 0)
    def _():
        state_ref[...] = jnp.zeros_like(state_ref)

    c2 = 2 * c_
    row = lax.broadcasted_iota(jnp.int32, (c2, c2), 0)
    col = lax.broadcasted_iota(jnp.int32, (c2, c2), 1)
    same_head = (row >= c_) == (col >= c_)
    strict = jnp.logical_and(same_head, row > col)
    incl = jnp.logical_and(same_head, row >= col)
    left = lax.broadcasted_iota(jnp.int32, (c_, LANES), 1) < n_

    def block_diag(x):
        return jnp.concatenate([jnp.where(left, x, 0.0), jnp.where(left, 0.0, x)], axis=0)

    pairs = range(RW_HEADS // 2)
    sls = [slice(p * LANES, (p + 1) * LANES) for p in pairs]
    ones2 = ones_ref[...]

    def head_sums(xs):
        parts = [part[:, sl] for x in xs for part in _split(x, SUM_PASSES) for sl in sls]
        prod = jnp.dot(jnp.concatenate(parts, axis=0), ones2, preferred_element_type=F32)
        outs = []
        for i in range(len(xs)):
            slabs = []
            for p in pairs:
                rows = [((SUM_PASSES * i + j) * len(sls) + p) * c_ for j in range(SUM_PASSES)]
                slabs.append(functools.reduce(jnp.add, [prod[r:r + c_] for r in rows]))
            outs.append(jnp.concatenate(slabs, axis=1))
        return outs

    def chunk_program(bb):
        ld = ld_ref[bb]
        r = r_ref[bb]
        k = k_ref[bb]
        v = v_ref[bb]
        cum = _dot_exact_lhs(tri_ref[...], ld, 3)
        mid = cum[c_ // 2 - 1:c_ // 2, :]
        e_in = jnp.exp(cum - mid)
        e_out = jnp.exp(mid - cum)
        r_t = r * e_in
        a_t = a_ref[bb] * jnp.exp(cum - ld - mid)
        b_t = b_ref[bb] * e_out
        k_t = k * e_out
        e_mid = jnp.exp(mid)
        w_all = jnp.exp(cum[c_ - 1:c_, :])
        w_tail = jnp.exp(cum[c_ - 1:c_, :] - mid)
        yield
        s0 = [state_ref[bb, p] for p in pairs]
        ar = [jnp.concatenate([block_diag(a_t[:, sl]), block_diag(r_t[:, sl])], axis=0) for sl in sls]
        bk = [jnp.concatenate([block_diag(b_t[:, sl]), block_diag(k_t[:, sl])], axis=0) for sl in sls]
        vb = [block_diag(v[:, sl]) for sl in sls]
        m1 = [_dot_nt(ar[p], bk[p]) for p in pairs]
        m2 = [_dot_nt(ar[p] * e_mid[:, sls[p]], s0[p]) for p in pairs]
        yield
        l_ab = [jnp.where(strict, m[:c2, :c2], 0.0) for m in m1]
        l_ak = [jnp.where(strict, m[:c2, c2:], 0.0) for m in m1]
        l_r = [jnp.concatenate([jnp.where(incl, m[c2:, :c2], 0.0), jnp.where(incl, m[c2:, c2:], 0.0)], axis=1)
               for m in m1]
        u = [m2[p][:c2] + _dot(l_ak[p], vb[p]) for p in pairs]
        pw = l_ab
        yield
        n_steps = int(math.log2(c_))
        for step in range(n_steps):
            if step < n_steps - 1:
                prod = [_dot(pw[p], jnp.concatenate([pw[p], u[p]], axis=1)) for p in pairs]
                pw = [q[:, :c2] for q in prod]
                u = [u[p] + prod[p][:, c2:] for p in pairs]
            else:
                u = [u[p] + _dot(pw[p], u[p]) for p in pairs]
            yield
        uv = [jnp.concatenate([u[p], vb[p]], axis=0) for p in pairs]
        ys = [m2[p][c2:] + _dot(l_r[p], uv[p]) for p in pairs]
        upd = [_dot_tn(uv[p], bk[p]) for p in pairs]
        yield
        for p in pairs:
            state_ref[bb, p] = s0[p] * w_all[:, sls[p]] + upd[p] * w_tail[:, sls[p]]
        y = jnp.concatenate([ys[p][:c_] + ys[p][c_:] for p in pairs], axis=1)
        sum_y, sum_rk = head_sums([y, r * k * rk_ref[...]])
        d = y - sum_y * (1.0 / n_)
        yield
        var = head_sums([d * d])[0] * (1.0 / n_)
        yn = d * lax.rsqrt(var + RW_LN_EPS) * lng_ref[...] + lnb_ref[...]
        o_ref[bb] = ((yn + sum_rk * v) * g_ref[bb]).astype(o_ref.dtype)
        yield

    for _ in zip(*[chunk_program(bb) for bb in range(RW_ROWS)]):
        pass


def _rw_scan(r, k, v, ld, a, b, g, b_, s_, r_k, ln_g, ln_b, ones_pair, tri):
    t = r.shape[0]
    nc = s_ // RW_CHUNK
    row = pl.BlockSpec((RW_ROWS, RW_CHUNK, RW_WIDTH), lambda bi, i: (bi, i, 0))
    vec = lambda x: x.reshape(1, -1)
    seq = lambda x: x.reshape(b_, s_, RW_WIDTH)
    out = pl.pallas_call(
        _rw_scan_kernel,
        grid=(b_ // RW_ROWS, nc),
        in_specs=[row] * 7 + [_const_spec(tri.shape), _const_spec(ones_pair.shape),
                              _const_spec((1, RW_WIDTH)), _const_spec((1, RW_WIDTH)), _const_spec((1, RW_WIDTH))],
        out_specs=row,
        out_shape=jax.ShapeDtypeStruct((b_, s_, RW_WIDTH), BF16),
        scratch_shapes=[pltpu.VMEM((RW_ROWS, RW_HEADS // 2, 2 * RW_HEAD_DIM, 2 * RW_HEAD_DIM), F32)],
        compiler_params=_params("parallel", "arbitrary"),
        name="rwkv_scan",
    )(seq(r), seq(k), seq(v), seq(ld), seq(a), seq(b), seq(g), tri, ones_pair, vec(r_k), vec(ln_g), vec(ln_b))
    return out.reshape(t, RW_WIDTH)


def _rt_log_gamma(h):
    return math.log1p(-(2.0 ** (-5.0 - h)))


def _retention_kernel(p_ref, cos_ref, sin_ref, gng_ref, gnb_ref, o_ref, state_ref):
    c_ = RT_CHUNK

    @pl.when(pl.program_id(1) == 0)
    def _():
        state_ref[...] = jnp.zeros_like(state_ref)

    cos = cos_ref[...]
    sin = sin_ref[...]
    row = lax.broadcasted_iota(jnp.int32, (c_, c_), 0).astype(F32)
    col = lax.broadcasted_iota(jnp.int32, (c_, c_), 1).astype(F32)
    rel = row - col
    heads = range(RT_HEADS)
    lgs = [_rt_log_gamma(h) for h in heads]
    sls = [slice(h * RT_DIM, (h + 1) * RT_DIM) for h in heads]
    decay = [jnp.where(rel >= 0, jnp.exp(jnp.maximum(rel, 0.0) * lg), 0.0) for lg in lgs]
    xi = [jnp.exp((row + 1.0) * lg) for lg in lgs]
    zeta = [jnp.exp((c_ - 1.0 - row) * lg) for lg in lgs]

    def chunk_program(bb):
        col_of = lambda part, h: slice(part * RT_WIDTH + h * RT_DIM, part * RT_WIDTH + (h + 1) * RT_DIM)
        q = [p_ref[bb, :, col_of(0, h)] for h in heads]
        k = [p_ref[bb, :, col_of(1, h)] for h in heads]
        v = [p_ref[bb, :, col_of(2, h)] for h in heads]
        q = [x * cos + pltpu.roll(x, RT_DIM // 2, 1) * sin for x in q]
        k = [(x * cos + pltpu.roll(x, RT_DIM // 2, 1) * sin) * (RT_DIM ** -0.5) for x in k]
        s0 = [state_ref[bb, h] for h in heads]
        yield
        scores = [_dot_nt(q[h], k[h]) * decay[h] for h in heads]
        cross = [_dot(q[h], s0[h]) * xi[h] for h in heads]
        for h in heads:
            state_ref[bb, h] = s0[h] * math.exp(c_ * lgs[h]) + _dot_tn(k[h] * zeta[h], v[h])
        yield
        o = [_dot(scores[h], v[h]) + cross[h] for h in heads]
        yield
        for h in heads:
            mu = jnp.mean(o[h], axis=-1, keepdims=True)
            d = o[h] - mu
            var = jnp.mean(d * d, axis=-1, keepdims=True)
            on = d * lax.rsqrt(var + EPS) * gng_ref[:, sls[h]] + gnb_ref[:, sls[h]]
            gate = p_ref[bb, :, col_of(3, h)]
            o_ref[bb, :, sls[h]] = (gate * jax.nn.sigmoid(gate) * on).astype(o_ref.dtype)
        yield

    for _ in zip(*[chunk_program(bb) for bb in range(RT_ROWS)]):
        pass


def _retention(p_rt, b_, s_, cos2, sin2, gn_g, gn_b):
    t = p_rt.shape[0]
    nc = s_ // RT_CHUNK
    out = pl.pallas_call(
        _retention_kernel,
        grid=(b_ // RT_ROWS, nc),
        in_specs=[pl.BlockSpec((RT_ROWS, RT_CHUNK, RT_IN), lambda bi, i: (bi, i, 0)),
                  pl.BlockSpec((RT_CHUNK, RT_DIM), lambda bi, i: (i, 0)),
                  pl.BlockSpec((RT_CHUNK, RT_DIM), lambda bi, i: (i, 0)),
                  _const_spec((1, RT_WIDTH)), _const_spec((1, RT_WIDTH))],
        out_specs=pl.BlockSpec((RT_ROWS, RT_CHUNK, RT_WIDTH), lambda bi, i: (bi, i, 0)),
        out_shape=jax.ShapeDtypeStruct((b_, s_, RT_WIDTH), BF16),
        scratch_shapes=[pltpu.VMEM((RT_ROWS, RT_HEADS, RT_DIM, RT_DIM), F32)],
        compiler_params=_params("parallel", "arbitrary"),
        name="retention",
    )(p_rt.reshape(b_, s_, RT_IN), cos2, sin2, gn_g.reshape(1, -1), gn_b.reshape(1, -1))
    return out.reshape(t, RT_WIDTH)


DSA_BLOCKS_PER_TILE = ROW_TILE // Q_BLOCK


def _dsa_prep(c_q, c_kv, kw, qg_ref, kvg_ref, wuq_ref, wuk_ref, wqi_ref,
              ckvn_ref, ckvt_ref, kidx_ref, widx_ref, qlat_ref, qidx_ref):
    cq = _rms(c_q, qg_ref[...]).astype(BF16)
    ckvn = _rms(c_kv, kvg_ref[...])
    ckvn_ref[...] = ckvn.astype(BF16)
    kidx_ref[...] = kw[:, :IDX_DIM].astype(BF16)
    q = jnp.dot(cq, wuq_ref[...], preferred_element_type=F32)
    qi = jnp.dot(cq, wqi_ref[...], preferred_element_type=F32)
    for blk in range(DSA_BLOCKS_PER_TILE):
        rows = slice(blk * Q_BLOCK, (blk + 1) * Q_BLOCK)
        ckvt_ref[blk] = ckvn[rows].T.astype(BF16)
        widx_ref[blk] = kw[rows].T[IDX_DIM:IDX_DIM + IDX_HEADS] * ((IDX_HEADS * IDX_DIM) ** -0.5)
    for h in range(DSA_HEADS):
        qh = q[:, h * DSA_HEAD_DIM:(h + 1) * DSA_HEAD_DIM]
        ql = (_dot(qh, wuk_ref[h]) * (DSA_HEAD_DIM ** -0.5)).astype(BF16)
        qih = qi[:, h * IDX_DIM:(h + 1) * IDX_DIM].astype(BF16)
        for blk in range(DSA_BLOCKS_PER_TILE):
            rows = slice(blk * Q_BLOCK, (blk + 1) * Q_BLOCK)
            qlat_ref[blk, h] = ql[rows]
            qidx_ref[blk, h] = qih[rows]


def _fold_rows(x, op):
    return functools.reduce(op, [x[i:i + 8] for i in range(0, x.shape[0], 8)])


def _bit_transpose32(words):
    a = list(words)
    j, m = 16, 0x0000FFFF
    while j:
        k = 0
        while k < 32:
            t = (a[k] ^ lax.shift_right_logical(a[k + j], jnp.int32(j))) & jnp.int32(m)
            a[k] = a[k] ^ t
            a[k + j] = a[k + j] ^ lax.shift_left(t, jnp.int32(j))
            k = (k + j + 1) & ~j
        j >>= 1
        m ^= m << j
    return a


def _dsa_kernel(top_k, qidx_ref, widx_ref, qlat_ref, kidx_ref, ckv_ref, ckvt_ref, wuv_ref, tril_ref, o_ref,
                key_ref, planes_ref, s_ref, acc_ref):
    qb = Q_BLOCK
    nh = DSA_HEADS
    j = pl.program_id(1)
    n_chunks = j + 1

    keyi = lax.broadcasted_iota(jnp.int32, (qb, qb), 0)
    qryi = lax.broadcasted_iota(jnp.int32, (qb, qb), 1)
    lanes = lambda h: slice(h * qb, (h + 1) * qb)

    def key_chunk(ref, c):
        return ref[pl.ds(pl.multiple_of(c * qb, qb), qb), :]

    @pl.when(j == 0)
    def _():
        planes_ref[...] = jnp.zeros_like(planes_ref)

    w_idx = widx_ref[0]
    q_idx = qidx_ref[0].reshape(IDX_HEADS * qb, IDX_DIM)
    q_lat = qlat_ref[0].reshape(nh * qb, DSA_KV_RANK)

    n_pairs = (n_chunks + 1) // 2

    def score_chunk(c):
        logits = lax.dot_general(key_chunk(kidx_ref, c), q_idx, (((1,), (1,)), ((), ())),
                                 preferred_element_type=F32)
        score = jnp.zeros((qb, qb), F32)
        for h in range(IDX_HEADS):
            score = score + jnp.maximum(logits[:, lanes(h)], 0.0) * w_idx[h:h + 1, :]
        bits = lax.bitcast_convert_type(score, jnp.int32)
        key = jnp.where(bits < 0, bits ^ jnp.int32(0x7FFFFFFF), bits)
        causal = keyi + c * qb <= qryi + j * qb
        key = jnp.where(causal, key, jnp.int32(INT_MIN))
        key_ref[c] = key
        return key

    def score_body(qi, carry):
        for pi in (2 * qi, 2 * qi + 1):
            keys = [score_chunk(2 * pi), score_chunk(2 * pi + 1)]
            words = [k[r:r + 8] ^ jnp.int32(INT_MIN) for k in keys for r in range(0, qb, 8)]
            for idx, plane in enumerate(_bit_transpose32(words)):
                planes_ref[31 - idx, pi] = plane
        return carry

    lax.fori_loop(0, (n_chunks + 3) // 4, score_body, 0)

    n_pairs_max = planes_ref.shape[1]
    live = tuple(jnp.where(p < n_pairs, jnp.full((8, qb), -1, jnp.int32), jnp.zeros((8, qb), jnp.int32))
                 for p in range(n_pairs_max))

    def bit_body(i, carry):
        thr_u, above, eq = carry
        plane = planes_ref[31 - i]
        hits = [eq[p] & plane[p] for p in range(n_pairs_max)]
        cnt = functools.reduce(jnp.add, [lax.population_count(h) for h in hits])
        cnt = above + jnp.sum(cnt, axis=0, keepdims=True)
        accept = cnt >= top_k
        thr_u = jnp.where(accept, thr_u | lax.shift_left(jnp.int32(1), 31 - i), thr_u)
        above = jnp.where(accept, above, cnt)
        eq = tuple(jnp.where(accept, h, e ^ h) for e, h in zip(eq, hits))
        return thr_u, above, eq

    zero_row = jnp.zeros((1, qb), jnp.int32)
    thr_u, above, _ = lax.fori_loop(0, 32, bit_body, (zero_row, zero_row, live))
    thr = thr_u ^ jnp.int32(INT_MIN)
    need = (top_k - above).astype(F32)

    def pair_logits(t, taken):
        tops = [None] * nh
        for c in (2 * t, 2 * t + 1):
            key = key_ref[c]
            causal = keyi + c * qb <= qryi + j * qb
            eq = jnp.logical_and(key == thr, causal)
            eq_f = jnp.where(eq, 1.0, 0.0)
            rank = taken + jnp.dot(tril_ref[...], eq_f.astype(BF16), preferred_element_type=F32)
            sel = jnp.logical_or(jnp.logical_and(key > thr, causal), jnp.logical_and(eq, rank <= need))
            s = lax.dot_general(key_chunk(ckv_ref, c), q_lat, (((1,), (1,)), ((), ())),
                                preferred_element_type=F32)
            for h in range(nh):
                sh = jnp.where(sel, s[:, lanes(h)], NEG_BIG)
                s_ref[c, :, lanes(h)] = sh
                top = _fold_rows(sh, jnp.maximum)
                tops[h] = top if tops[h] is None else jnp.maximum(tops[h], top)
            taken = taken + jnp.sum(eq_f, axis=0, keepdims=True)
        return taken, tuple(jnp.max(top, axis=0, keepdims=True) for top in tops)

    def pair_values(t, tops, peak, sums):
        ps, scales, new_peak, new_sums = [], [], [], []
        for h in range(nh):
            pk = jnp.maximum(peak[h], tops[h])
            scale = jnp.exp(peak[h] - pk)
            pa = jnp.exp(s_ref[2 * t, :, lanes(h)] - pk)
            pb = jnp.exp(s_ref[2 * t + 1, :, lanes(h)] - pk)
            new_sums.append(sums[h] * scale + _fold_rows(pa + pb, jnp.add))
            ps.append(jnp.concatenate([pa.astype(BF16), pb.astype(BF16)], axis=0))
            scales.append(scale)
            new_peak.append(pk)
        ckvt_pair = jnp.concatenate([ckvt_ref[2 * t], ckvt_ref[2 * t + 1]], axis=1)
        upd = jnp.dot(ckvt_pair, jnp.concatenate(ps, axis=1), preferred_element_type=F32)
        acc_ref[...] = acc_ref[...] * jnp.concatenate(scales, axis=1) + upd
        return tuple(new_peak), tuple(new_sums)

    acc_ref[...] = jnp.zeros_like(acc_ref)

    def attn_body(t, carry):
        taken, tops, peak, sums = carry
        peak, sums = pair_values(t, tops, peak, sums)
        taken, tops = pair_logits(jnp.minimum(t + 1, n_pairs - 1), taken)
        return taken, tops, peak, sums

    taken, tops = pair_logits(0, jnp.zeros((1, qb), F32))
    init = (taken, tops, tuple(jnp.full((1, qb), NEG_BIG, F32) for _ in range(nh)),
            tuple(jnp.zeros((8, qb), F32) for _ in range(nh)))
    _, _, _, sums = lax.fori_loop(0, n_pairs, attn_body, init)

    outs = []
    for h in range(nh):
        o_lat_t = acc_ref[:, lanes(h)] / jnp.sum(sums[h], axis=0, keepdims=True)
        outs.append(_dot_tn(o_lat_t, wuv_ref[h]))
    o_ref[...] = jnp.concatenate(outs, axis=1).astype(o_ref.dtype)


def _dsa(q_idx, w_idx, q_lat, k_idx, ckv_n, ckv_t, w_uv, tril, b_, s_):
    nb = s_ // Q_BLOCK
    t = b_ * s_
    top_k = min(TOPK_MAX, s_ // 4)
    blk = lambda n: pl.BlockSpec((Q_BLOCK, n), lambda bi, i: (bi * nb + i, 0))
    blk3 = lambda m, n: pl.BlockSpec((1, m, n), lambda bi, i: (bi * nb + i, 0, 0))
    blk4 = lambda n: pl.BlockSpec((1, DSA_HEADS, Q_BLOCK, n), lambda bi, i: (bi * nb + i, 0, 0, 0))
    seq = lambda n: pl.BlockSpec((s_, n), lambda bi, i: (bi, 0))
    stacked = DSA_HEADS * Q_BLOCK
    return pl.pallas_call(
        functools.partial(_dsa_kernel, top_k),
        grid=(b_, nb),
        in_specs=[blk4(IDX_DIM), blk3(IDX_HEADS, Q_BLOCK), blk4(DSA_KV_RANK),
                  seq(IDX_DIM), seq(DSA_KV_RANK),
                  pl.BlockSpec((nb, DSA_KV_RANK, Q_BLOCK), lambda bi, i: (bi, 0, 0)),
                  _const_spec(w_uv.shape), _const_spec(tril.shape)],
        out_specs=blk(DSA_WIDTH),
        out_shape=jax.ShapeDtypeStruct((t, DSA_WIDTH), BF16),
        scratch_shapes=[pltpu.VMEM((nb, Q_BLOCK, Q_BLOCK), jnp.int32),
                        pltpu.VMEM((32, nb // 2, 8, Q_BLOCK), jnp.int32),
                        pltpu.VMEM((nb, Q_BLOCK, stacked), F32),
                        pltpu.VMEM((DSA_KV_RANK, stacked), F32)],
        compiler_params=_params("parallel", "arbitrary"),
        name="dsa_attn",
    )(q_idx, w_idx, q_lat, k_idx, ckv_n, ckv_t, w_uv, tril)


SC_TILE = ROW_TILE


def _od_in_kernel(x_ref, gn_ref, wq_ref, wkv_ref, wkw_ref, wsc_ref, cw_ref, cb_ref,
                  qg_ref, kvg_ref, wuq_ref, wuk_ref, wqi_ref,
                  ckvn_ref, ckvt_ref, kidx_ref, widx_ref, qlat_ref, qidx_ref, yd_ref, carry_ref, p_ref):
    @pl.when(pl.program_id(1) == 0)
    def _():
        carry_ref[...] = jnp.zeros_like(carry_ref)

    xb = _rms(x_ref[...], gn_ref[...]).astype(BF16)
    c_q, c_kv, kw = (jnp.dot(xb, w_ref[...], preferred_element_type=F32) for w_ref in (wq_ref, wkv_ref, wkw_ref))
    for c in range(0, 3 * SC_WIDTH, PROJ_CHUNK):
        sl = slice(c, c + PROJ_CHUNK)
        p_ref[:, sl] = jnp.dot(xb, wsc_ref[:, sl], preferred_element_type=F32)
    _dsa_prep(c_q, c_kv, kw, qg_ref, kvg_ref, wuq_ref, wuk_ref, wqi_ref,
              ckvn_ref, ckvt_ref, kidx_ref, widx_ref, qlat_ref, qidx_ref)

    h = p_ref[:, :SC_WIDTH]
    gate_b = p_ref[:, SC_WIDTH:2 * SC_WIDTH]
    gate_c = p_ref[:, 2 * SC_WIDTH:]
    u = gate_c * h
    carry = carry_ref[...]
    y = u * cw_ref[2:3, :] + _shift_rows(u, carry, 1) * cw_ref[1:2, :] + _shift_rows(u, carry, 2) * cw_ref[0:1, :]
    carry_ref[...] = u[SC_TILE - 8:, :]
    yd_ref[...] = (gate_b * (y + cb_ref[...])).astype(yd_ref.dtype)


def _od_in(x, gn, w_q, w_kv, w_kw, w_sc, b_, s_, conv_w, conv_b, q_g, kv_g, w_uq, w_uk, w_qi):
    t, d = x.shape
    nt = s_ // SC_TILE
    row = lambda n: pl.BlockSpec((SC_TILE, n), lambda bi, i: (bi * nt + i, 0))
    blk3 = lambda m, n: pl.BlockSpec((DSA_BLOCKS_PER_TILE, m, n), lambda bi, i: (bi * nt + i, 0, 0))
    blk4 = lambda n: pl.BlockSpec((DSA_BLOCKS_PER_TILE, DSA_HEADS, Q_BLOCK, n), lambda bi, i: (bi * nt + i, 0, 0, 0))
    consts = [w_q, w_kv, w_kw, w_sc, jnp.pad(conv_w, ((0, 8 - SC_KERNEL), (0, 0))), conv_b.reshape(1, -1),
              q_g.reshape(1, -1), kv_g.reshape(1, -1), w_uq, w_uk, w_qi]
    return pl.pallas_call(
        _od_in_kernel,
        grid=(b_, nt),
        in_specs=[row(d), _const_spec((1, d))] + [_const_spec(c.shape) for c in consts],
        out_specs=[row(DSA_KV_RANK), blk3(DSA_KV_RANK, Q_BLOCK), row(IDX_DIM), blk3(IDX_HEADS, Q_BLOCK),
                   blk4(DSA_KV_RANK), blk4(IDX_DIM), row(SC_WIDTH)],
        out_shape=[jax.ShapeDtypeStruct((t, DSA_KV_RANK), BF16),
                   jax.ShapeDtypeStruct((t // Q_BLOCK, DSA_KV_RANK, Q_BLOCK), BF16),
                   jax.ShapeDtypeStruct((t, IDX_DIM), BF16),
                   jax.ShapeDtypeStruct((t // Q_BLOCK, IDX_HEADS, Q_BLOCK), F32),
                   jax.ShapeDtypeStruct((t // Q_BLOCK, DSA_HEADS, Q_BLOCK, DSA_KV_RANK), BF16),
                   jax.ShapeDtypeStruct((t // Q_BLOCK, IDX_HEADS, Q_BLOCK, IDX_DIM), BF16),
                   jax.ShapeDtypeStruct((t, SC_WIDTH), BF16)],
        scratch_shapes=[pltpu.VMEM((8, SC_WIDTH), F32), pltpu.VMEM((SC_TILE, 3 * SC_WIDTH), F32)],
        compiler_params=_params("parallel", "arbitrary"),
        name="od_in_proj",
    )(x, gn.reshape(1, -1), *consts)


def _xattn_kernel(y1_ref, y2_ref, w1_ref, w2_ref, gm_ref, x_ref, gq_ref, wq_ref, k_ref, v_ref, wo_ref, go_ref,
                  o_ref, att_ref):
    def rows_program(rows):
        mix = (jnp.dot(y1_ref[rows, :], w1_ref[...], preferred_element_type=F32)
               + jnp.dot(y2_ref[rows, :], w2_ref[...], preferred_element_type=F32))
        yield
        x = x_ref[rows, :] + _rms(mix, gm_ref[...])
        q = jnp.dot(_rms(x, gq_ref[...]).astype(BF16), wq_ref[...], preferred_element_type=F32)
        yield
        for h in range(XA_HEADS):
            sl = slice(h * XA_HEAD_DIM, (h + 1) * XA_HEAD_DIM)
            s = _dot_nt(q[:, sl], k_ref[:, sl]) * (XA_HEAD_DIM ** -0.5)
            s = s - jnp.max(s, axis=-1, keepdims=True)
            p = jnp.exp(s)
            p = p / jnp.sum(p, axis=-1, keepdims=True)
            att_ref[rows, sl] = _dot(p, v_ref[:, sl])
            yield
        hout = jnp.dot(att_ref[rows, :].astype(BF16), wo_ref[...], preferred_element_type=F32)
        yield
        o_ref[rows, :] = x + _rms(hout, go_ref[...])
        yield

    sub = ROW_TILE // XA_SUBTILES
    for _ in zip(*[rows_program(slice(i * sub, (i + 1) * sub)) for i in range(XA_SUBTILES)]):
        pass


def _xattn(y1, y2, w1, w2, g_mix, x, g_q, wq, k_mem, v_mem, wo, g_o, b_, s_):
    t = x.shape[0]
    nt = s_ // ROW_TILE
    row = lambda n: pl.BlockSpec((ROW_TILE, n), lambda bi, i: (bi * nt + i, 0))
    mem = pl.BlockSpec((MEM_LEN, XA_WIDTH), lambda bi, i: (bi, 0))
    vec = lambda g: g.reshape(1, -1)
    return pl.pallas_call(
        _xattn_kernel,
        grid=(b_, nt),
        in_specs=[row(y1.shape[1]), row(y2.shape[1]), _const_spec(w1.shape), _const_spec(w2.shape),
                  _const_spec((1, D_MODEL)), row(D_MODEL), _const_spec((1, D_MODEL)), _const_spec(wq.shape),
                  mem, mem, _const_spec(wo.shape), _const_spec((1, D_MODEL))],
        out_specs=row(D_MODEL),
        out_shape=jax.ShapeDtypeStruct((t, D_MODEL), F32),
        scratch_shapes=[pltpu.VMEM((ROW_TILE, XA_WIDTH), F32)],
        compiler_params=_params("parallel", "parallel"),
        name="mix_out_xattn",
    )(y1, y2, w1, w2, vec(g_mix), x, vec(g_q), wq, k_mem, v_mem, wo, vec(g_o))


def _block_diag(n_blocks, size, value):
    return np.kron(np.eye(n_blocks, dtype=np.float32), np.full((size, size), value, np.float32))


def _rope_tables(s_):
    half = RT_DIM // 2
    inv_freq = RT_ROPE_BASE ** (-jnp.arange(half, dtype=F32) / half)
    ang = jnp.arange(s_).astype(F32)[:, None] * inv_freq[None, :]
    cos, sin = jnp.cos(ang), jnp.sin(ang)
    return jnp.concatenate([cos, cos], axis=-1), jnp.concatenate([-sin, sin], axis=-1)


def kernel(x, mem, norm_g, mem_norm_g, ffn_w_gate, ffn_w_up, ffn_w_down, xa_wq, xa_wk, xa_wv, xa_wo, ev_w_in, ev_w_out, rw_mu, rw_w0, rw_w2, rw_a0, rw_a2, rw_g2, rw_k_k, rw_k_a, rw_r_k, rw_ln_g, rw_ln_b, rt_gn_g, rt_gn_b, od_w_in, od_w_out, dsa_q_norm_g, dsa_kv_norm_g, dsa_w_uq, dsa_w_uk, dsa_w_uv, dsa_w_qi, sc_conv_w, sc_conv_b):
    b_, s_, d_ = x.shape
    depth = norm_g.shape[0]
    t = b_ * s_
    bf = lambda w: w.astype(BF16)

    ones_blk = jnp.asarray(_block_diag(RW_HEADS, RW_HEAD_DIM, 1.0), BF16)
    ones_pair = jnp.asarray(_block_diag(2, RW_HEAD_DIM, 1.0), BF16)
    tri_rw = jnp.asarray(np.tril(np.ones((RW_CHUNK, RW_CHUNK), np.float32)), BF16)
    tril_dsa = jnp.asarray(np.tril(np.ones((Q_BLOCK, Q_BLOCK), np.float32)), BF16)
    cos2, sin2 = _rope_tables(s_)

    w_gate, w_up, w_down = bf(ffn_w_gate), bf(ffn_w_up), bf(ffn_w_down)
    xf = x.reshape(t, d_)
    mem_kv = _norm_proj(mem.reshape(b_ * MEM_LEN, d_), mem_norm_g,
                        [bf(w[l]) for l in range(depth) for w in (xa_wk, xa_wv)], "mem_kv_proj")
    for l in range(depth):
        ng = norm_g[l]
        i = l // 2
        xf = _ffn(xf, ng[0], w_gate, w_up, w_down, l, 0, ng[1])
        if l % 2 == 0:
            w_in = bf(ev_w_in[i])
            r, k, v, ld, a, b, g, p_rt = _ev_in(xf, ng[2], w_in[:, :RW_IN], w_in[:, RW_IN:], b_, s_, rw_mu[i],
                                                rw_w0[i], rw_w2[i], rw_a0[i], rw_a2[i], rw_g2[i], rw_k_k[i],
                                                rw_k_a[i], ones_blk)
            y_a = _rw_scan(r, k, v, ld, a, b, g, b_, s_, rw_r_k[i], rw_ln_g[i], rw_ln_b[i],
                           ones_pair, tri_rw)
            y_b = _retention(p_rt, b_, s_, cos2, sin2, rt_gn_g[i], rt_gn_b[i])
            ys, w_out, split = (y_a, y_b), bf(ev_w_out[i]), RW_WIDTH
        else:
            w_in = od_w_in[i]
            kw_w = jnp.pad(w_in[:, DSA_Q_RANK + DSA_KV_RANK:DSA_IN], ((0, 0), (0, LANES - IDX_DIM - IDX_HEADS)))
            ckv_n, ckv_t, k_idx, w_idx, q_lat, q_idx, y_d = _od_in(
                xf, ng[2], bf(w_in[:, :DSA_Q_RANK]), bf(w_in[:, DSA_Q_RANK:DSA_Q_RANK + DSA_KV_RANK]),
                bf(kw_w), bf(w_in[:, DSA_IN:]), b_, s_, sc_conv_w[i], sc_conv_b[i],
                dsa_q_norm_g[i], dsa_kv_norm_g[i],
                bf(dsa_w_uq[i].reshape(DSA_Q_RANK, DSA_WIDTH)), bf(dsa_w_uk[i]),
                bf(dsa_w_qi[i].reshape(DSA_Q_RANK, IDX_HEADS * IDX_DIM)))
            y_c = _dsa(q_idx, w_idx, q_lat, k_idx, ckv_n, ckv_t, bf(dsa_w_uv[i]), tril_dsa, b_, s_)
            ys, w_out, split = (y_c, y_d), bf(od_w_out[i]), DSA_WIDTH
        xf = _xattn(ys[0], ys[1], w_out[:split], w_out[split:], ng[3], xf, ng[4], bf(xa_wq[l]),
                    mem_kv[2 * l], mem_kv[2 * l + 1], bf(xa_wo[l]), ng[5], b_, s_)
        xf = _ffn(xf, ng[6], w_gate, w_up, w_down, l, 1, ng[7])
    return xf.reshape(b_, s_, d_)
```
